```python
import jax, jax.numpy as jnp
from jax import lax
import numpy as np

D_MODEL = 1024
BATCH = 8
SEQ = 16384
DEPTH = 2

D_FF = 2816
N_SUB = 3
W_A = D_MODEL // 2
W_B = D_MODEL // 2
H_A = 8
DH_A = W_A // H_A
G_B = 8
DG_B = W_B // G_B
CONV_A = 4
CONV_B = 31
LRU_C = 8.0
W_C = D_MODEL
H_C = 8
DH_C = W_C // H_C
CHUNK = 128
EPS = 1e-6

kernel_name = "hybrid_rglru_conformer_gmlp_block"


def _rmsnorm(x, g):
    x32 = x.astype(jnp.float32)
    y = x32 * lax.rsqrt(jnp.mean(x32 * x32, axis=-1, keepdims=True) + EPS)
    return (y * g.astype(jnp.float32)).astype(x.dtype)


def _layernorm(x, g, b):
    x32 = x.astype(jnp.float32)
    mu = jnp.mean(x32, axis=-1, keepdims=True)
    var = jnp.mean(jnp.square(x32 - mu), axis=-1, keepdims=True)
    y = (x32 - mu) * lax.rsqrt(var + EPS)
    return (y * g.astype(jnp.float32) + b.astype(jnp.float32)).astype(x.dtype)


def _causal_dwconv(x, w, b):
    k = w.shape[0]
    y = lax.conv_general_dilated(
        x, w[:, None, :].astype(x.dtype), window_strides=(1,), padding=[(k - 1, 0)],
        dimension_numbers=("NWC", "WIO", "NWC"), feature_group_count=x.shape[-1])
    return y + b


def _swiglu(h, w13, w2):
    g, u = jnp.split(h @ w13, 2, axis=-1)
    return (jax.nn.silu(g) * u) @ w2


def _rg_lru(xr, gate_w, gate_b, lam):
    bsz, s, _ = xr.shape
    xh = xr.reshape(bsz, s, H_A, DH_A)
    gates = (jnp.einsum("bshd,hde->bshe", xh, gate_w) + gate_b).astype(jnp.float32)
    r, i = jnp.split(jax.nn.sigmoid(gates), 2, axis=-1)
    r = r.reshape(bsz, s, W_A)
    i = i.reshape(bsz, s, W_A)
    log_a = LRU_C * r * jax.nn.log_sigmoid(lam.astype(jnp.float32))
    a = jnp.exp(log_a)
    u = jnp.sqrt(-jnp.expm1(2.0 * log_a)) * (i * xr.astype(jnp.float32))

    def combine(left, right):
        a1, b1 = left
        a2, b2 = right
        return a1 * a2, a2 * b1 + b2

    _, h = lax.associative_scan(combine, (a, u), axis=1)
    return h.astype(xr.dtype)


def _mixer_ab(h, w_in, a_conv_w, a_conv_b, a_gate_w, a_gate_b, a_lam,
              b_conv_w, b_conv_b, b_norm_g, b_norm_b, w_out):
    z = h @ w_in
    a_gate, a_x, b_val, b_gate = jnp.split(z, [W_A, 2 * W_A, 2 * W_A + W_B], axis=-1)
    a_x = _causal_dwconv(a_x, a_conv_w, a_conv_b)
    y_a = _rg_lru(a_x, a_gate_w, a_gate_b, a_lam) * jax.nn.gelu(a_gate)
    v = b_val * jax.nn.sigmoid(b_gate)
    v = _causal_dwconv(v, b_conv_w, b_conv_b)
    bsz, s, _ = v.shape
    v = _layernorm(v.reshape(bsz, s, G_B, DG_B), b_norm_g.reshape(G_B, DG_B),
                   b_norm_b.reshape(G_B, DG_B)).reshape(bsz, s, W_B)
    y_b = jax.nn.silu(v)
    return jnp.concatenate([y_a, y_b], axis=-1) @ w_out


def _mixer_c(h, w_in, b_in, norm_g, norm_b, w_s, b_s, w_out):
    z = jax.nn.gelu(h @ w_in + b_in)
    u, v = jnp.split(z, 2, axis=-1)
    v = _layernorm(v, norm_g, norm_b)
    bsz, s, _ = v.shape
    vc = v.reshape(bsz, s // CHUNK, CHUNK, H_C, DH_C)
    mask = jnp.tril(jnp.ones((CHUNK, CHUNK), dtype=bool))
    ws = jnp.where(mask, w_s, jnp.zeros_like(w_s)).astype(v.dtype)
    mixed = jnp.einsum("hts,bnshd->bnthd", ws, vc) + jnp.transpose(b_s)[:, :, None]
    return (u * mixed.reshape(bsz, s, W_C)) @ w_out


def _sublayer(x, fn, pre_g, post_g, shift, scale, gate, res_w):
    h = _rmsnorm(x, pre_g) * (1.0 + scale[:, None, :]) + shift[:, None, :]
    y = _rmsnorm(fn(h), post_g)
    return x + res_w * (1.0 + gate[:, None, :]) * y


def _fwd_setup_inputs(seed: int = 0) -> dict:
    key = jax.random.key(seed)
    ks = jax.random.split(key, 32)
    ne = (DEPTH + 1) // 2
    no = DEPTH // 2
    f32 = jnp.float32

    def nrm(k, shape, scale):
        return jax.random.normal(k, shape, f32) * scale

    u_lam = jax.random.uniform(ks[14], (ne, W_A), f32, minval=0.9, maxval=0.999)
    sa = u_lam ** (1.0 / LRU_C)
    a_lam = jnp.log(sa) - jnp.log1p(-sa)
    return {
        "x": nrm(ks[0], (BATCH, SEQ, D_MODEL), 1.0),
        "c": nrm(ks[1], (BATCH, D_MODEL), 1.0),
        "ada_w": nrm(ks[2], (DEPTH, D_MODEL, N_SUB * 3 * D_MODEL), 0.1 * D_MODEL ** -0.5),
        "ada_b": nrm(ks[3], (DEPTH, N_SUB * 3 * D_MODEL), 0.01),
        "norm_pre": 1.0 + nrm(ks[4], (DEPTH, N_SUB, D_MODEL), 0.02),
        "norm_post": 1.0 + nrm(ks[5], (DEPTH, N_SUB, D_MODEL), 0.02),
        "ffn_w13": nrm(ks[6], (DEPTH, 2, D_MODEL, 2 * D_FF), D_MODEL ** -0.5),
        "ffn_w2": nrm(ks[7], (DEPTH, 2, D_FF, D_MODEL), D_FF ** -0.5),
        "ab_w_in": nrm(ks[8], (ne, D_MODEL, 2 * W_A + 2 * W_B), D_MODEL ** -0.5),
        "a_conv_w": nrm(ks[9], (ne, CONV_A, W_A), CONV_A ** -0.5),
        "a_conv_b": nrm(ks[10], (ne, W_A), 0.01),
        "a_gate_w": nrm(ks[11], (ne, H_A, DH_A, 2 * DH_A), DH_A ** -0.5),
        "a_gate_b": nrm(ks[12], (ne, H_A, 2 * DH_A), 0.01),
        "a_lam": a_lam,
        "b_conv_w": nrm(ks[15], (ne, CONV_B, W_B), CONV_B ** -0.5),
        "b_conv_b": nrm(ks[16], (ne, W_B), 0.01),
        "b_norm_g": 1.0 + nrm(ks[17], (ne, W_B), 0.02),
        "b_norm_b": nrm(ks[18], (ne, W_B), 0.01),
        "ab_w_out": nrm(ks[19], (ne, W_A + W_B, D_MODEL), (W_A + W_B) ** -0.5),
        "c_w_in": nrm(ks[20], (no, D_MODEL, 2 * W_C), D_MODEL ** -0.5),
        "c_b_in": nrm(ks[21], (no, 2 * W_C), 0.01),
        "c_norm_g": 1.0 + nrm(ks[22], (no, W_C), 0.02),
        "c_norm_b": nrm(ks[23], (no, W_C), 0.01),
        "c_w_s": nrm(ks[24], (no, H_C, CHUNK, CHUNK), 0.5 * CHUNK ** -0.5),
        "c_b_s": 1.0 + nrm(ks[25], (no, H_C, CHUNK), 0.01),
        "c_w_out": nrm(ks[26], (no, W_C, D_MODEL), W_C ** -0.5),
    }


def _fwd_reference(x, c, ada_w, ada_b, norm_pre, norm_post, ffn_w13, ffn_w2,
              ab_w_in, a_conv_w, a_conv_b, a_gate_w, a_gate_b, a_lam,
              b_conv_w, b_conv_b, b_norm_g, b_norm_b, ab_w_out,
              c_w_in, c_b_in, c_norm_g, c_norm_b, c_w_s, c_b_s, c_w_out):
    bsz = x.shape[0]
    c_act = jax.nn.silu(c)
    for l in range(DEPTH):
        mod = (c_act @ ada_w[l] + ada_b[l]).reshape(bsz, N_SUB, 3, D_MODEL)

        def ffn_pre(h, l=l):
            return _swiglu(h, ffn_w13[l, 0], ffn_w2[l, 0])

        def ffn_post(h, l=l):
            return _swiglu(h, ffn_w13[l, 1], ffn_w2[l, 1])

        if l % 2 == 0:
            k = l // 2

            def mixer(h, k=k):
                return _mixer_ab(h, ab_w_in[k], a_conv_w[k], a_conv_b[k], a_gate_w[k],
                                 a_gate_b[k], a_lam[k], b_conv_w[k], b_conv_b[k],
                                 b_norm_g[k], b_norm_b[k], ab_w_out[k])
        else:
            k = l // 2

            def mixer(h, k=k):
                return _mixer_c(h, c_w_in[k], c_b_in[k], c_norm_g[k], c_norm_b[k],
                                c_w_s[k], c_b_s[k], c_w_out[k])

        x = _sublayer(x, ffn_pre, norm_pre[l, 0], norm_post[l, 0],
                      mod[:, 0, 0], mod[:, 0, 1], mod[:, 0, 2], 0.5)
        x = _sublayer(x, mixer, norm_pre[l, 1], norm_post[l, 1],
                      mod[:, 1, 0], mod[:, 1, 1], mod[:, 1, 2], 1.0)
        x = _sublayer(x, ffn_post, norm_pre[l, 2], norm_post[l, 2],
                      mod[:, 2, 0], mod[:, 2, 1], mod[:, 2, 2], 0.5)
    return x


import jax as _jax
import jax.numpy as _jnp

TWIN_FORMAT = 'train_step'
FWD_PARAMS = ['x', 'c', 'ada_w', 'ada_b', 'norm_pre', 'norm_post', 'ffn_w13', 'ffn_w2', 'ab_w_in', 'a_conv_w', 'a_conv_b', 'a_gate_w', 'a_gate_b', 'a_lam', 'b_conv_w', 'b_conv_b', 'b_norm_g', 'b_norm_b', 'ab_w_out', 'c_w_in', 'c_b_in', 'c_norm_g', 'c_norm_b', 'c_w_s', 'c_b_s', 'c_w_out']
TWIN_WEIGHTS = ['ada_w', 'ada_b', 'norm_pre', 'norm_post', 'ffn_w13', 'ffn_w2', 'ab_w_in', 'a_conv_w', 'a_conv_b', 'a_gate_w', 'a_gate_b', 'a_lam', 'b_conv_w', 'b_conv_b', 'b_norm_g', 'b_norm_b', 'ab_w_out', 'c_w_in', 'c_b_in', 'c_norm_g', 'c_norm_b', 'c_w_s', 'c_b_s', 'c_w_out']
TWIN_DIFF_INPUT = 'x'
TWIN_INPUTS = ['x', 'c', 'ada_w', 'ada_b', 'norm_pre', 'norm_post', 'ffn_w13', 'ffn_w2', 'ab_w_in', 'a_conv_w', 'a_conv_b', 'a_gate_w', 'a_gate_b', 'a_lam', 'b_conv_w', 'b_conv_b', 'b_norm_g', 'b_norm_b', 'ab_w_out', 'c_w_in', 'c_b_in', 'c_norm_g', 'c_norm_b', 'c_w_s', 'c_b_s', 'c_w_out', 'loss_target', 'm_ada_w', 'm_ada_b', 'm_norm_pre', 'm_norm_post', 'm_ffn_w13', 'm_ffn_w2', 'm_ab_w_in', 'm_a_conv_w', 'm_a_conv_b', 'm_a_gate_w', 'm_a_gate_b', 'm_a_lam', 'm_b_conv_w', 'm_b_conv_b', 'm_b_norm_g', 'm_b_norm_b', 'm_ab_w_out', 'm_c_w_in', 'm_c_b_in', 'm_c_norm_g', 'm_c_norm_b', 'm_c_w_s', 'm_c_b_s', 'm_c_w_out', 'v_ada_w', 'v_ada_b', 'v_norm_pre', 'v_norm_post', 'v_ffn_w13', 'v_ffn_w2', 'v_ab_w_in', 'v_a_conv_w', 'v_a_conv_b', 'v_a_gate_w', 'v_a_gate_b', 'v_a_lam', 'v_b_conv_w', 'v_b_conv_b', 'v_b_norm_g', 'v_b_norm_b', 'v_ab_w_out', 'v_c_w_in', 'v_c_b_in', 'v_c_norm_g', 'v_c_norm_b', 'v_c_w_s', 'v_c_b_s', 'v_c_w_out']
TWIN_OUTPUTS = ['loss', 'grad_x', 'grad_ada_w', 'grad_ada_b', 'grad_norm_pre', 'grad_norm_post', 'grad_ffn_w13', 'grad_ffn_w2', 'grad_ab_w_in', 'grad_a_conv_w', 'grad_a_conv_b', 'grad_a_gate_w', 'grad_a_gate_b', 'grad_a_lam', 'grad_b_conv_w', 'grad_b_conv_b', 'grad_b_norm_g', 'grad_b_norm_b', 'grad_ab_w_out', 'grad_c_w_in', 'grad_c_b_in', 'grad_c_norm_g', 'grad_c_norm_b', 'grad_c_w_s', 'grad_c_b_s', 'grad_c_w_out', 'delta_ada_w', 'delta_ada_b', 'delta_norm_pre', 'delta_norm_post', 'delta_ffn_w13', 'delta_ffn_w2', 'delta_ab_w_in', 'delta_a_conv_w', 'delta_a_conv_b', 'delta_a_gate_w', 'delta_a_gate_b', 'delta_a_lam', 'delta_b_conv_w', 'delta_b_conv_b', 'delta_b_norm_g', 'delta_b_norm_b', 'delta_ab_w_out', 'delta_c_w_in', 'delta_c_b_in', 'delta_c_norm_g', 'delta_c_norm_b', 'delta_c_w_s', 'delta_c_b_s', 'delta_c_w_out', 'new_m_ada_w', 'new_m_ada_b', 'new_m_norm_pre', 'new_m_norm_post', 'new_m_ffn_w13', 'new_m_ffn_w2', 'new_m_ab_w_in', 'new_m_a_conv_w', 'new_m_a_conv_b', 'new_m_a_gate_w', 'new_m_a_gate_b', 'new_m_a_lam', 'new_m_b_conv_w', 'new_m_b_conv_b', 'new_m_b_norm_g', 'new_m_b_norm_b', 'new_m_ab_w_out', 'new_m_c_w_in', 'new_m_c_b_in', 'new_m_c_norm_g', 'new_m_c_norm_b', 'new_m_c_w_s', 'new_m_c_b_s', 'new_m_c_w_out', 'new_v_ada_w', 'new_v_ada_b', 'new_v_norm_pre', 'new_v_norm_post', 'new_v_ffn_w13', 'new_v_ffn_w2', 'new_v_ab_w_in', 'new_v_a_conv_w', 'new_v_a_conv_b', 'new_v_a_gate_w', 'new_v_a_gate_b', 'new_v_a_lam', 'new_v_b_conv_w', 'new_v_b_conv_b', 'new_v_b_norm_g', 'new_v_b_norm_b', 'new_v_ab_w_out', 'new_v_c_w_in', 'new_v_c_b_in', 'new_v_c_norm_g', 'new_v_c_norm_b', 'new_v_c_w_s', 'new_v_c_b_s', 'new_v_c_w_out']
TWIN_LEAF_KINDS = {'loss': 'loss', 'grad_x': 'grad_x', 'grad_ada_w': 'grad_w', 'grad_ada_b': 'grad_w', 'grad_norm_pre': 'grad_w', 'grad_norm_post': 'grad_w', 'grad_ffn_w13': 'grad_w', 'grad_ffn_w2': 'grad_w', 'grad_ab_w_in': 'grad_w', 'grad_a_conv_w': 'grad_w', 'grad_a_conv_b': 'grad_w', 'grad_a_gate_w': 'grad_w', 'grad_a_gate_b': 'grad_w', 'grad_a_lam': 'grad_w', 'grad_b_conv_w': 'grad_w', 'grad_b_conv_b': 'grad_w', 'grad_b_norm_g': 'grad_w', 'grad_b_norm_b': 'grad_w', 'grad_ab_w_out': 'grad_w', 'grad_c_w_in': 'grad_w', 'grad_c_b_in': 'grad_w', 'grad_c_norm_g': 'grad_w', 'grad_c_norm_b': 'grad_w', 'grad_c_w_s': 'grad_w', 'grad_c_b_s': 'grad_w', 'grad_c_w_out': 'grad_w', 'delta_ada_w': 'delta_w', 'delta_ada_b': 'delta_w', 'delta_norm_pre': 'delta_w', 'delta_norm_post': 'delta_w', 'delta_ffn_w13': 'delta_w', 'delta_ffn_w2': 'delta_w', 'delta_ab_w_in': 'delta_w', 'delta_a_conv_w': 'delta_w', 'delta_a_conv_b': 'delta_w', 'delta_a_gate_w': 'delta_w', 'delta_a_gate_b': 'delta_w', 'delta_a_lam': 'delta_w', 'delta_b_conv_w': 'delta_w', 'delta_b_conv_b': 'delta_w', 'delta_b_norm_g': 'delta_w', 'delta_b_norm_b': 'delta_w', 'delta_ab_w_out': 'delta_w', 'delta_c_w_in': 'delta_w', 'delta_c_b_in': 'delta_w', 'delta_c_norm_g': 'delta_w', 'delta_c_norm_b': 'delta_w', 'delta_c_w_s': 'delta_w', 'delta_c_b_s': 'delta_w', 'delta_c_w_out': 'delta_w', 'new_m_ada_w': 'new_m', 'new_m_ada_b': 'new_m', 'new_m_norm_pre': 'new_m', 'new_m_norm_post': 'new_m', 'new_m_ffn_w13': 'new_m', 'new_m_ffn_w2': 'new_m', 'new_m_ab_w_in': 'new_m', 'new_m_a_conv_w': 'new_m', 'new_m_a_conv_b': 'new_m', 'new_m_a_gate_w': 'new_m', 'new_m_a_gate_b': 'new_m', 'new_m_a_lam': 'new_m', 'new_m_b_conv_w': 'new_m', 'new_m_b_conv_b': 'new_m', 'new_m_b_norm_g': 'new_m', 'new_m_b_norm_b': 'new_m', 'new_m_ab_w_out': 'new_m', 'new_m_c_w_in': 'new_m', 'new_m_c_b_in': 'new_m', 'new_m_c_norm_g': 'new_m', 'new_m_c_norm_b': 'new_m', 'new_m_c_w_s': 'new_m', 'new_m_c_b_s': 'new_m', 'new_m_c_w_out': 'new_m', 'new_v_ada_w': 'new_v', 'new_v_ada_b': 'new_v', 'new_v_norm_pre': 'new_v', 'new_v_norm_post': 'new_v', 'new_v_ffn_w13': 'new_v', 'new_v_ffn_w2': 'new_v', 'new_v_ab_w_in': 'new_v', 'new_v_a_conv_w': 'new_v', 'new_v_a_conv_b': 'new_v', 'new_v_a_gate_w': 'new_v', 'new_v_a_gate_b': 'new_v', 'new_v_a_lam': 'new_v', 'new_v_b_conv_w': 'new_v', 'new_v_b_conv_b': 'new_v', 'new_v_b_norm_g': 'new_v', 'new_v_b_norm_b': 'new_v', 'new_v_ab_w_out': 'new_v', 'new_v_c_w_in': 'new_v', 'new_v_c_b_in': 'new_v', 'new_v_c_norm_g': 'new_v', 'new_v_c_norm_b': 'new_v', 'new_v_c_w_s': 'new_v', 'new_v_c_b_s': 'new_v', 'new_v_c_w_out': 'new_v'}


def _forward(args):
    return _fwd_reference(*[args[k] for k in FWD_PARAMS])


def _output_shape():
    def fwd():
        inp = _fwd_setup_inputs(0)
        return _fwd_reference(*[inp[k] for k in FWD_PARAMS])
    out = _jax.eval_shape(fwd)
    return out.shape, out.dtype

N_MICROBATCH = 1
ADAM_LR = 0.001
ADAM_B1 = 0.9
ADAM_B2 = 0.999
ADAM_EPS = 1e-08
ADAM_WD = 0.01
ADAM_STEP = 10
PER_EXAMPLE_BATCH_AXIS = {'x': 0, 'c': 0, 'loss_target': 0}
SHARED_INPUTS = []
_WEIGHT_DTYPES = {'ada_w': _jnp.float32, 'ada_b': _jnp.float32, 'norm_pre': _jnp.float32, 'norm_post': _jnp.float32, 'ffn_w13': _jnp.float32, 'ffn_w2': _jnp.float32, 'ab_w_in': _jnp.float32, 'a_conv_w': _jnp.float32, 'a_conv_b': _jnp.float32, 'a_gate_w': _jnp.float32, 'a_gate_b': _jnp.float32, 'a_lam': _jnp.float32, 'b_conv_w': _jnp.float32, 'b_conv_b': _jnp.float32, 'b_norm_g': _jnp.float32, 'b_norm_b': _jnp.float32, 'ab_w_out': _jnp.float32, 'c_w_in': _jnp.float32, 'c_b_in': _jnp.float32, 'c_norm_g': _jnp.float32, 'c_norm_b': _jnp.float32, 'c_w_s': _jnp.float32, 'c_b_s': _jnp.float32, 'c_w_out': _jnp.float32}
MOMENT_SCALE = {'ada_w': 1.681331e+01, 'ada_b': 5.445985e+01, 'norm_pre': 3.304098e+00, 'norm_post': 8.151304e+01, 'ffn_w13': 1.143030e+00, 'ffn_w2': 2.502663e+00, 'ab_w_in': 2.202030e+00, 'a_conv_w': 1.205478e+01, 'a_conv_b': 1.350339e+02, 'a_gate_w': 6.857567e+00, 'a_gate_b': 3.430954e+00, 'a_lam': 4.253011e+00, 'b_conv_w': 3.987806e+00, 'b_conv_b': 8.025111e+01, 'b_norm_g': 3.488772e+01, 'b_norm_b': 5.101250e+01, 'ab_w_out': 1.750600e+01, 'c_w_in': 4.458488e+00, 'c_b_in': 2.066211e+01, 'c_norm_g': 2.607636e-01, 'c_norm_b': 3.217099e-01, 'c_w_s': 5.040194e-01, 'c_b_s': 7.846662e-01, 'c_w_out': 1.802140e+01}


def _to_microbatches(a, axis):
    t = _jnp.moveaxis(a, axis, 0)
    t = t.reshape((N_MICROBATCH, t.shape[0] // N_MICROBATCH) + t.shape[1:])
    return _jnp.moveaxis(t, 1, axis + 1)


def setup_inputs(seed: int = 0) -> dict:
    inp = _fwd_setup_inputs(seed)
    key = _jax.random.fold_in(_jax.random.key(seed), 7919)
    shape, _ = _output_shape()
    out = dict(inp)
    out["loss_target"] = _jax.random.normal(_jax.random.fold_in(key, 0), shape, _jnp.float32)
    for i, name in enumerate(TWIN_WEIGHTS):
        w = inp[name].astype(_jnp.float32)
        if MOMENT_SCALE is None:
            s = _jnp.sqrt(_jnp.mean(_jnp.square(w)) + 1e-30)
        else:
            s = MOMENT_SCALE[name]
        km, kv = _jax.random.split(_jax.random.fold_in(key, i + 1))
        out[name] = w
        out["m_" + name] = s * _jax.random.normal(km, w.shape, _jnp.float32)
        out["v_" + name] = (s * s) * _jax.random.uniform(kv, w.shape, _jnp.float32, 0.5, 1.5)
    if N_MICROBATCH > 1:
        for name, axis in PER_EXAMPLE_BATCH_AXIS.items():
            out[name] = _to_microbatches(out[name], axis)
    return {'x': out['x'], 'c': out['c'], 'ada_w': out['ada_w'], 'ada_b': out['ada_b'], 'norm_pre': out['norm_pre'], 'norm_post': out['norm_post'], 'ffn_w13': out['ffn_w13'], 'ffn_w2': out['ffn_w2'], 'ab_w_in': out['ab_w_in'], 'a_conv_w': out['a_conv_w'], 'a_conv_b': out['a_conv_b'], 'a_gate_w': out['a_gate_w'], 'a_gate_b': out['a_gate_b'], 'a_lam': out['a_lam'], 'b_conv_w': out['b_conv_w'], 'b_conv_b': out['b_conv_b'], 'b_norm_g': out['b_norm_g'], 'b_norm_b': out['b_norm_b'], 'ab_w_out': out['ab_w_out'], 'c_w_in': out['c_w_in'], 'c_b_in': out['c_b_in'], 'c_norm_g': out['c_norm_g'], 'c_norm_b': out['c_norm_b'], 'c_w_s': out['c_w_s'], 'c_b_s': out['c_b_s'], 'c_w_out': out['c_w_out'], 'loss_target': out['loss_target'], 'm_ada_w': out['m_ada_w'], 'm_ada_b': out['m_ada_b'], 'm_norm_pre': out['m_norm_pre'], 'm_norm_post': out['m_norm_post'], 'm_ffn_w13': out['m_ffn_w13'], 'm_ffn_w2': out['m_ffn_w2'], 'm_ab_w_in': out['m_ab_w_in'], 'm_a_conv_w': out['m_a_conv_w'], 'm_a_conv_b': out['m_a_conv_b'], 'm_a_gate_w': out['m_a_gate_w'], 'm_a_gate_b': out['m_a_gate_b'], 'm_a_lam': out['m_a_lam'], 'm_b_conv_w': out['m_b_conv_w'], 'm_b_conv_b': out['m_b_conv_b'], 'm_b_norm_g': out['m_b_norm_g'], 'm_b_norm_b': out['m_b_norm_b'], 'm_ab_w_out': out['m_ab_w_out'], 'm_c_w_in': out['m_c_w_in'], 'm_c_b_in': out['m_c_b_in'], 'm_c_norm_g': out['m_c_norm_g'], 'm_c_norm_b': out['m_c_norm_b'], 'm_c_w_s': out['m_c_w_s'], 'm_c_b_s': out['m_c_b_s'], 'm_c_w_out': out['m_c_w_out'], 'v_ada_w': out['v_ada_w'], 'v_ada_b': out['v_ada_b'], 'v_norm_pre': out['v_norm_pre'], 'v_norm_post': out['v_norm_post'], 'v_ffn_w13': out['v_ffn_w13'], 'v_ffn_w2': out['v_ffn_w2'], 'v_ab_w_in': out['v_ab_w_in'], 'v_a_conv_w': out['v_a_conv_w'], 'v_a_conv_b': out['v_a_conv_b'], 'v_a_gate_w': out['v_a_gate_w'], 'v_a_gate_b': out['v_a_gate_b'], 'v_a_lam': out['v_a_lam'], 'v_b_conv_w': out['v_b_conv_w'], 'v_b_conv_b': out['v_b_conv_b'], 'v_b_norm_g': out['v_b_norm_g'], 'v_b_norm_b': out['v_b_norm_b'], 'v_ab_w_out': out['v_ab_w_out'], 'v_c_w_in': out['v_c_w_in'], 'v_c_b_in': out['v_c_b_in'], 'v_c_norm_g': out['v_c_norm_g'], 'v_c_norm_b': out['v_c_norm_b'], 'v_c_w_s': out['v_c_w_s'], 'v_c_b_s': out['v_c_b_s'], 'v_c_w_out': out['v_c_w_out']}


def _loss(weights, diff, rest, loss_target):
    with _jax.named_scope("forward"):
        args = {**rest, TWIN_DIFF_INPUT: diff, **{k: w.astype(_WEIGHT_DTYPES[k]) for k, w in weights.items()}}
        y = _forward(args)
    with _jax.named_scope("loss_head"):
        err = _jnp.square(y.astype(_jnp.float32) - loss_target)
        return 0.5 * _jnp.sum(_jnp.mean(err, axis=-1)) if err.ndim else 0.5 * err


def _adamw(w, g, m, v):
    m = ADAM_B1 * m + (1.0 - ADAM_B1) * g
    v = ADAM_B2 * v + (1.0 - ADAM_B2) * _jnp.square(g)
    m_hat = m / (1.0 - ADAM_B1 ** ADAM_STEP)
    v_hat = v / (1.0 - ADAM_B2 ** ADAM_STEP)
    delta = -ADAM_LR * (m_hat / (_jnp.sqrt(v_hat) + ADAM_EPS) + ADAM_WD * w)
    return delta, m, v


def reference(x, c, ada_w, ada_b, norm_pre, norm_post, ffn_w13, ffn_w2, ab_w_in, a_conv_w, a_conv_b, a_gate_w, a_gate_b, a_lam, b_conv_w, b_conv_b, b_norm_g, b_norm_b, ab_w_out, c_w_in, c_b_in, c_norm_g, c_norm_b, c_w_s, c_b_s, c_w_out, loss_target, m_ada_w, m_ada_b, m_norm_pre, m_norm_post, m_ffn_w13, m_ffn_w2, m_ab_w_in, m_a_conv_w, m_a_conv_b, m_a_gate_w, m_a_gate_b, m_a_lam, m_b_conv_w, m_b_conv_b, m_b_norm_g, m_b_norm_b, m_ab_w_out, m_c_w_in, m_c_b_in, m_c_norm_g, m_c_norm_b, m_c_w_s, m_c_b_s, m_c_w_out, v_ada_w, v_ada_b, v_norm_pre, v_norm_post, v_ffn_w13, v_ffn_w2, v_ab_w_in, v_a_conv_w, v_a_conv_b, v_a_gate_w, v_a_gate_b, v_a_lam, v_b_conv_w, v_b_conv_b, v_b_norm_g, v_b_norm_b, v_ab_w_out, v_c_w_in, v_c_b_in, v_c_norm_g, v_c_norm_b, v_c_w_s, v_c_b_s, v_c_w_out):
    given = dict(x=x, c=c, ada_w=ada_w, ada_b=ada_b, norm_pre=norm_pre, norm_post=norm_post, ffn_w13=ffn_w13, ffn_w2=ffn_w2, ab_w_in=ab_w_in, a_conv_w=a_conv_w, a_conv_b=a_conv_b, a_gate_w=a_gate_w, a_gate_b=a_gate_b, a_lam=a_lam, b_conv_w=b_conv_w, b_conv_b=b_conv_b, b_norm_g=b_norm_g, b_norm_b=b_norm_b, ab_w_out=ab_w_out, c_w_in=c_w_in, c_b_in=c_b_in, c_norm_g=c_norm_g, c_norm_b=c_norm_b, c_w_s=c_w_s, c_b_s=c_b_s, c_w_out=c_w_out, loss_target=loss_target, m_ada_w=m_ada_w, m_ada_b=m_ada_b, m_norm_pre=m_norm_pre, m_norm_post=m_norm_post, m_ffn_w13=m_ffn_w13, m_ffn_w2=m_ffn_w2, m_ab_w_in=m_ab_w_in, m_a_conv_w=m_a_conv_w, m_a_conv_b=m_a_conv_b, m_a_gate_w=m_a_gate_w, m_a_gate_b=m_a_gate_b, m_a_lam=m_a_lam, m_b_conv_w=m_b_conv_w, m_b_conv_b=m_b_conv_b, m_b_norm_g=m_b_norm_g, m_b_norm_b=m_b_norm_b, m_ab_w_out=m_ab_w_out, m_c_w_in=m_c_w_in, m_c_b_in=m_c_b_in, m_c_norm_g=m_c_norm_g, m_c_norm_b=m_c_norm_b, m_c_w_s=m_c_w_s, m_c_b_s=m_c_b_s, m_c_w_out=m_c_w_out, v_ada_w=v_ada_w, v_ada_b=v_ada_b, v_norm_pre=v_norm_pre, v_norm_post=v_norm_post, v_ffn_w13=v_ffn_w13, v_ffn_w2=v_ffn_w2, v_ab_w_in=v_ab_w_in, v_a_conv_w=v_a_conv_w, v_a_conv_b=v_a_conv_b, v_a_gate_w=v_a_gate_w, v_a_gate_b=v_a_gate_b, v_a_lam=v_a_lam, v_b_conv_w=v_b_conv_w, v_b_conv_b=v_b_conv_b, v_b_norm_g=v_b_norm_g, v_b_norm_b=v_b_norm_b, v_ab_w_out=v_ab_w_out, v_c_w_in=v_c_w_in, v_c_b_in=v_c_b_in, v_c_norm_g=v_c_norm_g, v_c_norm_b=v_c_norm_b, v_c_w_s=v_c_w_s, v_c_b_s=v_c_b_s, v_c_w_out=v_c_w_out)
    weights = {n: given[n] for n in TWIN_WEIGHTS}
    shared = {n: given[n] for n in SHARED_INPUTS}
    per_example = {n: given[n] for n in ['x', 'c']}
    grad_fn = _jax.value_and_grad(_loss, argnums=(0, 1))

    def one_microbatch(ex, loss_target):
        ex = dict(ex)
        diff = ex.pop(TWIN_DIFF_INPUT)
        return grad_fn(weights, diff, {**shared, **ex}, loss_target)

    if N_MICROBATCH == 1:
        loss, (grad_w, grad_x) = one_microbatch(per_example, given["loss_target"])
    else:
        def body(carry, xs):
            loss_sum, grad_sum = carry
            l_k, (gw_k, gx_k) = one_microbatch(xs[0], xs[1])
            with _jax.named_scope("update"):
                return (loss_sum + l_k, _jax.tree.map(_jnp.add, grad_sum, gw_k)), gx_k

        init = (_jnp.zeros((), _jnp.float32), _jax.tree.map(_jnp.zeros_like, weights))
        (loss, grad_w), grad_x = _jax.lax.scan(body, init, (per_example, given["loss_target"]))
    with _jax.named_scope("update"):
        delta_w, new_m, new_v = {}, {}, {}
        for n in TWIN_WEIGHTS:
            delta_w[n], new_m[n], new_v[n] = _adamw(weights[n], grad_w[n], given["m_" + n], given["v_" + n])
    return (loss, grad_x, *[grad_w[n] for n in TWIN_WEIGHTS], *[delta_w[n] for n in TWIN_WEIGHTS],
            *[new_m[n] for n in TWIN_WEIGHTS], *[new_v[n] for n in TWIN_WEIGHTS])
```

```python
import functools
import math

import jax
import jax.numpy as jnp
from jax import lax
from jax.experimental import pallas as pl
from jax.experimental.pallas import tpu as pltpu

F32 = jnp.float32
MXU_DT = jnp.bfloat16
EPS = 1e-6
LRU_C = 8.0
N_SHARD = 4
N_DEV = 8
CHUNK = 128
N_HEAD = 8
ADAM_LR, ADAM_B1, ADAM_B2, ADAM_EPS, ADAM_WD, ADAM_STEP = 0.001, 0.9, 0.999, 1e-08, 0.01, 10
GELU_K0 = math.sqrt(2.0 / math.pi)
GELU_K1 = 0.044715
VMEM_LIMIT = 56 * 1024 * 1024
MESH = pl.DeviceIdType.MESH
ANY = pl.BlockSpec(memory_space=pl.ANY)


def _cp(sem=None, **kw):
    if sem is not None:
        kw["dimension_semantics"] = sem
    return pltpu.CompilerParams(vmem_limit_bytes=VMEM_LIMIT, **kw)


def _dot(a, b):
    return jnp.dot(a.astype(MXU_DT), b.astype(MXU_DT), preferred_element_type=F32)


def _dot_nt(a, b):
    return lax.dot_general(a.astype(MXU_DT), b.astype(MXU_DT), (((1,), (1,)), ((), ())), preferred_element_type=F32)


def _dot_tn(a, b):
    return lax.dot_general(a.astype(MXU_DT), b.astype(MXU_DT), (((0,), (0,)), ((), ())), preferred_element_type=F32)


def _dot_hi(a, b):
    return jnp.dot(a, b, precision=lax.Precision.HIGHEST, preferred_element_type=F32)


def _sig(x):
    return 1.0 / (1.0 + jnp.exp(-x))


def _logsig(x):
    return jnp.minimum(x, 0.0) - jnp.log(1.0 + jnp.exp(-jnp.abs(x)))


def _gelu(x):
    x2 = x * x
    t = jnp.tanh(GELU_K0 * (x + GELU_K1 * x * x2))
    val = 0.5 * x * (1.0 + t)
    der = 0.5 * (1.0 + t) + 0.5 * x * (1.0 - t * t) * (GELU_K0 * (1.0 + 3.0 * GELU_K1 * x2))
    return val, der


def _neg_expm1(x):
    small = -(x * (1.0 + x * (0.5 + x * (1.0 / 6.0 + x * (1.0 / 24.0)))))
    return jnp.where(x > -0.05, small, 1.0 - jnp.exp(x))


def _colsum(v):
    return jnp.sum(v, axis=0, keepdims=True)


def _rowmean(v):
    return jnp.mean(v, axis=-1, keepdims=True)


def _copy_out(pairs, sem):
    for src, dst in pairs:
        cp = pltpu.make_async_copy(src, dst, sem)
        cp.start()
        cp.wait()


def _shift_down(v, k):
    return v if k == 0 else pltpu.roll(v, k, 0)


def _shift_up(v, k):
    return v if k == 0 else pltpu.roll(v, v.shape[0] - k, 0)


def _pre_fwd(x, vp, name, tm=512):
    S, D = x.shape

    def body(x_ref, vp_ref, h_ref):
        xv = x_ref[...]
        r = lax.rsqrt(_rowmean(xv * xv) + EPS)
        a = vp_ref[0:1, :] * (1.0 + vp_ref[2:3, :])
        h_ref[...] = (xv * r * a + vp_ref[1:2, :]).astype(h_ref.dtype)

    return pl.pallas_call(
        body, name=name, grid=(S // tm,),
        in_specs=[pl.BlockSpec((tm, D), lambda t: (t, 0)), pl.BlockSpec((8, D), lambda t: (0, 0))],
        out_specs=pl.BlockSpec((tm, D), lambda t: (t, 0)),
        out_shape=jax.ShapeDtypeStruct((S, D), MXU_DT), compiler_params=_cp(("arbitrary",)),
    )(x, vp)


def _post_fwd(x, f, vp, res_w, name, tm=512):
    S, D = x.shape

    def body(x_ref, f_ref, vp_ref, o_ref):
        fv = f_ref[...]
        r = lax.rsqrt(_rowmean(fv * fv) + EPS)
        g = res_w * (1.0 + vp_ref[3:4, :]) * vp_ref[4:5, :]
        o_ref[...] = x_ref[...] + fv * r * g

    return pl.pallas_call(
        body, name=name, grid=(S // tm,),
        in_specs=[pl.BlockSpec((tm, D), lambda t: (t, 0)), pl.BlockSpec((tm, D), lambda t: (t, 0)),
                  pl.BlockSpec((8, D), lambda t: (0, 0))],
        out_specs=pl.BlockSpec((tm, D), lambda t: (t, 0)),
        out_shape=jax.ShapeDtypeStruct((S, D), F32), compiler_params=_cp(("arbitrary",)),
    )(x, f, vp)


def _post_bwd(f, dout, vp, res_w, name, tm=512):
    S, D = f.shape

    def body(f_ref, do_ref, vp_ref, df_ref, s_ref):
        @pl.when(pl.program_id(0) == 0)
        def _():
            s_ref[...] = jnp.zeros_like(s_ref)

        fv, dov = f_ref[...], do_ref[...]
        r = lax.rsqrt(_rowmean(fv * fv) + EPS)
        fn = fv * r
        pg = vp_ref[4:5, :]
        dy = (res_w * (1.0 + vp_ref[3:4, :])) * dov
        s_ref[0:1, :] += _colsum(res_w * fn * pg * dov)
        s_ref[1:2, :] += _colsum(fn * dy)
        q = dy * pg
        df_ref[...] = (r * (q - fn * _rowmean(fn * q))).astype(df_ref.dtype)

    return pl.pallas_call(
        body, name=name, grid=(S // tm,),
        in_specs=[pl.BlockSpec((tm, D), lambda t: (t, 0)), pl.BlockSpec((tm, D), lambda t: (t, 0)),
                  pl.BlockSpec((8, D), lambda t: (0, 0))],
        out_specs=[pl.BlockSpec((tm, D), lambda t: (t, 0)), pl.BlockSpec((8, D), lambda t: (0, 0))],
        out_shape=[jax.ShapeDtypeStruct((S, D), MXU_DT), jax.ShapeDtypeStruct((8, D), F32)],
        compiler_params=_cp(("arbitrary",)),
    )(f, dout, vp)


def _pre_bwd(x, dhp, dout, vp, name, tm=512):
    S, D = x.shape
    P = dhp.shape[0]

    def body(x_ref, dh_ref, do_ref, vp_ref, dx_ref, s_ref):
        @pl.when(pl.program_id(0) == 0)
        def _():
            s_ref[...] = jnp.zeros_like(s_ref)

        xv = x_ref[...]
        dh = dh_ref[0]
        for p in range(1, P):
            dh = dh + dh_ref[p]
        r = lax.rsqrt(_rowmean(xv * xv) + EPS)
        xn = xv * r
        pg = vp_ref[0:1, :]
        sc1 = 1.0 + vp_ref[2:3, :]
        s_ref[0:1, :] += _colsum(dh)
        s_ref[1:2, :] += _colsum(xn * pg * dh)
        s_ref[2:3, :] += _colsum(xn * dh * sc1)
        q = dh * (sc1 * pg)
        dx_ref[...] = do_ref[...] + r * (q - xn * _rowmean(xn * q))

    return pl.pallas_call(
        body, name=name, grid=(S // tm,),
        in_specs=[pl.BlockSpec((tm, D), lambda t: (t, 0)), pl.BlockSpec((P, tm, D), lambda t: (0, t, 0)),
                  pl.BlockSpec((tm, D), lambda t: (t, 0)), pl.BlockSpec((8, D), lambda t: (0, 0))],
        out_specs=[pl.BlockSpec((tm, D), lambda t: (t, 0)), pl.BlockSpec((8, D), lambda t: (0, 0))],
        out_shape=[jax.ShapeDtypeStruct((S, D), F32), jax.ShapeDtypeStruct((8, D), F32)],
        compiler_params=_cp(("arbitrary",)),
    )(x, dhp, dout, vp)


def _loss_head(y, tgt, name, tm=512):
    S, D = y.shape

    def body(y_ref, t_ref, dy_ref, l_ref):
        @pl.when(pl.program_id(0) == 0)
        def _():
            l_ref[...] = jnp.zeros_like(l_ref)

        e = y_ref[...] - t_ref[...]
        dy_ref[...] = e * (1.0 / D)
        l_ref[0:1, :] += _colsum(e * e) * (0.5 / D)

    return pl.pallas_call(
        body, name=name, grid=(S // tm,),
        in_specs=[pl.BlockSpec((tm, D), lambda t: (t, 0)), pl.BlockSpec((tm, D), lambda t: (t, 0))],
        out_specs=[pl.BlockSpec((tm, D), lambda t: (t, 0)), pl.BlockSpec((8, D), lambda t: (0, 0))],
        out_shape=[jax.ShapeDtypeStruct((S, D), F32), jax.ShapeDtypeStruct((8, D), F32)],
        compiler_params=_cp(("arbitrary",)),
    )(y, tgt)


def _ffn_fwd(h, w13g, w2v, g, name, tm=512):
    S, D = h.shape
    Fh = w13g.shape[-1]

    def body(h_ref, w1_ref, w3_ref, w2_ref, f_ref):
        @pl.when(pl.program_id(1) == 0)
        def _():
            f_ref[...] = jnp.zeros_like(f_ref)

        hv = h_ref[...]
        gg = _dot(hv, w1_ref[...])
        uu = _dot(hv, w3_ref[...])
        a = gg * _sig(gg) * uu
        f_ref[...] += _dot(a, w2_ref[...])

    return pl.pallas_call(
        body, name=name, grid=(S // tm, 2),
        in_specs=[pl.BlockSpec((tm, D), lambda t, j: (t, 0)),
                  pl.BlockSpec((None, None, D, Fh), lambda t, j: (g, j, 0, 0)),
                  pl.BlockSpec((None, None, D, Fh), lambda t, j: (g, 2 + j, 0, 0)),
                  pl.BlockSpec((None, None, Fh, D), lambda t, j: (g, j, 0, 0))],
        out_specs=pl.BlockSpec((tm, D), lambda t, j: (t, 0)),
        out_shape=jax.ShapeDtypeStruct((S, D), F32), compiler_params=_cp(("arbitrary", "arbitrary")),
    )(h, w13g, w13g, w2v)


def _ffn_bwd(h, df, w13g, w2v, dw13, dw2v, g, name, tm=256):
    S, D = h.shape
    Fh = w13g.shape[-1]
    T = S // tm

    def body(h_ref, df_ref, w1_ref, w3_ref, w2_ref, dw13_in, dw2_in, dh_ref, dw13_ref, dw2_ref, a1, a3, a2, sem):
        j, t = pl.program_id(0), pl.program_id(1)

        @pl.when(t == 0)
        def _():
            a1[...] = jnp.zeros_like(a1)
            a3[...] = jnp.zeros_like(a3)
            a2[...] = jnp.zeros_like(a2)

        hv, dfv = h_ref[...], df_ref[...]
        gg = _dot(hv, w1_ref[...])
        uu = _dot(hv, w3_ref[...])
        sg = _sig(gg)
        si = gg * sg
        da = _dot_nt(dfv, w2_ref[...])
        a2[...] += _dot_tn(si * uu, dfv)
        dg = da * uu * (sg * (1.0 + gg * (1.0 - sg)))
        du = da * si
        a1[...] += _dot_tn(hv, dg)
        a3[...] += _dot_tn(hv, du)
        dh_ref[...] = _dot_nt(dg, w1_ref[...]) + _dot_nt(du, w3_ref[...])

        @pl.when(t == T - 1)
        def _():
            _copy_out(((a1, dw13_ref.at[g, j]), (a3, dw13_ref.at[g, 2 + j]), (a2, dw2_ref.at[g, j])), sem)

    return pl.pallas_call(
        body, name=name, grid=(2, T),
        in_specs=[pl.BlockSpec((tm, D), lambda j, t: (t, 0)), pl.BlockSpec((tm, D), lambda j, t: (t, 0)),
                  pl.BlockSpec((None, None, D, Fh), lambda j, t: (g, j, 0, 0)),
                  pl.BlockSpec((None, None, D, Fh), lambda j, t: (g, 2 + j, 0, 0)),
                  pl.BlockSpec((None, None, Fh, D), lambda j, t: (g, j, 0, 0)), ANY, ANY],
        out_specs=[pl.BlockSpec((None, tm, D), lambda j, t: (j, t, 0)), ANY, ANY],
        out_shape=[jax.ShapeDtypeStruct((2, S, D), F32), jax.ShapeDtypeStruct(dw13.shape, F32),
                   jax.ShapeDtypeStruct(dw2v.shape, F32)],
        scratch_shapes=[pltpu.VMEM((D, Fh), F32), pltpu.VMEM((D, Fh), F32), pltpu.VMEM((Fh, D), F32),
                        pltpu.SemaphoreType.DMA],
        input_output_aliases={5: 1, 6: 2}, compiler_params=_cp(("arbitrary", "arbitrary")),
    )(h, df, w13g, w13g, w2v, dw13, dw2v)


def _scan_fwd(a, u, rows):
    n = a.shape[0]
    d = 1
    while d < n:
        m = rows >= d
        u = u + a * jnp.where(m, _shift_down(u, d), 0.0)
        a = a * jnp.where(m, _shift_down(a, d), 1.0)
        d *= 2
    return a, u


def _scan_bwd(a, u, rows):
    n = a.shape[0]
    d = 1
    while d < n:
        m = rows < n - d
        u = u + a * jnp.where(m, _shift_up(u, d), 0.0)
        a = a * jnp.where(m, _shift_up(a, d), 1.0)
        d *= 2
    return a, u


def _causal_conv(ext, w_ref, K, halo, tm):
    acc = None
    for k in range(K):
        term = w_ref[k:k + 1, :] * _shift_down(ext, K - 1 - k)[halo:, :]
        acc = term if acc is None else acc + term
    return acc


def _anticausal_conv(ext, w_ref, K, tm):
    acc = None
    for k in range(K):
        term = w_ref[k:k + 1, :] * _shift_up(ext, K - 1 - k)[:tm, :]
        acc = term if acc is None else acc + term
    return acc


def _group_norm(vc, pavg, g, b):
    mu = _dot_hi(vc, pavg)
    dv = vc - mu
    rstd = lax.rsqrt(_dot_hi(dv * dv, pavg) + EPS)
    vhat = dv * rstd
    return vhat, rstd, vhat * g + b


def _lru_gates(axc, wg_ref, bg_ref, lam, W):
    gp = _dot(axc, wg_ref[...]) + bg_ref[0:1, :]
    r = _sig(gp[:, :W])
    i = _sig(gp[:, W:])
    ls = _logsig(lam)
    L = (LRU_C * ls) * r
    a = jnp.exp(L)
    mult = jnp.sqrt(_neg_expm1(2.0 * L))
    return r, i, ls, a, mult


def _mix_ab_fwd(h, win4, cwa, wg, bg, cwb, v512, pavg, wout, name, tm=256):
    S, D = h.shape
    W = win4.shape[-1]
    KA, KB, HA, HB = 4, 31, 8, 32

    def body(h_ref, win_ref, cwa_ref, wg_ref, bg_ref, cwb_ref, v_ref, p_ref, wo_ref,
             f_ref, hs_ref, axp_ref, axc_ref, bv_ref, vc_ref, ahalo, bhalo, carry):
        @pl.when(pl.program_id(0) == 0)
        def _():
            ahalo[...] = jnp.zeros_like(ahalo)
            bhalo[...] = jnp.zeros_like(bhalo)
            carry[...] = jnp.zeros_like(carry)

        hv = h_ref[...]
        a_gate = _dot(hv, win_ref[0])
        axp = _dot(hv, win_ref[1])
        b_val = _dot(hv, win_ref[2])
        b_gate = _dot(hv, win_ref[3])
        rows = lax.broadcasted_iota(jnp.int32, (tm, W), 0)
        axc = _causal_conv(jnp.concatenate([ahalo[...], axp], axis=0), cwa_ref, KA, HA, tm) + v_ref[0:1, :]
        ahalo[...] = axp[tm - HA:, :]
        r, i, ls, a, mult = _lru_gates(axc, wg_ref, bg_ref, v_ref[1:2, :], W)
        acum, hloc = _scan_fwd(a, mult * i * axc, rows)
        hs = hloc + acum * carry[7:8, :]
        carry[...] = hs[tm - 8:, :]
        ya = hs * _gelu(a_gate)[0]
        bv = b_val * _sig(b_gate)
        vc = _causal_conv(jnp.concatenate([bhalo[...], bv], axis=0), cwb_ref, KB, HB, tm) + v_ref[2:3, :]
        bhalo[...] = bv[tm - HB:, :]
        _, _, vn = _group_norm(vc, p_ref[...], v_ref[3:4, :], v_ref[4:5, :])
        yb = vn * _sig(vn)
        f_ref[...] = _dot(ya, wo_ref[0:W, :]) + _dot(yb, wo_ref[W:, :])
        hs_ref[...] = hs
        axp_ref[...] = axp
        axc_ref[...] = axc
        bv_ref[...] = bv
        vc_ref[...] = vc

    full = lambda a: pl.BlockSpec(a.shape, lambda t: (0,) * a.ndim)
    tile = lambda w: pl.BlockSpec((tm, w), lambda t: (t, 0))
    sav = jax.ShapeDtypeStruct((S, W), F32)
    return pl.pallas_call(
        body, name=name, grid=(S // tm,),
        in_specs=[tile(D)] + [full(a) for a in (win4, cwa, wg, bg, cwb, v512, pavg, wout)],
        out_specs=[tile(D)] + [tile(W)] * 5,
        out_shape=[jax.ShapeDtypeStruct((S, D), F32)] + [sav] * 5,
        scratch_shapes=[pltpu.VMEM((HA, W), F32), pltpu.VMEM((HB, W), F32), pltpu.VMEM((8, W), F32)],
        compiler_params=_cp(("arbitrary",)),
    )(h, win4, cwa, wg, bg, cwb, v512, pavg, wout)


def _mix_ab_bwd(h, df, hs, axp, axc, bv, vc, win4, cwa, wg, bg, cwb, v512, pavg, wout, name, tm=256):
    S, D = h.shape
    W = win4.shape[-1]
    T = S // tm
    KA, KB, HA, HB = 4, 31, 8, 32

    def body(h_ref, df_ref, hs_ref, hsp_ref, axp_ref, axpp_ref, axc_ref, bv_ref, bvp_ref, vc_ref,
             win_ref, cwa_ref, wg_ref, bg_ref, cwb_ref, v_ref, p_ref, wo_ref,
             dh_ref, dwin_out, dwo_out, dwg_out, dcwa_ref, dcwb_ref, dbg_ref, dv_ref,
             danext, dvnext, gfirst, afirst, dwin_ref, dwo_ref, dwg_ref, sem):
        t = pl.program_id(0)

        @pl.when(t == 0)
        def _():
            for ref in (dwin_ref, dwo_ref, dwg_ref, dcwa_ref, dcwb_ref, dbg_ref, dv_ref, danext, dvnext, gfirst, afirst):
                ref[...] = jnp.zeros_like(ref)

        notfirst = jnp.where(t < T - 1, 1.0, 0.0).astype(F32)
        hv, dfv = h_ref[...], df_ref[...]
        a_gate = _dot(hv, win_ref[0])
        b_val = _dot(hv, win_ref[2])
        b_gate = _dot(hv, win_ref[3])
        rows = lax.broadcasted_iota(jnp.int32, (tm, W), 0)
        ge, dge = _gelu(a_gate)
        hsv = hs_ref[...]
        ya = hsv * ge
        vhat, rstd, vn = _group_norm(vc_ref[...], p_ref[...], v_ref[3:4, :], v_ref[4:5, :])
        sgn = _sig(vn)
        yb = vn * sgn
        dma = _dot_nt(dfv, wo_ref[0:W, :])
        dmb = _dot_nt(dfv, wo_ref[W:, :])
        dwo_ref[0:W, :] += _dot_tn(ya, dfv)
        dwo_ref[W:, :] += _dot_tn(yb, dfv)
        dhs = dma * ge
        d_a_gate = dma * hsv * dge
        axcv = axc_ref[...]
        lam = v_ref[1:2, :]
        r, i, ls, a, mult = _lru_gates(axcv, wg_ref, bg_ref, lam, W)
        ash = jnp.where(rows == tm - 1, afirst[0:1, :], _shift_up(a, 1))
        asuf, gloc = _scan_bwd(ash, dhs, rows)
        gsc = gloc + asuf * gfirst[0:1, :]
        afirst[...] = a[0:8, :]
        gfirst[...] = gsc[0:8, :]
        hprev = jnp.where(rows == 0, hsp_ref[HA - 1:HA, :] * notfirst, _shift_down(hsv, 1))
        da = gsc * hprev
        dL = da * a - gsc * (i * axcv) * (a * a) / mult
        dix = gsc * mult
        daxc = dix * i
        dr = dL * (LRU_C * ls)
        dv_ref[1:2, :] += _colsum(dL * r) * (LRU_C * _sig(-lam))
        dgate = jnp.concatenate([dr * r * (1.0 - r), (dix * axcv) * i * (1.0 - i)], axis=1)
        dbg_ref[0:1, :] += _colsum(dgate)
        dwg_ref[...] += _dot_tn(axcv, dgate)
        daxc = daxc + _dot_nt(dgate, wg_ref[...])
        daxp = _anticausal_conv(jnp.concatenate([daxc, danext[...]], axis=0), cwa_ref, KA, tm)
        ext = jnp.concatenate([axpp_ref[...] * notfirst, axp_ref[...]], axis=0)
        for k in range(KA):
            dcwa_ref[k:k + 1, :] += _colsum(daxc * _shift_down(ext, KA - 1 - k)[HA:, :])
        dv_ref[0:1, :] += _colsum(daxc)
        danext[...] = daxc[0:HA, :]
        dvn = dmb * (sgn * (1.0 + vn * (1.0 - sgn)))
        dv_ref[4:5, :] += _colsum(dvn)
        dv_ref[3:4, :] += _colsum(dvn * vhat)
        dvh = dvn * v_ref[3:4, :]
        dvc = rstd * (dvh - _dot_hi(dvh, p_ref[...]) - vhat * _dot_hi(dvh * vhat, p_ref[...]))
        dbv = _anticausal_conv(jnp.concatenate([dvc, dvnext[...]], axis=0), cwb_ref, KB, tm)
        ext = jnp.concatenate([bvp_ref[...] * notfirst, bv_ref[...]], axis=0)
        for k in range(KB):
            dcwb_ref[k:k + 1, :] += _colsum(dvc * _shift_down(ext, KB - 1 - k)[HB:, :])
        dv_ref[2:3, :] += _colsum(dvc)
        dvnext[...] = dvc[0:HB, :]
        sb = _sig(b_gate)
        dzs = (d_a_gate, daxp, dbv * sb, dbv * b_val * sb * (1.0 - sb))
        dh = None
        for s in range(4):
            part = _dot_nt(dzs[s], win_ref[s])
            dh = part if dh is None else dh + part
            dwin_ref[s] += _dot_tn(hv, dzs[s])
        dh_ref[...] = dh

        @pl.when(t == T - 1)
        def _():
            _copy_out(((dwin_ref, dwin_out), (dwo_ref, dwo_out), (dwg_ref, dwg_out)), sem)

    full = lambda a: pl.BlockSpec(a.shape, lambda t: (0,) * a.ndim)
    fullo = lambda shp: pl.BlockSpec(shp, lambda t: (0,) * len(shp))
    tile = lambda w: pl.BlockSpec((tm, w), lambda t: (T - 1 - t, 0))
    prev = lambda hh: pl.BlockSpec((hh, W), lambda t: (jnp.maximum((T - 1 - t) * (tm // hh) - 1, 0), 0))
    out_shapes = [(1, S, D), win4.shape, wout.shape, wg.shape, (8, W), (32, W), (8, 2 * W), (8, W)]
    return pl.pallas_call(
        body, name=name, grid=(T,),
        in_specs=[tile(D), tile(D), tile(W), prev(HA), tile(W), prev(HA), tile(W), tile(W), prev(HB), tile(W)]
        + [full(a) for a in (win4, cwa, wg, bg, cwb, v512, pavg, wout)],
        out_specs=[pl.BlockSpec((None, tm, D), lambda t: (0, T - 1 - t, 0)), ANY, ANY, ANY]
        + [fullo(s) for s in out_shapes[4:]],
        out_shape=[jax.ShapeDtypeStruct(s, F32) for s in out_shapes],
        scratch_shapes=[pltpu.VMEM((HA, W), F32), pltpu.VMEM((HB, W), F32), pltpu.VMEM((8, W), F32),
                        pltpu.VMEM((8, W), F32), pltpu.VMEM(win4.shape, F32), pltpu.VMEM(wout.shape, F32),
                        pltpu.VMEM(wg.shape, F32), pltpu.SemaphoreType.DMA],
        compiler_params=_cp(("arbitrary",)),
    )(h, df, hs, hs, axp, axp, axc, bv, bv, vc, win4, cwa, wg, bg, cwb, v512, pavg, wout)


def _mix_c_core(hv, win_ref, v2_ref, v1_ref, ws_ref, bsb_ref, tm, D):
    zp = jnp.concatenate([_dot(hv, win_ref[s]) for s in range(4)], axis=1) + v2_ref[0:1, :]
    z, dz = _gelu(zp)
    u, v = z[:, :D], z[:, D:]
    mu = _rowmean(v)
    dv = v - mu
    rstd = lax.rsqrt(_rowmean(dv * dv) + EPS)
    vhat = dv * rstd
    vn = vhat * v1_ref[0:1, :] + v1_ref[1:2, :]
    rows_out = []
    for cidx in range(tm // CHUNK):
        blk = vn[cidx * CHUNK:(cidx + 1) * CHUNK, :]
        heads = [_dot(ws_ref[hd], blk[:, hd * CHUNK:(hd + 1) * CHUNK]) for hd in range(N_HEAD)]
        rows_out.append(jnp.concatenate(heads, axis=1) + bsb_ref[...])
    mixed = jnp.concatenate(rows_out, axis=0)
    return dz, u, rstd, vhat, vn, mixed


def _mix_c_fwd(h, win4, v2d, v1d, ws, bsb, wout, name, tm=256):
    S, D = h.shape

    def body(h_ref, win_ref, v2_ref, v1_ref, ws_ref, bsb_ref, wo_ref, f_ref):
        _, u, _, _, _, mixed = _mix_c_core(h_ref[...], win_ref, v2_ref, v1_ref, ws_ref, bsb_ref, tm, D)
        f_ref[...] = _dot(u * mixed, wo_ref[...])

    full = lambda a: pl.BlockSpec(a.shape, lambda t: (0,) * a.ndim)
    return pl.pallas_call(
        body, name=name, grid=(S // tm,),
        in_specs=[pl.BlockSpec((tm, D), lambda t: (t, 0))] + [full(a) for a in (win4, v2d, v1d, ws, bsb, wout)],
        out_specs=pl.BlockSpec((tm, D), lambda t: (t, 0)),
        out_shape=jax.ShapeDtypeStruct((S, D), F32), compiler_params=_cp(("arbitrary",)),
    )(h, win4, v2d, v1d, ws, bsb, wout)


def _mix_c_bwd(h, df, win4, v2d, v1d, ws, bsb, wout, name, tm=256):
    S, D = h.shape

    def body(h_ref, df_ref, win_ref, v2_ref, v1_ref, ws_ref, bsb_ref, wo_ref,
             dh_ref, dwin_out, dwo_out, dws_ref, dbsb_ref, dv2_ref, dv1_ref, dwin_ref, dwo_ref, sem):
        @pl.when(pl.program_id(0) == 0)
        def _():
            for ref in (dwin_ref, dwo_ref, dws_ref, dbsb_ref, dv2_ref, dv1_ref):
                ref[...] = jnp.zeros_like(ref)

        hv, dfv = h_ref[...], df_ref[...]
        dz, u, rstd, vhat, vn, mixed = _mix_c_core(hv, win_ref, v2_ref, v1_ref, ws_ref, bsb_ref, tm, D)
        dp = _dot_nt(dfv, wo_ref[...])
        dwo_ref[...] += _dot_tn(u * mixed, dfv)
        du = dp * mixed
        dmx = dp * u
        rows_out = []
        for cidx in range(tm // CHUNK):
            dblk = dmx[cidx * CHUNK:(cidx + 1) * CHUNK, :]
            vblk = vn[cidx * CHUNK:(cidx + 1) * CHUNK, :]
            dbsb_ref[...] += dblk
            heads = []
            for hd in range(N_HEAD):
                dsl = dblk[:, hd * CHUNK:(hd + 1) * CHUNK]
                heads.append(_dot_tn(ws_ref[hd], dsl))
                dws_ref[hd] += _dot_nt(dsl, vblk[:, hd * CHUNK:(hd + 1) * CHUNK])
            rows_out.append(jnp.concatenate(heads, axis=1))
        dvn = jnp.concatenate(rows_out, axis=0)
        dv1_ref[1:2, :] += _colsum(dvn)
        dv1_ref[0:1, :] += _colsum(dvn * vhat)
        dvh = dvn * v1_ref[0:1, :]
        dv = rstd * (dvh - _rowmean(dvh) - vhat * _rowmean(dvh * vhat))
        dzp = jnp.concatenate([du, dv], axis=1) * dz
        dv2_ref[0:1, :] += _colsum(dzp)
        W = win_ref.shape[-1]
        dh = None
        for s in range(4):
            dzs = dzp[:, s * W:(s + 1) * W]
            part = _dot_nt(dzs, win_ref[s])
            dh = part if dh is None else dh + part
            dwin_ref[s] += _dot_tn(hv, dzs)
        dh_ref[...] = dh

        @pl.when(pl.program_id(0) == S // tm - 1)
        def _():
            _copy_out(((dwin_ref, dwin_out), (dwo_ref, dwo_out)), sem)

    full = lambda a: pl.BlockSpec(a.shape, lambda t: (0,) * a.ndim)
    fullo = lambda shp: pl.BlockSpec(shp, lambda t: (0,) * len(shp))
    out_shapes = [(1, S, D), win4.shape, wout.shape, ws.shape, bsb.shape, (8, 2 * D), (8, D)]
    return pl.pallas_call(
        body, name=name, grid=(S // tm,),
        in_specs=[pl.BlockSpec((tm, D), lambda t: (t, 0)), pl.BlockSpec((tm, D), lambda t: (t, 0))]
        + [full(a) for a in (win4, v2d, v1d, ws, bsb, wout)],
        out_specs=[pl.BlockSpec((None, tm, D), lambda t: (0, t, 0)), ANY, ANY] + [fullo(s) for s in out_shapes[3:]],
        out_shape=[jax.ShapeDtypeStruct(s, F32) for s in out_shapes],
        scratch_shapes=[pltpu.VMEM(win4.shape, F32), pltpu.VMEM(wout.shape, F32), pltpu.SemaphoreType.DMA],
        compiler_params=_cp(("arbitrary",)),
    )(h, df, win4, v2d, v1d, ws, bsb, wout)


def _ada_fwd(c_all, ada_w, ada_b_my, name):
    L, D, N = ada_w.shape
    tn = N // 3

    def body(c_ref, w_ref, b_ref, o_ref):
        cv = c_ref[...]
        o_ref[...] = _dot_hi(cv * _sig(cv), w_ref[...]) + b_ref[...]

    return pl.pallas_call(
        body, name=name, grid=(L, 3),
        in_specs=[pl.BlockSpec((8, D), lambda l, n: (0, 0)), pl.BlockSpec((None, D, tn), lambda l, n: (l, 0, n)),
                  pl.BlockSpec((None, 1, tn), lambda l, n: (l, 0, n))],
        out_specs=pl.BlockSpec((None, 8, tn), lambda l, n: (l, 0, n)),
        out_shape=jax.ShapeDtypeStruct((L, 8, N), F32), compiler_params=_cp(("arbitrary", "arbitrary")),
    )(c_all, ada_w, ada_b_my)


def _ada_bwd(c_all, dmod_my, name):
    L, _, N = dmod_my.shape
    D = c_all.shape[1]
    tn = N // 3

    def body(c_ref, d_ref, o_ref):
        cv = c_ref[...]
        o_ref[...] = lax.dot_general(cv * _sig(cv), d_ref[...], (((0,), (0,)), ((), ())),
                                     precision=lax.Precision.HIGHEST, preferred_element_type=F32)

    return pl.pallas_call(
        body, name=name, grid=(L, 3),
        in_specs=[pl.BlockSpec((8, D), lambda l, n: (0, 0)), pl.BlockSpec((None, 8, tn), lambda l, n: (l, 0, n))],
        out_specs=pl.BlockSpec((None, D, tn), lambda l, n: (l, 0, n)),
        out_shape=jax.ShapeDtypeStruct((L, D, N), F32), compiler_params=_cp(("arbitrary", "arbitrary")),
    )(c_all, dmod_my)


def _row_tile(rows, cols, budget=1 << 20):
    best = 8
    for rt in range(8, rows + 1, 8):
        if rows % rt == 0 and rt * cols * 4 <= budget:
            best = rt
    return best


def _add_half(gk, la, c_idx, name):
    Gk, _, R, C = gk.shape
    Rh = R // 2
    n = Gk * N_SHARD
    gv = gk.reshape(n, 2, Rh, C)
    lv = la.reshape(n, Rh, C)
    rt = _row_tile(Rh, C)

    def body(c_ref, g_ref, l_ref, o_ref):
        o_ref[...] = g_ref[...] + l_ref[...]

    out = pl.pallas_call(
        body, name=name,
        grid_spec=pltpu.PrefetchScalarGridSpec(
            num_scalar_prefetch=1, grid=(n, Rh // rt),
            in_specs=[pl.BlockSpec((None, None, rt, C), lambda i, r, c_ref: (i, c_ref[0], r, 0)),
                      pl.BlockSpec((None, rt, C), lambda i, r, c_ref: (i, r, 0))],
            out_specs=pl.BlockSpec((None, rt, C), lambda i, r, c_ref: (i, r, 0))),
        out_shape=jax.ShapeDtypeStruct((n, Rh, C), F32), compiler_params=_cp(("arbitrary", "arbitrary")),
    )(c_idx, gv, lv)
    return out.reshape(Gk, N_SHARD, Rh, C)


def _sum_slots(lb, name):
    Gk, n, Rh, C = lb.shape
    rt = _row_tile(Rh, C)

    def body(l_ref, o_ref):
        acc = l_ref[0]
        for s in range(1, n):
            acc = acc + l_ref[s]
        o_ref[...] = acc

    return pl.pallas_call(
        body, name=name, grid=(Gk, Rh // rt),
        in_specs=[pl.BlockSpec((None, n, rt, C), lambda g, r: (g, 0, r, 0))],
        out_specs=pl.BlockSpec((None, rt, C), lambda g, r: (g, r, 0)),
        out_shape=jax.ShapeDtypeStruct((Gk, Rh, C), F32), compiler_params=_cp(("arbitrary", "arbitrary")),
    )(lb)


def _adamw(w, g, m, v, name):
    rows, cols = w.shape
    rt = _row_tile(rows, cols) if rows % 8 == 0 else rows
    c1 = 1.0 - ADAM_B1 ** ADAM_STEP
    c2 = 1.0 - ADAM_B2 ** ADAM_STEP

    def body(w_ref, g_ref, m_ref, v_ref, d_ref, mo_ref, vo_ref):
        gv = g_ref[...]
        mn = ADAM_B1 * m_ref[...] + (1.0 - ADAM_B1) * gv
        vn = ADAM_B2 * v_ref[...] + (1.0 - ADAM_B2) * (gv * gv)
        d_ref[...] = -ADAM_LR * ((mn / c1) / (jnp.sqrt(vn / c2) + ADAM_EPS) + ADAM_WD * w_ref[...])
        mo_ref[...] = mn
        vo_ref[...] = vn

    spec = pl.BlockSpec((rt, cols), lambda r: (r, 0))
    sds = jax.ShapeDtypeStruct((rows, cols), F32)
    return pl.pallas_call(
        body, name=name, grid=(rows // rt,), in_specs=[spec] * 4, out_specs=[spec] * 3,
        out_shape=[sds] * 3, compiler_params=_cp(("arbitrary",)),
    )(w, g, m, v)


def _coords():
    return lax.axis_index("x"), lax.axis_index("y"), lax.axis_index("c")


def _all_gather8(blk, name):
    m_per, n = blk.shape

    def body(x_ref, out_ref, send_sems, recv_sems, local_sem):
        x, y, c = _coords()
        me, sibling = (x, y, c), (x, y, 1 - c)
        chips = [(1 - x, y), (x, 1 - y), (1 - x, 1 - y)]

        def rows(px, py, pc):
            return out_ref.at[pl.ds((4 * px + 2 * py + pc) * m_per, m_per), :]

        def copy(k, block, to, src=None):
            return pltpu.make_async_remote_copy(
                src_ref=rows(*block) if src is None else src, dst_ref=rows(*block),
                send_sem=send_sems.at[k], recv_sem=recv_sems.at[k], device_id=to, device_id_type=MESH)

        mine = pltpu.make_async_copy(x_ref, rows(*me), local_sem)
        mine.start()
        first = [copy(0, me, sibling, src=x_ref)]
        first += [copy(1 + j, me, (*chip, c), src=x_ref) for j, chip in enumerate(chips)]
        for cp in first:
            cp.start()
        passed = [copy(4 + j, (*chip, c), sibling) for j, chip in enumerate(chips)]
        for j, chip in enumerate(chips):
            copy(1 + j, (*chip, c), me).wait_recv()
            passed[j].start()
        copy(0, sibling, me).wait_recv()
        for j, chip in enumerate(chips):
            copy(4 + j, (*chip, 1 - c), me).wait_recv()
        for cp in first + passed:
            cp.wait_send()
        mine.wait()

    return pl.pallas_call(
        body, name=name, out_shape=jax.ShapeDtypeStruct((N_DEV * m_per, n), blk.dtype),
        in_specs=[pl.BlockSpec(memory_space=pltpu.VMEM)], out_specs=pl.BlockSpec(memory_space=pltpu.VMEM),
        scratch_shapes=[pltpu.SemaphoreType.DMA((7,)), pltpu.SemaphoreType.DMA((7,)), pltpu.SemaphoreType.DMA],
        compiler_params=_cp(),
    )(blk)


def _comm_call(name, inputs, out_shapes, plan, n_remote, n_local):
    n_in, n_out = len(inputs), len(out_shapes)

    def body(*refs):
        in_refs, out_refs = refs[:n_in], refs[n_in:n_in + n_out]
        send_sems, recv_sems, local_sems = refs[n_in + n_out:]

        def remote(k, src, dst, to):
            return pltpu.make_async_remote_copy(src_ref=src, dst_ref=dst, send_sem=send_sems.at[k],
                                                recv_sem=recv_sems.at[k], device_id=to, device_id_type=MESH)

        def local(k, src, dst):
            return pltpu.make_async_copy(src, dst, local_sems.at[k])

        plan(in_refs, out_refs, remote, local)

    return pl.pallas_call(
        body, name=name, out_shape=out_shapes, in_specs=[ANY] * n_in, out_specs=[ANY] * n_out,
        scratch_shapes=[pltpu.SemaphoreType.DMA((n_remote,)), pltpu.SemaphoreType.DMA((n_remote,)),
                        pltpu.SemaphoreType.DMA((max(n_local, 1),))],
        compiler_params=_cp(has_side_effects=True),
    )(*inputs)


def _gather_weights(shards):
    K = len(shards)

    def plan(ins, outs, remote, local):
        x, y, c = _coords()
        s_me = 2 * x + y
        sibling = (x, y, 1 - c)
        chips = [(1 - x, y), (x, 1 - y), (1 - x, 1 - y)]
        half = lambda k, s, cc: outs[k].at[:, s, pl.ds(cc * (shards[k].shape[1] // 2), shards[k].shape[1] // 2), :]
        mine = [local(k, ins[k], outs[k].at[:, s_me]) for k in range(K)]
        for cp in mine:
            cp.start()
        sent = []
        for j, (px, py) in enumerate(chips):
            for k in range(K):
                rh = shards[k].shape[1] // 2
                cp = remote(j * K + k, ins[k].at[:, pl.ds(c * rh, rh), :], half(k, s_me, c), (px, py, c))
                cp.start()
                sent.append(cp)
        for j, (px, py) in enumerate(chips):
            s_from = 2 * px + py
            for k in range(K):
                landed = half(k, s_from, c)
                remote(j * K + k, landed, landed, (px, py, c)).wait_recv()
                cp = remote((3 + j) * K + k, landed, landed, sibling)
                cp.start()
                sent.append(cp)
        for j, (px, py) in enumerate(chips):
            s_from = 2 * px + py
            for k in range(K):
                other = half(k, s_from, 1 - c)
                remote((3 + j) * K + k, other, other, sibling).wait_recv()
        for cp in sent:
            cp.wait_send()
        for cp in mine:
            cp.wait()

    out_shapes = [jax.ShapeDtypeStruct((s.shape[0], N_SHARD) + s.shape[1:], s.dtype) for s in shards]
    return _comm_call("gather_weights", shards, out_shapes, plan, 6 * K, K)


def _send_other_half(grads):
    K = len(grads)

    def plan(ins, outs, remote, local):
        x, y, c = _coords()
        cps = []
        for k in range(K):
            rh = grads[k].shape[2] // 2
            cps.append(remote(k, ins[k].at[:, :, pl.ds((1 - c) * rh, rh), :], outs[k], (x, y, 1 - c)))
        for cp in cps:
            cp.start()
        for cp in cps:
            cp.wait()

    out_shapes = [jax.ShapeDtypeStruct(g.shape[:2] + (g.shape[2] // 2, g.shape[3]), g.dtype) for g in grads]
    return _comm_call("send_other_half", grads, out_shapes, plan, K, 0)


def _scatter_chips(parts):
    K = len(parts)

    def plan(ins, outs, remote, local):
        x, y, c = _coords()
        s_me = 2 * x + y
        chips = [(1 - x, y), (x, 1 - y), (1 - x, 1 - y)]
        mine = [local(k, ins[k].at[:, s_me], outs[k].at[:, s_me]) for k in range(K)]
        for cp in mine:
            cp.start()
        sent = []
        for j, (px, py) in enumerate(chips):
            for k in range(K):
                cp = remote(j * K + k, ins[k].at[:, 2 * px + py], outs[k].at[:, s_me], (px, py, c))
                cp.start()
                sent.append(cp)
        for j, (px, py) in enumerate(chips):
            for k in range(K):
                landed = outs[k].at[:, 2 * px + py]
                remote(j * K + k, landed, landed, (px, py, c)).wait_recv()
        for cp in sent:
            cp.wait_send()
        for cp in mine:
            cp.wait()

    out_shapes = [jax.ShapeDtypeStruct(p.shape, p.dtype) for p in parts]
    return _comm_call("scatter_chips", parts, out_shapes, plan, 3 * K, K)


def _join_halves(halves):
    K = len(halves)

    def plan(ins, outs, remote, local):
        x, y, c = _coords()
        mine, sent = [], []
        for k in range(K):
            rh = halves[k].shape[1]
            place = outs[k].at[:, pl.ds(c * rh, rh), :]
            mine.append(local(k, ins[k], place))
            sent.append(remote(k, ins[k], place, (x, y, 1 - c)))
        for cp in mine + sent:
            cp.start()
        for k in range(K):
            rh = halves[k].shape[1]
            other = outs[k].at[:, pl.ds((1 - c) * rh, rh), :]
            remote(k, other, other, (x, y, 1 - c)).wait_recv()
        for cp in sent:
            cp.wait_send()
        for cp in mine:
            cp.wait()

    out_shapes = [jax.ShapeDtypeStruct((h.shape[0], 2 * h.shape[1], h.shape[2]), h.dtype) for h in halves]
    return _comm_call("join_halves", halves, out_shapes, plan, K, K)


def _pack(parts):
    flat = []
    for p in parts:
        v = p.reshape(-1).astype(F32)
        pad = (-v.shape[0]) % 1024
        flat.append(jnp.pad(v, (0, pad)) if pad else v)
    return jnp.concatenate(flat).reshape(-1, 128)


def _unpack(packed, shapes):
    flat = packed.reshape(-1)
    out, off = [], 0
    for shp in shapes:
        n = math.prod(shp)
        out.append(flat[off:off + n].reshape(shp))
        off += n + (-n) % 1024
    return out


def _shard_last(a, s, n):
    return lax.dynamic_slice_in_dim(a, s * n, n, axis=a.ndim - 1)


def _rows8(*vecs):
    n = vecs[0].shape[-1]
    rows = [v.reshape(1, n).astype(F32) for v in vecs]
    return jnp.concatenate(rows + [jnp.zeros((8 - len(rows), n), F32)], axis=0)


def kernel(x, c, ada_w, ada_b, norm_pre, norm_post, ffn_w13, ffn_w2, ab_w_in, a_conv_w, a_conv_b, a_gate_w, a_gate_b, a_lam, b_conv_w, b_conv_b, b_norm_g, b_norm_b, ab_w_out, c_w_in, c_b_in, c_norm_g, c_norm_b, c_w_s, c_b_s, c_w_out, loss_target, m_ada_w, m_ada_b, m_norm_pre, m_norm_post, m_ffn_w13, m_ffn_w2, m_ab_w_in, m_a_conv_w, m_a_conv_b, m_a_gate_w, m_a_gate_b, m_a_lam, m_b_conv_w, m_b_conv_b, m_b_norm_g, m_b_norm_b, m_ab_w_out, m_c_w_in, m_c_b_in, m_c_norm_g, m_c_norm_b, m_c_w_s, m_c_b_s, m_c_w_out, v_ada_w, v_ada_b, v_norm_pre, v_norm_post, v_ffn_w13, v_ffn_w2, v_ab_w_in, v_a_conv_w, v_a_conv_b, v_a_gate_w, v_a_gate_b, v_a_lam, v_b_conv_w, v_b_conv_b, v_b_norm_g, v_b_norm_b, v_ab_w_out, v_c_w_in, v_c_b_in, v_c_norm_g, v_c_norm_b, v_c_w_s, v_c_b_s, v_c_w_out):
    S, D = x.shape[1], x.shape[2]
    W = a_lam.shape[-1]
    Fh = ffn_w13.shape[-1]
    Fq = ffn_w2.shape[2]
    xi, yi, ci = _coords()
    shard = 2 * xi + yi
    me = 4 * xi + 2 * yi + ci
    x2, tgt = x[0], loss_target[0]

    sharded_small = [norm_pre, norm_post, a_conv_w, b_conv_w, c_b_in, c_norm_g, c_norm_b]
    gathered = _all_gather8(_pack([c] + sharded_small), "gather_small")
    blocks = gathered.reshape(N_DEV, -1, 128)
    per_dev = [_unpack(blocks[d], [c.shape] + [p.shape for p in sharded_small]) for d in range(0, N_DEV, 2)]
    c_all = jnp.concatenate([_unpack(blocks[d], [c.shape])[0] for d in range(N_DEV)], axis=0)
    npre, npost, acw, bcw, cbin, cng, cnb = [jnp.concatenate([per_dev[s][1 + i] for s in range(N_SHARD)], axis=-1)
                                             for i in range(len(sharded_small))]

    ada_b_my = _shard_last(ada_b, shard, ada_w.shape[-1])[:, None, :]
    modp = _ada_fwd(c_all, ada_w, ada_b_my, "ada_fwd")
    modg = _all_gather8(modp.reshape(16, -1), "gather_mod").reshape(N_DEV, 2, 8, -1)
    mod_me = lax.dynamic_index_in_dim(modg[0::2], me, axis=2, keepdims=False)
    mod = jnp.transpose(mod_me, (1, 0, 2)).reshape(2, 3, 3, D)

    to_mxu = lambda a: a.astype(MXU_DT)
    shards = [to_mxu(ffn_w13).reshape(4, D, Fh), to_mxu(ffn_w2).reshape(4, Fq, D), to_mxu(ab_w_in), to_mxu(ab_w_out),
              to_mxu(c_w_in), to_mxu(c_w_out)]
    w13g, w2g, abin_g, about_g, cin_g, cout_g = _gather_weights(shards)
    w2v = w2g.reshape(4, 2, Fh, D)
    abin4, cin4 = abin_g[0], cin_g[0]
    about, cout = about_g.reshape(D, D), cout_g.reshape(D, D)

    eye = jnp.eye(8, dtype=F32)
    dh_a = W // 8
    blockdiag = lambda w: jnp.einsum("hde,hg->hdge", w, eye).reshape(W, W)
    gw = a_gate_w[0]
    wg = jnp.concatenate([blockdiag(gw[:, :, :dh_a]), blockdiag(gw[:, :, dh_a:])], axis=1).astype(MXU_DT)
    bgv = jnp.concatenate([a_gate_b[0][:, :dh_a].reshape(-1), a_gate_b[0][:, dh_a:].reshape(-1)])
    bg = _rows8(bgv)
    cwa = jnp.concatenate([acw[0], jnp.zeros((4, W), F32)], axis=0)
    cwb = jnp.concatenate([bcw[0], jnp.zeros((1, W), F32)], axis=0)
    v512 = _rows8(a_conv_b[0], a_lam[0], b_conv_b[0], b_norm_g[0], b_norm_b[0])
    dg_b = W // 8
    gid = jnp.arange(W) // dg_b
    pavg = (gid[:, None] == gid[None, :]).astype(F32) / dg_b
    v2d = _rows8(cbin[0])
    v1d = _rows8(cng[0], cnb[0])
    tril = jnp.tril(jnp.ones((CHUNK, CHUNK), dtype=bool))
    ws = jnp.where(tril, c_w_s[0], 0.0).astype(MXU_DT)
    bsb = jnp.repeat(jnp.transpose(c_b_s[0]), D // N_HEAD, axis=1)

    res_ws = (0.5, 1.0, 0.5)
    vps, xs, hs_, fs = [], [], [], []
    saved_ab = None
    xc = x2
    for l in range(2):
        for j in range(3):
            k = 3 * l + j
            vp = _rows8(npre[l, j], mod[l, j, 0], mod[l, j, 1], mod[l, j, 2], npost[l, j])
            h = _pre_fwd(xc, vp, f"pre_fwd{k}")
            if j != 1:
                f = _ffn_fwd(h, w13g, w2v, 2 * l + j // 2, f"ffn_fwd{k}")
            elif l == 0:
                f, *saved_ab = _mix_ab_fwd(h, abin4, cwa, wg, bg, cwb, v512, pavg, about, "mix_ab_fwd")
            else:
                f = _mix_c_fwd(h, cin4, v2d, v1d, ws, bsb, cout, "mix_c_fwd")
            vps.append(vp)
            xs.append(xc)
            hs_.append(h)
            fs.append(f)
            xc = _post_fwd(xc, f, vp, res_ws[j], f"post_fwd{k}")

    dout, lrow = _loss_head(xc, tgt, "loss_head")
    loss = lax.psum(jnp.sum(lrow[0]), ("x", "y", "c"))

    dw13 = jnp.zeros((4, 4, D, Fh), F32)
    dw2v = jnp.zeros((4, 2, Fh, D), F32)
    d_npre = [[None] * 3 for _ in range(2)]
    d_npost = [[None] * 3 for _ in range(2)]
    d_mod = [[None] * 3 for _ in range(2)]
    for l in (1, 0):
        for j in (2, 1, 0):
            k = 3 * l + j
            df, s_post = _post_bwd(fs[k], dout, vps[k], res_ws[j], f"post_bwd{k}")
            if j != 1:
                dhp, dw13, dw2v = _ffn_bwd(hs_[k], df, w13g, w2v, dw13, dw2v, 2 * l + j // 2, f"ffn_bwd{k}")
            elif l == 0:
                dhp, d_abin, d_about, d_wg, d_cwa, d_cwb, d_bg, d_v512 = _mix_ab_bwd(
                    hs_[k], df, *saved_ab, abin4, cwa, wg, bg, cwb, v512, pavg, about, "mix_ab_bwd")
            else:
                dhp, d_cin, d_cout, d_ws, d_bsb, d_v2, d_v1 = _mix_c_bwd(
                    hs_[k], df, cin4, v2d, v1d, ws, bsb, cout, "mix_c_bwd")
            dout, s_pre = _pre_bwd(xs[k], dhp, dout, vps[k], f"pre_bwd{k}")
            d_npre[l][j], d_npost[l][j] = s_pre[2], s_post[1]
            d_mod[l][j] = jnp.stack([s_pre[0], s_pre[1], s_post[0]])
    grad_x = dout[None]

    dmod = jnp.stack([jnp.stack(d_mod[l]) for l in range(2)]).reshape(2, 9 * D)
    d_gate_w = jnp.concatenate([jnp.einsum("hdhe->hde", d_wg[:, :W].reshape(8, dh_a, 8, dh_a)),
                                jnp.einsum("hdhe->hde", d_wg[:, W:].reshape(8, dh_a, 8, dh_a))], axis=-1)
    d_gate_b = jnp.concatenate([d_bg[0, :W].reshape(8, dh_a), d_bg[0, W:].reshape(8, dh_a)], axis=-1)
    small_grads = [
        dmod, jnp.stack([jnp.stack(r) for r in d_npre]), jnp.stack([jnp.stack(r) for r in d_npost]),
        d_cwa[:4][None], d_v512[0][None], d_gate_w[None], d_gate_b[None], d_v512[1][None], d_cwb[:31][None],
        d_v512[2][None], d_v512[3][None], d_v512[4][None], d_v2[0][None], d_v1[0][None], d_v1[1][None],
        jnp.where(tril, d_ws, 0.0)[None], jnp.transpose(d_bsb.reshape(CHUNK, N_HEAD, D // N_HEAD).sum(-1))[None]]
    small_w = [ada_b, norm_pre, norm_post, a_conv_w, a_conv_b, a_gate_w, a_gate_b, a_lam, b_conv_w, b_conv_b,
               b_norm_g, b_norm_b, c_b_in, c_norm_g, c_norm_b, c_w_s, c_b_s]
    small_m = [m_ada_b, m_norm_pre, m_norm_post, m_a_conv_w, m_a_conv_b, m_a_gate_w, m_a_gate_b, m_a_lam, m_b_conv_w,
               m_b_conv_b, m_b_norm_g, m_b_norm_b, m_c_b_in, m_c_norm_g, m_c_norm_b, m_c_w_s, m_c_b_s]
    small_v = [v_ada_b, v_norm_pre, v_norm_post, v_a_conv_w, v_a_conv_b, v_a_gate_w, v_a_gate_b, v_a_lam, v_b_conv_w,
               v_b_conv_b, v_b_norm_g, v_b_norm_b, v_c_b_in, v_c_norm_g, v_c_norm_b, v_c_w_s, v_c_b_s]
    full_shapes = [g.shape for g in small_grads]
    sg_all = _all_gather8(_pack(small_grads), "gather_small_grads").reshape(N_DEV, -1, 128)
    sg_sum = _sum_slots(sg_all[None], "sum_small_grads")[0]
    g_full = _unpack(sg_sum, full_shapes)
    g_small = [g if g.shape == w.shape else _shard_last(g, shard, w.shape[-1]) for g, w in zip(g_full, small_w)]
    small_shapes = [w.shape for w in small_w]
    d_s, m_s, v_s = _adamw(_pack(small_w), _pack(g_small), _pack(small_m), _pack(small_v), "adamw_small")
    delta_small, newm_small, newv_small = (_unpack(a, small_shapes) for a in (d_s, m_s, v_s))

    dmod_all = jnp.stack([_unpack(sg_all[d], full_shapes[:1])[0] for d in range(N_DEV)], axis=1)
    n_ada = ada_w.shape[-1]
    g_ada_w = _ada_bwd(c_all, _shard_last(dmod_all, shard, n_ada), "ada_bwd")

    grads = [dw13, dw2v.reshape(4, 4, Fq, D), d_abin[None], d_about.reshape(1, 4, D // 4, D), d_cin[None],
             d_cout.reshape(1, 4, D // 4, D)]
    c_idx = jnp.reshape(ci, (1,)).astype(jnp.int32)
    landed = _send_other_half(grads)
    chip_sums = [_add_half(g, la, c_idx, f"add_half{k}") for k, (g, la) in enumerate(zip(grads, landed))]
    slots = _scatter_chips(chip_sums)
    reduced = [_sum_slots(sl, f"sum_chips{k}") for k, sl in enumerate(slots)]
    g_big = _join_halves(reduced)

    big_w = [ffn_w13, ffn_w2, ab_w_in, ab_w_out, c_w_in, c_w_out, ada_w]
    big_m = [m_ffn_w13, m_ffn_w2, m_ab_w_in, m_ab_w_out, m_c_w_in, m_c_w_out, m_ada_w]
    big_v = [v_ffn_w13, v_ffn_w2, v_ab_w_in, v_ab_w_out, v_c_w_in, v_c_w_out, v_ada_w]
    big_g = [g.reshape(w.shape) for g, w in zip(list(g_big) + [g_ada_w], big_w)]
    big_out = []
    for k, (w, g, m, v) in enumerate(zip(big_w, big_g, big_m, big_v)):
        two_d = lambda a: a.reshape(-1, a.shape[-1])
        res = _adamw(two_d(w), two_d(g), two_d(m), two_d(v), f"adamw_big{k}")
        big_out.append([r.reshape(w.shape) for r in res])

    names = ["ada_w", "ada_b", "norm_pre", "norm_post", "ffn_w13", "ffn_w2", "ab_w_in", "a_conv_w", "a_conv_b",
             "a_gate_w", "a_gate_b", "a_lam", "b_conv_w", "b_conv_b", "b_norm_g", "b_norm_b", "ab_w_out", "c_w_in",
             "c_b_in", "c_norm_g", "c_norm_b", "c_w_s", "c_b_s", "c_w_out"]
    big_names = ["ffn_w13", "ffn_w2", "ab_w_in", "ab_w_out", "c_w_in", "c_w_out", "ada_w"]
    small_names = ["ada_b", "norm_pre", "norm_post", "a_conv_w", "a_conv_b", "a_gate_w", "a_gate_b", "a_lam",
                   "b_conv_w", "b_conv_b", "b_norm_g", "b_norm_b", "c_b_in", "c_norm_g", "c_norm_b", "c_w_s", "c_b_s"]
    table = {}
    for k, n in enumerate(big_names):
        table[n] = (big_g[k], *big_out[k])
    for k, n in enumerate(small_names):
        table[n] = (g_small[k], delta_small[k], newm_small[k], newv_small[k])
    outs = [loss, grad_x]
    for field in range(4):
        outs += [table[n][field] for n in names]
    return tuple(outs)
```

```python
import functools
import math

import jax
import jax.numpy as jnp
from jax import lax
from jax.experimental import pallas as pl
from jax.experimental.pallas import tpu as pltpu

F32 = jnp.float32
MXU_DT = jnp.bfloat16
EPS = 1e-6
LRU_C = 8.0
N_SHARD = 4
N_DEV = 8
CHUNK = 128
N_HEAD = 8
ADAM_LR, ADAM_B1, ADAM_B2, ADAM_EPS, ADAM_WD, ADAM_STEP = 0.001, 0.9, 0.999, 1e-08, 0.01, 10
GELU_K0 = math.sqrt(2.0 / math.pi)
GELU_K1 = 0.044715
VMEM_LIMIT = 58 * 1024 * 1024
MESH = pl.DeviceIdType.MESH
ANY = pl.BlockSpec(memory_space=pl.ANY)


def _cp(sem=None, **kw):
    if sem is not None:
        kw["dimension_semantics"] = sem
    return pltpu.CompilerParams(vmem_limit_bytes=VMEM_LIMIT, **kw)


def _resident(a):
    return pl.BlockSpec(a.shape, lambda *_: (0,) * a.ndim, pipeline_mode=pl.Buffered(1))


def _whole(shape):
    return pl.BlockSpec(shape, lambda *_: (0,) * len(shape))


def _dot(a, b):
    return jnp.dot(a.astype(MXU_DT), b.astype(MXU_DT), preferred_element_type=F32)


def _dot_nt(a, b):
    return lax.dot_general(a.astype(MXU_DT), b.astype(MXU_DT), (((1,), (1,)), ((), ())), preferred_element_type=F32)


def _dot_tn(a, b):
    return lax.dot_general(a.astype(MXU_DT), b.astype(MXU_DT), (((0,), (0,)), ((), ())), preferred_element_type=F32)


def _dot_hi(a, b):
    return jnp.dot(a, b, precision=lax.Precision.HIGHEST, preferred_element_type=F32)


def _sig(x):
    return 1.0 / (1.0 + jnp.exp(-x))


def _logsig(x):
    return jnp.minimum(x, 0.0) - jnp.log(1.0 + jnp.exp(-jnp.abs(x)))


def _gelu(x):
    x2 = x * x
    t = jnp.tanh(GELU_K0 * (x + GELU_K1 * x * x2))
    val = 0.5 * x * (1.0 + t)
    der = 0.5 * (1.0 + t) + 0.5 * x * (1.0 - t * t) * (GELU_K0 * (1.0 + 3.0 * GELU_K1 * x2))
    return val, der


def _neg_expm1(x):
    small = -(x * (1.0 + x * (0.5 + x * (1.0 / 6.0 + x * (1.0 / 24.0)))))
    return jnp.where(x > -0.05, small, 1.0 - jnp.exp(x))


def _colsum(v):
    return jnp.sum(v, axis=0, keepdims=True)


def _rowmean(v):
    return jnp.mean(v, axis=-1, keepdims=True)


def _copy_out(pairs, sem):
    for src, dst in pairs:
        cp = pltpu.make_async_copy(src, dst, sem)
        cp.start()
        cp.wait()


def _shift_down(v, k):
    return v if k == 0 else pltpu.roll(v, k, 0)


def _shift_up(v, k):
    return v if k == 0 else pltpu.roll(v, v.shape[0] - k, 0)


def _shell_pre(xv, vp_ref):
    r = lax.rsqrt(_rowmean(xv * xv) + EPS)
    return xv * r * (vp_ref[0:1, :] * (1.0 + vp_ref[2:3, :])) + vp_ref[1:2, :]


def _shell_post(xv, fv, vp_ref, res_w):
    r = lax.rsqrt(_rowmean(fv * fv) + EPS)
    return xv + fv * r * (res_w * (1.0 + vp_ref[3:4, :]) * vp_ref[4:5, :])


def _shell_post_bwd(fv, dov, vp_ref, res_w, s_ref):
    r = lax.rsqrt(_rowmean(fv * fv) + EPS)
    fn = fv * r
    pg = vp_ref[4:5, :]
    dy = (res_w * (1.0 + vp_ref[3:4, :])) * dov
    if s_ref is not None:
        s_ref[0:1, :] += _colsum(res_w * fn * pg * dov)
        s_ref[1:2, :] += _colsum(fn * dy)
    q = dy * pg
    return r * (q - fn * _rowmean(fn * q))


def _shell_pre_bwd(xv, dh, dov, vp_ref, s_ref):
    r = lax.rsqrt(_rowmean(xv * xv) + EPS)
    xn = xv * r
    pg = vp_ref[0:1, :]
    sc1 = 1.0 + vp_ref[2:3, :]
    s_ref[2:3, :] += _colsum(dh)
    s_ref[3:4, :] += _colsum(xn * pg * dh)
    s_ref[4:5, :] += _colsum(xn * dh * sc1)
    q = dh * (sc1 * pg)
    return dov + r * (q - xn * _rowmean(xn * q))


def _loss_head(y, tgt, name, tm=512):
    S, D = y.shape

    def body(y_ref, t_ref, dy_ref, l_ref):
        @pl.when(pl.program_id(0) == 0)
        def _():
            l_ref[...] = jnp.zeros_like(l_ref)

        e = y_ref[...] - t_ref[...]
        dy_ref[...] = e * (1.0 / D)
        l_ref[0:1, :] += _colsum(e * e) * (0.5 / D)

    return pl.pallas_call(
        body, name=name, grid=(S // tm,),
        in_specs=[pl.BlockSpec((tm, D), lambda t: (t, 0)), pl.BlockSpec((tm, D), lambda t: (t, 0))],
        out_specs=[pl.BlockSpec((tm, D), lambda t: (t, 0)), pl.BlockSpec((8, D), lambda t: (0, 0))],
        out_shape=[jax.ShapeDtypeStruct((S, D), F32), jax.ShapeDtypeStruct((8, D), F32)],
        compiler_params=_cp(("arbitrary",)),
    )(y, tgt)


def _ffn_fwd(x, vp, w13g, w2v, g, res_w, name, tm=256):
    S, D = x.shape
    Fh = w13g.shape[-1]

    def body(x_ref, vp_ref, w13_ref, w2_ref, xo_ref, f_ref, h_ref, g_ref, u_ref):
        xv = x_ref[...]
        hb = _shell_pre(xv, vp_ref).astype(MXU_DT)
        h_ref[...] = hb
        acc = None
        for j in range(2):
            gg = _dot(hb, w13_ref[j])
            uu = _dot(hb, w13_ref[2 + j])
            g_ref[:, j * Fh:(j + 1) * Fh] = gg.astype(g_ref.dtype)
            u_ref[:, j * Fh:(j + 1) * Fh] = uu.astype(u_ref.dtype)
            part = _dot(gg * _sig(gg) * uu, w2_ref[j])
            acc = part if acc is None else acc + part
        f_ref[...] = acc
        xo_ref[...] = _shell_post(xv, acc, vp_ref, res_w)

    tile = lambda w: pl.BlockSpec((tm, w), lambda t: (t, 0))
    return pl.pallas_call(
        body, name=name, grid=(S // tm,),
        in_specs=[tile(D), _whole((8, D)),
                  pl.BlockSpec((None, 4, D, Fh), lambda t: (g, 0, 0, 0), pipeline_mode=pl.Buffered(1)),
                  pl.BlockSpec((None, 2, Fh, D), lambda t: (g, 0, 0, 0), pipeline_mode=pl.Buffered(1))],
        out_specs=[tile(D), tile(D), tile(D), tile(2 * Fh), tile(2 * Fh)],
        out_shape=[jax.ShapeDtypeStruct((S, D), F32), jax.ShapeDtypeStruct((S, D), F32),
                   jax.ShapeDtypeStruct((S, D), MXU_DT), jax.ShapeDtypeStruct((S, 2 * Fh), MXU_DT),
                   jax.ShapeDtypeStruct((S, 2 * Fh), MXU_DT)],
        compiler_params=_cp(("arbitrary",)),
    )(x, vp, w13g, w2v)


def _ffn_bwd_half(j, x, f, dout, vp, h, gpre, upre, w13g, w2v, dw13, dw2v, g, res_w, name, df=None, dh0=None, tm=256):
    S, D = h.shape
    Fh = w13g.shape[-1]
    T = S // tm
    first = j == 0

    def body(*refs):
        if first:
            (f_ref, do_ref, vp_ref, h_ref, g_ref, u_ref, w1_ref, w3_ref, w2_ref, _, _,
             df_ref, dh_ref, s_ref, dw13_ref, dw2_ref, a1, a3, a2, sem) = refs
        else:
            (x_ref, do_ref, vp_ref, h_ref, g_ref, u_ref, w1_ref, w3_ref, w2_ref, dfi_ref, dh0_ref, _, _,
             dx_ref, s_ref, dw13_ref, dw2_ref, a1, a3, a2, sem) = refs
        t = pl.program_id(0)

        @pl.when(t == 0)
        def _():
            for ref in (a1, a3, a2, s_ref):
                ref[...] = jnp.zeros_like(ref)

        hv = h_ref[...]
        if first:
            dfv = _shell_post_bwd(f_ref[...], do_ref[...], vp_ref, res_w, s_ref).astype(MXU_DT)
            df_ref[...] = dfv
        else:
            dfv = dfi_ref[...]
        gg = g_ref[...].astype(F32)
        uu = u_ref[...].astype(F32)
        sg = _sig(gg)
        si = gg * sg
        da = _dot_nt(dfv, w2_ref[...])
        a2[...] += _dot_tn(si * uu, dfv)
        dg = da * uu * (sg * (1.0 + gg * (1.0 - sg)))
        du = da * si
        a1[...] += _dot_tn(hv, dg)
        a3[...] += _dot_tn(hv, du)
        dh = _dot_nt(dg, w1_ref[...]) + _dot_nt(du, w3_ref[...])
        if first:
            dh_ref[...] = dh
        else:
            dx_ref[...] = _shell_pre_bwd(x_ref[...], dh0_ref[...] + dh, do_ref[...], vp_ref, s_ref)

        @pl.when(t == T - 1)
        def _():
            _copy_out(((a1, dw13_ref.at[g, j]), (a3, dw13_ref.at[g, 2 + j]), (a2, dw2_ref.at[g, j])), sem)

    tile = lambda w: pl.BlockSpec((tm, w), lambda t: (t, 0))
    half = pl.BlockSpec((tm, Fh), lambda t: (t, j))
    wspec = lambda s: pl.BlockSpec((None, None) + s, lambda t: (g, 0, 0, 0))
    weights = [pl.BlockSpec((None, None, D, Fh), lambda t: (g, j, 0, 0), pipeline_mode=pl.Buffered(1)),
               pl.BlockSpec((None, None, D, Fh), lambda t: (g, 2 + j, 0, 0), pipeline_mode=pl.Buffered(1)),
               pl.BlockSpec((None, None, Fh, D), lambda t: (g, j, 0, 0), pipeline_mode=pl.Buffered(1))]
    sd = jax.ShapeDtypeStruct
    grads = [sd(dw13.shape, F32), sd(dw2v.shape, F32)]
    scratch = [pltpu.VMEM((D, Fh), F32), pltpu.VMEM((D, Fh), F32), pltpu.VMEM((Fh, D), F32), pltpu.SemaphoreType.DMA]
    if first:
        return pl.pallas_call(
            body, name=name, grid=(T,),
            in_specs=[tile(D), tile(D), _whole((8, D)), tile(D), half, half] + weights + [ANY, ANY],
            out_specs=[tile(D), tile(D), _whole((8, D)), ANY, ANY],
            out_shape=[sd((S, D), MXU_DT), sd((S, D), F32), sd((8, D), F32)] + grads,
            scratch_shapes=scratch, input_output_aliases={9: 3, 10: 4}, compiler_params=_cp(("arbitrary",)),
        )(f, dout, vp, h, gpre, upre, w13g, w13g, w2v, dw13, dw2v)
    return pl.pallas_call(
        body, name=name, grid=(T,),
        in_specs=[tile(D), tile(D), _whole((8, D)), tile(D), half, half] + weights + [tile(D), tile(D), ANY, ANY],
        out_specs=[tile(D), _whole((8, D)), ANY, ANY],
        out_shape=[sd((S, D), F32), sd((8, D), F32)] + grads,
        scratch_shapes=scratch, input_output_aliases={11: 2, 12: 3}, compiler_params=_cp(("arbitrary",)),
    )(x, dout, vp, h, gpre, upre, w13g, w13g, w2v, df, dh0, dw13, dw2v)


def _scan_fwd(a, u, rows):
    n = a.shape[0]
    d = 1
    while d < n:
        m = rows >= d
        u = u + a * jnp.where(m, _shift_down(u, d), 0.0)
        a = a * jnp.where(m, _shift_down(a, d), 1.0)
        d *= 2
    return a, u


def _scan_bwd(a, u, rows):
    n = a.shape[0]
    d = 1
    while d < n:
        m = rows < n - d
        u = u + a * jnp.where(m, _shift_up(u, d), 0.0)
        a = a * jnp.where(m, _shift_up(a, d), 1.0)
        d *= 2
    return a, u


def _causal_conv(ext, w_ref, K, halo, tm):
    acc = None
    for k in range(K):
        term = w_ref[k:k + 1, :] * _shift_down(ext, K - 1 - k)[halo:, :]
        acc = term if acc is None else acc + term
    return acc


def _anticausal_conv(ext, w_ref, K, tm):
    acc = None
    for k in range(K):
        term = w_ref[k:k + 1, :] * _shift_up(ext, K - 1 - k)[:tm, :]
        acc = term if acc is None else acc + term
    return acc


def _group_norm(vc, pavg, g, b):
    mu = _dot_hi(vc, pavg)
    dv = vc - mu
    rstd = lax.rsqrt(_dot_hi(dv * dv, pavg) + EPS)
    vhat = dv * rstd
    return vhat, rstd, vhat * g + b


def _lru_gates(axc, wg_ref, bg_ref, lam, W):
    gp = _dot(axc, wg_ref[...]) + bg_ref[0:1, :]
    r = _sig(gp[:, :W])
    i = _sig(gp[:, W:])
    ls = _logsig(lam)
    L = (LRU_C * ls) * r
    a = jnp.exp(L)
    mult = jnp.sqrt(_neg_expm1(2.0 * L))
    return r, i, ls, a, mult


def _mix_ab_fwd(x, vp, win4, cwa, wg, bg, cwb, v512, pavg, wout, res_w, name, tm=256):
    S, D = x.shape
    W = win4.shape[-1]
    KA, KB, HA, HB = 4, 31, 8, 32

    def body(x_ref, vp_ref, win_ref, cwa_ref, wg_ref, bg_ref, cwb_ref, v_ref, p_ref, wo_ref,
             xo_ref, f_ref, h_ref, hs_ref, axp_ref, axc_ref, bv_ref, vc_ref, ahalo, bhalo, carry):
        @pl.when(pl.program_id(0) == 0)
        def _():
            ahalo[...] = jnp.zeros_like(ahalo)
            bhalo[...] = jnp.zeros_like(bhalo)
            carry[...] = jnp.zeros_like(carry)

        xv = x_ref[...]
        hv = _shell_pre(xv, vp_ref).astype(MXU_DT)
        h_ref[...] = hv
        a_gate = _dot(hv, win_ref[0])
        axp = _dot(hv, win_ref[1])
        b_val = _dot(hv, win_ref[2])
        b_gate = _dot(hv, win_ref[3])
        rows = lax.broadcasted_iota(jnp.int32, (tm, W), 0)
        axc = _causal_conv(jnp.concatenate([ahalo[...], axp], axis=0), cwa_ref, KA, HA, tm) + v_ref[0:1, :]
        ahalo[...] = axp[tm - HA:, :]
        r, i, ls, a, mult = _lru_gates(axc, wg_ref, bg_ref, v_ref[1:2, :], W)
        acum, hloc = _scan_fwd(a, mult * i * axc, rows)
        hs = hloc + acum * carry[7:8, :]
        carry[...] = hs[tm - 8:, :]
        ya = hs * _gelu(a_gate)[0]
        bv = b_val * _sig(b_gate)
        vc = _causal_conv(jnp.concatenate([bhalo[...], bv], axis=0), cwb_ref, KB, HB, tm) + v_ref[2:3, :]
        bhalo[...] = bv[tm - HB:, :]
        _, _, vn = _group_norm(vc, p_ref[...], v_ref[3:4, :], v_ref[4:5, :])
        yb = vn * _sig(vn)
        fv = _dot(ya, wo_ref[0:W, :]) + _dot(yb, wo_ref[W:, :])
        f_ref[...] = fv
        xo_ref[...] = _shell_post(xv, fv, vp_ref, res_w)
        hs_ref[...] = hs
        axp_ref[...] = axp
        axc_ref[...] = axc
        bv_ref[...] = bv
        vc_ref[...] = vc

    tile = lambda w: pl.BlockSpec((tm, w), lambda t: (t, 0))
    sd = jax.ShapeDtypeStruct
    return pl.pallas_call(
        body, name=name, grid=(S // tm,),
        in_specs=[tile(D), _whole((8, D))] + [_resident(a) for a in (win4, cwa, wg, bg, cwb, v512, pavg, wout)],
        out_specs=[tile(D), tile(D), tile(D)] + [tile(W)] * 5,
        out_shape=[sd((S, D), F32), sd((S, D), F32), sd((S, D), MXU_DT)] + [sd((S, W), F32)] * 5,
        scratch_shapes=[pltpu.VMEM((HA, W), F32), pltpu.VMEM((HB, W), F32), pltpu.VMEM((8, W), F32)],
        compiler_params=_cp(("arbitrary",)),
    )(x, vp, win4, cwa, wg, bg, cwb, v512, pavg, wout)


def _mix_ab_bwd(x, f, dout, vp, h, hs, axp, axc, bv, vc, win4, cwa, wg, bg, cwb, v512, pavg, wout, res_w, name, tm=256):
    S, D = x.shape
    W = win4.shape[-1]
    T = S // tm
    KA, KB, HA, HB = 4, 31, 8, 32

    def body(x_ref, f_ref, do_ref, vp_ref, h_ref, hs_ref, hsp_ref, axp_ref, axpp_ref, axc_ref, bv_ref, bvp_ref, vc_ref,
             win_ref, cwa_ref, wg_ref, bg_ref, cwb_ref, v_ref, p_ref, wo_ref,
             dx_ref, s_ref, dwin_out, dwo_out, dwg_out, dcwa_ref, dcwb_ref, dbg_ref, dv_ref,
             danext, dvnext, gfirst, afirst, dwin_ref, dwo_ref, dwg_ref, sem):
        t = pl.program_id(0)

        @pl.when(t == 0)
        def _():
            for ref in (s_ref, dwin_ref, dwo_ref, dwg_ref, dcwa_ref, dcwb_ref, dbg_ref, dv_ref, danext, dvnext,
                        gfirst, afirst):
                ref[...] = jnp.zeros_like(ref)

        notfirst = jnp.where(t < T - 1, 1.0, 0.0).astype(F32)
        hv = h_ref[...]
        dov = do_ref[...]
        dfv = _shell_post_bwd(f_ref[...], dov, vp_ref, res_w, s_ref).astype(MXU_DT)
        a_gate = _dot(hv, win_ref[0])
        b_val = _dot(hv, win_ref[2])
        b_gate = _dot(hv, win_ref[3])
        rows = lax.broadcasted_iota(jnp.int32, (tm, W), 0)
        ge, dge = _gelu(a_gate)
        hsv = hs_ref[...]
        ya = hsv * ge
        vhat, rstd, vn = _group_norm(vc_ref[...], p_ref[...], v_ref[3:4, :], v_ref[4:5, :])
        sgn = _sig(vn)
        yb = vn * sgn
        dma = _dot_nt(dfv, wo_ref[0:W, :])
        dmb = _dot_nt(dfv, wo_ref[W:, :])
        dwo_ref[0:W, :] += _dot_tn(ya, dfv)
        dwo_ref[W:, :] += _dot_tn(yb, dfv)
        dhs = dma * ge
        d_a_gate = dma * hsv * dge
        axcv = axc_ref[...]
        lam = v_ref[1:2, :]
        r, i, ls, a, mult = _lru_gates(axcv, wg_ref, bg_ref, lam, W)
        ash = jnp.where(rows == tm - 1, afirst[0:1, :], _shift_up(a, 1))
        asuf, gloc = _scan_bwd(ash, dhs, rows)
        gsc = gloc + asuf * gfirst[0:1, :]
        afirst[...] = a[0:8, :]
        gfirst[...] = gsc[0:8, :]
        hprev = jnp.where(rows == 0, hsp_ref[HA - 1:HA, :] * notfirst, _shift_down(hsv, 1))
        da = gsc * hprev
        dL = da * a - gsc * (i * axcv) * (a * a) / mult
        dix = gsc * mult
        daxc = dix * i
        dr = dL * (LRU_C * ls)
        dv_ref[1:2, :] += _colsum(dL * r) * (LRU_C * _sig(-lam))
        dgate = jnp.concatenate([dr * r * (1.0 - r), (dix * axcv) * i * (1.0 - i)], axis=1)
        dbg_ref[0:1, :] += _colsum(dgate)
        dwg_ref[...] += _dot_tn(axcv, dgate)
        daxc = daxc + _dot_nt(dgate, wg_ref[...])
        daxp = _anticausal_conv(jnp.concatenate([daxc, danext[...]], axis=0), cwa_ref, KA, tm)
        ext = jnp.concatenate([axpp_ref[...] * notfirst, axp_ref[...]], axis=0)
        for k in range(KA):
            dcwa_ref[k:k + 1, :] += _colsum(daxc * _shift_down(ext, KA - 1 - k)[HA:, :])
        dv_ref[0:1, :] += _colsum(daxc)
        danext[...] = daxc[0:HA, :]
        dvn = dmb * (sgn * (1.0 + vn * (1.0 - sgn)))
        dv_ref[4:5, :] += _colsum(dvn)
        dv_ref[3:4, :] += _colsum(dvn * vhat)
        dvh = dvn * v_ref[3:4, :]
        dvc = rstd * (dvh - _dot_hi(dvh, p_ref[...]) - vhat * _dot_hi(dvh * vhat, p_ref[...]))
        dbv = _anticausal_conv(jnp.concatenate([dvc, dvnext[...]], axis=0), cwb_ref, KB, tm)
        ext = jnp.concatenate([bvp_ref[...] * notfirst, bv_ref[...]], axis=0)
        for k in range(KB):
            dcwb_ref[k:k + 1, :] += _colsum(dvc * _shift_down(ext, KB - 1 - k)[HB:, :])
        dv_ref[2:3, :] += _colsum(dvc)
        dvnext[...] = dvc[0:HB, :]
        sb = _sig(b_gate)
        dzs = (d_a_gate, daxp, dbv * sb, dbv * b_val * sb * (1.0 - sb))
        dh = None
        for s in range(4):
            part = _dot_nt(dzs[s], win_ref[s])
            dh = part if dh is None else dh + part
            dwin_ref[s] += _dot_tn(hv, dzs[s])
        dx_ref[...] = _shell_pre_bwd(x_ref[...], dh, dov, vp_ref, s_ref)

        @pl.when(t == T - 1)
        def _():
            _copy_out(((dwin_ref, dwin_out), (dwo_ref, dwo_out), (dwg_ref, dwg_out)), sem)

    tile = lambda w: pl.BlockSpec((tm, w), lambda t: (T - 1 - t, 0))
    prev = lambda hh: pl.BlockSpec((hh, W), lambda t: (jnp.maximum((T - 1 - t) * (tm // hh) - 1, 0), 0))
    out_shapes = [(S, D), (8, D), win4.shape, wout.shape, wg.shape, (8, W), (32, W), (8, 2 * W), (8, W)]
    return pl.pallas_call(
        body, name=name, grid=(T,),
        in_specs=[tile(D), tile(D), tile(D), _whole((8, D)), tile(D),
                  tile(W), prev(HA), tile(W), prev(HA), tile(W), tile(W), prev(HB), tile(W)]
        + [_resident(a) for a in (win4, cwa, wg, bg, cwb, v512, pavg, wout)],
        out_specs=[tile(D), _whole((8, D)), ANY, ANY, ANY] + [_whole(s) for s in out_shapes[5:]],
        out_shape=[jax.ShapeDtypeStruct(s, F32) for s in out_shapes],
        scratch_shapes=[pltpu.VMEM((HA, W), F32), pltpu.VMEM((HB, W), F32), pltpu.VMEM((8, W), F32),
                        pltpu.VMEM((8, W), F32), pltpu.VMEM(win4.shape, F32), pltpu.VMEM(wout.shape, F32),
                        pltpu.VMEM(wg.shape, F32), pltpu.SemaphoreType.DMA],
        compiler_params=_cp(("arbitrary",)),
    )(x, f, dout, vp, h, hs, hs, axp, axp, axc, bv, bv, vc, win4, cwa, wg, bg, cwb, v512, pavg, wout)


def _mix_c_core(hv, win_ref, v2_ref, v1_ref, ws_ref, bsb_ref, tm, D):
    zp = jnp.concatenate([_dot(hv, win_ref[s]) for s in range(4)], axis=1) + v2_ref[0:1, :]
    z, dz = _gelu(zp)
    u, v = z[:, :D], z[:, D:]
    mu = _rowmean(v)
    dv = v - mu
    rstd = lax.rsqrt(_rowmean(dv * dv) + EPS)
    vhat = dv * rstd
    vn = vhat * v1_ref[0:1, :] + v1_ref[1:2, :]
    rows_out = []
    for cidx in range(tm // CHUNK):
        blk = vn[cidx * CHUNK:(cidx + 1) * CHUNK, :]
        heads = [_dot(ws_ref[hd], blk[:, hd * CHUNK:(hd + 1) * CHUNK]) for hd in range(N_HEAD)]
        rows_out.append(jnp.concatenate(heads, axis=1) + bsb_ref[...])
    mixed = jnp.concatenate(rows_out, axis=0)
    return dz, u, rstd, vhat, vn, mixed


def _mix_c_fwd(x, vp, win4, v2d, v1d, ws, bsb, wout, res_w, name, tm=256):
    S, D = x.shape

    def body(x_ref, vp_ref, win_ref, v2_ref, v1_ref, ws_ref, bsb_ref, wo_ref, xo_ref, f_ref, h_ref):
        xv = x_ref[...]
        hv = _shell_pre(xv, vp_ref).astype(MXU_DT)
        h_ref[...] = hv
        _, u, _, _, _, mixed = _mix_c_core(hv, win_ref, v2_ref, v1_ref, ws_ref, bsb_ref, tm, D)
        fv = _dot(u * mixed, wo_ref[...])
        f_ref[...] = fv
        xo_ref[...] = _shell_post(xv, fv, vp_ref, res_w)

    tile = pl.BlockSpec((tm, D), lambda t: (t, 0))
    sd = jax.ShapeDtypeStruct
    return pl.pallas_call(
        body, name=name, grid=(S // tm,),
        in_specs=[tile, _whole((8, D))] + [_resident(a) for a in (win4, v2d, v1d, ws, bsb, wout)],
        out_specs=[tile, tile, tile],
        out_shape=[sd((S, D), F32), sd((S, D), F32), sd((S, D), MXU_DT)], compiler_params=_cp(("arbitrary",)),
    )(x, vp, win4, v2d, v1d, ws, bsb, wout)


def _mix_c_bwd(x, f, dout, vp, h, win4, v2d, v1d, ws, bsb, wout, res_w, name, tm=256):
    S, D = x.shape

    def body(x_ref, f_ref, do_ref, vp_ref, h_ref, win_ref, v2_ref, v1_ref, ws_ref, bsb_ref, wo_ref,
             dx_ref, s_ref, dwin_out, dwo_out, dws_ref, dbsb_ref, dv2_ref, dv1_ref, dwin_ref, dwo_ref, sem):
        @pl.when(pl.program_id(0) == 0)
        def _():
            for ref in (s_ref, dwin_ref, dwo_ref, dws_ref, dbsb_ref, dv2_ref, dv1_ref):
                ref[...] = jnp.zeros_like(ref)

        hv = h_ref[...]
        dov = do_ref[...]
        dfv = _shell_post_bwd(f_ref[...], dov, vp_ref, res_w, s_ref).astype(MXU_DT)
        dz, u, rstd, vhat, vn, mixed = _mix_c_core(hv, win_ref, v2_ref, v1_ref, ws_ref, bsb_ref, tm, D)
        dp = _dot_nt(dfv, wo_ref[...])
        dwo_ref[...] += _dot_tn(u * mixed, dfv)
        du = dp * mixed
        dmx = dp * u
        rows_out = []
        for cidx in range(tm // CHUNK):
            dblk = dmx[cidx * CHUNK:(cidx + 1) * CHUNK, :]
            vblk = vn[cidx * CHUNK:(cidx + 1) * CHUNK, :]
            dbsb_ref[...] += dblk
            heads = []
            for hd in range(N_HEAD):
                dsl = dblk[:, hd * CHUNK:(hd + 1) * CHUNK]
                heads.append(_dot_tn(ws_ref[hd], dsl))
                dws_ref[hd] += _dot_nt(dsl, vblk[:, hd * CHUNK:(hd + 1) * CHUNK])
            rows_out.append(jnp.concatenate(heads, axis=1))
        dvn = jnp.concatenate(rows_out, axis=0)
        dv1_ref[1:2, :] += _colsum(dvn)
        dv1_ref[0:1, :] += _colsum(dvn * vhat)
        dvh = dvn * v1_ref[0:1, :]
        dv = rstd * (dvh - _rowmean(dvh) - vhat * _rowmean(dvh * vhat))
        dzp = jnp.concatenate([du, dv], axis=1) * dz
        dv2_ref[0:1, :] += _colsum(dzp)
        W = win_ref.shape[-1]
        dh = None
        for s in range(4):
            dzs = dzp[:, s * W:(s + 1) * W]
            part = _dot_nt(dzs, win_ref[s])
            dh = part if dh is None else dh + part
            dwin_ref[s] += _dot_tn(hv, dzs)
        dx_ref[...] = _shell_pre_bwd(x_ref[...], dh, dov, vp_ref, s_ref)

        @pl.when(pl.program_id(0) == S // tm - 1)
        def _():
            _copy_out(((dwin_ref, dwin_out), (dwo_ref, dwo_out)), sem)

    tile = pl.BlockSpec((tm, D), lambda t: (t, 0))
    out_shapes = [(S, D), (8, D), win4.shape, wout.shape, ws.shape, bsb.shape, (8, 2 * D), (8, D)]
    return pl.pallas_call(
        body, name=name, grid=(S // tm,),
        in_specs=[tile, tile, tile, _whole((8, D)), tile] + [_resident(a) for a in (win4, v2d, v1d, ws, bsb, wout)],
        out_specs=[tile, _whole((8, D)), ANY, ANY] + [_whole(s) for s in out_shapes[4:]],
        out_shape=[jax.ShapeDtypeStruct(s, F32) for s in out_shapes],
        scratch_shapes=[pltpu.VMEM(win4.shape, F32), pltpu.VMEM(wout.shape, F32), pltpu.SemaphoreType.DMA],
        compiler_params=_cp(("arbitrary",)),
    )(x, f, dout, vp, h, win4, v2d, v1d, ws, bsb, wout)


def _ada_fwd(c_all, ada_w, ada_b_my, name):
    L, D, N = ada_w.shape
    tn = N // 3

    def body(c_ref, w_ref, b_ref, o_ref):
        cv = c_ref[...]
        o_ref[...] = _dot_hi(cv * _sig(cv), w_ref[...]) + b_ref[...]

    return pl.pallas_call(
        body, name=name, grid=(L, 3),
        in_specs=[pl.BlockSpec((8, D), lambda l, n: (0, 0)), pl.BlockSpec((None, D, tn), lambda l, n: (l, 0, n)),
                  pl.BlockSpec((None, 1, tn), lambda l, n: (l, 0, n))],
        out_specs=pl.BlockSpec((None, 8, tn), lambda l, n: (l, 0, n)),
        out_shape=jax.ShapeDtypeStruct((L, 8, N), F32), compiler_params=_cp(("arbitrary", "arbitrary")),
    )(c_all, ada_w, ada_b_my)


def _ada_bwd(c_all, dmod_my, name):
    L, _, N = dmod_my.shape
    D = c_all.shape[1]
    tn = N // 3

    def body(c_ref, d_ref, o_ref):
        cv = c_ref[...]
        o_ref[...] = lax.dot_general(cv * _sig(cv), d_ref[...], (((0,), (0,)), ((), ())),
                                     precision=lax.Precision.HIGHEST, preferred_element_type=F32)

    return pl.pallas_call(
        body, name=name, grid=(L, 3),
        in_specs=[pl.BlockSpec((8, D), lambda l, n: (0, 0)), pl.BlockSpec((None, 8, tn), lambda l, n: (l, 0, n))],
        out_specs=pl.BlockSpec((None, D, tn), lambda l, n: (l, 0, n)),
        out_shape=jax.ShapeDtypeStruct((L, D, N), F32), compiler_params=_cp(("arbitrary", "arbitrary")),
    )(c_all, dmod_my)


def _row_tile(rows, cols, budget=1 << 20):
    best = 8
    for rt in range(8, rows + 1, 8):
        if rows % rt == 0 and rt * cols * 4 <= budget:
            best = rt
    return best


def _cast_place(w, s_idx, name):
    Gk, R, C = w.shape
    rt = _row_tile(R, C)

    def body(s_ref, w_ref, o_ref):
        o_ref[...] = w_ref[...].astype(o_ref.dtype)

    return pl.pallas_call(
        body, name=name,
        grid_spec=pltpu.PrefetchScalarGridSpec(
            num_scalar_prefetch=1, grid=(Gk, R // rt),
            in_specs=[pl.BlockSpec((None, rt, C), lambda g, r, s_ref: (g, r, 0))],
            out_specs=pl.BlockSpec((None, None, rt, C), lambda g, r, s_ref: (g, s_ref[0], r, 0))),
        out_shape=jax.ShapeDtypeStruct((Gk, N_SHARD, R, C), MXU_DT), compiler_params=_cp(("arbitrary", "arbitrary")),
    )(s_idx, w)


def _add_half(gk, la, c_idx, name):
    Gk, _, R, C = gk.shape
    Rh = R // 2
    n = Gk * N_SHARD
    gv = gk.reshape(n, 2, Rh, C)
    lv = la.reshape(n, Rh, C)
    rt = _row_tile(Rh, C)

    def body(c_ref, g_ref, l_ref, o_ref):
        o_ref[...] = g_ref[...] + l_ref[...]

    out = pl.pallas_call(
        body, name=name,
        grid_spec=pltpu.PrefetchScalarGridSpec(
            num_scalar_prefetch=1, grid=(n, Rh // rt),
            in_specs=[pl.BlockSpec((None, None, rt, C), lambda i, r, c_ref: (i, c_ref[0], r, 0)),
                      pl.BlockSpec((None, rt, C), lambda i, r, c_ref: (i, r, 0))],
            out_specs=pl.BlockSpec((None, rt, C), lambda i, r, c_ref: (i, r, 0))),
        out_shape=jax.ShapeDtypeStruct((n, Rh, C), F32), compiler_params=_cp(("arbitrary", "arbitrary")),
    )(c_idx, gv, lv)
    return out.reshape(Gk, N_SHARD, Rh, C)


def _sum_chips(part, landed, sc_idx, name):
    Gk, _, Rh, C = part.shape
    rt = _row_tile(Rh, C)
    nb = Rh // rt

    def body(i_ref, p_ref, l_ref, o_ref):
        o_ref[...] = ((p_ref[...] + l_ref[0]) + l_ref[1]) + l_ref[2]

    return pl.pallas_call(
        body, name=name,
        grid_spec=pltpu.PrefetchScalarGridSpec(
            num_scalar_prefetch=1, grid=(Gk, nb),
            in_specs=[pl.BlockSpec((None, None, rt, C), lambda g, r, i_ref: (g, i_ref[0], r, 0)),
                      pl.BlockSpec((None, 3, rt, C), lambda g, r, i_ref: (g, 0, r, 0))],
            out_specs=pl.BlockSpec((None, rt, C), lambda g, r, i_ref: (g, i_ref[1] * nb + r, 0))),
        out_shape=jax.ShapeDtypeStruct((Gk, 2 * Rh, C), F32), compiler_params=_cp(("arbitrary", "arbitrary")),
    )(sc_idx, part, landed)


def _sum_slots(lb, name):
    Gk, n, Rh, C = lb.shape
    rt = _row_tile(Rh, C)

    def body(l_ref, o_ref):
        acc = l_ref[0]
        for s in range(1, n):
            acc = acc + l_ref[s]
        o_ref[...] = acc

    return pl.pallas_call(
        body, name=name, grid=(Gk, Rh // rt),
        in_specs=[pl.BlockSpec((None, n, rt, C), lambda g, r: (g, 0, r, 0))],
        out_specs=pl.BlockSpec((None, rt, C), lambda g, r: (g, r, 0)),
        out_shape=jax.ShapeDtypeStruct((Gk, Rh, C), F32), compiler_params=_cp(("arbitrary", "arbitrary")),
    )(lb)


def _adamw(w, g, m, v, name):
    rows, cols = w.shape
    rt = _row_tile(rows, cols) if rows % 8 == 0 else rows
    c1 = 1.0 - ADAM_B1 ** ADAM_STEP
    c2 = 1.0 - ADAM_B2 ** ADAM_STEP

    def body(w_ref, g_ref, m_ref, v_ref, d_ref, mo_ref, vo_ref):
        gv = g_ref[...]
        mn = ADAM_B1 * m_ref[...] + (1.0 - ADAM_B1) * gv
        vn = ADAM_B2 * v_ref[...] + (1.0 - ADAM_B2) * (gv * gv)
        d_ref[...] = -ADAM_LR * ((mn / c1) / (jnp.sqrt(vn / c2) + ADAM_EPS) + ADAM_WD * w_ref[...])
        mo_ref[...] = mn
        vo_ref[...] = vn

    spec = pl.BlockSpec((rt, cols), lambda r: (r, 0))
    sds = jax.ShapeDtypeStruct((rows, cols), F32)
    return pl.pallas_call(
        body, name=name, grid=(rows // rt,), in_specs=[spec] * 4, out_specs=[spec] * 3,
        out_shape=[sds] * 3, compiler_params=_cp(("arbitrary",)),
    )(w, g, m, v)


def _coords():
    return lax.axis_index("x"), lax.axis_index("y"), lax.axis_index("c")


def _all_gather8(blk, name):
    m_per, n = blk.shape

    def body(x_ref, out_ref, send_sems, recv_sems, local_sem):
        x, y, c = _coords()
        me, sibling = (x, y, c), (x, y, 1 - c)
        chips = [(1 - x, y), (x, 1 - y), (1 - x, 1 - y)]

        def rows(px, py, pc):
            return out_ref.at[pl.ds((4 * px + 2 * py + pc) * m_per, m_per), :]

        def copy(k, block, to, src=None):
            return pltpu.make_async_remote_copy(
                src_ref=rows(*block) if src is None else src, dst_ref=rows(*block),
                send_sem=send_sems.at[k], recv_sem=recv_sems.at[k], device_id=to, device_id_type=MESH)

        mine = pltpu.make_async_copy(x_ref, rows(*me), local_sem)
        mine.start()
        first = [copy(0, me, sibling, src=x_ref)]
        first += [copy(1 + j, me, (*chip, c), src=x_ref) for j, chip in enumerate(chips)]
        for cp in first:
            cp.start()
        passed = [copy(4 + j, (*chip, c), sibling) for j, chip in enumerate(chips)]
        for j, chip in enumerate(chips):
            copy(1 + j, (*chip, c), me).wait_recv()
            passed[j].start()
        copy(0, sibling, me).wait_recv()
        for j, chip in enumerate(chips):
            copy(4 + j, (*chip, 1 - c), me).wait_recv()
        for cp in first + passed:
            cp.wait_send()
        mine.wait()

    return pl.pallas_call(
        body, name=name, out_shape=jax.ShapeDtypeStruct((N_DEV * m_per, n), blk.dtype),
        in_specs=[pl.BlockSpec(memory_space=pltpu.VMEM)], out_specs=pl.BlockSpec(memory_space=pltpu.VMEM),
        scratch_shapes=[pltpu.SemaphoreType.DMA((7,)), pltpu.SemaphoreType.DMA((7,)), pltpu.SemaphoreType.DMA],
        compiler_params=_cp(),
    )(blk)


def _comm_call(name, inputs, out_shapes, plan, n_remote, aliases=None):
    n_in, n_out = len(inputs), len(out_shapes)

    def body(*refs):
        in_refs, out_refs = refs[:n_in], refs[n_in:n_in + n_out]
        send_sems, recv_sems = refs[n_in + n_out:]

        def remote(k, src, dst, to):
            return pltpu.make_async_remote_copy(src_ref=src, dst_ref=dst, send_sem=send_sems.at[k],
                                                recv_sem=recv_sems.at[k], device_id=to, device_id_type=MESH)

        plan(in_refs, out_refs, remote)

    return pl.pallas_call(
        body, name=name, out_shape=out_shapes, in_specs=[ANY] * n_in, out_specs=[ANY] * n_out,
        scratch_shapes=[pltpu.SemaphoreType.DMA((n_remote,)), pltpu.SemaphoreType.DMA((n_remote,))],
        input_output_aliases=aliases or {}, compiler_params=_cp(has_side_effects=True),
    )(*inputs)


def _gather_weights(placed):
    K = len(placed)

    def plan(ins, outs, remote):
        x, y, c = _coords()
        s_me = 2 * x + y
        sibling = (x, y, 1 - c)
        chips = [(1 - x, y), (x, 1 - y), (1 - x, 1 - y)]
        half = lambda k, s, cc: outs[k].at[:, s, pl.ds(cc * (placed[k].shape[2] // 2), placed[k].shape[2] // 2), :]
        sent = []
        for j, (px, py) in enumerate(chips):
            for k in range(K):
                own = half(k, s_me, c)
                cp = remote(j * K + k, own, own, (px, py, c))
                cp.start()
                sent.append(cp)
        for j, (px, py) in enumerate(chips):
            s_from = 2 * px + py
            for k in range(K):
                landed = half(k, s_from, c)
                remote(j * K + k, landed, landed, (px, py, c)).wait_recv()
                cp = remote((3 + j) * K + k, landed, landed, sibling)
                cp.start()
                sent.append(cp)
        for j, (px, py) in enumerate(chips):
            s_from = 2 * px + py
            for k in range(K):
                other = half(k, s_from, 1 - c)
                remote((3 + j) * K + k, other, other, sibling).wait_recv()
        for cp in sent:
            cp.wait_send()

    out_shapes = [jax.ShapeDtypeStruct(p.shape, p.dtype) for p in placed]
    return _comm_call("gather_weights", placed, out_shapes, plan, 6 * K, aliases={k: k for k in range(K)})


def _send_other_half(grads):
    K = len(grads)

    def plan(ins, outs, remote):
        x, y, c = _coords()
        cps = []
        for k in range(K):
            rh = grads[k].shape[2] // 2
            cps.append(remote(k, ins[k].at[:, :, pl.ds((1 - c) * rh, rh), :], outs[k], (x, y, 1 - c)))
        for cp in cps:
            cp.start()
        for cp in cps:
            cp.wait()

    out_shapes = [jax.ShapeDtypeStruct(g.shape[:2] + (g.shape[2] // 2, g.shape[3]), g.dtype) for g in grads]
    return _comm_call("send_other_half", grads, out_shapes, plan, K)


def _scatter_chips(parts):
    K = len(parts)

    def plan(ins, outs, remote):
        x, y, c = _coords()
        chips = [(1 - x, y), (x, 1 - y), (1 - x, 1 - y)]
        sent = []
        for j, (px, py) in enumerate(chips):
            for k in range(K):
                cp = remote(j * K + k, ins[k].at[:, 2 * px + py], outs[k].at[:, j], (px, py, c))
                cp.start()
                sent.append(cp)
        for cp in sent:
            cp.wait()

    out_shapes = [jax.ShapeDtypeStruct((p.shape[0], 3) + p.shape[2:], p.dtype) for p in parts]
    return _comm_call("scatter_chips", parts, out_shapes, plan, 3 * K)


def _join_halves(joined):
    K = len(joined)

    def plan(ins, outs, remote):
        x, y, c = _coords()
        cps = []
        for k in range(K):
            rh = joined[k].shape[1] // 2
            mine = outs[k].at[:, pl.ds(c * rh, rh), :]
            cps.append(remote(k, mine, mine, (x, y, 1 - c)))
        for cp in cps:
            cp.start()
        for k in range(K):
            rh = joined[k].shape[1] // 2
            other = outs[k].at[:, pl.ds((1 - c) * rh, rh), :]
            remote(k, other, other, (x, y, 1 - c)).wait_recv()
        for cp in cps:
            cp.wait_send()

    out_shapes = [jax.ShapeDtypeStruct(h.shape, h.dtype) for h in joined]
    return _comm_call("join_halves", joined, out_shapes, plan, K, aliases={k: k for k in range(K)})


def _pack(parts):
    flat = []
    for p in parts:
        v = p.reshape(-1).astype(F32)
        pad = (-v.shape[0]) % 1024
        flat.append(jnp.pad(v, (0, pad)) if pad else v)
    return jnp.concatenate(flat).reshape(-1, 128)


def _unpack(packed, shapes):
    flat = packed.reshape(-1)
    out, off = [], 0
    for shp in shapes:
        n = math.prod(shp)
        out.append(flat[off:off + n].reshape(shp))
        off += n + (-n) % 1024
    return out


def _shard_last(a, s, n):
    return lax.dynamic_slice_in_dim(a, s * n, n, axis=a.ndim - 1)


def _rows8(*vecs):
    n = vecs[0].shape[-1]
    rows = [v.reshape(1, n).astype(F32) for v in vecs]
    return jnp.concatenate(rows + [jnp.zeros((8 - len(rows), n), F32)], axis=0)


def kernel(x, c, ada_w, ada_b, norm_pre, norm_post, ffn_w13, ffn_w2, ab_w_in, a_conv_w, a_conv_b, a_gate_w, a_gate_b, a_lam, b_conv_w, b_conv_b, b_norm_g, b_norm_b, ab_w_out, c_w_in, c_b_in, c_norm_g, c_norm_b, c_w_s, c_b_s, c_w_out, loss_target, m_ada_w, m_ada_b, m_norm_pre, m_norm_post, m_ffn_w13, m_ffn_w2, m_ab_w_in, m_a_conv_w, m_a_conv_b, m_a_gate_w, m_a_gate_b, m_a_lam, m_b_conv_w, m_b_conv_b, m_b_norm_g, m_b_norm_b, m_ab_w_out, m_c_w_in, m_c_b_in, m_c_norm_g, m_c_norm_b, m_c_w_s, m_c_b_s, m_c_w_out, v_ada_w, v_ada_b, v_norm_pre, v_norm_post, v_ffn_w13, v_ffn_w2, v_ab_w_in, v_a_conv_w, v_a_conv_b, v_a_gate_w, v_a_gate_b, v_a_lam, v_b_conv_w, v_b_conv_b, v_b_norm_g, v_b_norm_b, v_ab_w_out, v_c_w_in, v_c_b_in, v_c_norm_g, v_c_norm_b, v_c_w_s, v_c_b_s, v_c_w_out):
    S, D = x.shape[1], x.shape[2]
    W = a_lam.shape[-1]
    Fh = ffn_w13.shape[-1]
    Fq = ffn_w2.shape[2]
    xi, yi, ci = _coords()
    shard = 2 * xi + yi
    me = 4 * xi + 2 * yi + ci
    s_idx = jnp.reshape(shard, (1,)).astype(jnp.int32)
    c_idx = jnp.reshape(ci, (1,)).astype(jnp.int32)
    sc_idx = jnp.stack([shard, ci]).astype(jnp.int32)
    x2, tgt = x[0], loss_target[0]

    sharded_small = [norm_pre, norm_post, a_conv_w, b_conv_w, c_b_in, c_norm_g, c_norm_b]
    gathered = _all_gather8(_pack([c] + sharded_small), "gather_small")
    blocks = gathered.reshape(N_DEV, -1, 128)
    per_dev = [_unpack(blocks[d], [c.shape] + [p.shape for p in sharded_small]) for d in range(0, N_DEV, 2)]
    c_all = jnp.concatenate([_unpack(blocks[d], [c.shape])[0] for d in range(N_DEV)], axis=0)
    npre, npost, acw, bcw, cbin, cng, cnb = [jnp.concatenate([per_dev[s][1 + i] for s in range(N_SHARD)], axis=-1)
                                             for i in range(len(sharded_small))]

    ada_b_my = _shard_last(ada_b, shard, ada_w.shape[-1])[:, None, :]
    modp = _ada_fwd(c_all, ada_w, ada_b_my, "ada_fwd")
    modg = _all_gather8(modp.reshape(16, -1), "gather_mod").reshape(N_DEV, 2, 8, -1)
    mod_me = lax.dynamic_index_in_dim(modg[0::2], me, axis=2, keepdims=False)
    mod = jnp.transpose(mod_me, (1, 0, 2)).reshape(2, 3, 3, D)

    shards = [ffn_w13.reshape(4, D, Fh), ffn_w2.reshape(4, Fq, D), ab_w_in, ab_w_out, c_w_in, c_w_out]
    placed = [_cast_place(w, s_idx, f"cast_place{k}") for k, w in enumerate(shards)]
    w13g, w2g, abin_g, about_g, cin_g, cout_g = _gather_weights(placed)
    w2v = w2g.reshape(4, 2, Fh, D)
    abin4, cin4 = abin_g[0], cin_g[0]
    about, cout = about_g.reshape(D, D), cout_g.reshape(D, D)

    eye = jnp.eye(8, dtype=F32)
    dh_a = W // 8
    blockdiag = lambda w: jnp.einsum("hde,hg->hdge", w, eye).reshape(W, W)
    gw = a_gate_w[0]
    wg = jnp.concatenate([blockdiag(gw[:, :, :dh_a]), blockdiag(gw[:, :, dh_a:])], axis=1).astype(MXU_DT)
    bgv = jnp.concatenate([a_gate_b[0][:, :dh_a].reshape(-1), a_gate_b[0][:, dh_a:].reshape(-1)])
    bg = _rows8(bgv)
    cwa = jnp.concatenate([acw[0], jnp.zeros((4, W), F32)], axis=0)
    cwb = jnp.concatenate([bcw[0], jnp.zeros((1, W), F32)], axis=0)
    v512 = _rows8(a_conv_b[0], a_lam[0], b_conv_b[0], b_norm_g[0], b_norm_b[0])
    dg_b = W // 8
    gid = jnp.arange(W) // dg_b
    pavg = (gid[:, None] == gid[None, :]).astype(F32) / dg_b
    v2d = _rows8(cbin[0])
    v1d = _rows8(cng[0], cnb[0])
    tril = jnp.tril(jnp.ones((CHUNK, CHUNK), dtype=bool))
    ws = jnp.where(tril, c_w_s[0], 0.0).astype(MXU_DT)
    bsb = jnp.repeat(jnp.transpose(c_b_s[0]), D // N_HEAD, axis=1)
    ab_ops = (abin4, cwa, wg, bg, cwb, v512, pavg, about)
    c_ops = (cin4, v2d, v1d, ws, bsb, cout)

    res_ws = (0.5, 1.0, 0.5)
    vps, xs, saved = [], [], []
    xc = x2
    for l in range(2):
        for j in range(3):
            k = 3 * l + j
            vp = _rows8(npre[l, j], mod[l, j, 0], mod[l, j, 1], mod[l, j, 2], npost[l, j])
            vps.append(vp)
            xs.append(xc)
            if j != 1:
                xc, *keep = _ffn_fwd(xc, vp, w13g, w2v, 2 * l + j // 2, res_ws[j], f"ffn_fwd{k}")
            elif l == 0:
                xc, *keep = _mix_ab_fwd(xc, vp, *ab_ops, res_ws[j], "mix_ab_fwd")
            else:
                xc, *keep = _mix_c_fwd(xc, vp, *c_ops, res_ws[j], "mix_c_fwd")
            saved.append(keep)

    dout, lrow = _loss_head(xc, tgt, "loss_head")
    loss = lax.psum(jnp.sum(lrow[0]), ("x", "y", "c"))

    dw13 = lax.empty((4, 4, D, Fh), F32)
    dw2v = lax.empty((4, 2, Fh, D), F32)
    d_npre = [[None] * 3 for _ in range(2)]
    d_npost = [[None] * 3 for _ in range(2)]
    d_mod = [[None] * 3 for _ in range(2)]
    for l in (1, 0):
        for j in (2, 1, 0):
            k = 3 * l + j
            if j != 1:
                f, h, gpre, upre = saved[k]
                gi = 2 * l + j // 2
                df, dh0, s_a, dw13, dw2v = _ffn_bwd_half(0, xs[k], f, dout, vps[k], h, gpre, upre, w13g, w2v, dw13,
                                                          dw2v, gi, res_ws[j], f"ffn_bwd{k}a")
                dout, s_b, dw13, dw2v = _ffn_bwd_half(1, xs[k], f, dout, vps[k], h, gpre, upre, w13g, w2v, dw13, dw2v,
                                                      gi, res_ws[j], f"ffn_bwd{k}b", df=df, dh0=dh0)
                sums = s_a + s_b
            elif l == 0:
                dout, sums, d_abin, d_about, d_wg, d_cwa, d_cwb, d_bg, d_v512 = _mix_ab_bwd(
                    xs[k], saved[k][0], dout, vps[k], *saved[k][1:], *ab_ops, res_ws[j], "mix_ab_bwd")
            else:
                dout, sums, d_cin, d_cout, d_ws, d_bsb, d_v2, d_v1 = _mix_c_bwd(
                    xs[k], saved[k][0], dout, vps[k], saved[k][1], *c_ops, res_ws[j], "mix_c_bwd")
            d_npre[l][j], d_npost[l][j] = sums[4], sums[1]
            d_mod[l][j] = jnp.stack([sums[2], sums[3], sums[0]])
    grad_x = dout[None]

    dmod = jnp.stack([jnp.stack(d_mod[l]) for l in range(2)]).reshape(2, 9 * D)
    d_gate_w = jnp.concatenate([jnp.einsum("hdhe->hde", d_wg[:, :W].reshape(8, dh_a, 8, dh_a)),
                                jnp.einsum("hdhe->hde", d_wg[:, W:].reshape(8, dh_a, 8, dh_a))], axis=-1)
    d_gate_b = jnp.concatenate([d_bg[0, :W].reshape(8, dh_a), d_bg[0, W:].reshape(8, dh_a)], axis=-1)
    small_grads = [
        dmod, jnp.stack([jnp.stack(r) for r in d_npre]), jnp.stack([jnp.stack(r) for r in d_npost]),
        d_cwa[:4][None], d_v512[0][None], d_gate_w[None], d_gate_b[None], d_v512[1][None], d_cwb[:31][None],
        d_v512[2][None], d_v512[3][None], d_v512[4][None], d_v2[0][None], d_v1[0][None], d_v1[1][None],
        jnp.where(tril, d_ws, 0.0)[None], jnp.transpose(d_bsb.reshape(CHUNK, N_HEAD, D // N_HEAD).sum(-1))[None]]
    small_w = [ada_b, norm_pre, norm_post, a_conv_w, a_conv_b, a_gate_w, a_gate_b, a_lam, b_conv_w, b_conv_b,
               b_norm_g, b_norm_b, c_b_in, c_norm_g, c_norm_b, c_w_s, c_b_s]
    small_m = [m_ada_b, m_norm_pre, m_norm_post, m_a_conv_w, m_a_conv_b, m_a_gate_w, m_a_gate_b, m_a_lam, m_b_conv_w,
               m_b_conv_b, m_b_norm_g, m_b_norm_b, m_c_b_in, m_c_norm_g, m_c_norm_b, m_c_w_s, m_c_b_s]
    small_v = [v_ada_b, v_norm_pre, v_norm_post, v_a_conv_w, v_a_conv_b, v_a_gate_w, v_a_gate_b, v_a_lam, v_b_conv_w,
               v_b_conv_b, v_b_norm_g, v_b_norm_b, v_c_b_in, v_c_norm_g, v_c_norm_b, v_c_w_s, v_c_b_s]
    full_shapes = [g.shape for g in small_grads]
    sg_all = _all_gather8(_pack(small_grads), "gather_small_grads").reshape(N_DEV, -1, 128)
    sg_sum = _sum_slots(sg_all[None], "sum_small_grads")[0]
    g_full = _unpack(sg_sum, full_shapes)
    g_small = [g if g.shape == w.shape else _shard_last(g, shard, w.shape[-1]) for g, w in zip(g_full, small_w)]
    small_shapes = [w.shape for w in small_w]
    d_s, m_s, v_s = _adamw(_pack(small_w), _pack(g_small), _pack(small_m), _pack(small_v), "adamw_small")
    delta_small, newm_small, newv_small = (_unpack(a, small_shapes) for a in (d_s, m_s, v_s))

    dmod_all = jnp.stack([_unpack(sg_all[d], full_shapes[:1])[0] for d in range(N_DEV)], axis=1)
    n_ada = ada_w.shape[-1]
    g_ada_w = _ada_bwd(c_all, _shard_last(dmod_all, shard, n_ada), "ada_bwd")

    grads = [dw13, dw2v.reshape(4, 4, Fq, D), d_abin[None], d_about.reshape(1, 4, D // 4, D), d_cin[None],
             d_cout.reshape(1, 4, D // 4, D)]
    landed = _send_other_half(grads)
    chip_sums = [_add_half(g, la, c_idx, f"add_half{k}") for k, (g, la) in enumerate(zip(grads, landed))]
    slots = _scatter_chips(chip_sums)
    reduced = [_sum_chips(p, sl, sc_idx, f"sum_chips{k}") for k, (p, sl) in enumerate(zip(chip_sums, slots))]
    g_big = _join_halves(reduced)

    big_w = [ffn_w13, ffn_w2, ab_w_in, ab_w_out, c_w_in, c_w_out, ada_w]
    big_m = [m_ffn_w13, m_ffn_w2, m_ab_w_in, m_ab_w_out, m_c_w_in, m_c_w_out, m_ada_w]
    big_v = [v_ffn_w13, v_ffn_w2, v_ab_w_in, v_ab_w_out, v_c_w_in, v_c_w_out, v_ada_w]
    big_g = [g.reshape(w.shape) for g, w in zip(list(g_big) + [g_ada_w], big_w)]
    big_out = []
    for k, (w, g, m, v) in enumerate(zip(big_w, big_g, big_m, big_v)):
        two_d = lambda a: a.reshape(-1, a.shape[-1])
        res = _adamw(two_d(w), two_d(g), two_d(m), two_d(v), f"adamw_big{k}")
        big_out.append([r.reshape(w.shape) for r in res])

    names = ["ada_w", "ada_b", "norm_pre", "norm_post", "ffn_w13", "ffn_w2", "ab_w_in", "a_conv_w", "a_conv_b",
             "a_gate_w", "a_gate_b", "a_lam", "b_conv_w", "b_conv_b", "b_norm_g", "b_norm_b", "ab_w_out", "c_w_in",
             "c_b_in", "c_norm_g", "c_norm_b", "c_w_s", "c_b_s", "c_w_out"]
    big_names = ["ffn_w13", "ffn_w2", "ab_w_in", "ab_w_out", "c_w_in", "c_w_out", "ada_w"]
    small_names = ["ada_b", "norm_pre", "norm_post", "a_conv_w", "a_conv_b", "a_gate_w", "a_gate_b", "a_lam",
                   "b_conv_w", "b_conv_b", "b_norm_g", "b_norm_b", "c_b_in", "c_norm_g", "c_norm_b", "c_w_s", "c_b_s"]
    table = {}
    for k, n in enumerate(big_names):
        table[n] = (big_g[k], *big_out[k])
    for k, n in enumerate(small_names):
        table[n] = (g_small[k], delta_small[k], newm_small[k], newv_small[k])
    outs = [loss, grad_x]
    for field in range(4):
        outs += [table[n][field] for n in names]
    return tuple(outs)
```

```python
import functools
import math

import jax
import jax.numpy as jnp
from jax import lax
from jax.experimental import pallas as pl
from jax.experimental.pallas import tpu as pltpu

F32 = jnp.float32
MXU_DT = jnp.bfloat16
EPS = 1e-6
LRU_C = 8.0
N_SHARD = 4
N_DEV = 8
CHUNK = 128
N_HEAD = 8
ADAM_LR, ADAM_B1, ADAM_B2, ADAM_EPS, ADAM_WD, ADAM_STEP = 0.001, 0.9, 0.999, 1e-08, 0.01, 10
GELU_K0 = math.sqrt(2.0 / math.pi)
GELU_K1 = 0.044715
VMEM_LIMIT = 58 * 1024 * 1024
MESH = pl.DeviceIdType.MESH
ANY = pl.BlockSpec(memory_space=pl.ANY)


def _cp(sem=None, **kw):
    if sem is not None:
        kw["dimension_semantics"] = sem
    return pltpu.CompilerParams(vmem_limit_bytes=VMEM_LIMIT, **kw)


def _resident(a):
    return pl.BlockSpec(a.shape, lambda *_: (0,) * a.ndim, pipeline_mode=pl.Buffered(1))


def _whole(shape):
    return pl.BlockSpec(shape, lambda *_: (0,) * len(shape))


def _dot(a, b):
    return jnp.dot(a.astype(MXU_DT), b.astype(MXU_DT), preferred_element_type=F32)


def _dot_nt(a, b):
    return lax.dot_general(a.astype(MXU_DT), b.astype(MXU_DT), (((1,), (1,)), ((), ())), preferred_element_type=F32)


def _dot_tn(a, b):
    return lax.dot_general(a.astype(MXU_DT), b.astype(MXU_DT), (((0,), (0,)), ((), ())), preferred_element_type=F32)


def _dot_hi(a, b):
    return jnp.dot(a, b, precision=lax.Precision.HIGHEST, preferred_element_type=F32)


def _sig(x):
    return 1.0 / (1.0 + jnp.exp(-x))


def _logsig(x):
    return jnp.minimum(x, 0.0) - jnp.log(1.0 + jnp.exp(-jnp.abs(x)))


def _gelu(x):
    x2 = x * x
    t = jnp.tanh(GELU_K0 * (x + GELU_K1 * x * x2))
    val = 0.5 * x * (1.0 + t)
    der = 0.5 * (1.0 + t) + 0.5 * x * (1.0 - t * t) * (GELU_K0 * (1.0 + 3.0 * GELU_K1 * x2))
    return val, der


def _neg_expm1(x):
    small = -(x * (1.0 + x * (0.5 + x * (1.0 / 6.0 + x * (1.0 / 24.0)))))
    return jnp.where(x > -0.05, small, 1.0 - jnp.exp(x))


def _colsum(v):
    return jnp.sum(v, axis=0, keepdims=True)


def _rowmean(v):
    return jnp.mean(v, axis=-1, keepdims=True)


def _copy_out(pairs, sem):
    for src, dst in pairs:
        cp = pltpu.make_async_copy(src, dst, sem)
        cp.start()
        cp.wait()


class _Carry:
    def __init__(self, srcs, out_shapes, aliases, n, plan):
        self.srcs, self.out_shapes, self.aliases, self.n, self.plan = list(srcs), list(out_shapes), aliases, n, plan


def _carry_args(carry, n_in, n_out):
    if carry is None:
        return [], [], [], [], {}
    sems = [pltpu.SemaphoreType.DMA((carry.n,)), pltpu.SemaphoreType.DMA((carry.n,))]
    aliases = {n_in + i: n_out + o for i, o in carry.aliases.items()}
    return [ANY] * len(carry.srcs), [ANY] * len(carry.out_shapes), carry.out_shapes, sems, aliases


def _split_refs(refs, n_in, n_out, n_scratch, carry):
    nci, nco = (len(carry.srcs), len(carry.out_shapes)) if carry is not None else (0, 0)
    cuts = [n_in, nci, n_out, nco, n_scratch]
    parts, i = [], 0
    for n in cuts:
        parts.append(refs[i:i + n])
        i += n
    ins, cins, outs, couts, scr = parts
    return ins, outs, scr, (cins, couts, refs[i:])


def _carry_run(carry, carry_refs, first, last):
    if carry is None:
        return
    cins, couts, (send_sems, recv_sems) = carry_refs

    def remote(k, src, dst, to):
        return pltpu.make_async_remote_copy(src_ref=src, dst_ref=dst, send_sem=send_sems.at[k],
                                            recv_sem=recv_sems.at[k], device_id=to, device_id_type=MESH)

    @pl.when(first)
    def _():
        for cp in carry.plan(cins, couts, remote):
            cp.start()

    @pl.when(last)
    def _():
        for cp in carry.plan(cins, couts, remote):
            cp.wait()


def _shift_down(v, k):
    return v if k == 0 else pltpu.roll(v, k, 0)


def _shift_up(v, k):
    return v if k == 0 else pltpu.roll(v, v.shape[0] - k, 0)


def _shell_pre(xv, vp_ref):
    r = lax.rsqrt(_rowmean(xv * xv) + EPS)
    return xv * r * (vp_ref[0:1, :] * (1.0 + vp_ref[2:3, :])) + vp_ref[1:2, :]


def _shell_post(xv, fv, vp_ref, res_w):
    r = lax.rsqrt(_rowmean(fv * fv) + EPS)
    return xv + fv * r * (res_w * (1.0 + vp_ref[3:4, :]) * vp_ref[4:5, :])


def _shell_post_bwd(fv, dov, vp_ref, res_w, s_ref):
    r = lax.rsqrt(_rowmean(fv * fv) + EPS)
    fn = fv * r
    pg = vp_ref[4:5, :]
    dy = (res_w * (1.0 + vp_ref[3:4, :])) * dov
    if s_ref is not None:
        s_ref[0:1, :] += _colsum(res_w * fn * pg * dov)
        s_ref[1:2, :] += _colsum(fn * dy)
    q = dy * pg
    return r * (q - fn * _rowmean(fn * q))


def _shell_pre_bwd(xv, dh, dov, vp_ref, s_ref):
    r = lax.rsqrt(_rowmean(xv * xv) + EPS)
    xn = xv * r
    pg = vp_ref[0:1, :]
    sc1 = 1.0 + vp_ref[2:3, :]
    s_ref[2:3, :] += _colsum(dh)
    s_ref[3:4, :] += _colsum(xn * pg * dh)
    s_ref[4:5, :] += _colsum(xn * dh * sc1)
    q = dh * (sc1 * pg)
    return dov + r * (q - xn * _rowmean(xn * q))


def _loss_head(y, tgt, name, tm=512):
    S, D = y.shape

    def body(y_ref, t_ref, dy_ref, l_ref):
        @pl.when(pl.program_id(0) == 0)
        def _():
            l_ref[...] = jnp.zeros_like(l_ref)

        e = y_ref[...] - t_ref[...]
        dy_ref[...] = e * (1.0 / D)
        l_ref[0:1, :] += _colsum(e * e) * (0.5 / D)

    return pl.pallas_call(
        body, name=name, grid=(S // tm,),
        in_specs=[pl.BlockSpec((tm, D), lambda t: (t, 0)), pl.BlockSpec((tm, D), lambda t: (t, 0))],
        out_specs=[pl.BlockSpec((tm, D), lambda t: (t, 0)), pl.BlockSpec((8, D), lambda t: (0, 0))],
        out_shape=[jax.ShapeDtypeStruct((S, D), F32), jax.ShapeDtypeStruct((8, D), F32)],
        compiler_params=_cp(("arbitrary",)),
    )(y, tgt)


def _ffn_fwd(x, vp, w13g, w2v, res_w, name, carry=None, tm=256):
    S, D = x.shape
    Fh = w13g.shape[-1]
    T = S // tm

    def body(*refs):
        (x_ref, vp_ref, w13_ref, w2_ref), (xo_ref, f_ref, h_ref, g_ref, u_ref), _, carry_refs = _split_refs(
            refs, 4, 5, 0, carry)
        _carry_run(carry, carry_refs, pl.program_id(0) == 0, pl.program_id(0) == T - 1)
        xv = x_ref[...]
        hb = _shell_pre(xv, vp_ref).astype(MXU_DT)
        h_ref[...] = hb
        acc = None
        for j in range(2):
            gg = _dot(hb, w13_ref[j])
            uu = _dot(hb, w13_ref[2 + j])
            g_ref[:, j * Fh:(j + 1) * Fh] = gg.astype(g_ref.dtype)
            u_ref[:, j * Fh:(j + 1) * Fh] = uu.astype(u_ref.dtype)
            part = _dot(gg * _sig(gg) * uu, w2_ref[j])
            acc = part if acc is None else acc + part
        f_ref[...] = acc
        xo_ref[...] = _shell_post(xv, acc, vp_ref, res_w)

    tile = lambda w: pl.BlockSpec((tm, w), lambda t: (t, 0))
    c_in, c_out, c_shapes, c_sems, c_alias = _carry_args(carry, 4, 5)
    return pl.pallas_call(
        body, name=name, grid=(T,),
        in_specs=[tile(D), _whole((8, D)),
                  pl.BlockSpec((None, 4, D, Fh), lambda t: (0, 0, 0, 0), pipeline_mode=pl.Buffered(1)),
                  pl.BlockSpec((None, 2, Fh, D), lambda t: (0, 0, 0, 0), pipeline_mode=pl.Buffered(1))] + c_in,
        out_specs=[tile(D), tile(D), tile(D), tile(2 * Fh), tile(2 * Fh)] + c_out,
        out_shape=[jax.ShapeDtypeStruct((S, D), F32), jax.ShapeDtypeStruct((S, D), F32),
                   jax.ShapeDtypeStruct((S, D), MXU_DT), jax.ShapeDtypeStruct((S, 2 * Fh), MXU_DT),
                   jax.ShapeDtypeStruct((S, 2 * Fh), MXU_DT)] + c_shapes,
        scratch_shapes=c_sems, input_output_aliases=c_alias,
        compiler_params=_cp(("arbitrary",), has_side_effects=carry is not None),
    )(x, vp, w13g, w2v, *(carry.srcs if carry is not None else []))


def _ffn_bwd_half(j, x, f, dout, vp, h, gpre, upre, w13g, w2v, dw13, dw2v, res_w, name, df=None, dh0=None, carry=None,
                  tm=256):
    S, D = h.shape
    Fh = w13g.shape[-1]
    T = S // tm
    first = j == 0
    n_in, n_out = (11, 5) if first else (13, 4)

    def body(*refs):
        ins, outs, (a1, a3, a2, sem), carry_refs = _split_refs(refs, n_in, n_out, 4, carry)
        if first:
            f_ref, do_ref, vp_ref, h_ref, g_ref, u_ref, w1_ref, w3_ref, w2_ref, _, _ = ins
            df_ref, dh_ref, s_ref, dw13_ref, dw2_ref = outs
        else:
            x_ref, do_ref, vp_ref, h_ref, g_ref, u_ref, w1_ref, w3_ref, w2_ref, dfi_ref, dh0_ref, _, _ = ins
            dx_ref, s_ref, dw13_ref, dw2_ref = outs
        t = pl.program_id(0)
        _carry_run(carry, carry_refs, t == 0, t == T - 1)

        @pl.when(t == 0)
        def _():
            for ref in (a1, a3, a2, s_ref):
                ref[...] = jnp.zeros_like(ref)

        hv = h_ref[...]
        if first:
            dfv = _shell_post_bwd(f_ref[...], do_ref[...], vp_ref, res_w, s_ref).astype(MXU_DT)
            df_ref[...] = dfv
        else:
            dfv = dfi_ref[...]
        gg = g_ref[...].astype(F32)
        uu = u_ref[...].astype(F32)
        sg = _sig(gg)
        si = gg * sg
        da = _dot_nt(dfv, w2_ref[...])
        a2[...] += _dot_tn(si * uu, dfv)
        dg = da * uu * (sg * (1.0 + gg * (1.0 - sg)))
        du = da * si
        a1[...] += _dot_tn(hv, dg)
        a3[...] += _dot_tn(hv, du)
        dh = _dot_nt(dg, w1_ref[...]) + _dot_nt(du, w3_ref[...])
        if first:
            dh_ref[...] = dh
        else:
            dx_ref[...] = _shell_pre_bwd(x_ref[...], dh0_ref[...] + dh, do_ref[...], vp_ref, s_ref)

        @pl.when(t == T - 1)
        def _():
            _copy_out(((a1, dw13_ref.at[0, j]), (a3, dw13_ref.at[0, 2 + j]), (a2, dw2_ref.at[0, j])), sem)

    tile = lambda w: pl.BlockSpec((tm, w), lambda t: (t, 0))
    half = pl.BlockSpec((tm, Fh), lambda t: (t, j))
    weights = [pl.BlockSpec((None, None, D, Fh), lambda t: (0, j, 0, 0), pipeline_mode=pl.Buffered(1)),
               pl.BlockSpec((None, None, D, Fh), lambda t: (0, 2 + j, 0, 0), pipeline_mode=pl.Buffered(1)),
               pl.BlockSpec((None, None, Fh, D), lambda t: (0, j, 0, 0), pipeline_mode=pl.Buffered(1))]
    sd = jax.ShapeDtypeStruct
    grads = [sd(dw13.shape, F32), sd(dw2v.shape, F32)]
    scratch = [pltpu.VMEM((D, Fh), F32), pltpu.VMEM((D, Fh), F32), pltpu.VMEM((Fh, D), F32), pltpu.SemaphoreType.DMA]
    c_in, c_out, c_shapes, c_sems, c_alias = _carry_args(carry, n_in, n_out)
    params = _cp(("arbitrary",), has_side_effects=carry is not None)
    extra = carry.srcs if carry is not None else []
    if first:
        return pl.pallas_call(
            body, name=name, grid=(T,),
            in_specs=[tile(D), tile(D), _whole((8, D)), tile(D), half, half] + weights + [ANY, ANY] + c_in,
            out_specs=[tile(D), tile(D), _whole((8, D)), ANY, ANY] + c_out,
            out_shape=[sd((S, D), MXU_DT), sd((S, D), F32), sd((8, D), F32)] + grads + c_shapes,
            scratch_shapes=scratch + c_sems, input_output_aliases={9: 3, 10: 4, **c_alias}, compiler_params=params,
        )(f, dout, vp, h, gpre, upre, w13g, w13g, w2v, dw13, dw2v, *extra)
    return pl.pallas_call(
        body, name=name, grid=(T,),
        in_specs=[tile(D), tile(D), _whole((8, D)), tile(D), half, half] + weights + [tile(D), tile(D), ANY, ANY] + c_in,
        out_specs=[tile(D), _whole((8, D)), ANY, ANY] + c_out,
        out_shape=[sd((S, D), F32), sd((8, D), F32)] + grads + c_shapes,
        scratch_shapes=scratch + c_sems, input_output_aliases={11: 2, 12: 3, **c_alias}, compiler_params=params,
    )(x, dout, vp, h, gpre, upre, w13g, w13g, w2v, df, dh0, dw13, dw2v, *extra)


def _scan_fwd(a, u, rows):
    n = a.shape[0]
    d = 1
    while d < n:
        m = rows >= d
        u = u + a * jnp.where(m, _shift_down(u, d), 0.0)
        a = a * jnp.where(m, _shift_down(a, d), 1.0)
        d *= 2
    return a, u


def _scan_bwd(a, u, rows):
    n = a.shape[0]
    d = 1
    while d < n:
        m = rows < n - d
        u = u + a * jnp.where(m, _shift_up(u, d), 0.0)
        a = a * jnp.where(m, _shift_up(a, d), 1.0)
        d *= 2
    return a, u


def _causal_conv(ext, w_ref, K, halo, tm):
    acc = None
    for k in range(K):
        term = w_ref[k:k + 1, :] * _shift_down(ext, K - 1 - k)[halo:, :]
        acc = term if acc is None else acc + term
    return acc


def _anticausal_conv(ext, w_ref, K, tm):
    acc = None
    for k in range(K):
        term = w_ref[k:k + 1, :] * _shift_up(ext, K - 1 - k)[:tm, :]
        acc = term if acc is None else acc + term
    return acc


def _group_norm(vc, pavg, g, b):
    mu = _dot_hi(vc, pavg)
    dv = vc - mu
    rstd = lax.rsqrt(_dot_hi(dv * dv, pavg) + EPS)
    vhat = dv * rstd
    return vhat, rstd, vhat * g + b


def _lru_gates(axc, wg_ref, bg_ref, lam, W):
    gp = _dot(axc, wg_ref[...]) + bg_ref[0:1, :]
    r = _sig(gp[:, :W])
    i = _sig(gp[:, W:])
    ls = _logsig(lam)
    L = (LRU_C * ls) * r
    a = jnp.exp(L)
    mult = jnp.sqrt(_neg_expm1(2.0 * L))
    return r, i, ls, a, mult


def _mix_ab_fwd(x, vp, win4, cwa, wg, bg, cwb, v512, pavg, wout, res_w, name, tm=256):
    S, D = x.shape
    W = win4.shape[-1]
    KA, KB, HA, HB = 4, 31, 8, 32

    def body(x_ref, vp_ref, win_ref, cwa_ref, wg_ref, bg_ref, cwb_ref, v_ref, p_ref, wo_ref,
             xo_ref, f_ref, h_ref, hs_ref, axp_ref, axc_ref, bv_ref, vc_ref, ahalo, bhalo, carry):
        @pl.when(pl.program_id(0) == 0)
        def _():
            ahalo[...] = jnp.zeros_like(ahalo)
            bhalo[...] = jnp.zeros_like(bhalo)
            carry[...] = jnp.zeros_like(carry)

        xv = x_ref[...]
        hv = _shell_pre(xv, vp_ref).astype(MXU_DT)
        h_ref[...] = hv
        a_gate = _dot(hv, win_ref[0])
        axp = _dot(hv, win_ref[1])
        b_val = _dot(hv, win_ref[2])
        b_gate = _dot(hv, win_ref[3])
        rows = lax.broadcasted_iota(jnp.int32, (tm, W), 0)
        axc = _causal_conv(jnp.concatenate([ahalo[...], axp], axis=0), cwa_ref, KA, HA, tm) + v_ref[0:1, :]
        ahalo[...] = axp[tm - HA:, :]
        r, i, ls, a, mult = _lru_gates(axc, wg_ref, bg_ref, v_ref[1:2, :], W)
        acum, hloc = _scan_fwd(a, mult * i * axc, rows)
        hs = hloc + acum * carry[7:8, :]
        carry[...] = hs[tm - 8:, :]
        ya = hs * _gelu(a_gate)[0]
        bv = b_val * _sig(b_gate)
        vc = _causal_conv(jnp.concatenate([bhalo[...], bv], axis=0), cwb_ref, KB, HB, tm) + v_ref[2:3, :]
        bhalo[...] = bv[tm - HB:, :]
        _, _, vn = _group_norm(vc, p_ref[...], v_ref[3:4, :], v_ref[4:5, :])
        yb = vn * _sig(vn)
        fv = _dot(ya, wo_ref[0:W, :]) + _dot(yb, wo_ref[W:, :])
        f_ref[...] = fv
        xo_ref[...] = _shell_post(xv, fv, vp_ref, res_w)
        hs_ref[...] = hs
        axp_ref[...] = axp
        axc_ref[...] = axc
        bv_ref[...] = bv
        vc_ref[...] = vc

    tile = lambda w: pl.BlockSpec((tm, w), lambda t: (t, 0))
    sd = jax.ShapeDtypeStruct
    return pl.pallas_call(
        body, name=name, grid=(S // tm,),
        in_specs=[tile(D), _whole((8, D))] + [_resident(a) for a in (win4, cwa, wg, bg, cwb, v512, pavg, wout)],
        out_specs=[tile(D), tile(D), tile(D)] + [tile(W)] * 5,
        out_shape=[sd((S, D), F32), sd((S, D), F32), sd((S, D), MXU_DT)] + [sd((S, W), F32)] * 5,
        scratch_shapes=[pltpu.VMEM((HA, W), F32), pltpu.VMEM((HB, W), F32), pltpu.VMEM((8, W), F32)],
        compiler_params=_cp(("arbitrary",)),
    )(x, vp, win4, cwa, wg, bg, cwb, v512, pavg, wout)


def _mix_ab_bwd(x, f, dout, vp, h, hs, axp, axc, bv, vc, win4, cwa, wg, bg, cwb, v512, pavg, wout, res_w, name,
                carry=None, tm=256):
    S, D = x.shape
    W = win4.shape[-1]
    T = S // tm
    KA, KB, HA, HB = 4, 31, 8, 32

    def body(*refs):
        ins, outs, scr, carry_refs = _split_refs(refs, 21, 9, 8, carry)
        (x_ref, f_ref, do_ref, vp_ref, h_ref, hs_ref, hsp_ref, axp_ref, axpp_ref, axc_ref, bv_ref, bvp_ref, vc_ref,
         win_ref, cwa_ref, wg_ref, bg_ref, cwb_ref, v_ref, p_ref, wo_ref) = ins
        dx_ref, s_ref, dwin_out, dwo_out, dwg_out, dcwa_ref, dcwb_ref, dbg_ref, dv_ref = outs
        danext, dvnext, gfirst, afirst, dwin_ref, dwo_ref, dwg_ref, sem = scr
        t = pl.program_id(0)
        _carry_run(carry, carry_refs, t == 0, t == T - 1)

        @pl.when(t == 0)
        def _():
            for ref in (s_ref, dwin_ref, dwo_ref, dwg_ref, dcwa_ref, dcwb_ref, dbg_ref, dv_ref, danext, dvnext,
                        gfirst, afirst):
                ref[...] = jnp.zeros_like(ref)

        notfirst = jnp.where(t < T - 1, 1.0, 0.0).astype(F32)
        hv = h_ref[...]
        dov = do_ref[...]
        dfv = _shell_post_bwd(f_ref[...], dov, vp_ref, res_w, s_ref).astype(MXU_DT)
        a_gate = _dot(hv, win_ref[0])
        b_val = _dot(hv, win_ref[2])
        b_gate = _dot(hv, win_ref[3])
        rows = lax.broadcasted_iota(jnp.int32, (tm, W), 0)
        ge, dge = _gelu(a_gate)
        hsv = hs_ref[...]
        ya = hsv * ge
        vhat, rstd, vn = _group_norm(vc_ref[...], p_ref[...], v_ref[3:4, :], v_ref[4:5, :])
        sgn = _sig(vn)
        yb = vn * sgn
        dma = _dot_nt(dfv, wo_ref[0:W, :])
        dmb = _dot_nt(dfv, wo_ref[W:, :])
        dwo_ref[0:W, :] += _dot_tn(ya, dfv)
        dwo_ref[W:, :] += _dot_tn(yb, dfv)
        dhs = dma * ge
        d_a_gate = dma * hsv * dge
        axcv = axc_ref[...]
        lam = v_ref[1:2, :]
        r, i, ls, a, mult = _lru_gates(axcv, wg_ref, bg_ref, lam, W)
        ash = jnp.where(rows == tm - 1, afirst[0:1, :], _shift_up(a, 1))
        asuf, gloc = _scan_bwd(ash, dhs, rows)
        gsc = gloc + asuf * gfirst[0:1, :]
        afirst[...] = a[0:8, :]
        gfirst[...] = gsc[0:8, :]
        hprev = jnp.where(rows == 0, hsp_ref[HA - 1:HA, :] * notfirst, _shift_down(hsv, 1))
        da = gsc * hprev
        dL = da * a - gsc * (i * axcv) * (a * a) / mult
        dix = gsc * mult
        daxc = dix * i
        dr = dL * (LRU_C * ls)
        dv_ref[1:2, :] += _colsum(dL * r) * (LRU_C * _sig(-lam))
        dgate = jnp.concatenate([dr * r * (1.0 - r), (dix * axcv) * i * (1.0 - i)], axis=1)
        dbg_ref[0:1, :] += _colsum(dgate)
        dwg_ref[...] += _dot_tn(axcv, dgate)
        daxc = daxc + _dot_nt(dgate, wg_ref[...])
        daxp = _anticausal_conv(jnp.concatenate([daxc, danext[...]], axis=0), cwa_ref, KA, tm)
        ext = jnp.concatenate([axpp_ref[...] * notfirst, axp_ref[...]], axis=0)
        for k in range(KA):
            dcwa_ref[k:k + 1, :] += _colsum(daxc * _shift_down(ext, KA - 1 - k)[HA:, :])
        dv_ref[0:1, :] += _colsum(daxc)
        danext[...] = daxc[0:HA, :]
        dvn = dmb * (sgn * (1.0 + vn * (1.0 - sgn)))
        dv_ref[4:5, :] += _colsum(dvn)
        dv_ref[3:4, :] += _colsum(dvn * vhat)
        dvh = dvn * v_ref[3:4, :]
        dvc = rstd * (dvh - _dot_hi(dvh, p_ref[...]) - vhat * _dot_hi(dvh * vhat, p_ref[...]))
        dbv = _anticausal_conv(jnp.concatenate([dvc, dvnext[...]], axis=0), cwb_ref, KB, tm)
        ext = jnp.concatenate([bvp_ref[...] * notfirst, bv_ref[...]], axis=0)
        for k in range(KB):
            dcwb_ref[k:k + 1, :] += _colsum(dvc * _shift_down(ext, KB - 1 - k)[HB:, :])
        dv_ref[2:3, :] += _colsum(dvc)
        dvnext[...] = dvc[0:HB, :]
        sb = _sig(b_gate)
        dzs = (d_a_gate, daxp, dbv * sb, dbv * b_val * sb * (1.0 - sb))
        dh = None
        for s in range(4):
            part = _dot_nt(dzs[s], win_ref[s])
            dh = part if dh is None else dh + part
            dwin_ref[s] += _dot_tn(hv, dzs[s])
        dx_ref[...] = _shell_pre_bwd(x_ref[...], dh, dov, vp_ref, s_ref)

        @pl.when(t == T - 1)
        def _():
            _copy_out(((dwin_ref, dwin_out), (dwo_ref, dwo_out), (dwg_ref, dwg_out)), sem)

    tile = lambda w: pl.BlockSpec((tm, w), lambda t: (T - 1 - t, 0))
    prev = lambda hh: pl.BlockSpec((hh, W), lambda t: (jnp.maximum((T - 1 - t) * (tm // hh) - 1, 0), 0))
    out_shapes = [(S, D), (8, D), win4.shape, wout.shape, wg.shape, (8, W), (32, W), (8, 2 * W), (8, W)]
    c_in, c_out, c_shapes, c_sems, c_alias = _carry_args(carry, 21, 9)
    return pl.pallas_call(
        body, name=name, grid=(T,),
        in_specs=[tile(D), tile(D), tile(D), _whole((8, D)), tile(D),
                  tile(W), prev(HA), tile(W), prev(HA), tile(W), tile(W), prev(HB), tile(W)]
        + [_resident(a) for a in (win4, cwa, wg, bg, cwb, v512, pavg, wout)] + c_in,
        out_specs=[tile(D), _whole((8, D)), ANY, ANY, ANY] + [_whole(s) for s in out_shapes[5:]] + c_out,
        out_shape=[jax.ShapeDtypeStruct(s, F32) for s in out_shapes] + c_shapes,
        scratch_shapes=[pltpu.VMEM((HA, W), F32), pltpu.VMEM((HB, W), F32), pltpu.VMEM((8, W), F32),
                        pltpu.VMEM((8, W), F32), pltpu.VMEM(win4.shape, F32), pltpu.VMEM(wout.shape, F32),
                        pltpu.VMEM(wg.shape, F32), pltpu.SemaphoreType.DMA] + c_sems,
        input_output_aliases=c_alias, compiler_params=_cp(("arbitrary",), has_side_effects=carry is not None),
    )(x, f, dout, vp, h, hs, hs, axp, axp, axc, bv, bv, vc, win4, cwa, wg, bg, cwb, v512, pavg, wout,
      *(carry.srcs if carry is not None else []))


def _mix_c_core(hv, win_ref, v2_ref, v1_ref, ws_ref, bsb_ref, tm, D):
    zp = jnp.concatenate([_dot(hv, win_ref[s]) for s in range(4)], axis=1) + v2_ref[0:1, :]
    z, dz = _gelu(zp)
    u, v = z[:, :D], z[:, D:]
    mu = _rowmean(v)
    dv = v - mu
    rstd = lax.rsqrt(_rowmean(dv * dv) + EPS)
    vhat = dv * rstd
    vn = vhat * v1_ref[0:1, :] + v1_ref[1:2, :]
    rows_out = []
    for cidx in range(tm // CHUNK):
        blk = vn[cidx * CHUNK:(cidx + 1) * CHUNK, :]
        heads = [_dot(ws_ref[hd], blk[:, hd * CHUNK:(hd + 1) * CHUNK]) for hd in range(N_HEAD)]
        rows_out.append(jnp.concatenate(heads, axis=1) + bsb_ref[...])
    mixed = jnp.concatenate(rows_out, axis=0)
    return dz, u, rstd, vhat, vn, mixed


def _mix_c_fwd(x, vp, win4, v2d, v1d, ws, bsb, wout, res_w, name, tm=256):
    S, D = x.shape

    def body(x_ref, vp_ref, win_ref, v2_ref, v1_ref, ws_ref, bsb_ref, wo_ref, xo_ref, f_ref, h_ref):
        xv = x_ref[...]
        hv = _shell_pre(xv, vp_ref).astype(MXU_DT)
        h_ref[...] = hv
        _, u, _, _, _, mixed = _mix_c_core(hv, win_ref, v2_ref, v1_ref, ws_ref, bsb_ref, tm, D)
        fv = _dot(u * mixed, wo_ref[...])
        f_ref[...] = fv
        xo_ref[...] = _shell_post(xv, fv, vp_ref, res_w)

    tile = pl.BlockSpec((tm, D), lambda t: (t, 0))
    sd = jax.ShapeDtypeStruct
    return pl.pallas_call(
        body, name=name, grid=(S // tm,),
        in_specs=[tile, _whole((8, D))] + [_resident(a) for a in (win4, v2d, v1d, ws, bsb, wout)],
        out_specs=[tile, tile, tile],
        out_shape=[sd((S, D), F32), sd((S, D), F32), sd((S, D), MXU_DT)], compiler_params=_cp(("arbitrary",)),
    )(x, vp, win4, v2d, v1d, ws, bsb, wout)


def _mix_c_bwd(x, f, dout, vp, h, win4, v2d, v1d, ws, bsb, wout, res_w, name, carry=None, tm=256):
    S, D = x.shape

    def body(*refs):
        ins, outs, (dwin_ref, dwo_ref, sem), carry_refs = _split_refs(refs, 11, 8, 3, carry)
        x_ref, f_ref, do_ref, vp_ref, h_ref, win_ref, v2_ref, v1_ref, ws_ref, bsb_ref, wo_ref = ins
        dx_ref, s_ref, dwin_out, dwo_out, dws_ref, dbsb_ref, dv2_ref, dv1_ref = outs
        _carry_run(carry, carry_refs, pl.program_id(0) == 0, pl.program_id(0) == S // tm - 1)

        @pl.when(pl.program_id(0) == 0)
        def _():
            for ref in (s_ref, dwin_ref, dwo_ref, dws_ref, dbsb_ref, dv2_ref, dv1_ref):
                ref[...] = jnp.zeros_like(ref)

        hv = h_ref[...]
        dov = do_ref[...]
        dfv = _shell_post_bwd(f_ref[...], dov, vp_ref, res_w, s_ref).astype(MXU_DT)
        dz, u, rstd, vhat, vn, mixed = _mix_c_core(hv, win_ref, v2_ref, v1_ref, ws_ref, bsb_ref, tm, D)
        dp = _dot_nt(dfv, wo_ref[...])
        dwo_ref[...] += _dot_tn(u * mixed, dfv)
        du = dp * mixed
        dmx = dp * u
        rows_out = []
        for cidx in range(tm // CHUNK):
            dblk = dmx[cidx * CHUNK:(cidx + 1) * CHUNK, :]
            vblk = vn[cidx * CHUNK:(cidx + 1) * CHUNK, :]
            dbsb_ref[...] += dblk
            heads = []
            for hd in range(N_HEAD):
                dsl = dblk[:, hd * CHUNK:(hd + 1) * CHUNK]
                heads.append(_dot_tn(ws_ref[hd], dsl))
                dws_ref[hd] += _dot_nt(dsl, vblk[:, hd * CHUNK:(hd + 1) * CHUNK])
            rows_out.append(jnp.concatenate(heads, axis=1))
        dvn = jnp.concatenate(rows_out, axis=0)
        dv1_ref[1:2, :] += _colsum(dvn)
        dv1_ref[0:1, :] += _colsum(dvn * vhat)
        dvh = dvn * v1_ref[0:1, :]
        dv = rstd * (dvh - _rowmean(dvh) - vhat * _rowmean(dvh * vhat))
        dzp = jnp.concatenate([du, dv], axis=1) * dz
        dv2_ref[0:1, :] += _colsum(dzp)
        W = win_ref.shape[-1]
        dh = None
        for s in range(4):
            dzs = dzp[:, s * W:(s + 1) * W]
            part = _dot_nt(dzs, win_ref[s])
            dh = part if dh is None else dh + part
            dwin_ref[s] += _dot_tn(hv, dzs)
        dx_ref[...] = _shell_pre_bwd(x_ref[...], dh, dov, vp_ref, s_ref)

        @pl.when(pl.program_id(0) == S // tm - 1)
        def _():
            _copy_out(((dwin_ref, dwin_out), (dwo_ref, dwo_out)), sem)

    tile = pl.BlockSpec((tm, D), lambda t: (t, 0))
    out_shapes = [(S, D), (8, D), win4.shape, wout.shape, ws.shape, bsb.shape, (8, 2 * D), (8, D)]
    c_in, c_out, c_shapes, c_sems, c_alias = _carry_args(carry, 11, 8)
    return pl.pallas_call(
        body, name=name, grid=(S // tm,),
        in_specs=[tile, tile, tile, _whole((8, D)), tile] + [_resident(a) for a in (win4, v2d, v1d, ws, bsb, wout)]
        + c_in,
        out_specs=[tile, _whole((8, D)), ANY, ANY] + [_whole(s) for s in out_shapes[4:]] + c_out,
        out_shape=[jax.ShapeDtypeStruct(s, F32) for s in out_shapes] + c_shapes,
        scratch_shapes=[pltpu.VMEM(win4.shape, F32), pltpu.VMEM(wout.shape, F32), pltpu.SemaphoreType.DMA] + c_sems,
        input_output_aliases=c_alias, compiler_params=_cp(("arbitrary",), has_side_effects=carry is not None),
    )(x, f, dout, vp, h, win4, v2d, v1d, ws, bsb, wout, *(carry.srcs if carry is not None else []))


def _ada_fwd(c_all, ada_w, ada_b_my, name):
    L, D, N = ada_w.shape
    tn = N // 3

    def body(c_ref, w_ref, b_ref, o_ref):
        cv = c_ref[...]
        o_ref[...] = _dot_hi(cv * _sig(cv), w_ref[...]) + b_ref[...]

    return pl.pallas_call(
        body, name=name, grid=(L, 3),
        in_specs=[pl.BlockSpec((8, D), lambda l, n: (0, 0)), pl.BlockSpec((None, D, tn), lambda l, n: (l, 0, n)),
                  pl.BlockSpec((None, 1, tn), lambda l, n: (l, 0, n))],
        out_specs=pl.BlockSpec((None, 8, tn), lambda l, n: (l, 0, n)),
        out_shape=jax.ShapeDtypeStruct((L, 8, N), F32), compiler_params=_cp(("arbitrary", "arbitrary")),
    )(c_all, ada_w, ada_b_my)


def _ada_bwd(c_all, dmod_my, name):
    L, _, N = dmod_my.shape
    D = c_all.shape[1]
    tn = N // 3

    def body(c_ref, d_ref, o_ref):
        cv = c_ref[...]
        o_ref[...] = lax.dot_general(cv * _sig(cv), d_ref[...], (((0,), (0,)), ((), ())),
                                     precision=lax.Precision.HIGHEST, preferred_element_type=F32)

    return pl.pallas_call(
        body, name=name, grid=(L, 3),
        in_specs=[pl.BlockSpec((8, D), lambda l, n: (0, 0)), pl.BlockSpec((None, 8, tn), lambda l, n: (l, 0, n))],
        out_specs=pl.BlockSpec((None, D, tn), lambda l, n: (l, 0, n)),
        out_shape=jax.ShapeDtypeStruct((L, D, N), F32), compiler_params=_cp(("arbitrary", "arbitrary")),
    )(c_all, dmod_my)


def _row_tile(rows, cols, budget=1 << 20):
    best = 8
    for rt in range(8, rows + 1, 8):
        if rows % rt == 0 and rt * cols * 4 <= budget:
            best = rt
    return best


def _cast_place(w, g, s_idx, name):
    _, R, C = w.shape
    rt = _row_tile(R, C)

    def body(s_ref, w_ref, o_ref):
        o_ref[...] = w_ref[...].astype(o_ref.dtype)

    return pl.pallas_call(
        body, name=name,
        grid_spec=pltpu.PrefetchScalarGridSpec(
            num_scalar_prefetch=1, grid=(R // rt,),
            in_specs=[pl.BlockSpec((None, rt, C), lambda r, s_ref: (g, r, 0))],
            out_specs=pl.BlockSpec((None, None, rt, C), lambda r, s_ref: (0, s_ref[0], r, 0))),
        out_shape=jax.ShapeDtypeStruct((1, N_SHARD, R, C), MXU_DT), compiler_params=_cp(("arbitrary",)),
    )(s_idx, w)


def _add_half(gk, la, c_idx, name):
    Gk, _, R, C = gk.shape
    Rh = R // 2
    n = Gk * N_SHARD
    gv = gk.reshape(n, 2, Rh, C)
    lv = la.reshape(n, Rh, C)
    rt = _row_tile(Rh, C)

    def body(c_ref, g_ref, l_ref, o_ref):
        o_ref[...] = g_ref[...] + l_ref[...]

    out = pl.pallas_call(
        body, name=name,
        grid_spec=pltpu.PrefetchScalarGridSpec(
            num_scalar_prefetch=1, grid=(n, Rh // rt),
            in_specs=[pl.BlockSpec((None, None, rt, C), lambda i, r, c_ref: (i, c_ref[0], r, 0)),
                      pl.BlockSpec((None, rt, C), lambda i, r, c_ref: (i, r, 0))],
            out_specs=pl.BlockSpec((None, rt, C), lambda i, r, c_ref: (i, r, 0))),
        out_shape=jax.ShapeDtypeStruct((n, Rh, C), F32), compiler_params=_cp(("arbitrary", "arbitrary")),
    )(c_idx, gv, lv)
    return out.reshape(Gk, N_SHARD, Rh, C)


def _sum_chips(part, landed, sc_idx, joined, g, name):
    _, _, Rh, C = part.shape
    rt = _row_tile(Rh, C)
    nb = Rh // rt

    def body(i_ref, p_ref, l_ref, j_ref, o_ref):
        o_ref[...] = ((p_ref[...] + l_ref[0]) + l_ref[1]) + l_ref[2]

    return pl.pallas_call(
        body, name=name,
        grid_spec=pltpu.PrefetchScalarGridSpec(
            num_scalar_prefetch=1, grid=(nb,),
            in_specs=[pl.BlockSpec((None, None, rt, C), lambda r, i_ref: (0, i_ref[0], r, 0)),
                      pl.BlockSpec((None, 3, rt, C), lambda r, i_ref: (0, 0, r, 0)), ANY],
            out_specs=pl.BlockSpec((None, rt, C), lambda r, i_ref: (g, i_ref[1] * nb + r, 0))),
        out_shape=jax.ShapeDtypeStruct(joined.shape, F32), input_output_aliases={3: 0},
        compiler_params=_cp(("arbitrary",)),
    )(sc_idx, part, landed, joined)


def _sum_slots(lb, name):
    Gk, n, Rh, C = lb.shape
    rt = _row_tile(Rh, C)

    def body(l_ref, o_ref):
        acc = l_ref[0]
        for s in range(1, n):
            acc = acc + l_ref[s]
        o_ref[...] = acc

    return pl.pallas_call(
        body, name=name, grid=(Gk, Rh // rt),
        in_specs=[pl.BlockSpec((None, n, rt, C), lambda g, r: (g, 0, r, 0))],
        out_specs=pl.BlockSpec((None, rt, C), lambda g, r: (g, r, 0)),
        out_shape=jax.ShapeDtypeStruct((Gk, Rh, C), F32), compiler_params=_cp(("arbitrary", "arbitrary")),
    )(lb)


def _adamw(w, g, m, v, name):
    rows, cols = w.shape
    rt = _row_tile(rows, cols) if rows % 8 == 0 else rows
    c1 = 1.0 - ADAM_B1 ** ADAM_STEP
    c2 = 1.0 - ADAM_B2 ** ADAM_STEP

    def body(w_ref, g_ref, m_ref, v_ref, d_ref, mo_ref, vo_ref):
        gv = g_ref[...]
        mn = ADAM_B1 * m_ref[...] + (1.0 - ADAM_B1) * gv
        vn = ADAM_B2 * v_ref[...] + (1.0 - ADAM_B2) * (gv * gv)
        d_ref[...] = -ADAM_LR * ((mn / c1) / (jnp.sqrt(vn / c2) + ADAM_EPS) + ADAM_WD * w_ref[...])
        mo_ref[...] = mn
        vo_ref[...] = vn

    spec = pl.BlockSpec((rt, cols), lambda r: (r, 0))
    sds = jax.ShapeDtypeStruct((rows, cols), F32)
    return pl.pallas_call(
        body, name=name, grid=(rows // rt,), in_specs=[spec] * 4, out_specs=[spec] * 3,
        out_shape=[sds] * 3, compiler_params=_cp(("arbitrary",)),
    )(w, g, m, v)


def _coords():
    return lax.axis_index("x"), lax.axis_index("y"), lax.axis_index("c")


def _all_gather8(blk, name):
    m_per, n = blk.shape

    def body(x_ref, out_ref, send_sems, recv_sems, local_sem):
        x, y, c = _coords()
        me, sibling = (x, y, c), (x, y, 1 - c)
        chips = [(1 - x, y), (x, 1 - y), (1 - x, 1 - y)]

        def rows(px, py, pc):
            return out_ref.at[pl.ds((4 * px + 2 * py + pc) * m_per, m_per), :]

        def copy(k, block, to, src=None):
            return pltpu.make_async_remote_copy(
                src_ref=rows(*block) if src is None else src, dst_ref=rows(*block),
                send_sem=send_sems.at[k], recv_sem=recv_sems.at[k], device_id=to, device_id_type=MESH)

        mine = pltpu.make_async_copy(x_ref, rows(*me), local_sem)
        mine.start()
        first = [copy(0, me, sibling, src=x_ref)]
        first += [copy(1 + j, me, (*chip, c), src=x_ref) for j, chip in enumerate(chips)]
        for cp in first:
            cp.start()
        passed = [copy(4 + j, (*chip, c), sibling) for j, chip in enumerate(chips)]
        for j, chip in enumerate(chips):
            copy(1 + j, (*chip, c), me).wait_recv()
            passed[j].start()
        copy(0, sibling, me).wait_recv()
        for j, chip in enumerate(chips):
            copy(4 + j, (*chip, 1 - c), me).wait_recv()
        for cp in first + passed:
            cp.wait_send()
        mine.wait()

    return pl.pallas_call(
        body, name=name, out_shape=jax.ShapeDtypeStruct((N_DEV * m_per, n), blk.dtype),
        in_specs=[pl.BlockSpec(memory_space=pltpu.VMEM)], out_specs=pl.BlockSpec(memory_space=pltpu.VMEM),
        scratch_shapes=[pltpu.SemaphoreType.DMA((7,)), pltpu.SemaphoreType.DMA((7,)), pltpu.SemaphoreType.DMA],
        compiler_params=_cp(),
    )(blk)


def _comm_call(name, inputs, out_shapes, plan, n_remote, aliases=None):
    n_in, n_out = len(inputs), len(out_shapes)

    def body(*refs):
        in_refs, out_refs = refs[:n_in], refs[n_in:n_in + n_out]
        send_sems, recv_sems = refs[n_in + n_out:]

        def remote(k, src, dst, to):
            return pltpu.make_async_remote_copy(src_ref=src, dst_ref=dst, send_sem=send_sems.at[k],
                                                recv_sem=recv_sems.at[k], device_id=to, device_id_type=MESH)

        plan(in_refs, out_refs, remote)

    return pl.pallas_call(
        body, name=name, out_shape=out_shapes, in_specs=[ANY] * n_in, out_specs=[ANY] * n_out,
        scratch_shapes=[pltpu.SemaphoreType.DMA((n_remote,)), pltpu.SemaphoreType.DMA((n_remote,))],
        input_output_aliases=aliases or {}, compiler_params=_cp(has_side_effects=True),
    )(*inputs)


def _gather_ici_carry(placed):
    K = len(placed)

    def plan(ins, outs, remote):
        x, y, c = _coords()
        s_me = 2 * x + y
        cps = []
        for j, (px, py) in enumerate([(1 - x, y), (x, 1 - y), (1 - x, 1 - y)]):
            for k in range(K):
                rh = placed[k].shape[2] // 2
                own = outs[k].at[:, s_me, pl.ds(c * rh, rh), :]
                cps.append(remote(j * K + k, own, own, (px, py, c)))
        return cps

    shapes = [jax.ShapeDtypeStruct(p.shape, p.dtype) for p in placed]
    return _Carry(placed, shapes, {k: k for k in range(K)}, 3 * K, plan)


def _forward_sibling(placed, name):
    K = len(placed)

    def plan(ins, outs, remote):
        x, y, c = _coords()
        cps = []
        for j, (px, py) in enumerate([(1 - x, y), (x, 1 - y), (1 - x, 1 - y)]):
            for k in range(K):
                rh = placed[k].shape[2] // 2
                landed = outs[k].at[:, 2 * px + py, pl.ds(c * rh, rh), :]
                cps.append(remote(j * K + k, landed, landed, (x, y, 1 - c)))
        for cp in cps:
            cp.start()
        for j, (px, py) in enumerate([(1 - x, y), (x, 1 - y), (1 - x, 1 - y)]):
            for k in range(K):
                rh = placed[k].shape[2] // 2
                other = outs[k].at[:, 2 * px + py, pl.ds((1 - c) * rh, rh), :]
                remote(j * K + k, other, other, (x, y, 1 - c)).wait_recv()
        for cp in cps:
            cp.wait_send()

    out_shapes = [jax.ShapeDtypeStruct(p.shape, p.dtype) for p in placed]
    return _comm_call(name, placed, out_shapes, plan, 3 * K, aliases={k: k for k in range(K)})


def _gather_weights(placed, name):
    K = len(placed)

    def plan(ins, outs, remote):
        x, y, c = _coords()
        s_me = 2 * x + y
        sibling = (x, y, 1 - c)
        chips = [(1 - x, y), (x, 1 - y), (1 - x, 1 - y)]
        half = lambda k, s, cc: outs[k].at[:, s, pl.ds(cc * (placed[k].shape[2] // 2), placed[k].shape[2] // 2), :]
        sent = []
        for j, (px, py) in enumerate(chips):
            for k in range(K):
                own = half(k, s_me, c)
                cp = remote(j * K + k, own, own, (px, py, c))
                cp.start()
                sent.append(cp)
        for j, (px, py) in enumerate(chips):
            s_from = 2 * px + py
            for k in range(K):
                landed = half(k, s_from, c)
                remote(j * K + k, landed, landed, (px, py, c)).wait_recv()
                cp = remote((3 + j) * K + k, landed, landed, sibling)
                cp.start()
                sent.append(cp)
        for j, (px, py) in enumerate(chips):
            s_from = 2 * px + py
            for k in range(K):
                other = half(k, s_from, 1 - c)
                remote((3 + j) * K + k, other, other, sibling).wait_recv()
        for cp in sent:
            cp.wait_send()

    out_shapes = [jax.ShapeDtypeStruct(p.shape, p.dtype) for p in placed]
    return _comm_call(name, placed, out_shapes, plan, 6 * K, aliases={k: k for k in range(K)})


def _send_other_half(grads, name):
    K = len(grads)

    def plan(ins, outs, remote):
        x, y, c = _coords()
        cps = []
        for k in range(K):
            rh = grads[k].shape[2] // 2
            cps.append(remote(k, ins[k].at[:, :, pl.ds((1 - c) * rh, rh), :], outs[k], (x, y, 1 - c)))
        for cp in cps:
            cp.start()
        for cp in cps:
            cp.wait()

    out_shapes = [jax.ShapeDtypeStruct(g.shape[:2] + (g.shape[2] // 2, g.shape[3]), g.dtype) for g in grads]
    return _comm_call(name, grads, out_shapes, plan, K)


def _scatter_carry(parts):
    K = len(parts)

    def plan(ins, outs, remote):
        x, y, c = _coords()
        chips = [(1 - x, y), (x, 1 - y), (1 - x, 1 - y)]
        return [remote(j * K + k, ins[k].at[:, 2 * px + py], outs[k].at[:, j], (px, py, c))
                for j, (px, py) in enumerate(chips) for k in range(K)]

    out_shapes = [jax.ShapeDtypeStruct((p.shape[0], 3) + p.shape[2:], p.dtype) for p in parts]
    return _Carry(parts, out_shapes, {}, 3 * K, plan)


def _scatter_chips(parts, name):
    carry = _scatter_carry(parts)

    def plan(ins, outs, remote):
        cps = carry.plan(ins, outs, remote)
        for cp in cps:
            cp.start()
        for cp in cps:
            cp.wait()

    return _comm_call(name, parts, carry.out_shapes, plan, carry.n)


def _join_halves(joined):
    K = len(joined)

    def plan(ins, outs, remote):
        x, y, c = _coords()
        cps = []
        for k in range(K):
            rh = joined[k].shape[1] // 2
            mine = outs[k].at[:, pl.ds(c * rh, rh), :]
            cps.append(remote(k, mine, mine, (x, y, 1 - c)))
        for cp in cps:
            cp.start()
        for k in range(K):
            rh = joined[k].shape[1] // 2
            other = outs[k].at[:, pl.ds((1 - c) * rh, rh), :]
            remote(k, other, other, (x, y, 1 - c)).wait_recv()
        for cp in cps:
            cp.wait_send()

    out_shapes = [jax.ShapeDtypeStruct(h.shape, h.dtype) for h in joined]
    return _comm_call("join_halves", joined, out_shapes, plan, K, aliases={k: k for k in range(K)})


def _pack(parts):
    flat = []
    for p in parts:
        v = p.reshape(-1).astype(F32)
        pad = (-v.shape[0]) % 1024
        flat.append(jnp.pad(v, (0, pad)) if pad else v)
    return jnp.concatenate(flat).reshape(-1, 128)


def _unpack(packed, shapes):
    flat = packed.reshape(-1)
    out, off = [], 0
    for shp in shapes:
        n = math.prod(shp)
        out.append(flat[off:off + n].reshape(shp))
        off += n + (-n) % 1024
    return out


def _shard_last(a, s, n):
    return lax.dynamic_slice_in_dim(a, s * n, n, axis=a.ndim - 1)


def _rows8(*vecs):
    n = vecs[0].shape[-1]
    rows = [v.reshape(1, n).astype(F32) for v in vecs]
    return jnp.concatenate(rows + [jnp.zeros((8 - len(rows), n), F32)], axis=0)


def kernel(x, c, ada_w, ada_b, norm_pre, norm_post, ffn_w13, ffn_w2, ab_w_in, a_conv_w, a_conv_b, a_gate_w, a_gate_b, a_lam, b_conv_w, b_conv_b, b_norm_g, b_norm_b, ab_w_out, c_w_in, c_b_in, c_norm_g, c_norm_b, c_w_s, c_b_s, c_w_out, loss_target, m_ada_w, m_ada_b, m_norm_pre, m_norm_post, m_ffn_w13, m_ffn_w2, m_ab_w_in, m_a_conv_w, m_a_conv_b, m_a_gate_w, m_a_gate_b, m_a_lam, m_b_conv_w, m_b_conv_b, m_b_norm_g, m_b_norm_b, m_ab_w_out, m_c_w_in, m_c_b_in, m_c_norm_g, m_c_norm_b, m_c_w_s, m_c_b_s, m_c_w_out, v_ada_w, v_ada_b, v_norm_pre, v_norm_post, v_ffn_w13, v_ffn_w2, v_ab_w_in, v_a_conv_w, v_a_conv_b, v_a_gate_w, v_a_gate_b, v_a_lam, v_b_conv_w, v_b_conv_b, v_b_norm_g, v_b_norm_b, v_ab_w_out, v_c_w_in, v_c_b_in, v_c_norm_g, v_c_norm_b, v_c_w_s, v_c_b_s, v_c_w_out):
    S, D = x.shape[1], x.shape[2]
    W = a_lam.shape[-1]
    Fh = ffn_w13.shape[-1]
    Fq = ffn_w2.shape[2]
    xi, yi, ci = _coords()
    shard = 2 * xi + yi
    me = 4 * xi + 2 * yi + ci
    s_idx = jnp.reshape(shard, (1,)).astype(jnp.int32)
    c_idx = jnp.reshape(ci, (1,)).astype(jnp.int32)
    sc_idx = jnp.stack([shard, ci]).astype(jnp.int32)
    x2, tgt = x[0], loss_target[0]

    sharded_small = [norm_pre, norm_post, a_conv_w, b_conv_w, c_b_in, c_norm_g, c_norm_b]
    gathered = _all_gather8(_pack([c] + sharded_small), "gather_small")
    blocks = gathered.reshape(N_DEV, -1, 128)
    per_dev = [_unpack(blocks[d], [c.shape] + [p.shape for p in sharded_small]) for d in range(0, N_DEV, 2)]
    c_all = jnp.concatenate([_unpack(blocks[d], [c.shape])[0] for d in range(N_DEV)], axis=0)
    npre, npost, acw, bcw, cbin, cng, cnb = [jnp.concatenate([per_dev[s][1 + i] for s in range(N_SHARD)], axis=-1)
                                             for i in range(len(sharded_small))]

    ada_b_my = _shard_last(ada_b, shard, ada_w.shape[-1])[:, None, :]
    modp = _ada_fwd(c_all, ada_w, ada_b_my, "ada_fwd")
    modg = _all_gather8(modp.reshape(16, -1), "gather_mod").reshape(N_DEV, 2, 8, -1)
    mod_me = lax.dynamic_index_in_dim(modg[0::2], me, axis=2, keepdims=False)
    mod = jnp.transpose(mod_me, (1, 0, 2)).reshape(2, 3, 3, D)

    w13s, w2s = ffn_w13.reshape(4, D, Fh), ffn_w2.reshape(4, Fq, D)
    placed13 = [_cast_place(w13s, g, s_idx, f"cast_w13_{g}") for g in range(4)]
    placed2 = [_cast_place(w2s, g, s_idx, f"cast_w2_{g}") for g in range(4)]
    placed_mix = [_cast_place(w, 0, s_idx, f"cast_mix{k}") for k, w in enumerate((ab_w_in, ab_w_out, c_w_in, c_w_out))]
    w13_first, w2_first = _gather_weights([placed13[0], placed2[0]], "gather_first")
    gather_carry = _gather_ici_carry(placed13[1:] + placed2[1:] + placed_mix)

    eye = jnp.eye(8, dtype=F32)
    dh_a = W // 8
    blockdiag = lambda w: jnp.einsum("hde,hg->hdge", w, eye).reshape(W, W)
    gw = a_gate_w[0]
    wg = jnp.concatenate([blockdiag(gw[:, :, :dh_a]), blockdiag(gw[:, :, dh_a:])], axis=1).astype(MXU_DT)
    bgv = jnp.concatenate([a_gate_b[0][:, :dh_a].reshape(-1), a_gate_b[0][:, dh_a:].reshape(-1)])
    bg = _rows8(bgv)
    cwa = jnp.concatenate([acw[0], jnp.zeros((4, W), F32)], axis=0)
    cwb = jnp.concatenate([bcw[0], jnp.zeros((1, W), F32)], axis=0)
    v512 = _rows8(a_conv_b[0], a_lam[0], b_conv_b[0], b_norm_g[0], b_norm_b[0])
    dg_b = W // 8
    gid = jnp.arange(W) // dg_b
    pavg = (gid[:, None] == gid[None, :]).astype(F32) / dg_b
    v2d = _rows8(cbin[0])
    v1d = _rows8(cng[0], cnb[0])
    tril = jnp.tril(jnp.ones((CHUNK, CHUNK), dtype=bool))
    ws = jnp.where(tril, c_w_s[0], 0.0).astype(MXU_DT)
    bsb = jnp.repeat(jnp.transpose(c_b_s[0]), D // N_HEAD, axis=1)

    res_ws = (0.5, 1.0, 0.5)
    vps, xs, saved = [], [], []
    xc = x2
    w13g, w2v = [w13_first], [w2_first.reshape(1, 2, Fh, D)]
    for l in range(2):
        for j in range(3):
            k = 3 * l + j
            vp = _rows8(npre[l, j], mod[l, j, 0], mod[l, j, 1], mod[l, j, 2], npost[l, j])
            vps.append(vp)
            xs.append(xc)
            gi = 2 * l + j // 2
            if k == 0:
                xc, *keep = _ffn_fwd(xc, vp, w13g[0], w2v[0], res_ws[j], "ffn_fwd0", carry=gather_carry)
                keep, landed = keep[:4], keep[4:]
                full = _forward_sibling(landed, "forward_sibling")
                w13g += list(full[0:3])
                w2v += [w.reshape(1, 2, Fh, D) for w in full[3:6]]
                abin_g, about_g, cin_g, cout_g = full[6:]
                ab_ops = (abin_g[0], cwa, wg, bg, cwb, v512, pavg, about_g.reshape(D, D))
                c_ops = (cin_g[0], v2d, v1d, ws, bsb, cout_g.reshape(D, D))
            elif j != 1:
                xc, *keep = _ffn_fwd(xc, vp, w13g[gi], w2v[gi], res_ws[j], f"ffn_fwd{k}")
            elif l == 0:
                xc, *keep = _mix_ab_fwd(xc, vp, *ab_ops, res_ws[j], "mix_ab_fwd")
            else:
                xc, *keep = _mix_c_fwd(xc, vp, *c_ops, res_ws[j], "mix_c_fwd")
            saved.append(keep)

    dout, lrow = _loss_head(xc, tgt, "loss_head")

    joined = {"w13": lax.empty((4, D, Fh), F32), "w2": lax.empty((4, Fq, D), F32), "abin": lax.empty((1, D, 4 * W // 4), F32),
              "about": lax.empty((1, D // 4, D), F32), "cin": lax.empty((1, D, 2 * D // 4), F32),
              "cout": lax.empty((1, D // 4, D), F32)}

    def start_reduce(grads, keys, g, tag):
        got = _send_other_half(grads, f"send_half_{tag}")
        parts = [_add_half(gr, la, c_idx, f"add_half_{tag}{i}") for i, (gr, la) in enumerate(zip(grads, got))]
        return parts, keys, g

    def finish_reduce(pending, landed_slots):
        parts, keys, g = pending
        for part, key, slots in zip(parts, keys, landed_slots):
            joined[key] = _sum_chips(part, slots, sc_idx, joined[key], g, f"sum_chips_{key}{g}")

    pending = None
    d_npre = [[None] * 3 for _ in range(2)]
    d_npost = [[None] * 3 for _ in range(2)]
    d_mod = [[None] * 3 for _ in range(2)]
    for l in (1, 0):
        for j in (2, 1, 0):
            k = 3 * l + j
            carry = _scatter_carry(pending[0]) if pending is not None else None
            n_land = len(pending[0]) if pending is not None else 0
            if j != 1:
                f, h, gpre, upre = saved[k]
                gi = 2 * l + j // 2
                dw13 = lax.empty((1, 4, D, Fh), F32)
                dw2v = lax.empty((1, 2, Fh, D), F32)
                df, dh0, s_a, dw13, dw2v, *slots = _ffn_bwd_half(
                    0, xs[k], f, dout, vps[k], h, gpre, upre, w13g[gi], w2v[gi], dw13, dw2v, res_ws[j],
                    f"ffn_bwd{k}a", carry=carry)
                dout, s_b, dw13, dw2v = _ffn_bwd_half(
                    1, xs[k], f, dout, vps[k], h, gpre, upre, w13g[gi], w2v[gi], dw13, dw2v, res_ws[j],
                    f"ffn_bwd{k}b", df=df, dh0=dh0)
                sums = s_a + s_b
                mine = ([dw13, dw2v.reshape(1, 4, Fq, D)], ["w13", "w2"], gi, f"ffn{gi}")
            elif l == 0:
                dout, sums, d_abin, d_about, d_wg, d_cwa, d_cwb, d_bg, d_v512, *slots = _mix_ab_bwd(
                    xs[k], saved[k][0], dout, vps[k], *saved[k][1:], *ab_ops, res_ws[j], "mix_ab_bwd", carry=carry)
                mine = ([d_abin[None], d_about.reshape(1, 4, D // 4, D)], ["abin", "about"], 0, "ab")
            else:
                dout, sums, d_cin, d_cout, d_ws, d_bsb, d_v2, d_v1, *slots = _mix_c_bwd(
                    xs[k], saved[k][0], dout, vps[k], saved[k][1], *c_ops, res_ws[j], "mix_c_bwd", carry=carry)
                mine = ([d_cin[None], d_cout.reshape(1, 4, D // 4, D)], ["cin", "cout"], 0, "c")
            if pending is not None:
                finish_reduce(pending, slots[:n_land])
            pending = start_reduce(*mine)
            d_npre[l][j], d_npost[l][j] = sums[4], sums[1]
            d_mod[l][j] = jnp.stack([sums[2], sums[3], sums[0]])
    finish_reduce(pending, _scatter_chips(pending[0], "scatter_last"))
    grad_x = dout[None]

    dmod = jnp.stack([jnp.stack(d_mod[l]) for l in range(2)]).reshape(2, 9 * D)
    d_gate_w = jnp.concatenate([jnp.einsum("hdhe->hde", d_wg[:, :W].reshape(8, dh_a, 8, dh_a)),
                                jnp.einsum("hdhe->hde", d_wg[:, W:].reshape(8, dh_a, 8, dh_a))], axis=-1)
    d_gate_b = jnp.concatenate([d_bg[0, :W].reshape(8, dh_a), d_bg[0, W:].reshape(8, dh_a)], axis=-1)
    small_grads = [
        dmod, jnp.stack([jnp.stack(r) for r in d_npre]), jnp.stack([jnp.stack(r) for r in d_npost]),
        d_cwa[:4][None], d_v512[0][None], d_gate_w[None], d_gate_b[None], d_v512[1][None], d_cwb[:31][None],
        d_v512[2][None], d_v512[3][None], d_v512[4][None], d_v2[0][None], d_v1[0][None], d_v1[1][None],
        jnp.where(tril, d_ws, 0.0)[None], jnp.transpose(d_bsb.reshape(CHUNK, N_HEAD, D // N_HEAD).sum(-1))[None]]
    small_w = [ada_b, norm_pre, norm_post, a_conv_w, a_conv_b, a_gate_w, a_gate_b, a_lam, b_conv_w, b_conv_b,
               b_norm_g, b_norm_b, c_b_in, c_norm_g, c_norm_b, c_w_s, c_b_s]
    small_m = [m_ada_b, m_norm_pre, m_norm_post, m_a_conv_w, m_a_conv_b, m_a_gate_w, m_a_gate_b, m_a_lam, m_b_conv_w,
               m_b_conv_b, m_b_norm_g, m_b_norm_b, m_c_b_in, m_c_norm_g, m_c_norm_b, m_c_w_s, m_c_b_s]
    small_v = [v_ada_b, v_norm_pre, v_norm_post, v_a_conv_w, v_a_conv_b, v_a_gate_w, v_a_gate_b, v_a_lam, v_b_conv_w,
               v_b_conv_b, v_b_norm_g, v_b_norm_b, v_c_b_in, v_c_norm_g, v_c_norm_b, v_c_w_s, v_c_b_s]
    full_shapes = [g.shape for g in small_grads]
    loss_part = jnp.sum(lrow[0]).reshape(1, 1)
    sg_all = _all_gather8(_pack(small_grads + [loss_part]), "gather_small_grads").reshape(N_DEV, -1, 128)
    sg_sum = _sum_slots(sg_all[None], "sum_small_grads")[0]
    *g_full, loss_sum = _unpack(sg_sum, full_shapes + [(1, 1)])
    loss = loss_sum[0, 0]
    g_small = [g if g.shape == w.shape else _shard_last(g, shard, w.shape[-1]) for g, w in zip(g_full, small_w)]
    small_shapes = [w.shape for w in small_w]
    d_s, m_s, v_s = _adamw(_pack(small_w), _pack(g_small), _pack(small_m), _pack(small_v), "adamw_small")
    delta_small, newm_small, newv_small = (_unpack(a, small_shapes) for a in (d_s, m_s, v_s))

    dmod_all = jnp.stack([_unpack(sg_all[d], full_shapes[:1])[0] for d in range(N_DEV)], axis=1)
    n_ada = ada_w.shape[-1]
    g_ada_w = _ada_bwd(c_all, _shard_last(dmod_all, shard, n_ada), "ada_bwd")

    g_big = _join_halves([joined[key] for key in ("w13", "w2", "abin", "about", "cin", "cout")])

    big_w = [ffn_w13, ffn_w2, ab_w_in, ab_w_out, c_w_in, c_w_out, ada_w]
    big_m = [m_ffn_w13, m_ffn_w2, m_ab_w_in, m_ab_w_out, m_c_w_in, m_c_w_out, m_ada_w]
    big_v = [v_ffn_w13, v_ffn_w2, v_ab_w_in, v_ab_w_out, v_c_w_in, v_c_w_out, v_ada_w]
    big_g = [g.reshape(w.shape) for g, w in zip(list(g_big) + [g_ada_w], big_w)]
    big_out = []
    for k, (w, g, m, v) in enumerate(zip(big_w, big_g, big_m, big_v)):
        two_d = lambda a: a.reshape(-1, a.shape[-1])
        res = _adamw(two_d(w), two_d(g), two_d(m), two_d(v), f"adamw_big{k}")
        big_out.append([r.reshape(w.shape) for r in res])

    names = ["ada_w", "ada_b", "norm_pre", "norm_post", "ffn_w13", "ffn_w2", "ab_w_in", "a_conv_w", "a_conv_b",
             "a_gate_w", "a_gate_b", "a_lam", "b_conv_w", "b_conv_b", "b_norm_g", "b_norm_b", "ab_w_out", "c_w_in",
             "c_b_in", "c_norm_g", "c_norm_b", "c_w_s", "c_b_s", "c_w_out"]
    big_names = ["ffn_w13", "ffn_w2", "ab_w_in", "ab_w_out", "c_w_in", "c_w_out", "ada_w"]
    small_names = ["ada_b", "norm_pre", "norm_post", "a_conv_w", "a_conv_b", "a_gate_w", "a_gate_b", "a_lam",
                   "b_conv_w", "b_conv_b", "b_norm_g", "b_norm_b", "c_b_in", "c_norm_g", "c_norm_b", "c_w_s", "c_b_s"]
    table = {}
    for k, n in enumerate(big_names):
        table[n] = (big_g[k], *big_out[k])
    for k, n in enumerate(small_names):
        table[n] = (g_small[k], delta_small[k], newm_small[k], newv_small[k])
    outs = [loss, grad_x]
    for field in range(4):
        outs += [table[n][field] for n in names]
    return tuple(outs)
```

```python
import functools
import math

import jax
import jax.numpy as jnp
from jax import lax
from jax.experimental import pallas as pl
from jax.experimental.pallas import tpu as pltpu

F32 = jnp.float32
MXU_DT = jnp.bfloat16
EPS = 1e-6
LRU_C = 8.0
N_SHARD = 4
N_DEV = 8
CHUNK = 128
N_HEAD = 8
ADAM_LR, ADAM_B1, ADAM_B2, ADAM_EPS, ADAM_WD, ADAM_STEP = 0.001, 0.9, 0.999, 1e-08, 0.01, 10
GELU_K0 = math.sqrt(2.0 / math.pi)
GELU_K1 = 0.044715
VMEM_LIMIT = 58 * 1024 * 1024
MESH = pl.DeviceIdType.MESH
ANY = pl.BlockSpec(memory_space=pl.ANY)


def _cp(sem=None, **kw):
    if sem is not None:
        kw["dimension_semantics"] = sem
    return pltpu.CompilerParams(vmem_limit_bytes=VMEM_LIMIT, **kw)


def _resident(a):
    return pl.BlockSpec(a.shape, lambda *_: (0,) * a.ndim, pipeline_mode=pl.Buffered(1))


def _whole(shape):
    return pl.BlockSpec(shape, lambda *_: (0,) * len(shape))


def _dot(a, b):
    return jnp.dot(a.astype(MXU_DT), b.astype(MXU_DT), preferred_element_type=F32)


def _dot_nt(a, b):
    return lax.dot_general(a.astype(MXU_DT), b.astype(MXU_DT), (((1,), (1,)), ((), ())), preferred_element_type=F32)


def _dot_tn(a, b):
    return lax.dot_general(a.astype(MXU_DT), b.astype(MXU_DT), (((0,), (0,)), ((), ())), preferred_element_type=F32)


def _dot_hi(a, b):
    return jnp.dot(a, b, precision=lax.Precision.HIGHEST, preferred_element_type=F32)


def _sig(x):
    return 1.0 / (1.0 + jnp.exp(-x))


def _logsig(x):
    return jnp.minimum(x, 0.0) - jnp.log(1.0 + jnp.exp(-jnp.abs(x)))


def _gelu(x):
    x2 = x * x
    t = jnp.tanh(GELU_K0 * (x + GELU_K1 * x * x2))
    val = 0.5 * x * (1.0 + t)
    der = 0.5 * (1.0 + t) + 0.5 * x * (1.0 - t * t) * (GELU_K0 * (1.0 + 3.0 * GELU_K1 * x2))
    return val, der


def _neg_expm1(x):
    small = -(x * (1.0 + x * (0.5 + x * (1.0 / 6.0 + x * (1.0 / 24.0)))))
    return jnp.where(x > -0.05, small, 1.0 - jnp.exp(x))


def _colsum(v):
    return jnp.sum(v, axis=0, keepdims=True)


def _rowmean(v):
    return jnp.mean(v, axis=-1, keepdims=True)


def _copy_out(pairs, sem):
    for src, dst in pairs:
        cp = pltpu.make_async_copy(src, dst, sem)
        cp.start()
        cp.wait()


class _Carry:
    def __init__(self, srcs, out_shapes, aliases, n, plan):
        self.srcs, self.out_shapes, self.aliases, self.n, self.plan = list(srcs), list(out_shapes), aliases, n, plan


def _carry_args(carry, n_in, n_out):
    if carry is None:
        return [], [], [], [], {}
    sems = [pltpu.SemaphoreType.DMA((carry.n,)), pltpu.SemaphoreType.DMA((carry.n,))]
    aliases = {n_in + i: n_out + o for i, o in carry.aliases.items()}
    return [ANY] * len(carry.srcs), [ANY] * len(carry.out_shapes), carry.out_shapes, sems, aliases


def _split_refs(refs, n_in, n_out, n_scratch, carry):
    nci, nco = (len(carry.srcs), len(carry.out_shapes)) if carry is not None else (0, 0)
    cuts = [n_in, nci, n_out, nco, n_scratch]
    parts, i = [], 0
    for n in cuts:
        parts.append(refs[i:i + n])
        i += n
    ins, cins, outs, couts, scr = parts
    return ins, outs, scr, (cins, couts, refs[i:])


def _carry_run(carry, carry_refs, first, last):
    if carry is None:
        return
    cins, couts, (send_sems, recv_sems) = carry_refs

    def remote(k, src, dst, to):
        return pltpu.make_async_remote_copy(src_ref=src, dst_ref=dst, send_sem=send_sems.at[k],
                                            recv_sem=recv_sems.at[k], device_id=to, device_id_type=MESH)

    @pl.when(first)
    def _():
        for cp in carry.plan(cins, couts, remote):
            cp.start()

    @pl.when(last)
    def _():
        for cp in carry.plan(cins, couts, remote):
            cp.wait()


def _shift_down(v, k):
    return v if k == 0 else pltpu.roll(v, k, 0)


def _shift_up(v, k):
    return v if k == 0 else pltpu.roll(v, v.shape[0] - k, 0)


def _shell_pre(xv, vp_ref):
    r = lax.rsqrt(_rowmean(xv * xv) + EPS)
    return xv * r * (vp_ref[0:1, :] * (1.0 + vp_ref[2:3, :])) + vp_ref[1:2, :]


def _shell_post(xv, fv, vp_ref, res_w):
    r = lax.rsqrt(_rowmean(fv * fv) + EPS)
    return xv + fv * r * (res_w * (1.0 + vp_ref[3:4, :]) * vp_ref[4:5, :])


def _shell_post_bwd(fv, dov, vp_ref, res_w, s_ref):
    r = lax.rsqrt(_rowmean(fv * fv) + EPS)
    fn = fv * r
    pg = vp_ref[4:5, :]
    dy = (res_w * (1.0 + vp_ref[3:4, :])) * dov
    if s_ref is not None:
        s_ref[0:1, :] += _colsum(res_w * fn * pg * dov)
        s_ref[1:2, :] += _colsum(fn * dy)
    q = dy * pg
    return r * (q - fn * _rowmean(fn * q))


def _shell_pre_bwd(xv, dh, dov, vp_ref, s_ref):
    r = lax.rsqrt(_rowmean(xv * xv) + EPS)
    xn = xv * r
    pg = vp_ref[0:1, :]
    sc1 = 1.0 + vp_ref[2:3, :]
    s_ref[2:3, :] += _colsum(dh)
    s_ref[3:4, :] += _colsum(xn * pg * dh)
    s_ref[4:5, :] += _colsum(xn * dh * sc1)
    q = dh * (sc1 * pg)
    return dov + r * (q - xn * _rowmean(xn * q))


def _loss_head(y, tgt, name, tm=512):
    S, D = y.shape

    def body(y_ref, t_ref, dy_ref, l_ref):
        @pl.when(pl.program_id(0) == 0)
        def _():
            l_ref[...] = jnp.zeros_like(l_ref)

        e = y_ref[...] - t_ref[...]
        dy_ref[...] = e * (1.0 / D)
        l_ref[0:1, :] += _colsum(e * e) * (0.5 / D)

    return pl.pallas_call(
        body, name=name, grid=(S // tm,),
        in_specs=[pl.BlockSpec((tm, D), lambda t: (t, 0)), pl.BlockSpec((tm, D), lambda t: (t, 0))],
        out_specs=[pl.BlockSpec((tm, D), lambda t: (t, 0)), pl.BlockSpec((8, D), lambda t: (0, 0))],
        out_shape=[jax.ShapeDtypeStruct((S, D), F32), jax.ShapeDtypeStruct((8, D), F32)],
        compiler_params=_cp(("arbitrary",)),
    )(y, tgt)


MXU_COLS = 256


def _col_chunks(n, width=2 * MXU_COLS):
    return [(c0, min(c0 + width, n)) for c0 in range(0, n, width)]


def _ffn_fwd(x, vp, w13g, w2v, res_w, name, carry=None, tm=512):
    S, D = x.shape
    Fh = w13g.shape[-1]
    T = S // tm

    def body(*refs):
        (x_ref, vp_ref, w13_ref, w2_ref), (xo_ref, f_ref, h_ref, g_ref, u_ref), _, carry_refs = _split_refs(
            refs, 4, 5, 0, carry)
        _carry_run(carry, carry_refs, pl.program_id(0) == 0, pl.program_id(0) == T - 1)
        xv = x_ref[...]
        hb = _shell_pre(xv, vp_ref).astype(MXU_DT)
        h_ref[...] = hb
        acc = None
        for j in range(2):
            gg = _dot(hb, w13_ref[j])
            uu = _dot(hb, w13_ref[2 + j])
            g_ref[:, j * Fh:(j + 1) * Fh] = gg.astype(g_ref.dtype)
            u_ref[:, j * Fh:(j + 1) * Fh] = uu.astype(u_ref.dtype)
            part = _dot(gg * _sig(gg) * uu, w2_ref[j])
            acc = part if acc is None else acc + part
        f_ref[...] = acc
        xo_ref[...] = _shell_post(xv, acc, vp_ref, res_w)

    tile = lambda w: pl.BlockSpec((tm, w), lambda t: (t, 0))
    c_in, c_out, c_shapes, c_sems, c_alias = _carry_args(carry, 4, 5)
    return pl.pallas_call(
        body, name=name, grid=(T,),
        in_specs=[tile(D), _whole((8, D)),
                  pl.BlockSpec((None, 4, D, Fh), lambda t: (0, 0, 0, 0), pipeline_mode=pl.Buffered(1)),
                  pl.BlockSpec((None, 2, Fh, D), lambda t: (0, 0, 0, 0), pipeline_mode=pl.Buffered(1))] + c_in,
        out_specs=[tile(D), tile(D), tile(D), tile(2 * Fh), tile(2 * Fh)] + c_out,
        out_shape=[jax.ShapeDtypeStruct((S, D), F32), jax.ShapeDtypeStruct((S, D), F32),
                   jax.ShapeDtypeStruct((S, D), MXU_DT), jax.ShapeDtypeStruct((S, 2 * Fh), MXU_DT),
                   jax.ShapeDtypeStruct((S, 2 * Fh), MXU_DT)] + c_shapes,
        scratch_shapes=c_sems, input_output_aliases=c_alias,
        compiler_params=_cp(("arbitrary",), has_side_effects=carry is not None),
    )(x, vp, w13g, w2v, *(carry.srcs if carry is not None else []))


def _ffn_bwd_half(j, x, f, dout, vp, h, gpre, upre, w13g, w2v, dw13, dw2v, res_w, name, df=None, dh0=None, carry=None,
                  tm=256):
    S, D = h.shape
    Fh = w13g.shape[-1]
    T = S // tm
    first = j == 0
    n_in, n_out = (11, 5) if first else (13, 4)

    def body(*refs):
        ins, outs, (a1, a3, a2, sem), carry_refs = _split_refs(refs, n_in, n_out, 4, carry)
        if first:
            f_ref, do_ref, vp_ref, h_ref, g_ref, u_ref, w1_ref, w3_ref, w2_ref, _, _ = ins
            df_ref, dh_ref, s_ref, dw13_ref, dw2_ref = outs
        else:
            x_ref, do_ref, vp_ref, h_ref, g_ref, u_ref, w1_ref, w3_ref, w2_ref, dfi_ref, dh0_ref, _, _ = ins
            dx_ref, s_ref, dw13_ref, dw2_ref = outs
        t = pl.program_id(0)
        _carry_run(carry, carry_refs, t == 0, t == T - 1)

        @pl.when(t == 0)
        def _():
            for ref in (a1, a3, a2, s_ref):
                ref[...] = jnp.zeros_like(ref)

        hv = h_ref[...]
        if first:
            dfv = _shell_post_bwd(f_ref[...], do_ref[...], vp_ref, res_w, s_ref).astype(MXU_DT)
            df_ref[...] = dfv
        else:
            dfv = dfi_ref[...]
        dh = None
        for c0, c1 in _col_chunks(Fh):
            gg = g_ref[:, c0:c1].astype(F32)
            uu = u_ref[:, c0:c1].astype(F32)
            sg = _sig(gg)
            si = gg * sg
            da = _dot_nt(dfv, w2_ref[c0:c1, :])
            a2[c0:c1, :] += _dot_tn(si * uu, dfv)
            dg = da * uu * (sg * (1.0 + gg * (1.0 - sg)))
            du = da * si
            a1[:, c0:c1] += _dot_tn(hv, dg)
            a3[:, c0:c1] += _dot_tn(hv, du)
            part = _dot_nt(dg, w1_ref[:, c0:c1]) + _dot_nt(du, w3_ref[:, c0:c1])
            dh = part if dh is None else dh + part
        if first:
            dh_ref[...] = dh
        else:
            dx_ref[...] = _shell_pre_bwd(x_ref[...], dh0_ref[...] + dh, do_ref[...], vp_ref, s_ref)

        @pl.when(t == T - 1)
        def _():
            _copy_out(((a1, dw13_ref.at[0, j]), (a3, dw13_ref.at[0, 2 + j]), (a2, dw2_ref.at[0, j])), sem)

    tile = lambda w: pl.BlockSpec((tm, w), lambda t: (t, 0))
    half = pl.BlockSpec((tm, Fh), lambda t: (t, j))
    weights = [pl.BlockSpec((None, None, D, Fh), lambda t: (0, j, 0, 0), pipeline_mode=pl.Buffered(1)),
               pl.BlockSpec((None, None, D, Fh), lambda t: (0, 2 + j, 0, 0), pipeline_mode=pl.Buffered(1)),
               pl.BlockSpec((None, None, Fh, D), lambda t: (0, j, 0, 0), pipeline_mode=pl.Buffered(1))]
    sd = jax.ShapeDtypeStruct
    grads = [sd(dw13.shape, F32), sd(dw2v.shape, F32)]
    scratch = [pltpu.VMEM((D, Fh), F32), pltpu.VMEM((D, Fh), F32), pltpu.VMEM((Fh, D), F32), pltpu.SemaphoreType.DMA]
    c_in, c_out, c_shapes, c_sems, c_alias = _carry_args(carry, n_in, n_out)
    params = _cp(("arbitrary",), has_side_effects=carry is not None)
    extra = carry.srcs if carry is not None else []
    if first:
        return pl.pallas_call(
            body, name=name, grid=(T,),
            in_specs=[tile(D), tile(D), _whole((8, D)), tile(D), half, half] + weights + [ANY, ANY] + c_in,
            out_specs=[tile(D), tile(D), _whole((8, D)), ANY, ANY] + c_out,
            out_shape=[sd((S, D), MXU_DT), sd((S, D), F32), sd((8, D), F32)] + grads + c_shapes,
            scratch_shapes=scratch + c_sems, input_output_aliases={9: 3, 10: 4, **c_alias}, compiler_params=params,
        )(f, dout, vp, h, gpre, upre, w13g, w13g, w2v, dw13, dw2v, *extra)
    return pl.pallas_call(
        body, name=name, grid=(T,),
        in_specs=[tile(D), tile(D), _whole((8, D)), tile(D), half, half] + weights + [tile(D), tile(D), ANY, ANY] + c_in,
        out_specs=[tile(D), _whole((8, D)), ANY, ANY] + c_out,
        out_shape=[sd((S, D), F32), sd((8, D), F32)] + grads + c_shapes,
        scratch_shapes=scratch + c_sems, input_output_aliases={11: 2, 12: 3, **c_alias}, compiler_params=params,
    )(x, dout, vp, h, gpre, upre, w13g, w13g, w2v, df, dh0, dw13, dw2v, *extra)


def _scan_fwd(a, u, rows):
    n = a.shape[0]
    d = 1
    while d < n:
        m = rows >= d
        u = u + a * jnp.where(m, _shift_down(u, d), 0.0)
        a = a * jnp.where(m, _shift_down(a, d), 1.0)
        d *= 2
    return a, u


def _scan_bwd(a, u, rows):
    n = a.shape[0]
    d = 1
    while d < n:
        m = rows < n - d
        u = u + a * jnp.where(m, _shift_up(u, d), 0.0)
        a = a * jnp.where(m, _shift_up(a, d), 1.0)
        d *= 2
    return a, u


def _causal_conv(ext, w_ref, K, halo, tm):
    acc = None
    for k in range(K):
        term = w_ref[k:k + 1, :] * _shift_down(ext, K - 1 - k)[halo:, :]
        acc = term if acc is None else acc + term
    return acc


def _anticausal_conv(ext, w_ref, K, tm):
    acc = None
    for k in range(K):
        term = w_ref[k:k + 1, :] * _shift_up(ext, K - 1 - k)[:tm, :]
        acc = term if acc is None else acc + term
    return acc


def _dot_split(a, b):
    hi = a.astype(MXU_DT)
    lo = (a - hi.astype(F32)).astype(MXU_DT)
    return jnp.dot(hi, b, preferred_element_type=F32) + jnp.dot(lo, b, preferred_element_type=F32)


def _group_mean(v, p_ref):
    return _dot_nt_exact(_dot_split(v, p_ref[0]), p_ref[1])


def _dot_nt_exact(a, bt):
    hi = a.astype(MXU_DT)
    lo = (a - hi.astype(F32)).astype(MXU_DT)
    dims = (((1,), (1,)), ((), ()))
    return (lax.dot_general(hi, bt, dims, preferred_element_type=F32)
            + lax.dot_general(lo, bt, dims, preferred_element_type=F32))


def _group_norm(vc, p_ref, g, b):
    mu = _group_mean(vc, p_ref)
    dv = vc - mu
    rstd = lax.rsqrt(_group_mean(dv * dv, p_ref) + EPS)
    vhat = dv * rstd
    return vhat, rstd, vhat * g + b


def _lru_gates(axc, wg_ref, bg_ref, lam, W):
    gp = _dot(axc, wg_ref[...]) + bg_ref[0:1, :]
    r = _sig(gp[:, :W])
    i = _sig(gp[:, W:])
    ls = _logsig(lam)
    L = (LRU_C * ls) * r
    a = jnp.exp(L)
    mult = jnp.sqrt(_neg_expm1(2.0 * L))
    return r, i, ls, a, mult


def _mix_ab_fwd(x, vp, win4, cwa, wg, bg, cwb, v512, pavg, wout, res_w, name, tm=256):
    S, D = x.shape
    W = win4.shape[-1]
    KA, KB, HA, HB = 4, 31, 8, 32

    def body(x_ref, vp_ref, win_ref, cwa_ref, wg_ref, bg_ref, cwb_ref, v_ref, p_ref, wo_ref,
             xo_ref, f_ref, h_ref, hs_ref, axp_ref, axc_ref, bv_ref, vc_ref, ahalo, bhalo, carry):
        @pl.when(pl.program_id(0) == 0)
        def _():
            ahalo[...] = jnp.zeros_like(ahalo)
            bhalo[...] = jnp.zeros_like(bhalo)
            carry[...] = jnp.zeros_like(carry)

        xv = x_ref[...]
        hv = _shell_pre(xv, vp_ref).astype(MXU_DT)
        h_ref[...] = hv
        a_gate = _dot(hv, win_ref[0])
        axp = _dot(hv, win_ref[1])
        b_val = _dot(hv, win_ref[2])
        b_gate = _dot(hv, win_ref[3])
        rows = lax.broadcasted_iota(jnp.int32, (tm, W), 0)
        axc = _causal_conv(jnp.concatenate([ahalo[...], axp], axis=0), cwa_ref, KA, HA, tm) + v_ref[0:1, :]
        ahalo[...] = axp[tm - HA:, :]
        r, i, ls, a, mult = _lru_gates(axc, wg_ref, bg_ref, v_ref[1:2, :], W)
        acum, hloc = _scan_fwd(a, mult * i * axc, rows)
        hs = hloc + acum * carry[7:8, :]
        carry[...] = hs[tm - 8:, :]
        ya = hs * _gelu(a_gate)[0]
        bv = b_val * _sig(b_gate)
        vc = _causal_conv(jnp.concatenate([bhalo[...], bv], axis=0), cwb_ref, KB, HB, tm) + v_ref[2:3, :]
        bhalo[...] = bv[tm - HB:, :]
        _, _, vn = _group_norm(vc, p_ref, v_ref[3:4, :], v_ref[4:5, :])
        yb = vn * _sig(vn)
        fv = _dot(ya, wo_ref[0:W, :]) + _dot(yb, wo_ref[W:, :])
        f_ref[...] = fv
        xo_ref[...] = _shell_post(xv, fv, vp_ref, res_w)
        hs_ref[...] = hs
        axp_ref[...] = axp
        axc_ref[...] = axc
        bv_ref[...] = bv
        vc_ref[...] = vc

    tile = lambda w: pl.BlockSpec((tm, w), lambda t: (t, 0))
    sd = jax.ShapeDtypeStruct
    return pl.pallas_call(
        body, name=name, grid=(S // tm,),
        in_specs=[tile(D), _whole((8, D))] + [_resident(a) for a in (win4, cwa, wg, bg, cwb, v512, pavg, wout)],
        out_specs=[tile(D), tile(D), tile(D)] + [tile(W)] * 5,
        out_shape=[sd((S, D), F32), sd((S, D), F32), sd((S, D), MXU_DT)] + [sd((S, W), F32)] * 5,
        scratch_shapes=[pltpu.VMEM((HA, W), F32), pltpu.VMEM((HB, W), F32), pltpu.VMEM((8, W), F32)],
        compiler_params=_cp(("arbitrary",)),
    )(x, vp, win4, cwa, wg, bg, cwb, v512, pavg, wout)


def _mix_ab_bwd(x, f, dout, vp, h, hs, axp, axc, bv, vc, win4, cwa, wg, bg, cwb, v512, pavg, wout, res_w, name,
                carry=None, tm=256):
    S, D = x.shape
    W = win4.shape[-1]
    T = S // tm
    KA, KB, HA, HB = 4, 31, 8, 32

    def body(*refs):
        ins, outs, scr, carry_refs = _split_refs(refs, 21, 9, 8, carry)
        (x_ref, f_ref, do_ref, vp_ref, h_ref, hs_ref, hsp_ref, axp_ref, axpp_ref, axc_ref, bv_ref, bvp_ref, vc_ref,
         win_ref, cwa_ref, wg_ref, bg_ref, cwb_ref, v_ref, p_ref, wo_ref) = ins
        dx_ref, s_ref, dwin_out, dwo_out, dwg_out, dcwa_ref, dcwb_ref, dbg_ref, dv_ref = outs
        danext, dvnext, gfirst, afirst, dwin_ref, dwo_ref, dwg_ref, sem = scr
        t = pl.program_id(0)
        _carry_run(carry, carry_refs, t == 0, t == T - 1)

        @pl.when(t == 0)
        def _():
            for ref in (s_ref, dwin_ref, dwo_ref, dwg_ref, dcwa_ref, dcwb_ref, dbg_ref, dv_ref, danext, dvnext,
                        gfirst, afirst):
                ref[...] = jnp.zeros_like(ref)

        notfirst = jnp.where(t < T - 1, 1.0, 0.0).astype(F32)
        hv = h_ref[...]
        dov = do_ref[...]
        dfv = _shell_post_bwd(f_ref[...], dov, vp_ref, res_w, s_ref).astype(MXU_DT)
        a_gate = _dot(hv, win_ref[0])
        b_val = _dot(hv, win_ref[2])
        b_gate = _dot(hv, win_ref[3])
        rows = lax.broadcasted_iota(jnp.int32, (tm, W), 0)
        ge, dge = _gelu(a_gate)
        hsv = hs_ref[...]
        ya = hsv * ge
        vhat, rstd, vn = _group_norm(vc_ref[...], p_ref, v_ref[3:4, :], v_ref[4:5, :])
        sgn = _sig(vn)
        yb = vn * sgn
        dma = _dot_nt(dfv, wo_ref[0:W, :])
        dmb = _dot_nt(dfv, wo_ref[W:, :])
        dwo_ref[0:W, :] += _dot_tn(ya, dfv)
        dwo_ref[W:, :] += _dot_tn(yb, dfv)
        dhs = dma * ge
        d_a_gate = dma * hsv * dge
        axcv = axc_ref[...]
        lam = v_ref[1:2, :]
        r, i, ls, a, mult = _lru_gates(axcv, wg_ref, bg_ref, lam, W)
        ash = jnp.where(rows == tm - 1, afirst[0:1, :], _shift_up(a, 1))
        asuf, gloc = _scan_bwd(ash, dhs, rows)
        gsc = gloc + asuf * gfirst[0:1, :]
        afirst[...] = a[0:8, :]
        gfirst[...] = gsc[0:8, :]
        hprev = jnp.where(rows == 0, hsp_ref[HA - 1:HA, :] * notfirst, _shift_down(hsv, 1))
        da = gsc * hprev
        dL = da * a - gsc * (i * axcv) * (a * a) / mult
        dix = gsc * mult
        daxc = dix * i
        dr = dL * (LRU_C * ls)
        dv_ref[1:2, :] += _colsum(dL * r) * (LRU_C * _sig(-lam))
        dgate = jnp.concatenate([dr * r * (1.0 - r), (dix * axcv) * i * (1.0 - i)], axis=1)
        dbg_ref[0:1, :] += _colsum(dgate)
        dwg_ref[...] += _dot_tn(axcv, dgate)
        daxc = daxc + _dot_nt(dgate, wg_ref[...])
        daxp = _anticausal_conv(jnp.concatenate([daxc, danext[...]], axis=0), cwa_ref, KA, tm)
        ext = jnp.concatenate([axpp_ref[...] * notfirst, axp_ref[...]], axis=0)
        for k in range(KA):
            dcwa_ref[k:k + 1, :] += _colsum(daxc * _shift_down(ext, KA - 1 - k)[HA:, :])
        dv_ref[0:1, :] += _colsum(daxc)
        danext[...] = daxc[0:HA, :]
        dvn = dmb * (sgn * (1.0 + vn * (1.0 - sgn)))
        dv_ref[4:5, :] += _colsum(dvn)
        dv_ref[3:4, :] += _colsum(dvn * vhat)
        dvh = dvn * v_ref[3:4, :]
        dvc = rstd * (dvh - _group_mean(dvh, p_ref) - vhat * _group_mean(dvh * vhat, p_ref))
        dbv = _anticausal_conv(jnp.concatenate([dvc, dvnext[...]], axis=0), cwb_ref, KB, tm)
        ext = jnp.concatenate([bvp_ref[...] * notfirst, bv_ref[...]], axis=0)
        for k in range(KB):
            dcwb_ref[k:k + 1, :] += _colsum(dvc * _shift_down(ext, KB - 1 - k)[HB:, :])
        dv_ref[2:3, :] += _colsum(dvc)
        dvnext[...] = dvc[0:HB, :]
        sb = _sig(b_gate)
        dzs = (d_a_gate, daxp, dbv * sb, dbv * b_val * sb * (1.0 - sb))
        dh = None
        for s in range(4):
            part = _dot_nt(dzs[s], win_ref[s])
            dh = part if dh is None else dh + part
            dwin_ref[s] += _dot_tn(hv, dzs[s])
        dx_ref[...] = _shell_pre_bwd(x_ref[...], dh, dov, vp_ref, s_ref)

        @pl.when(t == T - 1)
        def _():
            _copy_out(((dwin_ref, dwin_out), (dwo_ref, dwo_out), (dwg_ref, dwg_out)), sem)

    tile = lambda w: pl.BlockSpec((tm, w), lambda t: (T - 1 - t, 0))
    prev = lambda hh: pl.BlockSpec((hh, W), lambda t: (jnp.maximum((T - 1 - t) * (tm // hh) - 1, 0), 0))
    out_shapes = [(S, D), (8, D), win4.shape, wout.shape, wg.shape, (8, W), (32, W), (8, 2 * W), (8, W)]
    c_in, c_out, c_shapes, c_sems, c_alias = _carry_args(carry, 21, 9)
    return pl.pallas_call(
        body, name=name, grid=(T,),
        in_specs=[tile(D), tile(D), tile(D), _whole((8, D)), tile(D),
                  tile(W), prev(HA), tile(W), prev(HA), tile(W), tile(W), prev(HB), tile(W)]
        + [_resident(a) for a in (win4, cwa, wg, bg, cwb, v512, pavg, wout)] + c_in,
        out_specs=[tile(D), _whole((8, D)), ANY, ANY, ANY] + [_whole(s) for s in out_shapes[5:]] + c_out,
        out_shape=[jax.ShapeDtypeStruct(s, F32) for s in out_shapes] + c_shapes,
        scratch_shapes=[pltpu.VMEM((HA, W), F32), pltpu.VMEM((HB, W), F32), pltpu.VMEM((8, W), F32),
                        pltpu.VMEM((8, W), F32), pltpu.VMEM(win4.shape, F32), pltpu.VMEM(wout.shape, F32),
                        pltpu.VMEM(wg.shape, F32), pltpu.SemaphoreType.DMA] + c_sems,
        input_output_aliases=c_alias, compiler_params=_cp(("arbitrary",), has_side_effects=carry is not None),
    )(x, f, dout, vp, h, hs, hs, axp, axp, axc, bv, bv, vc, win4, cwa, wg, bg, cwb, v512, pavg, wout,
      *(carry.srcs if carry is not None else []))


def _mix_c_core(hv, win_ref, v2_ref, v1_ref, ws_ref, bsb_ref, tm, D):
    zp = jnp.concatenate([_dot(hv, win_ref[s]) for s in range(4)], axis=1) + v2_ref[0:1, :]
    z, dz = _gelu(zp)
    u, v = z[:, :D], z[:, D:]
    mu = _rowmean(v)
    dv = v - mu
    rstd = lax.rsqrt(_rowmean(dv * dv) + EPS)
    vhat = dv * rstd
    vn = vhat * v1_ref[0:1, :] + v1_ref[1:2, :]
    rows_out = []
    for cidx in range(tm // CHUNK):
        blk = vn[cidx * CHUNK:(cidx + 1) * CHUNK, :]
        heads = [_dot(ws_ref[hd], blk[:, hd * CHUNK:(hd + 1) * CHUNK]) for hd in range(N_HEAD)]
        rows_out.append(jnp.concatenate(heads, axis=1) + bsb_ref[...])
    mixed = jnp.concatenate(rows_out, axis=0)
    return dz, u, rstd, vhat, vn, mixed


def _mix_c_fwd(x, vp, win4, v2d, v1d, ws, bsb, wout, res_w, name, tm=256):
    S, D = x.shape

    def body(x_ref, vp_ref, win_ref, v2_ref, v1_ref, ws_ref, bsb_ref, wo_ref, xo_ref, f_ref, h_ref):
        xv = x_ref[...]
        hv = _shell_pre(xv, vp_ref).astype(MXU_DT)
        h_ref[...] = hv
        _, u, _, _, _, mixed = _mix_c_core(hv, win_ref, v2_ref, v1_ref, ws_ref, bsb_ref, tm, D)
        fv = _dot(u * mixed, wo_ref[...])
        f_ref[...] = fv
        xo_ref[...] = _shell_post(xv, fv, vp_ref, res_w)

    tile = pl.BlockSpec((tm, D), lambda t: (t, 0))
    sd = jax.ShapeDtypeStruct
    return pl.pallas_call(
        body, name=name, grid=(S // tm,),
        in_specs=[tile, _whole((8, D))] + [_resident(a) for a in (win4, v2d, v1d, ws, bsb, wout)],
        out_specs=[tile, tile, tile],
        out_shape=[sd((S, D), F32), sd((S, D), F32), sd((S, D), MXU_DT)], compiler_params=_cp(("arbitrary",)),
    )(x, vp, win4, v2d, v1d, ws, bsb, wout)


def _mix_c_bwd(x, f, dout, vp, h, win4, v2d, v1d, ws, bsb, wout, res_w, name, carry=None, tm=256):
    S, D = x.shape

    def body(*refs):
        ins, outs, (dwin_ref, dwo_ref, sem), carry_refs = _split_refs(refs, 11, 8, 3, carry)
        x_ref, f_ref, do_ref, vp_ref, h_ref, win_ref, v2_ref, v1_ref, ws_ref, bsb_ref, wo_ref = ins
        dx_ref, s_ref, dwin_out, dwo_out, dws_ref, dbsb_ref, dv2_ref, dv1_ref = outs
        _carry_run(carry, carry_refs, pl.program_id(0) == 0, pl.program_id(0) == S // tm - 1)

        @pl.when(pl.program_id(0) == 0)
        def _():
            for ref in (s_ref, dwin_ref, dwo_ref, dws_ref, dbsb_ref, dv2_ref, dv1_ref):
                ref[...] = jnp.zeros_like(ref)

        hv = h_ref[...]
        dov = do_ref[...]
        dfv = _shell_post_bwd(f_ref[...], dov, vp_ref, res_w, s_ref).astype(MXU_DT)
        dz, u, rstd, vhat, vn, mixed = _mix_c_core(hv, win_ref, v2_ref, v1_ref, ws_ref, bsb_ref, tm, D)
        dp = _dot_nt(dfv, wo_ref[...])
        dwo_ref[...] += _dot_tn(u * mixed, dfv)
        du = dp * mixed
        dmx = dp * u
        rows_out = []
        for cidx in range(tm // CHUNK):
            dblk = dmx[cidx * CHUNK:(cidx + 1) * CHUNK, :]
            vblk = vn[cidx * CHUNK:(cidx + 1) * CHUNK, :]
            dbsb_ref[...] += dblk
            heads = []
            for hd in range(N_HEAD):
                dsl = dblk[:, hd * CHUNK:(hd + 1) * CHUNK]
                heads.append(_dot_tn(ws_ref[hd], dsl))
                dws_ref[hd] += _dot_nt(dsl, vblk[:, hd * CHUNK:(hd + 1) * CHUNK])
            rows_out.append(jnp.concatenate(heads, axis=1))
        dvn = jnp.concatenate(rows_out, axis=0)
        dv1_ref[1:2, :] += _colsum(dvn)
        dv1_ref[0:1, :] += _colsum(dvn * vhat)
        dvh = dvn * v1_ref[0:1, :]
        dv = rstd * (dvh - _rowmean(dvh) - vhat * _rowmean(dvh * vhat))
        dzp = jnp.concatenate([du, dv], axis=1) * dz
        dv2_ref[0:1, :] += _colsum(dzp)
        W = win_ref.shape[-1]
        dh = None
        for s in range(4):
            dzs = dzp[:, s * W:(s + 1) * W]
            part = _dot_nt(dzs, win_ref[s])
            dh = part if dh is None else dh + part
            dwin_ref[s] += _dot_tn(hv, dzs)
        dx_ref[...] = _shell_pre_bwd(x_ref[...], dh, dov, vp_ref, s_ref)

        @pl.when(pl.program_id(0) == S // tm - 1)
        def _():
            _copy_out(((dwin_ref, dwin_out), (dwo_ref, dwo_out)), sem)

    tile = pl.BlockSpec((tm, D), lambda t: (t, 0))
    out_shapes = [(S, D), (8, D), win4.shape, wout.shape, ws.shape, bsb.shape, (8, 2 * D), (8, D)]
    c_in, c_out, c_shapes, c_sems, c_alias = _carry_args(carry, 11, 8)
    return pl.pallas_call(
        body, name=name, grid=(S // tm,),
        in_specs=[tile, tile, tile, _whole((8, D)), tile] + [_resident(a) for a in (win4, v2d, v1d, ws, bsb, wout)]
        + c_in,
        out_specs=[tile, _whole((8, D)), ANY, ANY] + [_whole(s) for s in out_shapes[4:]] + c_out,
        out_shape=[jax.ShapeDtypeStruct(s, F32) for s in out_shapes] + c_shapes,
        scratch_shapes=[pltpu.VMEM(win4.shape, F32), pltpu.VMEM(wout.shape, F32), pltpu.SemaphoreType.DMA] + c_sems,
        input_output_aliases=c_alias, compiler_params=_cp(("arbitrary",), has_side_effects=carry is not None),
    )(x, f, dout, vp, h, win4, v2d, v1d, ws, bsb, wout, *(carry.srcs if carry is not None else []))


def _ada_fwd(c_all, ada_w, ada_b_my, name):
    L, D, N = ada_w.shape
    tn = N // 3

    def body(c_ref, w_ref, b_ref, o_ref):
        cv = c_ref[...]
        o_ref[...] = _dot_hi(cv * _sig(cv), w_ref[...]) + b_ref[...]

    return pl.pallas_call(
        body, name=name, grid=(L, 3),
        in_specs=[pl.BlockSpec((8, D), lambda l, n: (0, 0)), pl.BlockSpec((None, D, tn), lambda l, n: (l, 0, n)),
                  pl.BlockSpec((None, 1, tn), lambda l, n: (l, 0, n))],
        out_specs=pl.BlockSpec((None, 8, tn), lambda l, n: (l, 0, n)),
        out_shape=jax.ShapeDtypeStruct((L, 8, N), F32), compiler_params=_cp(("arbitrary", "arbitrary")),
    )(c_all, ada_w, ada_b_my)


def _ada_bwd(c_all, dmod_my, name):
    L, _, N = dmod_my.shape
    D = c_all.shape[1]
    tn = N // 3

    def body(c_ref, d_ref, o_ref):
        cv = c_ref[...]
        o_ref[...] = lax.dot_general(cv * _sig(cv), d_ref[...], (((0,), (0,)), ((), ())),
                                     precision=lax.Precision.HIGHEST, preferred_element_type=F32)

    return pl.pallas_call(
        body, name=name, grid=(L, 3),
        in_specs=[pl.BlockSpec((8, D), lambda l, n: (0, 0)), pl.BlockSpec((None, 8, tn), lambda l, n: (l, 0, n))],
        out_specs=pl.BlockSpec((None, D, tn), lambda l, n: (l, 0, n)),
        out_shape=jax.ShapeDtypeStruct((L, D, N), F32), compiler_params=_cp(("arbitrary", "arbitrary")),
    )(c_all, dmod_my)


def _row_tile(rows, cols, budget=1 << 20):
    best = 8
    for rt in range(8, rows + 1, 8):
        if rows % rt == 0 and rt * cols * 4 <= budget:
            best = rt
    return best


def _cast_place(w, g, s_idx, name):
    _, R, C = w.shape
    rt = _row_tile(R, C)

    def body(s_ref, w_ref, o_ref):
        o_ref[...] = w_ref[...].astype(o_ref.dtype)

    return pl.pallas_call(
        body, name=name,
        grid_spec=pltpu.PrefetchScalarGridSpec(
            num_scalar_prefetch=1, grid=(R // rt,),
            in_specs=[pl.BlockSpec((None, rt, C), lambda r, s_ref: (g, r, 0))],
            out_specs=pl.BlockSpec((None, None, rt, C), lambda r, s_ref: (0, s_ref[0], r, 0))),
        out_shape=jax.ShapeDtypeStruct((1, N_SHARD, R, C), MXU_DT), compiler_params=_cp(("arbitrary",)),
    )(s_idx, w)


def _add_half(gk, la, c_idx, name, out_dtype=F32):
    Gk, _, R, C = gk.shape
    Rh = R // 2
    n = Gk * N_SHARD
    gv = gk.reshape(n, 2, Rh, C)
    lv = la.reshape(n, Rh, C)
    rt = _row_tile(Rh, C)

    def body(c_ref, g_ref, l_ref, o_ref):
        o_ref[...] = (g_ref[...] + l_ref[...]).astype(o_ref.dtype)

    out = pl.pallas_call(
        body, name=name,
        grid_spec=pltpu.PrefetchScalarGridSpec(
            num_scalar_prefetch=1, grid=(n, Rh // rt),
            in_specs=[pl.BlockSpec((None, None, rt, C), lambda i, r, c_ref: (i, c_ref[0], r, 0)),
                      pl.BlockSpec((None, rt, C), lambda i, r, c_ref: (i, r, 0))],
            out_specs=pl.BlockSpec((None, rt, C), lambda i, r, c_ref: (i, r, 0))),
        out_shape=jax.ShapeDtypeStruct((n, Rh, C), out_dtype), compiler_params=_cp(("arbitrary", "arbitrary")),
    )(c_idx, gv, lv)
    return out.reshape(Gk, N_SHARD, Rh, C)


def _sum_chips(part, landed, sc_idx, joined, g, name):
    _, _, Rh, C = part.shape
    rt = _row_tile(Rh, C)
    nb = Rh // rt

    def body(i_ref, p_ref, l_ref, j_ref, o_ref):
        up = lambda v: v.astype(F32)
        o_ref[...] = ((up(p_ref[...]) + up(l_ref[0])) + up(l_ref[1])) + up(l_ref[2])

    return pl.pallas_call(
        body, name=name,
        grid_spec=pltpu.PrefetchScalarGridSpec(
            num_scalar_prefetch=1, grid=(nb,),
            in_specs=[pl.BlockSpec((None, None, rt, C), lambda r, i_ref: (0, i_ref[0], r, 0)),
                      pl.BlockSpec((None, 3, rt, C), lambda r, i_ref: (0, 0, r, 0)), ANY],
            out_specs=pl.BlockSpec((None, rt, C), lambda r, i_ref: (g, i_ref[1] * nb + r, 0))),
        out_shape=jax.ShapeDtypeStruct(joined.shape, F32), input_output_aliases={3: 0},
        compiler_params=_cp(("arbitrary",)),
    )(sc_idx, part, landed, joined)


def _sum_slots(lb, name):
    Gk, n, Rh, C = lb.shape
    rt = _row_tile(Rh, C)

    def body(l_ref, o_ref):
        acc = l_ref[0]
        for s in range(1, n):
            acc = acc + l_ref[s]
        o_ref[...] = acc

    return pl.pallas_call(
        body, name=name, grid=(Gk, Rh // rt),
        in_specs=[pl.BlockSpec((None, n, rt, C), lambda g, r: (g, 0, r, 0))],
        out_specs=pl.BlockSpec((None, rt, C), lambda g, r: (g, r, 0)),
        out_shape=jax.ShapeDtypeStruct((Gk, Rh, C), F32), compiler_params=_cp(("arbitrary", "arbitrary")),
    )(lb)


def _adamw(w, g, m, v, name):
    rows, cols = w.shape
    rt = _row_tile(rows, cols) if rows % 8 == 0 else rows
    c1 = 1.0 - ADAM_B1 ** ADAM_STEP
    c2 = 1.0 - ADAM_B2 ** ADAM_STEP

    def body(w_ref, g_ref, m_ref, v_ref, d_ref, mo_ref, vo_ref):
        gv = g_ref[...]
        mn = ADAM_B1 * m_ref[...] + (1.0 - ADAM_B1) * gv
        vn = ADAM_B2 * v_ref[...] + (1.0 - ADAM_B2) * (gv * gv)
        d_ref[...] = -ADAM_LR * ((mn / c1) / (jnp.sqrt(vn / c2) + ADAM_EPS) + ADAM_WD * w_ref[...])
        mo_ref[...] = mn
        vo_ref[...] = vn

    spec = pl.BlockSpec((rt, cols), lambda r: (r, 0))
    sds = jax.ShapeDtypeStruct((rows, cols), F32)
    return pl.pallas_call(
        body, name=name, grid=(rows // rt,), in_specs=[spec] * 4, out_specs=[spec] * 3,
        out_shape=[sds] * 3, compiler_params=_cp(("arbitrary",)),
    )(w, g, m, v)


def _coords():
    return lax.axis_index("x"), lax.axis_index("y"), lax.axis_index("c")


def _all_gather8(blk, name):
    m_per, n = blk.shape

    def body(x_ref, out_ref, send_sems, recv_sems, local_sem):
        x, y, c = _coords()
        me, sibling = (x, y, c), (x, y, 1 - c)
        chips = [(1 - x, y), (x, 1 - y), (1 - x, 1 - y)]

        def rows(px, py, pc):
            return out_ref.at[pl.ds((4 * px + 2 * py + pc) * m_per, m_per), :]

        def copy(k, block, to, src=None):
            return pltpu.make_async_remote_copy(
                src_ref=rows(*block) if src is None else src, dst_ref=rows(*block),
                send_sem=send_sems.at[k], recv_sem=recv_sems.at[k], device_id=to, device_id_type=MESH)

        mine = pltpu.make_async_copy(x_ref, rows(*me), local_sem)
        mine.start()
        first = [copy(0, me, sibling, src=x_ref)]
        first += [copy(1 + j, me, (*chip, c), src=x_ref) for j, chip in enumerate(chips)]
        for cp in first:
            cp.start()
        passed = [copy(4 + j, (*chip, c), sibling) for j, chip in enumerate(chips)]
        for j, chip in enumerate(chips):
            copy(1 + j, (*chip, c), me).wait_recv()
            passed[j].start()
        copy(0, sibling, me).wait_recv()
        for j, chip in enumerate(chips):
            copy(4 + j, (*chip, 1 - c), me).wait_recv()
        for cp in first + passed:
            cp.wait_send()
        mine.wait()

    return pl.pallas_call(
        body, name=name, out_shape=jax.ShapeDtypeStruct((N_DEV * m_per, n), blk.dtype),
        in_specs=[pl.BlockSpec(memory_space=pltpu.VMEM)], out_specs=pl.BlockSpec(memory_space=pltpu.VMEM),
        scratch_shapes=[pltpu.SemaphoreType.DMA((7,)), pltpu.SemaphoreType.DMA((7,)), pltpu.SemaphoreType.DMA],
        compiler_params=_cp(),
    )(blk)


def _comm_call(name, inputs, out_shapes, plan, n_remote, aliases=None):
    n_in, n_out = len(inputs), len(out_shapes)

    def body(*refs):
        in_refs, out_refs = refs[:n_in], refs[n_in:n_in + n_out]
        send_sems, recv_sems = refs[n_in + n_out:]

        def remote(k, src, dst, to):
            return pltpu.make_async_remote_copy(src_ref=src, dst_ref=dst, send_sem=send_sems.at[k],
                                                recv_sem=recv_sems.at[k], device_id=to, device_id_type=MESH)

        plan(in_refs, out_refs, remote)

    return pl.pallas_call(
        body, name=name, out_shape=out_shapes, in_specs=[ANY] * n_in, out_specs=[ANY] * n_out,
        scratch_shapes=[pltpu.SemaphoreType.DMA((n_remote,)), pltpu.SemaphoreType.DMA((n_remote,))],
        input_output_aliases=aliases or {}, compiler_params=_cp(has_side_effects=True),
    )(*inputs)


def _gather_ici_carry(placed):
    K = len(placed)

    def plan(ins, outs, remote):
        x, y, c = _coords()
        s_me = 2 * x + y
        cps = []
        for j, (px, py) in enumerate([(1 - x, y), (x, 1 - y), (1 - x, 1 - y)]):
            for k in range(K):
                rh = placed[k].shape[2] // 2
                own = outs[k].at[:, s_me, pl.ds(c * rh, rh), :]
                cps.append(remote(j * K + k, own, own, (px, py, c)))
        return cps

    shapes = [jax.ShapeDtypeStruct(p.shape, p.dtype) for p in placed]
    return _Carry(placed, shapes, {k: k for k in range(K)}, 3 * K, plan)


def _forward_sibling(placed, name):
    K = len(placed)

    def plan(ins, outs, remote):
        x, y, c = _coords()
        cps = []
        for j, (px, py) in enumerate([(1 - x, y), (x, 1 - y), (1 - x, 1 - y)]):
            for k in range(K):
                rh = placed[k].shape[2] // 2
                landed = outs[k].at[:, 2 * px + py, pl.ds(c * rh, rh), :]
                cps.append(remote(j * K + k, landed, landed, (x, y, 1 - c)))
        for cp in cps:
            cp.start()
        for j, (px, py) in enumerate([(1 - x, y), (x, 1 - y), (1 - x, 1 - y)]):
            for k in range(K):
                rh = placed[k].shape[2] // 2
                other = outs[k].at[:, 2 * px + py, pl.ds((1 - c) * rh, rh), :]
                remote(j * K + k, other, other, (x, y, 1 - c)).wait_recv()
        for cp in cps:
            cp.wait_send()

    out_shapes = [jax.ShapeDtypeStruct(p.shape, p.dtype) for p in placed]
    return _comm_call(name, placed, out_shapes, plan, 3 * K, aliases={k: k for k in range(K)})


def _gather_weights(placed, name):
    K = len(placed)

    def plan(ins, outs, remote):
        x, y, c = _coords()
        s_me = 2 * x + y
        sibling = (x, y, 1 - c)
        chips = [(1 - x, y), (x, 1 - y), (1 - x, 1 - y)]
        half = lambda k, s, cc: outs[k].at[:, s, pl.ds(cc * (placed[k].shape[2] // 2), placed[k].shape[2] // 2), :]
        sent = []
        for j, (px, py) in enumerate(chips):
            for k in range(K):
                own = half(k, s_me, c)
                cp = remote(j * K + k, own, own, (px, py, c))
                cp.start()
                sent.append(cp)
        for j, (px, py) in enumerate(chips):
            s_from = 2 * px + py
            for k in range(K):
                landed = half(k, s_from, c)
                remote(j * K + k, landed, landed, (px, py, c)).wait_recv()
                cp = remote((3 + j) * K + k, landed, landed, sibling)
                cp.start()
                sent.append(cp)
        for j, (px, py) in enumerate(chips):
            s_from = 2 * px + py
            for k in range(K):
                other = half(k, s_from, 1 - c)
                remote((3 + j) * K + k, other, other, sibling).wait_recv()
        for cp in sent:
            cp.wait_send()

    out_shapes = [jax.ShapeDtypeStruct(p.shape, p.dtype) for p in placed]
    return _comm_call(name, placed, out_shapes, plan, 6 * K, aliases={k: k for k in range(K)})


def _send_other_half(grads, name):
    K = len(grads)

    def plan(ins, outs, remote):
        x, y, c = _coords()
        cps = []
        for k in range(K):
            rh = grads[k].shape[2] // 2
            cps.append(remote(k, ins[k].at[:, :, pl.ds((1 - c) * rh, rh), :], outs[k], (x, y, 1 - c)))
        for cp in cps:
            cp.start()
        for cp in cps:
            cp.wait()

    out_shapes = [jax.ShapeDtypeStruct(g.shape[:2] + (g.shape[2] // 2, g.shape[3]), g.dtype) for g in grads]
    return _comm_call(name, grads, out_shapes, plan, K)


def _scatter_carry(parts):
    K = len(parts)

    def plan(ins, outs, remote):
        x, y, c = _coords()
        chips = [(1 - x, y), (x, 1 - y), (1 - x, 1 - y)]
        return [remote(j * K + k, ins[k].at[:, 2 * px + py], outs[k].at[:, j], (px, py, c))
                for j, (px, py) in enumerate(chips) for k in range(K)]

    out_shapes = [jax.ShapeDtypeStruct((p.shape[0], 3) + p.shape[2:], p.dtype) for p in parts]
    return _Carry(parts, out_shapes, {}, 3 * K, plan)


def _scatter_chips(parts, name):
    carry = _scatter_carry(parts)

    def plan(ins, outs, remote):
        cps = carry.plan(ins, outs, remote)
        for cp in cps:
            cp.start()
        for cp in cps:
            cp.wait()

    return _comm_call(name, parts, carry.out_shapes, plan, carry.n)


def _join_halves(joined):
    K = len(joined)

    def plan(ins, outs, remote):
        x, y, c = _coords()
        cps = []
        for k in range(K):
            rh = joined[k].shape[1] // 2
            mine = outs[k].at[:, pl.ds(c * rh, rh), :]
            cps.append(remote(k, mine, mine, (x, y, 1 - c)))
        for cp in cps:
            cp.start()
        for k in range(K):
            rh = joined[k].shape[1] // 2
            other = outs[k].at[:, pl.ds((1 - c) * rh, rh), :]
            remote(k, other, other, (x, y, 1 - c)).wait_recv()
        for cp in cps:
            cp.wait_send()

    out_shapes = [jax.ShapeDtypeStruct(h.shape, h.dtype) for h in joined]
    return _comm_call("join_halves", joined, out_shapes, plan, K, aliases={k: k for k in range(K)})


def _pack(parts):
    flat = []
    for p in parts:
        v = p.reshape(-1).astype(F32)
        pad = (-v.shape[0]) % 1024
        flat.append(jnp.pad(v, (0, pad)) if pad else v)
    return jnp.concatenate(flat).reshape(-1, 128)


def _unpack(packed, shapes):
    flat = packed.reshape(-1)
    out, off = [], 0
    for shp in shapes:
        n = math.prod(shp)
        out.append(flat[off:off + n].reshape(shp))
        off += n + (-n) % 1024
    return out


def _shard_last(a, s, n):
    return lax.dynamic_slice_in_dim(a, s * n, n, axis=a.ndim - 1)


def _rows8(*vecs):
    n = vecs[0].shape[-1]
    rows = [v.reshape(1, n).astype(F32) for v in vecs]
    return jnp.concatenate(rows + [jnp.zeros((8 - len(rows), n), F32)], axis=0)


def kernel(x, c, ada_w, ada_b, norm_pre, norm_post, ffn_w13, ffn_w2, ab_w_in, a_conv_w, a_conv_b, a_gate_w, a_gate_b, a_lam, b_conv_w, b_conv_b, b_norm_g, b_norm_b, ab_w_out, c_w_in, c_b_in, c_norm_g, c_norm_b, c_w_s, c_b_s, c_w_out, loss_target, m_ada_w, m_ada_b, m_norm_pre, m_norm_post, m_ffn_w13, m_ffn_w2, m_ab_w_in, m_a_conv_w, m_a_conv_b, m_a_gate_w, m_a_gate_b, m_a_lam, m_b_conv_w, m_b_conv_b, m_b_norm_g, m_b_norm_b, m_ab_w_out, m_c_w_in, m_c_b_in, m_c_norm_g, m_c_norm_b, m_c_w_s, m_c_b_s, m_c_w_out, v_ada_w, v_ada_b, v_norm_pre, v_norm_post, v_ffn_w13, v_ffn_w2, v_ab_w_in, v_a_conv_w, v_a_conv_b, v_a_gate_w, v_a_gate_b, v_a_lam, v_b_conv_w, v_b_conv_b, v_b_norm_g, v_b_norm_b, v_ab_w_out, v_c_w_in, v_c_b_in, v_c_norm_g, v_c_norm_b, v_c_w_s, v_c_b_s, v_c_w_out):
    S, D = x.shape[1], x.shape[2]
    W = a_lam.shape[-1]
    Fh = ffn_w13.shape[-1]
    Fq = ffn_w2.shape[2]
    xi, yi, ci = _coords()
    shard = 2 * xi + yi
    me = 4 * xi + 2 * yi + ci
    s_idx = jnp.reshape(shard, (1,)).astype(jnp.int32)
    c_idx = jnp.reshape(ci, (1,)).astype(jnp.int32)
    sc_idx = jnp.stack([shard, ci]).astype(jnp.int32)
    x2, tgt = x[0], loss_target[0]

    sharded_small = [norm_pre, norm_post, a_conv_w, b_conv_w, c_b_in, c_norm_g, c_norm_b]
    gathered = _all_gather8(_pack([c] + sharded_small), "gather_small")
    blocks = gathered.reshape(N_DEV, -1, 128)
    per_dev = [_unpack(blocks[d], [c.shape] + [p.shape for p in sharded_small]) for d in range(0, N_DEV, 2)]
    c_all = jnp.concatenate([_unpack(blocks[d], [c.shape])[0] for d in range(N_DEV)], axis=0)
    npre, npost, acw, bcw, cbin, cng, cnb = [jnp.concatenate([per_dev[s][1 + i] for s in range(N_SHARD)], axis=-1)
                                             for i in range(len(sharded_small))]

    ada_b_my = _shard_last(ada_b, shard, ada_w.shape[-1])[:, None, :]
    modp = _ada_fwd(c_all, ada_w, ada_b_my, "ada_fwd")
    modg = _all_gather8(modp.reshape(16, -1), "gather_mod").reshape(N_DEV, 2, 8, -1)
    mod_me = lax.dynamic_index_in_dim(modg[0::2], me, axis=2, keepdims=False)
    mod = jnp.transpose(mod_me, (1, 0, 2)).reshape(2, 3, 3, D)

    w13s, w2s = ffn_w13.reshape(4, D, Fh), ffn_w2.reshape(4, Fq, D)
    placed13 = [_cast_place(w13s, g, s_idx, f"cast_w13_{g}") for g in range(4)]
    placed2 = [_cast_place(w2s, g, s_idx, f"cast_w2_{g}") for g in range(4)]
    placed_mix = [_cast_place(w, 0, s_idx, f"cast_mix{k}") for k, w in enumerate((ab_w_in, ab_w_out, c_w_in, c_w_out))]
    w13_first, w2_first = _gather_weights([placed13[0], placed2[0]], "gather_first")
    gather_carry = _gather_ici_carry(placed13[1:] + placed2[1:] + placed_mix)

    eye = jnp.eye(8, dtype=F32)
    dh_a = W // 8
    blockdiag = lambda w: jnp.einsum("hde,hg->hdge", w, eye).reshape(W, W)
    gw = a_gate_w[0]
    wg = jnp.concatenate([blockdiag(gw[:, :, :dh_a]), blockdiag(gw[:, :, dh_a:])], axis=1).astype(MXU_DT)
    bgv = jnp.concatenate([a_gate_b[0][:, :dh_a].reshape(-1), a_gate_b[0][:, dh_a:].reshape(-1)])
    bg = _rows8(bgv)
    cwa = jnp.concatenate([acw[0], jnp.zeros((4, W), F32)], axis=0)
    cwb = jnp.concatenate([bcw[0], jnp.zeros((1, W), F32)], axis=0)
    v512 = _rows8(a_conv_b[0], a_lam[0], b_conv_b[0], b_norm_g[0], b_norm_b[0])
    dg_b = W // 8
    gid = jnp.arange(W) // dg_b
    member = (gid[:, None] == jnp.arange(128)[None, :]).astype(F32)
    pavg = jnp.stack([member / dg_b, member]).astype(MXU_DT)
    v2d = _rows8(cbin[0])
    v1d = _rows8(cng[0], cnb[0])
    tril = jnp.tril(jnp.ones((CHUNK, CHUNK), dtype=bool))
    ws = jnp.where(tril, c_w_s[0], 0.0).astype(MXU_DT)
    bsb = jnp.repeat(jnp.transpose(c_b_s[0]), D // N_HEAD, axis=1)

    res_ws = (0.5, 1.0, 0.5)
    vps, xs, saved = [], [], []
    xc = x2
    w13g, w2v = [w13_first], [w2_first.reshape(1, 2, Fh, D)]
    for l in range(2):
        for j in range(3):
            k = 3 * l + j
            vp = _rows8(npre[l, j], mod[l, j, 0], mod[l, j, 1], mod[l, j, 2], npost[l, j])
            vps.append(vp)
            xs.append(xc)
            gi = 2 * l + j // 2
            if k == 0:
                xc, *keep = _ffn_fwd(xc, vp, w13g[0], w2v[0], res_ws[j], "ffn_fwd0", carry=gather_carry)
                keep, landed = keep[:4], keep[4:]
                full = _forward_sibling(landed, "forward_sibling")
                w13g += list(full[0:3])
                w2v += [w.reshape(1, 2, Fh, D) for w in full[3:6]]
                abin_g, about_g, cin_g, cout_g = full[6:]
                ab_ops = (abin_g[0], cwa, wg, bg, cwb, v512, pavg, about_g.reshape(D, D))
                c_ops = (cin_g[0], v2d, v1d, ws, bsb, cout_g.reshape(D, D))
            elif j != 1:
                xc, *keep = _ffn_fwd(xc, vp, w13g[gi], w2v[gi], res_ws[j], f"ffn_fwd{k}")
            elif l == 0:
                xc, *keep = _mix_ab_fwd(xc, vp, *ab_ops, res_ws[j], "mix_ab_fwd")
            else:
                xc, *keep = _mix_c_fwd(xc, vp, *c_ops, res_ws[j], "mix_c_fwd")
            saved.append(keep)

    dout, lrow = _loss_head(xc, tgt, "loss_head")

    joined = {"w13": lax.empty((4, D, Fh), F32), "w2": lax.empty((4, Fq, D), F32), "abin": lax.empty((1, D, 4 * W // 4), F32),
              "about": lax.empty((1, D // 4, D), F32), "cin": lax.empty((1, D, 2 * D // 4), F32),
              "cout": lax.empty((1, D // 4, D), F32)}

    def start_reduce(grads, keys, g, tag):
        dt = MXU_DT if tag == "ffn0" else F32
        got = _send_other_half(grads, f"send_half_{tag}")
        parts = [_add_half(gr, la, c_idx, f"add_half_{tag}{i}", dt) for i, (gr, la) in enumerate(zip(grads, got))]
        return parts, keys, g

    def finish_reduce(pending, landed_slots):
        parts, keys, g = pending
        for part, key, slots in zip(parts, keys, landed_slots):
            joined[key] = _sum_chips(part, slots, sc_idx, joined[key], g, f"sum_chips_{key}{g}")

    pending = None
    d_npre = [[None] * 3 for _ in range(2)]
    d_npost = [[None] * 3 for _ in range(2)]
    d_mod = [[None] * 3 for _ in range(2)]
    for l in (1, 0):
        for j in (2, 1, 0):
            k = 3 * l + j
            carry = _scatter_carry(pending[0]) if pending is not None else None
            n_land = len(pending[0]) if pending is not None else 0
            if j != 1:
                f, h, gpre, upre = saved[k]
                gi = 2 * l + j // 2
                dw13 = lax.empty((1, 4, D, Fh), F32)
                dw2v = lax.empty((1, 2, Fh, D), F32)
                df, dh0, s_a, dw13, dw2v, *slots = _ffn_bwd_half(
                    0, xs[k], f, dout, vps[k], h, gpre, upre, w13g[gi], w2v[gi], dw13, dw2v, res_ws[j],
                    f"ffn_bwd{k}a", carry=carry)
                dout, s_b, dw13, dw2v = _ffn_bwd_half(
                    1, xs[k], f, dout, vps[k], h, gpre, upre, w13g[gi], w2v[gi], dw13, dw2v, res_ws[j],
                    f"ffn_bwd{k}b", df=df, dh0=dh0)
                sums = s_a + s_b
                mine = ([dw13, dw2v.reshape(1, 4, Fq, D)], ["w13", "w2"], gi, f"ffn{gi}")
            elif l == 0:
                dout, sums, d_abin, d_about, d_wg, d_cwa, d_cwb, d_bg, d_v512, *slots = _mix_ab_bwd(
                    xs[k], saved[k][0], dout, vps[k], *saved[k][1:], *ab_ops, res_ws[j], "mix_ab_bwd", carry=carry)
                mine = ([d_abin[None], d_about.reshape(1, 4, D // 4, D)], ["abin", "about"], 0, "ab")
            else:
                dout, sums, d_cin, d_cout, d_ws, d_bsb, d_v2, d_v1, *slots = _mix_c_bwd(
                    xs[k], saved[k][0], dout, vps[k], saved[k][1], *c_ops, res_ws[j], "mix_c_bwd", carry=carry)
                mine = ([d_cin[None], d_cout.reshape(1, 4, D // 4, D)], ["cin", "cout"], 0, "c")
            if pending is not None:
                finish_reduce(pending, slots[:n_land])
            pending = start_reduce(*mine)
            d_npre[l][j], d_npost[l][j] = sums[4], sums[1]
            d_mod[l][j] = jnp.stack([sums[2], sums[3], sums[0]])
    finish_reduce(pending, _scatter_chips(pending[0], "scatter_last"))
    grad_x = dout[None]

    dmod = jnp.stack([jnp.stack(d_mod[l]) for l in range(2)]).reshape(2, 9 * D)
    d_gate_w = jnp.concatenate([jnp.einsum("hdhe->hde", d_wg[:, :W].reshape(8, dh_a, 8, dh_a)),
                                jnp.einsum("hdhe->hde", d_wg[:, W:].reshape(8, dh_a, 8, dh_a))], axis=-1)
    d_gate_b = jnp.concatenate([d_bg[0, :W].reshape(8, dh_a), d_bg[0, W:].reshape(8, dh_a)], axis=-1)
    small_grads = [
        dmod, jnp.stack([jnp.stack(r) for r in d_npre]), jnp.stack([jnp.stack(r) for r in d_npost]),
        d_cwa[:4][None], d_v512[0][None], d_gate_w[None], d_gate_b[None], d_v512[1][None], d_cwb[:31][None],
        d_v512[2][None], d_v512[3][None], d_v512[4][None], d_v2[0][None], d_v1[0][None], d_v1[1][None],
        jnp.where(tril, d_ws, 0.0)[None], jnp.transpose(d_bsb.reshape(CHUNK, N_HEAD, D // N_HEAD).sum(-1))[None]]
    small_w = [ada_b, norm_pre, norm_post, a_conv_w, a_conv_b, a_gate_w, a_gate_b, a_lam, b_conv_w, b_conv_b,
               b_norm_g, b_norm_b, c_b_in, c_norm_g, c_norm_b, c_w_s, c_b_s]
    small_m = [m_ada_b, m_norm_pre, m_norm_post, m_a_conv_w, m_a_conv_b, m_a_gate_w, m_a_gate_b, m_a_lam, m_b_conv_w,
               m_b_conv_b, m_b_norm_g, m_b_norm_b, m_c_b_in, m_c_norm_g, m_c_norm_b, m_c_w_s, m_c_b_s]
    small_v = [v_ada_b, v_norm_pre, v_norm_post, v_a_conv_w, v_a_conv_b, v_a_gate_w, v_a_gate_b, v_a_lam, v_b_conv_w,
               v_b_conv_b, v_b_norm_g, v_b_norm_b, v_c_b_in, v_c_norm_g, v_c_norm_b, v_c_w_s, v_c_b_s]
    full_shapes = [g.shape for g in small_grads]
    loss_part = jnp.sum(lrow[0]).reshape(1, 1)
    sg_all = _all_gather8(_pack(small_grads + [loss_part]), "gather_small_grads").reshape(N_DEV, -1, 128)
    sg_sum = _sum_slots(sg_all[None], "sum_small_grads")[0]
    *g_full, loss_sum = _unpack(sg_sum, full_shapes + [(1, 1)])
    loss = loss_sum[0, 0]
    g_small = [g if g.shape == w.shape else _shard_last(g, shard, w.shape[-1]) for g, w in zip(g_full, small_w)]
    small_shapes = [w.shape for w in small_w]
    d_s, m_s, v_s = _adamw(_pack(small_w), _pack(g_small), _pack(small_m), _pack(small_v), "adamw_small")
    delta_small, newm_small, newv_small = (_unpack(a, small_shapes) for a in (d_s, m_s, v_s))

    dmod_all = jnp.stack([_unpack(sg_all[d], full_shapes[:1])[0] for d in range(N_DEV)], axis=1)
    n_ada = ada_w.shape[-1]
    g_ada_w = _ada_bwd(c_all, _shard_last(dmod_all, shard, n_ada), "ada_bwd")

    g_big = _join_halves([joined[key] for key in ("w13", "w2", "abin", "about", "cin", "cout")])

    big_w = [ffn_w13, ffn_w2, ab_w_in, ab_w_out, c_w_in, c_w_out, ada_w]
    big_m = [m_ffn_w13, m_ffn_w2, m_ab_w_in, m_ab_w_out, m_c_w_in, m_c_w_out, m_ada_w]
    big_v = [v_ffn_w13, v_ffn_w2, v_ab_w_in, v_ab_w_out, v_c_w_in, v_c_w_out, v_ada_w]
    big_g = [g.reshape(w.shape) for g, w in zip(list(g_big) + [g_ada_w], big_w)]
    big_out = []
    for k, (w, g, m, v) in enumerate(zip(big_w, big_g, big_m, big_v)):
        two_d = lambda a: a.reshape(-1, a.shape[-1])
        res = _adamw(two_d(w), two_d(g), two_d(m), two_d(v), f"adamw_big{k}")
        big_out.append([r.reshape(w.shape) for r in res])

    names = ["ada_w", "ada_b", "norm_pre", "norm_post", "ffn_w13", "ffn_w2", "ab_w_in", "a_conv_w", "a_conv_b",
             "a_gate_w", "a_gate_b", "a_lam", "b_conv_w", "b_conv_b", "b_norm_g", "b_norm_b", "ab_w_out", "c_w_in",
             "c_b_in", "c_norm_g", "c_norm_b", "c_w_s", "c_b_s", "c_w_out"]
    big_names = ["ffn_w13", "ffn_w2", "ab_w_in", "ab_w_out", "c_w_in", "c_w_out", "ada_w"]
    small_names = ["ada_b", "norm_pre", "norm_post", "a_conv_w", "a_conv_b", "a_gate_w", "a_gate_b", "a_lam",
                   "b_conv_w", "b_conv_b", "b_norm_g", "b_norm_b", "c_b_in", "c_norm_g", "c_norm_b", "c_w_s", "c_b_s"]
    table = {}
    for k, n in enumerate(big_names):
        table[n] = (big_g[k], *big_out[k])
    for k, n in enumerate(small_names):
        table[n] = (g_small[k], delta_small[k], newm_small[k], newv_small[k])
    outs = [loss, grad_x]
    for field in range(4):
        outs += [table[n][field] for n in names]
    return tuple(outs)
```

```python
import functools
import math

import jax
import jax.numpy as jnp
from jax import lax
from jax.experimental import pallas as pl
from jax.experimental.pallas import tpu as pltpu

F32 = jnp.float32
MXU_DT = jnp.bfloat16
EPS = 1e-6
LRU_C = 8.0
N_SHARD = 4
N_DEV = 8
CHUNK = 128
N_HEAD = 8
ADAM_LR, ADAM_B1, ADAM_B2, ADAM_EPS, ADAM_WD, ADAM_STEP = 0.001, 0.9, 0.999, 1e-08, 0.01, 10
GELU_K0 = math.sqrt(2.0 / math.pi)
GELU_K1 = 0.044715
VMEM_LIMIT = 58 * 1024 * 1024
MESH = pl.DeviceIdType.MESH
ANY = pl.BlockSpec(memory_space=pl.ANY)


def _cp(sem=None, **kw):
    if sem is not None:
        kw["dimension_semantics"] = sem
    return pltpu.CompilerParams(vmem_limit_bytes=VMEM_LIMIT, **kw)


def _resident(a):
    return pl.BlockSpec(a.shape, lambda *_: (0,) * a.ndim, pipeline_mode=pl.Buffered(1))


def _whole(shape):
    return pl.BlockSpec(shape, lambda *_: (0,) * len(shape))


def _dot(a, b):
    return jnp.dot(a.astype(MXU_DT), b.astype(MXU_DT), preferred_element_type=F32)


def _dot_nt(a, b):
    return lax.dot_general(a.astype(MXU_DT), b.astype(MXU_DT), (((1,), (1,)), ((), ())), preferred_element_type=F32)


def _dot_tn(a, b):
    return lax.dot_general(a.astype(MXU_DT), b.astype(MXU_DT), (((0,), (0,)), ((), ())), preferred_element_type=F32)


def _dot_hi(a, b):
    return jnp.dot(a, b, precision=lax.Precision.HIGHEST, preferred_element_type=F32)


def _sig(x):
    return 1.0 / (1.0 + jnp.exp(-x))


def _logsig(x):
    return jnp.minimum(x, 0.0) - jnp.log(1.0 + jnp.exp(-jnp.abs(x)))


def _gelu(x):
    x2 = x * x
    t = jnp.tanh(GELU_K0 * (x + GELU_K1 * x * x2))
    val = 0.5 * x * (1.0 + t)
    der = 0.5 * (1.0 + t) + 0.5 * x * (1.0 - t * t) * (GELU_K0 * (1.0 + 3.0 * GELU_K1 * x2))
    return val, der


def _neg_expm1(x):
    small = -(x * (1.0 + x * (0.5 + x * (1.0 / 6.0 + x * (1.0 / 24.0)))))
    return jnp.where(x > -0.05, small, 1.0 - jnp.exp(x))


def _colsum(v):
    return jnp.sum(v, axis=0, keepdims=True)


def _rowmean(v):
    return jnp.mean(v, axis=-1, keepdims=True)


def _copy_out(pairs, sem):
    for src, dst in pairs:
        cp = pltpu.make_async_copy(src, dst, sem)
        cp.start()
        cp.wait()


class _Carry:
    def __init__(self, srcs, out_shapes, aliases, n, plan):
        self.srcs, self.out_shapes, self.aliases, self.n, self.plan = list(srcs), list(out_shapes), aliases, n, plan


def _carry_args(carry, n_in, n_out):
    if carry is None:
        return [], [], [], [], {}
    sems = [pltpu.SemaphoreType.DMA((carry.n,)), pltpu.SemaphoreType.DMA((carry.n,))]
    aliases = {n_in + i: n_out + o for i, o in carry.aliases.items()}
    return [ANY] * len(carry.srcs), [ANY] * len(carry.out_shapes), carry.out_shapes, sems, aliases


def _split_refs(refs, n_in, n_out, n_scratch, carry):
    nci, nco = (len(carry.srcs), len(carry.out_shapes)) if carry is not None else (0, 0)
    cuts = [n_in, nci, n_out, nco, n_scratch]
    parts, i = [], 0
    for n in cuts:
        parts.append(refs[i:i + n])
        i += n
    ins, cins, outs, couts, scr = parts
    return ins, outs, scr, (cins, couts, refs[i:])


def _carry_run(carry, carry_refs, first, last):
    if carry is None:
        return
    cins, couts, (send_sems, recv_sems) = carry_refs

    def remote(k, src, dst, to):
        return pltpu.make_async_remote_copy(src_ref=src, dst_ref=dst, send_sem=send_sems.at[k],
                                            recv_sem=recv_sems.at[k], device_id=to, device_id_type=MESH)

    @pl.when(first)
    def _():
        for cp in carry.plan(cins, couts, remote):
            cp.start()

    @pl.when(last)
    def _():
        for cp in carry.plan(cins, couts, remote):
            cp.wait()


def _shift_down(v, k):
    return v if k == 0 else pltpu.roll(v, k, 0)


def _shift_up(v, k):
    return v if k == 0 else pltpu.roll(v, v.shape[0] - k, 0)


def _shell_pre(xv, vp_ref):
    r = lax.rsqrt(_rowmean(xv * xv) + EPS)
    return xv * r * (vp_ref[0:1, :] * (1.0 + vp_ref[2:3, :])) + vp_ref[1:2, :]


def _shell_post(xv, fv, vp_ref, res_w):
    r = lax.rsqrt(_rowmean(fv * fv) + EPS)
    return xv + fv * r * (res_w * (1.0 + vp_ref[3:4, :]) * vp_ref[4:5, :])


def _shell_post_bwd(fv, dov, vp_ref, res_w, s_ref):
    r = lax.rsqrt(_rowmean(fv * fv) + EPS)
    fn = fv * r
    pg = vp_ref[4:5, :]
    dy = (res_w * (1.0 + vp_ref[3:4, :])) * dov
    if s_ref is not None:
        s_ref[0:1, :] += _colsum(res_w * fn * pg * dov)
        s_ref[1:2, :] += _colsum(fn * dy)
    q = dy * pg
    return r * (q - fn * _rowmean(fn * q))


def _shell_pre_bwd(xv, dh, dov, vp_ref, s_ref):
    r = lax.rsqrt(_rowmean(xv * xv) + EPS)
    xn = xv * r
    pg = vp_ref[0:1, :]
    sc1 = 1.0 + vp_ref[2:3, :]
    s_ref[2:3, :] += _colsum(dh)
    s_ref[3:4, :] += _colsum(xn * pg * dh)
    s_ref[4:5, :] += _colsum(xn * dh * sc1)
    q = dh * (sc1 * pg)
    return dov + r * (q - xn * _rowmean(xn * q))


def _loss_head(y, tgt, name, tm=512):
    S, D = y.shape

    def body(y_ref, t_ref, dy_ref, l_ref):
        @pl.when(pl.program_id(0) == 0)
        def _():
            l_ref[...] = jnp.zeros_like(l_ref)

        e = y_ref[...] - t_ref[...]
        dy_ref[...] = e * (1.0 / D)
        l_ref[0:1, :] += _colsum(e * e) * (0.5 / D)

    return pl.pallas_call(
        body, name=name, grid=(S // tm,),
        in_specs=[pl.BlockSpec((tm, D), lambda t: (t, 0)), pl.BlockSpec((tm, D), lambda t: (t, 0))],
        out_specs=[pl.BlockSpec((tm, D), lambda t: (t, 0)), pl.BlockSpec((8, D), lambda t: (0, 0))],
        out_shape=[jax.ShapeDtypeStruct((S, D), F32), jax.ShapeDtypeStruct((8, D), F32)],
        compiler_params=_cp(("arbitrary",)),
    )(y, tgt)


MXU_COLS = 256


def _col_chunks(n, width=2 * MXU_COLS):
    return [(c0, min(c0 + width, n)) for c0 in range(0, n, width)]


def _ffn_fwd(x, vp, w13g, w2v, res_w, name, carry=None, tm=512):
    S, D = x.shape
    Fh = w13g.shape[-1]
    T = S // tm

    def body(*refs):
        (x_ref, vp_ref, w13_ref, w2_ref), (xo_ref, f_ref, h_ref, g_ref, u_ref), _, carry_refs = _split_refs(
            refs, 4, 5, 0, carry)
        _carry_run(carry, carry_refs, pl.program_id(0) == 0, pl.program_id(0) == T - 1)
        xv = x_ref[...]
        hb = _shell_pre(xv, vp_ref).astype(MXU_DT)
        h_ref[...] = hb
        acc = None
        for j in range(2):
            gg = _dot(hb, w13_ref[j])
            uu = _dot(hb, w13_ref[2 + j])
            g_ref[:, j * Fh:(j + 1) * Fh] = gg.astype(g_ref.dtype)
            u_ref[:, j * Fh:(j + 1) * Fh] = uu.astype(u_ref.dtype)
            part = _dot(gg * _sig(gg) * uu, w2_ref[j])
            acc = part if acc is None else acc + part
        f_ref[...] = acc
        xo_ref[...] = _shell_post(xv, acc, vp_ref, res_w)

    tile = lambda w: pl.BlockSpec((tm, w), lambda t: (t, 0))
    c_in, c_out, c_shapes, c_sems, c_alias = _carry_args(carry, 4, 5)
    return pl.pallas_call(
        body, name=name, grid=(T,),
        in_specs=[tile(D), _whole((8, D)),
                  pl.BlockSpec((None, 4, D, Fh), lambda t: (0, 0, 0, 0), pipeline_mode=pl.Buffered(1)),
                  pl.BlockSpec((None, 2, Fh, D), lambda t: (0, 0, 0, 0), pipeline_mode=pl.Buffered(1))] + c_in,
        out_specs=[tile(D), tile(D), tile(D), tile(2 * Fh), tile(2 * Fh)] + c_out,
        out_shape=[jax.ShapeDtypeStruct((S, D), F32), jax.ShapeDtypeStruct((S, D), F32),
                   jax.ShapeDtypeStruct((S, D), MXU_DT), jax.ShapeDtypeStruct((S, 2 * Fh), MXU_DT),
                   jax.ShapeDtypeStruct((S, 2 * Fh), MXU_DT)] + c_shapes,
        scratch_shapes=c_sems, input_output_aliases=c_alias,
        compiler_params=_cp(("arbitrary",), has_side_effects=carry is not None),
    )(x, vp, w13g, w2v, *(carry.srcs if carry is not None else []))


def _ffn_bwd_half(j, x, f, dout, vp, h, gpre, upre, w13g, w2v, dw13, dw2v, res_w, name, df=None, dh0=None, carry=None,
                  tm=512):
    S, D = h.shape
    Fh = w13g.shape[-1]
    T = S // tm
    first = j == 0
    n_in, n_out = (11, 5) if first else (13, 4)

    def body(*refs):
        ins, outs, (a1, a3, a2, sem), carry_refs = _split_refs(refs, n_in, n_out, 4, carry)
        if first:
            f_ref, do_ref, vp_ref, h_ref, g_ref, u_ref, w1_ref, w3_ref, w2_ref, _, _ = ins
            df_ref, dh_ref, s_ref, dw13_ref, dw2_ref = outs
        else:
            x_ref, do_ref, vp_ref, h_ref, g_ref, u_ref, w1_ref, w3_ref, w2_ref, dfi_ref, dh0_ref, _, _ = ins
            dx_ref, s_ref, dw13_ref, dw2_ref = outs
        t = pl.program_id(0)
        _carry_run(carry, carry_refs, t == 0, t == T - 1)

        @pl.when(t == 0)
        def _():
            for ref in (a1, a3, a2, s_ref):
                ref[...] = jnp.zeros_like(ref)

        hv = h_ref[...]
        if first:
            dfv = _shell_post_bwd(f_ref[...], do_ref[...], vp_ref, res_w, s_ref).astype(MXU_DT)
            df_ref[...] = dfv
        else:
            dfv = dfi_ref[...]
        dh = None
        for c0, c1 in _col_chunks(Fh):
            gg = g_ref[:, c0:c1].astype(F32)
            uu = u_ref[:, c0:c1].astype(F32)
            sg = _sig(gg)
            si = gg * sg
            da = _dot_nt(dfv, w2_ref[c0:c1, :])
            a2[c0:c1, :] += _dot_tn(si * uu, dfv)
            dg = da * uu * (sg * (1.0 + gg * (1.0 - sg)))
            du = da * si
            a1[:, c0:c1] += _dot_tn(hv, dg)
            a3[:, c0:c1] += _dot_tn(hv, du)
            part = _dot_nt(dg, w1_ref[:, c0:c1]) + _dot_nt(du, w3_ref[:, c0:c1])
            dh = part if dh is None else dh + part
        if first:
            dh_ref[...] = dh
        else:
            dx_ref[...] = _shell_pre_bwd(x_ref[...], dh0_ref[...] + dh, do_ref[...], vp_ref, s_ref)

        @pl.when(t == T - 1)
        def _():
            _copy_out(((a1, dw13_ref.at[0, j]), (a3, dw13_ref.at[0, 2 + j]), (a2, dw2_ref.at[0, j])), sem)

    tile = lambda w: pl.BlockSpec((tm, w), lambda t: (t, 0))
    half = pl.BlockSpec((tm, Fh), lambda t: (t, j))
    weights = [pl.BlockSpec((None, None, D, Fh), lambda t: (0, j, 0, 0), pipeline_mode=pl.Buffered(1)),
               pl.BlockSpec((None, None, D, Fh), lambda t: (0, 2 + j, 0, 0), pipeline_mode=pl.Buffered(1)),
               pl.BlockSpec((None, None, Fh, D), lambda t: (0, j, 0, 0), pipeline_mode=pl.Buffered(1))]
    sd = jax.ShapeDtypeStruct
    grads = [sd(dw13.shape, F32), sd(dw2v.shape, F32)]
    scratch = [pltpu.VMEM((D, Fh), F32), pltpu.VMEM((D, Fh), F32), pltpu.VMEM((Fh, D), F32), pltpu.SemaphoreType.DMA]
    c_in, c_out, c_shapes, c_sems, c_alias = _carry_args(carry, n_in, n_out)
    params = _cp(("arbitrary",), has_side_effects=carry is not None)
    extra = carry.srcs if carry is not None else []
    if first:
        return pl.pallas_call(
            body, name=name, grid=(T,),
            in_specs=[tile(D), tile(D), _whole((8, D)), tile(D), half, half] + weights + [ANY, ANY] + c_in,
            out_specs=[tile(D), tile(D), _whole((8, D)), ANY, ANY] + c_out,
            out_shape=[sd((S, D), MXU_DT), sd((S, D), F32), sd((8, D), F32)] + grads + c_shapes,
            scratch_shapes=scratch + c_sems, input_output_aliases={9: 3, 10: 4, **c_alias}, compiler_params=params,
        )(f, dout, vp, h, gpre, upre, w13g, w13g, w2v, dw13, dw2v, *extra)
    return pl.pallas_call(
        body, name=name, grid=(T,),
        in_specs=[tile(D), tile(D), _whole((8, D)), tile(D), half, half] + weights + [tile(D), tile(D), ANY, ANY] + c_in,
        out_specs=[tile(D), _whole((8, D)), ANY, ANY] + c_out,
        out_shape=[sd((S, D), F32), sd((8, D), F32)] + grads + c_shapes,
        scratch_shapes=scratch + c_sems, input_output_aliases={11: 2, 12: 3, **c_alias}, compiler_params=params,
    )(x, dout, vp, h, gpre, upre, w13g, w13g, w2v, df, dh0, dw13, dw2v, *extra)


def _scan_fwd(a, u, rows):
    n = a.shape[0]
    d = 1
    while d < n:
        m = rows >= d
        u = u + a * jnp.where(m, _shift_down(u, d), 0.0)
        a = a * jnp.where(m, _shift_down(a, d), 1.0)
        d *= 2
    return a, u


def _scan_bwd(a, u, rows):
    n = a.shape[0]
    d = 1
    while d < n:
        m = rows < n - d
        u = u + a * jnp.where(m, _shift_up(u, d), 0.0)
        a = a * jnp.where(m, _shift_up(a, d), 1.0)
        d *= 2
    return a, u


def _causal_conv(ext, w_ref, K, halo, tm):
    acc = None
    for k in range(K):
        term = w_ref[k:k + 1, :] * _shift_down(ext, K - 1 - k)[halo:, :]
        acc = term if acc is None else acc + term
    return acc


def _anticausal_conv(ext, w_ref, K, tm):
    acc = None
    for k in range(K):
        term = w_ref[k:k + 1, :] * _shift_up(ext, K - 1 - k)[:tm, :]
        acc = term if acc is None else acc + term
    return acc


def _dot_split(a, b):
    hi = a.astype(MXU_DT)
    lo = (a - hi.astype(F32)).astype(MXU_DT)
    return jnp.dot(hi, b, preferred_element_type=F32) + jnp.dot(lo, b, preferred_element_type=F32)


def _group_mean(v, p_ref):
    return _dot_nt_exact(_dot_split(v, p_ref[0]), p_ref[1])


def _dot_nt_exact(a, bt):
    hi = a.astype(MXU_DT)
    lo = (a - hi.astype(F32)).astype(MXU_DT)
    dims = (((1,), (1,)), ((), ()))
    return (lax.dot_general(hi, bt, dims, preferred_element_type=F32)
            + lax.dot_general(lo, bt, dims, preferred_element_type=F32))


def _group_norm(vc, p_ref, g, b):
    mu = _group_mean(vc, p_ref)
    dv = vc - mu
    rstd = lax.rsqrt(_group_mean(dv * dv, p_ref) + EPS)
    vhat = dv * rstd
    return vhat, rstd, vhat * g + b


def _lru_gates(axc, wg_ref, bg_ref, lam, W):
    gp = _dot(axc, wg_ref[...]) + bg_ref[0:1, :]
    r = _sig(gp[:, :W])
    i = _sig(gp[:, W:])
    ls = _logsig(lam)
    L = (LRU_C * ls) * r
    a = jnp.exp(L)
    mult = jnp.sqrt(_neg_expm1(2.0 * L))
    return r, i, ls, a, mult


def _mix_ab_fwd(x, vp, win4, cwa, wg, bg, cwb, v512, pavg, wout, res_w, name, tm=256):
    S, D = x.shape
    W = win4.shape[-1]
    KA, KB, HA, HB = 4, 31, 8, 32

    def body(x_ref, vp_ref, win_ref, cwa_ref, wg_ref, bg_ref, cwb_ref, v_ref, p_ref, wo_ref,
             xo_ref, f_ref, h_ref, hs_ref, axp_ref, axc_ref, bv_ref, vc_ref, ahalo, bhalo, carry):
        @pl.when(pl.program_id(0) == 0)
        def _():
            ahalo[...] = jnp.zeros_like(ahalo)
            bhalo[...] = jnp.zeros_like(bhalo)
            carry[...] = jnp.zeros_like(carry)

        xv = x_ref[...]
        hv = _shell_pre(xv, vp_ref).astype(MXU_DT)
        h_ref[...] = hv
        a_gate = _dot(hv, win_ref[0])
        axp = _dot(hv, win_ref[1])
        b_val = _dot(hv, win_ref[2])
        b_gate = _dot(hv, win_ref[3])
        rows = lax.broadcasted_iota(jnp.int32, (tm, W), 0)
        axc = _causal_conv(jnp.concatenate([ahalo[...], axp], axis=0), cwa_ref, KA, HA, tm) + v_ref[0:1, :]
        ahalo[...] = axp[tm - HA:, :]
        r, i, ls, a, mult = _lru_gates(axc, wg_ref, bg_ref, v_ref[1:2, :], W)
        acum, hloc = _scan_fwd(a, mult * i * axc, rows)
        hs = hloc + acum * carry[7:8, :]
        carry[...] = hs[tm - 8:, :]
        ya = hs * _gelu(a_gate)[0]
        bv = b_val * _sig(b_gate)
        vc = _causal_conv(jnp.concatenate([bhalo[...], bv], axis=0), cwb_ref, KB, HB, tm) + v_ref[2:3, :]
        bhalo[...] = bv[tm - HB:, :]
        _, _, vn = _group_norm(vc, p_ref, v_ref[3:4, :], v_ref[4:5, :])
        yb = vn * _sig(vn)
        fv = _dot(ya, wo_ref[0:W, :]) + _dot(yb, wo_ref[W:, :])
        f_ref[...] = fv
        xo_ref[...] = _shell_post(xv, fv, vp_ref, res_w)
        hs_ref[...] = hs
        axp_ref[...] = axp
        axc_ref[...] = axc
        bv_ref[...] = bv
        vc_ref[...] = vc

    tile = lambda w: pl.BlockSpec((tm, w), lambda t: (t, 0))
    sd = jax.ShapeDtypeStruct
    return pl.pallas_call(
        body, name=name, grid=(S // tm,),
        in_specs=[tile(D), _whole((8, D))] + [_resident(a) for a in (win4, cwa, wg, bg, cwb, v512, pavg, wout)],
        out_specs=[tile(D), tile(D), tile(D)] + [tile(W)] * 5,
        out_shape=[sd((S, D), F32), sd((S, D), F32), sd((S, D), MXU_DT)] + [sd((S, W), F32)] * 5,
        scratch_shapes=[pltpu.VMEM((HA, W), F32), pltpu.VMEM((HB, W), F32), pltpu.VMEM((8, W), F32)],
        compiler_params=_cp(("arbitrary",)),
    )(x, vp, win4, cwa, wg, bg, cwb, v512, pavg, wout)


def _mix_ab_bwd(x, f, dout, vp, h, hs, axp, axc, bv, vc, win4, cwa, wg, bg, cwb, v512, pavg, wout, res_w, name,
                carry=None, tm=256):
    S, D = x.shape
    W = win4.shape[-1]
    T = S // tm
    KA, KB, HA, HB = 4, 31, 8, 32

    def body(*refs):
        ins, outs, scr, carry_refs = _split_refs(refs, 21, 9, 8, carry)
        (x_ref, f_ref, do_ref, vp_ref, h_ref, hs_ref, hsp_ref, axp_ref, axpp_ref, axc_ref, bv_ref, bvp_ref, vc_ref,
         win_ref, cwa_ref, wg_ref, bg_ref, cwb_ref, v_ref, p_ref, wo_ref) = ins
        dx_ref, s_ref, dwin_out, dwo_out, dwg_out, dcwa_ref, dcwb_ref, dbg_ref, dv_ref = outs
        danext, dvnext, gfirst, afirst, dwin_ref, dwo_ref, dwg_ref, sem = scr
        t = pl.program_id(0)
        _carry_run(carry, carry_refs, t == 0, t == T - 1)

        @pl.when(t == 0)
        def _():
            for ref in (s_ref, dwin_ref, dwo_ref, dwg_ref, dcwa_ref, dcwb_ref, dbg_ref, dv_ref, danext, dvnext,
                        gfirst, afirst):
                ref[...] = jnp.zeros_like(ref)

        notfirst = jnp.where(t < T - 1, 1.0, 0.0).astype(F32)
        hv = h_ref[...]
        dov = do_ref[...]
        dfv = _shell_post_bwd(f_ref[...], dov, vp_ref, res_w, s_ref).astype(MXU_DT)
        a_gate = _dot(hv, win_ref[0])
        b_val = _dot(hv, win_ref[2])
        b_gate = _dot(hv, win_ref[3])
        rows = lax.broadcasted_iota(jnp.int32, (tm, W), 0)
        ge, dge = _gelu(a_gate)
        hsv = hs_ref[...]
        ya = hsv * ge
        vhat, rstd, vn = _group_norm(vc_ref[...], p_ref, v_ref[3:4, :], v_ref[4:5, :])
        sgn = _sig(vn)
        yb = vn * sgn
        dma = _dot_nt(dfv, wo_ref[0:W, :])
        dmb = _dot_nt(dfv, wo_ref[W:, :])
        dwo_ref[0:W, :] += _dot_tn(ya, dfv)
        dwo_ref[W:, :] += _dot_tn(yb, dfv)
        dhs = dma * ge
        d_a_gate = dma * hsv * dge
        axcv = axc_ref[...]
        lam = v_ref[1:2, :]
        r, i, ls, a, mult = _lru_gates(axcv, wg_ref, bg_ref, lam, W)
        ash = jnp.where(rows == tm - 1, afirst[0:1, :], _shift_up(a, 1))
        asuf, gloc = _scan_bwd(ash, dhs, rows)
        gsc = gloc + asuf * gfirst[0:1, :]
        afirst[...] = a[0:8, :]
        gfirst[...] = gsc[0:8, :]
        hprev = jnp.where(rows == 0, hsp_ref[HA - 1:HA, :] * notfirst, _shift_down(hsv, 1))
        da = gsc * hprev
        dL = da * a - gsc * (i * axcv) * (a * a) / mult
        dix = gsc * mult
        daxc = dix * i
        dr = dL * (LRU_C * ls)
        dv_ref[1:2, :] += _colsum(dL * r) * (LRU_C * _sig(-lam))
        dgate = jnp.concatenate([dr * r * (1.0 - r), (dix * axcv) * i * (1.0 - i)], axis=1)
        dbg_ref[0:1, :] += _colsum(dgate)
        dwg_ref[...] += _dot_tn(axcv, dgate)
        daxc = daxc + _dot_nt(dgate, wg_ref[...])
        daxp = _anticausal_conv(jnp.concatenate([daxc, danext[...]], axis=0), cwa_ref, KA, tm)
        ext = jnp.concatenate([axpp_ref[...] * notfirst, axp_ref[...]], axis=0)
        for k in range(KA):
            dcwa_ref[k:k + 1, :] += _colsum(daxc * _shift_down(ext, KA - 1 - k)[HA:, :])
        dv_ref[0:1, :] += _colsum(daxc)
        danext[...] = daxc[0:HA, :]
        dvn = dmb * (sgn * (1.0 + vn * (1.0 - sgn)))
        dv_ref[4:5, :] += _colsum(dvn)
        dv_ref[3:4, :] += _colsum(dvn * vhat)
        dvh = dvn * v_ref[3:4, :]
        dvc = rstd * (dvh - _group_mean(dvh, p_ref) - vhat * _group_mean(dvh * vhat, p_ref))
        dbv = _anticausal_conv(jnp.concatenate([dvc, dvnext[...]], axis=0), cwb_ref, KB, tm)
        ext = jnp.concatenate([bvp_ref[...] * notfirst, bv_ref[...]], axis=0)
        for k in range(KB):
            dcwb_ref[k:k + 1, :] += _colsum(dvc * _shift_down(ext, KB - 1 - k)[HB:, :])
        dv_ref[2:3, :] += _colsum(dvc)
        dvnext[...] = dvc[0:HB, :]
        sb = _sig(b_gate)
        dzs = (d_a_gate, daxp, dbv * sb, dbv * b_val * sb * (1.0 - sb))
        dh = None
        for s in range(4):
            part = _dot_nt(dzs[s], win_ref[s])
            dh = part if dh is None else dh + part
            dwin_ref[s] += _dot_tn(hv, dzs[s])
        dx_ref[...] = _shell_pre_bwd(x_ref[...], dh, dov, vp_ref, s_ref)

        @pl.when(t == T - 1)
        def _():
            _copy_out(((dwin_ref, dwin_out), (dwo_ref, dwo_out), (dwg_ref, dwg_out)), sem)

    tile = lambda w: pl.BlockSpec((tm, w), lambda t: (T - 1 - t, 0))
    prev = lambda hh: pl.BlockSpec((hh, W), lambda t: (jnp.maximum((T - 1 - t) * (tm // hh) - 1, 0), 0))
    out_shapes = [(S, D), (8, D), win4.shape, wout.shape, wg.shape, (8, W), (32, W), (8, 2 * W), (8, W)]
    c_in, c_out, c_shapes, c_sems, c_alias = _carry_args(carry, 21, 9)
    return pl.pallas_call(
        body, name=name, grid=(T,),
        in_specs=[tile(D), tile(D), tile(D), _whole((8, D)), tile(D),
                  tile(W), prev(HA), tile(W), prev(HA), tile(W), tile(W), prev(HB), tile(W)]
        + [_resident(a) for a in (win4, cwa, wg, bg, cwb, v512, pavg, wout)] + c_in,
        out_specs=[tile(D), _whole((8, D)), ANY, ANY, ANY] + [_whole(s) for s in out_shapes[5:]] + c_out,
        out_shape=[jax.ShapeDtypeStruct(s, F32) for s in out_shapes] + c_shapes,
        scratch_shapes=[pltpu.VMEM((HA, W), F32), pltpu.VMEM((HB, W), F32), pltpu.VMEM((8, W), F32),
                        pltpu.VMEM((8, W), F32), pltpu.VMEM(win4.shape, F32), pltpu.VMEM(wout.shape, F32),
                        pltpu.VMEM(wg.shape, F32), pltpu.SemaphoreType.DMA] + c_sems,
        input_output_aliases=c_alias, compiler_params=_cp(("arbitrary",), has_side_effects=carry is not None),
    )(x, f, dout, vp, h, hs, hs, axp, axp, axc, bv, bv, vc, win4, cwa, wg, bg, cwb, v512, pavg, wout,
      *(carry.srcs if carry is not None else []))


def _mix_c_core(hv, win_ref, v2_ref, v1_ref, ws_ref, bsb_ref, tm, D):
    zp = jnp.concatenate([_dot(hv, win_ref[s]) for s in range(4)], axis=1) + v2_ref[0:1, :]
    z, dz = _gelu(zp)
    u, v = z[:, :D], z[:, D:]
    mu = _rowmean(v)
    dv = v - mu
    rstd = lax.rsqrt(_rowmean(dv * dv) + EPS)
    vhat = dv * rstd
    vn = vhat * v1_ref[0:1, :] + v1_ref[1:2, :]
    rows_out = []
    for cidx in range(tm // CHUNK):
        blk = vn[cidx * CHUNK:(cidx + 1) * CHUNK, :]
        heads = [_dot(ws_ref[hd], blk[:, hd * CHUNK:(hd + 1) * CHUNK]) for hd in range(N_HEAD)]
        rows_out.append(jnp.concatenate(heads, axis=1) + bsb_ref[...])
    mixed = jnp.concatenate(rows_out, axis=0)
    return dz, u, rstd, vhat, vn, mixed


def _mix_c_fwd(x, vp, win4, v2d, v1d, ws, bsb, wout, res_w, name, tm=512):
    S, D = x.shape

    def body(x_ref, vp_ref, win_ref, v2_ref, v1_ref, ws_ref, bsb_ref, wo_ref, xo_ref, f_ref, h_ref):
        xv = x_ref[...]
        hv = _shell_pre(xv, vp_ref).astype(MXU_DT)
        h_ref[...] = hv
        _, u, _, _, _, mixed = _mix_c_core(hv, win_ref, v2_ref, v1_ref, ws_ref, bsb_ref, tm, D)
        fv = _dot(u * mixed, wo_ref[...])
        f_ref[...] = fv
        xo_ref[...] = _shell_post(xv, fv, vp_ref, res_w)

    tile = pl.BlockSpec((tm, D), lambda t: (t, 0))
    sd = jax.ShapeDtypeStruct
    return pl.pallas_call(
        body, name=name, grid=(S // tm,),
        in_specs=[tile, _whole((8, D))] + [_resident(a) for a in (win4, v2d, v1d, ws, bsb, wout)],
        out_specs=[tile, tile, tile],
        out_shape=[sd((S, D), F32), sd((S, D), F32), sd((S, D), MXU_DT)], compiler_params=_cp(("arbitrary",)),
    )(x, vp, win4, v2d, v1d, ws, bsb, wout)


def _mix_c_bwd(x, f, dout, vp, h, win4, v2d, v1d, ws, bsb, wout, res_w, name, carry=None, tm=256):
    S, D = x.shape

    def body(*refs):
        ins, outs, (dwin_ref, dwo_ref, sem), carry_refs = _split_refs(refs, 11, 8, 3, carry)
        x_ref, f_ref, do_ref, vp_ref, h_ref, win_ref, v2_ref, v1_ref, ws_ref, bsb_ref, wo_ref = ins
        dx_ref, s_ref, dwin_out, dwo_out, dws_ref, dbsb_ref, dv2_ref, dv1_ref = outs
        _carry_run(carry, carry_refs, pl.program_id(0) == 0, pl.program_id(0) == S // tm - 1)

        @pl.when(pl.program_id(0) == 0)
        def _():
            for ref in (s_ref, dwin_ref, dwo_ref, dws_ref, dbsb_ref, dv2_ref, dv1_ref):
                ref[...] = jnp.zeros_like(ref)

        hv = h_ref[...]
        dov = do_ref[...]
        dfv = _shell_post_bwd(f_ref[...], dov, vp_ref, res_w, s_ref).astype(MXU_DT)
        dz, u, rstd, vhat, vn, mixed = _mix_c_core(hv, win_ref, v2_ref, v1_ref, ws_ref, bsb_ref, tm, D)
        dp = _dot_nt(dfv, wo_ref[...])
        dwo_ref[...] += _dot_tn(u * mixed, dfv)
        du = dp * mixed
        dmx = dp * u
        rows_out = []
        for cidx in range(tm // CHUNK):
            dblk = dmx[cidx * CHUNK:(cidx + 1) * CHUNK, :]
            vblk = vn[cidx * CHUNK:(cidx + 1) * CHUNK, :]
            dbsb_ref[...] += dblk
            heads = []
            for hd in range(N_HEAD):
                dsl = dblk[:, hd * CHUNK:(hd + 1) * CHUNK]
                heads.append(_dot_tn(ws_ref[hd], dsl))
                dws_ref[hd] += _dot_nt(dsl, vblk[:, hd * CHUNK:(hd + 1) * CHUNK])
            rows_out.append(jnp.concatenate(heads, axis=1))
        dvn = jnp.concatenate(rows_out, axis=0)
        dv1_ref[1:2, :] += _colsum(dvn)
        dv1_ref[0:1, :] += _colsum(dvn * vhat)
        dvh = dvn * v1_ref[0:1, :]
        dv = rstd * (dvh - _rowmean(dvh) - vhat * _rowmean(dvh * vhat))
        dzp = jnp.concatenate([du, dv], axis=1) * dz
        dv2_ref[0:1, :] += _colsum(dzp)
        W = win_ref.shape[-1]
        dh = None
        for s in range(4):
            dzs = dzp[:, s * W:(s + 1) * W]
            part = _dot_nt(dzs, win_ref[s])
            dh = part if dh is None else dh + part
            dwin_ref[s] += _dot_tn(hv, dzs)
        dx_ref[...] = _shell_pre_bwd(x_ref[...], dh, dov, vp_ref, s_ref)

        @pl.when(pl.program_id(0) == S // tm - 1)
        def _():
            _copy_out(((dwin_ref, dwin_out), (dwo_ref, dwo_out)), sem)

    tile = pl.BlockSpec((tm, D), lambda t: (t, 0))
    out_shapes = [(S, D), (8, D), win4.shape, wout.shape, ws.shape, bsb.shape, (8, 2 * D), (8, D)]
    c_in, c_out, c_shapes, c_sems, c_alias = _carry_args(carry, 11, 8)
    return pl.pallas_call(
        body, name=name, grid=(S // tm,),
        in_specs=[tile, tile, tile, _whole((8, D)), tile] + [_resident(a) for a in (win4, v2d, v1d, ws, bsb, wout)]
        + c_in,
        out_specs=[tile, _whole((8, D)), ANY, ANY] + [_whole(s) for s in out_shapes[4:]] + c_out,
        out_shape=[jax.ShapeDtypeStruct(s, F32) for s in out_shapes] + c_shapes,
        scratch_shapes=[pltpu.VMEM(win4.shape, F32), pltpu.VMEM(wout.shape, F32), pltpu.SemaphoreType.DMA] + c_sems,
        input_output_aliases=c_alias, compiler_params=_cp(("arbitrary",), has_side_effects=carry is not None),
    )(x, f, dout, vp, h, win4, v2d, v1d, ws, bsb, wout, *(carry.srcs if carry is not None else []))


def _ada_fwd(c_all, ada_w, ada_b_my, name):
    L, D, N = ada_w.shape
    tn = N // 3

    def body(c_ref, w_ref, b_ref, o_ref):
        cv = c_ref[...]
        o_ref[...] = _dot_hi(cv * _sig(cv), w_ref[...]) + b_ref[...]

    return pl.pallas_call(
        body, name=name, grid=(L, 3),
        in_specs=[pl.BlockSpec((8, D), lambda l, n: (0, 0)), pl.BlockSpec((None, D, tn), lambda l, n: (l, 0, n)),
                  pl.BlockSpec((None, 1, tn), lambda l, n: (l, 0, n))],
        out_specs=pl.BlockSpec((None, 8, tn), lambda l, n: (l, 0, n)),
        out_shape=jax.ShapeDtypeStruct((L, 8, N), F32), compiler_params=_cp(("arbitrary", "arbitrary")),
    )(c_all, ada_w, ada_b_my)


def _ada_bwd(c_all, dmod_my, name):
    L, _, N = dmod_my.shape
    D = c_all.shape[1]
    tn = N // 3

    def body(c_ref, d_ref, o_ref):
        cv = c_ref[...]
        o_ref[...] = lax.dot_general(cv * _sig(cv), d_ref[...], (((0,), (0,)), ((), ())),
                                     precision=lax.Precision.HIGHEST, preferred_element_type=F32)

    return pl.pallas_call(
        body, name=name, grid=(L, 3),
        in_specs=[pl.BlockSpec((8, D), lambda l, n: (0, 0)), pl.BlockSpec((None, 8, tn), lambda l, n: (l, 0, n))],
        out_specs=pl.BlockSpec((None, D, tn), lambda l, n: (l, 0, n)),
        out_shape=jax.ShapeDtypeStruct((L, D, N), F32), compiler_params=_cp(("arbitrary", "arbitrary")),
    )(c_all, dmod_my)


def _row_tile(rows, cols, budget=1 << 20):
    best = 8
    for rt in range(8, rows + 1, 8):
        if rows % rt == 0 and rt * cols * 4 <= budget:
            best = rt
    return best


def _my_chip():
    return 2 * lax.axis_index("x") + lax.axis_index("y")


def _cast_place(w, g, name):
    _, R, C = w.shape
    rt = _row_tile(R, C)

    def body(w_ref, o_ref):
        o_ref[...] = w_ref[...].astype(o_ref.dtype)

    return pl.pallas_call(
        body, name=name, grid=(R // rt,),
        in_specs=[pl.BlockSpec((None, rt, C), lambda r: (g, r, 0))],
        out_specs=pl.BlockSpec((None, None, rt, C), lambda r: (0, _my_chip(), r, 0)),
        out_shape=jax.ShapeDtypeStruct((1, N_SHARD, R, C), MXU_DT), compiler_params=_cp(("arbitrary",)),
    )(w)


def _add_half(gk, la, name, out_dtype=F32):
    Gk, _, R, C = gk.shape
    Rh = R // 2
    n = Gk * N_SHARD
    gv = gk.reshape(n, 2, Rh, C)
    lv = la.reshape(n, Rh, C)
    rt = _row_tile(Rh, C)

    def body(g_ref, l_ref, o_ref):
        o_ref[...] = (g_ref[...] + l_ref[...]).astype(o_ref.dtype)

    out = pl.pallas_call(
        body, name=name, grid=(n, Rh // rt),
        in_specs=[pl.BlockSpec((None, None, rt, C), lambda i, r: (i, lax.axis_index("c"), r, 0)),
                  pl.BlockSpec((None, rt, C), lambda i, r: (i, r, 0))],
        out_specs=pl.BlockSpec((None, rt, C), lambda i, r: (i, r, 0)),
        out_shape=jax.ShapeDtypeStruct((n, Rh, C), out_dtype), compiler_params=_cp(("arbitrary", "arbitrary")),
    )(gv, lv)
    return out.reshape(Gk, N_SHARD, Rh, C)


def _sum_chips(part, landed, joined, g, name):
    _, _, Rh, C = part.shape
    rt = _row_tile(Rh, C)
    nb = Rh // rt

    def body(p_ref, l_ref, j_ref, o_ref):
        up = lambda v: v.astype(F32)
        o_ref[...] = ((up(p_ref[...]) + up(l_ref[0])) + up(l_ref[1])) + up(l_ref[2])

    return pl.pallas_call(
        body, name=name, grid=(nb,),
        in_specs=[pl.BlockSpec((None, None, rt, C), lambda r: (0, _my_chip(), r, 0)),
                  pl.BlockSpec((None, 3, rt, C), lambda r: (0, 0, r, 0)), ANY],
        out_specs=pl.BlockSpec((None, rt, C), lambda r: (g, lax.axis_index("c") * nb + r, 0)),
        out_shape=jax.ShapeDtypeStruct(joined.shape, F32), input_output_aliases={2: 0},
        compiler_params=_cp(("arbitrary",)),
    )(part, landed, joined)


def _sum_slots(lb, name):
    Gk, n, Rh, C = lb.shape
    rt = _row_tile(Rh, C)

    def body(l_ref, o_ref):
        acc = l_ref[0]
        for s in range(1, n):
            acc = acc + l_ref[s]
        o_ref[...] = acc

    return pl.pallas_call(
        body, name=name, grid=(Gk, Rh // rt),
        in_specs=[pl.BlockSpec((None, n, rt, C), lambda g, r: (g, 0, r, 0))],
        out_specs=pl.BlockSpec((None, rt, C), lambda g, r: (g, r, 0)),
        out_shape=jax.ShapeDtypeStruct((Gk, Rh, C), F32), compiler_params=_cp(("arbitrary", "arbitrary")),
    )(lb)


def _adamw(w, g, m, v, name):
    rows, cols = w.shape
    rt = _row_tile(rows, cols) if rows % 8 == 0 else rows
    c1 = 1.0 - ADAM_B1 ** ADAM_STEP
    c2 = 1.0 - ADAM_B2 ** ADAM_STEP

    def body(w_ref, g_ref, m_ref, v_ref, d_ref, mo_ref, vo_ref):
        gv = g_ref[...]
        mn = ADAM_B1 * m_ref[...] + (1.0 - ADAM_B1) * gv
        vn = ADAM_B2 * v_ref[...] + (1.0 - ADAM_B2) * (gv * gv)
        d_ref[...] = -ADAM_LR * ((mn / c1) / (jnp.sqrt(vn / c2) + ADAM_EPS) + ADAM_WD * w_ref[...])
        mo_ref[...] = mn
        vo_ref[...] = vn

    spec = pl.BlockSpec((rt, cols), lambda r: (r, 0))
    sds = jax.ShapeDtypeStruct((rows, cols), F32)
    return pl.pallas_call(
        body, name=name, grid=(rows // rt,), in_specs=[spec] * 4, out_specs=[spec] * 3,
        out_shape=[sds] * 3, compiler_params=_cp(("arbitrary",)),
    )(w, g, m, v)


def _coords():
    return lax.axis_index("x"), lax.axis_index("y"), lax.axis_index("c")


def _all_gather8(blk, name):
    m_per, n = blk.shape

    def body(x_ref, out_ref, send_sems, recv_sems, local_sem):
        x, y, c = _coords()
        me, sibling = (x, y, c), (x, y, 1 - c)
        chips = [(1 - x, y), (x, 1 - y), (1 - x, 1 - y)]

        def rows(px, py, pc):
            return out_ref.at[pl.ds((4 * px + 2 * py + pc) * m_per, m_per), :]

        def copy(k, block, to, src=None):
            return pltpu.make_async_remote_copy(
                src_ref=rows(*block) if src is None else src, dst_ref=rows(*block),
                send_sem=send_sems.at[k], recv_sem=recv_sems.at[k], device_id=to, device_id_type=MESH)

        mine = pltpu.make_async_copy(x_ref, rows(*me), local_sem)
        mine.start()
        first = [copy(0, me, sibling, src=x_ref)]
        first += [copy(1 + j, me, (*chip, c), src=x_ref) for j, chip in enumerate(chips)]
        for cp in first:
            cp.start()
        passed = [copy(4 + j, (*chip, c), sibling) for j, chip in enumerate(chips)]
        for j, chip in enumerate(chips):
            copy(1 + j, (*chip, c), me).wait_recv()
            passed[j].start()
        copy(0, sibling, me).wait_recv()
        for j, chip in enumerate(chips):
            copy(4 + j, (*chip, 1 - c), me).wait_recv()
        for cp in first + passed:
            cp.wait_send()
        mine.wait()

    return pl.pallas_call(
        body, name=name, out_shape=jax.ShapeDtypeStruct((N_DEV * m_per, n), blk.dtype),
        in_specs=[pl.BlockSpec(memory_space=pltpu.VMEM)], out_specs=pl.BlockSpec(memory_space=pltpu.VMEM),
        scratch_shapes=[pltpu.SemaphoreType.DMA((7,)), pltpu.SemaphoreType.DMA((7,)), pltpu.SemaphoreType.DMA],
        compiler_params=_cp(),
    )(blk)


def _comm_call(name, inputs, out_shapes, plan, n_remote, aliases=None):
    n_in, n_out = len(inputs), len(out_shapes)

    def body(*refs):
        in_refs, out_refs = refs[:n_in], refs[n_in:n_in + n_out]
        send_sems, recv_sems = refs[n_in + n_out:]

        def remote(k, src, dst, to):
            return pltpu.make_async_remote_copy(src_ref=src, dst_ref=dst, send_sem=send_sems.at[k],
                                                recv_sem=recv_sems.at[k], device_id=to, device_id_type=MESH)

        plan(in_refs, out_refs, remote)

    return pl.pallas_call(
        body, name=name, out_shape=out_shapes, in_specs=[ANY] * n_in, out_specs=[ANY] * n_out,
        scratch_shapes=[pltpu.SemaphoreType.DMA((n_remote,)), pltpu.SemaphoreType.DMA((n_remote,))],
        input_output_aliases=aliases or {}, compiler_params=_cp(has_side_effects=True),
    )(*inputs)


def _gather_ici_carry(placed):
    K = len(placed)

    def plan(ins, outs, remote):
        x, y, c = _coords()
        s_me = 2 * x + y
        cps = []
        for j, (px, py) in enumerate([(1 - x, y), (x, 1 - y), (1 - x, 1 - y)]):
            for k in range(K):
                rh = placed[k].shape[2] // 2
                own = outs[k].at[:, s_me, pl.ds(c * rh, rh), :]
                cps.append(remote(j * K + k, own, own, (px, py, c)))
        return cps

    shapes = [jax.ShapeDtypeStruct(p.shape, p.dtype) for p in placed]
    return _Carry(placed, shapes, {k: k for k in range(K)}, 3 * K, plan)


def _forward_sibling(placed, name):
    K = len(placed)

    def plan(ins, outs, remote):
        x, y, c = _coords()
        cps = []
        for j, (px, py) in enumerate([(1 - x, y), (x, 1 - y), (1 - x, 1 - y)]):
            for k in range(K):
                rh = placed[k].shape[2] // 2
                landed = outs[k].at[:, 2 * px + py, pl.ds(c * rh, rh), :]
                cps.append(remote(j * K + k, landed, landed, (x, y, 1 - c)))
        for cp in cps:
            cp.start()
        for j, (px, py) in enumerate([(1 - x, y), (x, 1 - y), (1 - x, 1 - y)]):
            for k in range(K):
                rh = placed[k].shape[2] // 2
                other = outs[k].at[:, 2 * px + py, pl.ds((1 - c) * rh, rh), :]
                remote(j * K + k, other, other, (x, y, 1 - c)).wait_recv()
        for cp in cps:
            cp.wait_send()

    out_shapes = [jax.ShapeDtypeStruct(p.shape, p.dtype) for p in placed]
    return _comm_call(name, placed, out_shapes, plan, 3 * K, aliases={k: k for k in range(K)})


def _gather_weights(placed, name):
    K = len(placed)

    def plan(ins, outs, remote):
        x, y, c = _coords()
        s_me = 2 * x + y
        sibling = (x, y, 1 - c)
        chips = [(1 - x, y), (x, 1 - y), (1 - x, 1 - y)]
        half = lambda k, s, cc: outs[k].at[:, s, pl.ds(cc * (placed[k].shape[2] // 2), placed[k].shape[2] // 2), :]
        sent = []
        for j, (px, py) in enumerate(chips):
            for k in range(K):
                own = half(k, s_me, c)
                cp = remote(j * K + k, own, own, (px, py, c))
                cp.start()
                sent.append(cp)
        for j, (px, py) in enumerate(chips):
            s_from = 2 * px + py
            for k in range(K):
                landed = half(k, s_from, c)
                remote(j * K + k, landed, landed, (px, py, c)).wait_recv()
                cp = remote((3 + j) * K + k, landed, landed, sibling)
                cp.start()
                sent.append(cp)
        for j, (px, py) in enumerate(chips):
            s_from = 2 * px + py
            for k in range(K):
                other = half(k, s_from, 1 - c)
                remote((3 + j) * K + k, other, other, sibling).wait_recv()
        for cp in sent:
            cp.wait_send()

    out_shapes = [jax.ShapeDtypeStruct(p.shape, p.dtype) for p in placed]
    return _comm_call(name, placed, out_shapes, plan, 6 * K, aliases={k: k for k in range(K)})


def _send_other_half(grads, name):
    K = len(grads)

    def plan(ins, outs, remote):
        x, y, c = _coords()
        cps = []
        for k in range(K):
            rh = grads[k].shape[2] // 2
            cps.append(remote(k, ins[k].at[:, :, pl.ds((1 - c) * rh, rh), :], outs[k], (x, y, 1 - c)))
        for cp in cps:
            cp.start()
        for cp in cps:
            cp.wait()

    out_shapes = [jax.ShapeDtypeStruct(g.shape[:2] + (g.shape[2] // 2, g.shape[3]), g.dtype) for g in grads]
    return _comm_call(name, grads, out_shapes, plan, K)


def _send_half_carry(grads):
    K = len(grads)

    def plan(ins, outs, remote):
        x, y, c = _coords()
        cps = []
        for k in range(K):
            rh = grads[k].shape[2] // 2
            cps.append(remote(k, ins[k].at[:, :, pl.ds((1 - c) * rh, rh), :], outs[k], (x, y, 1 - c)))
        return cps

    out_shapes = [jax.ShapeDtypeStruct(g.shape[:2] + (g.shape[2] // 2, g.shape[3]), g.dtype) for g in grads]
    return _Carry(grads, out_shapes, {}, K, plan)


def _merge_carries(carries):
    if not carries:
        return None
    if len(carries) == 1:
        return carries[0]
    srcs, shapes, aliases = [], [], {}
    for cy in carries:
        aliases.update({len(srcs) + i: len(shapes) + o for i, o in cy.aliases.items()})
        srcs += cy.srcs
        shapes += cy.out_shapes

    def plan(ins, outs, remote):
        cps, i0, o0, k0 = [], 0, 0, 0
        for cy in carries:
            shifted = functools.partial(lambda k, src, dst, to, base: remote(base + k, src, dst, to), base=k0)
            cps += cy.plan(ins[i0:i0 + len(cy.srcs)], outs[o0:o0 + len(cy.out_shapes)], shifted)
            i0, o0, k0 = i0 + len(cy.srcs), o0 + len(cy.out_shapes), k0 + cy.n
        return cps

    return _Carry(srcs, shapes, aliases, sum(cy.n for cy in carries), plan)


def _scatter_carry(parts):
    K = len(parts)

    def plan(ins, outs, remote):
        x, y, c = _coords()
        chips = [(1 - x, y), (x, 1 - y), (1 - x, 1 - y)]
        return [remote(j * K + k, ins[k].at[:, 2 * px + py], outs[k].at[:, j], (px, py, c))
                for j, (px, py) in enumerate(chips) for k in range(K)]

    out_shapes = [jax.ShapeDtypeStruct((p.shape[0], 3) + p.shape[2:], p.dtype) for p in parts]
    return _Carry(parts, out_shapes, {}, 3 * K, plan)


def _scatter_chips(parts, name):
    carry = _scatter_carry(parts)

    def plan(ins, outs, remote):
        cps = carry.plan(ins, outs, remote)
        for cp in cps:
            cp.start()
        for cp in cps:
            cp.wait()

    return _comm_call(name, parts, carry.out_shapes, plan, carry.n)


def _join_halves(joined):
    K = len(joined)

    def plan(ins, outs, remote):
        x, y, c = _coords()
        cps = []
        for k in range(K):
            rh = joined[k].shape[1] // 2
            mine = outs[k].at[:, pl.ds(c * rh, rh), :]
            cps.append(remote(k, mine, mine, (x, y, 1 - c)))
        for cp in cps:
            cp.start()
        for k in range(K):
            rh = joined[k].shape[1] // 2
            other = outs[k].at[:, pl.ds((1 - c) * rh, rh), :]
            remote(k, other, other, (x, y, 1 - c)).wait_recv()
        for cp in cps:
            cp.wait_send()

    out_shapes = [jax.ShapeDtypeStruct(h.shape, h.dtype) for h in joined]
    return _comm_call("join_halves", joined, out_shapes, plan, K, aliases={k: k for k in range(K)})


def _pack(parts):
    flat = []
    for p in parts:
        v = p.reshape(-1).astype(F32)
        pad = (-v.shape[0]) % 1024
        flat.append(jnp.pad(v, (0, pad)) if pad else v)
    return jnp.concatenate(flat).reshape(-1, 128)


def _unpack(packed, shapes):
    flat = packed.reshape(-1)
    out, off = [], 0
    for shp in shapes:
        n = math.prod(shp)
        out.append(flat[off:off + n].reshape(shp))
        off += n + (-n) % 1024
    return out


def _shard_last(a, s, n):
    return lax.dynamic_slice_in_dim(a, s * n, n, axis=a.ndim - 1)


def _rows8(*vecs):
    n = vecs[0].shape[-1]
    rows = [v.reshape(1, n).astype(F32) for v in vecs]
    return jnp.concatenate(rows + [jnp.zeros((8 - len(rows), n), F32)], axis=0)


def kernel(x, c, ada_w, ada_b, norm_pre, norm_post, ffn_w13, ffn_w2, ab_w_in, a_conv_w, a_conv_b, a_gate_w, a_gate_b, a_lam, b_conv_w, b_conv_b, b_norm_g, b_norm_b, ab_w_out, c_w_in, c_b_in, c_norm_g, c_norm_b, c_w_s, c_b_s, c_w_out, loss_target, m_ada_w, m_ada_b, m_norm_pre, m_norm_post, m_ffn_w13, m_ffn_w2, m_ab_w_in, m_a_conv_w, m_a_conv_b, m_a_gate_w, m_a_gate_b, m_a_lam, m_b_conv_w, m_b_conv_b, m_b_norm_g, m_b_norm_b, m_ab_w_out, m_c_w_in, m_c_b_in, m_c_norm_g, m_c_norm_b, m_c_w_s, m_c_b_s, m_c_w_out, v_ada_w, v_ada_b, v_norm_pre, v_norm_post, v_ffn_w13, v_ffn_w2, v_ab_w_in, v_a_conv_w, v_a_conv_b, v_a_gate_w, v_a_gate_b, v_a_lam, v_b_conv_w, v_b_conv_b, v_b_norm_g, v_b_norm_b, v_ab_w_out, v_c_w_in, v_c_b_in, v_c_norm_g, v_c_norm_b, v_c_w_s, v_c_b_s, v_c_w_out):
    S, D = x.shape[1], x.shape[2]
    W = a_lam.shape[-1]
    Fh = ffn_w13.shape[-1]
    Fq = ffn_w2.shape[2]
    xi, yi, ci = _coords()
    shard = 2 * xi + yi
    me = 4 * xi + 2 * yi + ci
    x2, tgt = x[0], loss_target[0]

    sharded_small = [norm_pre, norm_post, a_conv_w, b_conv_w, c_b_in, c_norm_g, c_norm_b]
    gathered = _all_gather8(_pack([c] + sharded_small), "gather_small")
    blocks = gathered.reshape(N_DEV, -1, 128)
    per_dev = [_unpack(blocks[d], [c.shape] + [p.shape for p in sharded_small]) for d in range(0, N_DEV, 2)]
    c_all = jnp.concatenate([_unpack(blocks[d], [c.shape])[0] for d in range(N_DEV)], axis=0)
    npre, npost, acw, bcw, cbin, cng, cnb = [jnp.concatenate([per_dev[s][1 + i] for s in range(N_SHARD)], axis=-1)
                                             for i in range(len(sharded_small))]

    ada_b_my = _shard_last(ada_b, shard, ada_w.shape[-1])[:, None, :]
    modp = _ada_fwd(c_all, ada_w, ada_b_my, "ada_fwd")
    modg = _all_gather8(modp.reshape(16, -1), "gather_mod").reshape(N_DEV, 2, 8, -1)
    mod_me = lax.dynamic_index_in_dim(modg[0::2], me, axis=2, keepdims=False)
    mod = jnp.transpose(mod_me, (1, 0, 2)).reshape(2, 3, 3, D)

    w13s, w2s = ffn_w13.reshape(4, D, Fh), ffn_w2.reshape(4, Fq, D)
    placed13 = [_cast_place(w13s, g, f"cast_w13_{g}") for g in range(4)]
    placed2 = [_cast_place(w2s, g, f"cast_w2_{g}") for g in range(4)]
    placed_mix = [_cast_place(w, 0, f"cast_mix{k}") for k, w in enumerate((ab_w_in, ab_w_out, c_w_in, c_w_out))]
    w13_first, w2_first = _gather_weights([placed13[0], placed2[0]], "gather_first")
    gather_carry = _gather_ici_carry(placed13[1:] + placed2[1:] + placed_mix)

    eye = jnp.eye(8, dtype=F32)
    dh_a = W // 8
    blockdiag = lambda w: jnp.einsum("hde,hg->hdge", w, eye).reshape(W, W)
    gw = a_gate_w[0]
    wg = jnp.concatenate([blockdiag(gw[:, :, :dh_a]), blockdiag(gw[:, :, dh_a:])], axis=1).astype(MXU_DT)
    bgv = jnp.concatenate([a_gate_b[0][:, :dh_a].reshape(-1), a_gate_b[0][:, dh_a:].reshape(-1)])
    bg = _rows8(bgv)
    cwa = jnp.concatenate([acw[0], jnp.zeros((4, W), F32)], axis=0)
    cwb = jnp.concatenate([bcw[0], jnp.zeros((1, W), F32)], axis=0)
    v512 = _rows8(a_conv_b[0], a_lam[0], b_conv_b[0], b_norm_g[0], b_norm_b[0])
    dg_b = W // 8
    gid = jnp.arange(W) // dg_b
    member = (gid[:, None] == jnp.arange(128)[None, :]).astype(F32)
    pavg = jnp.stack([member / dg_b, member]).astype(MXU_DT)
    v2d = _rows8(cbin[0])
    v1d = _rows8(cng[0], cnb[0])
    tril = jnp.tril(jnp.ones((CHUNK, CHUNK), dtype=bool))
    ws = jnp.where(tril, c_w_s[0], 0.0).astype(MXU_DT)
    bsb = jnp.repeat(jnp.transpose(c_b_s[0]), D // N_HEAD, axis=1)

    res_ws = (0.5, 1.0, 0.5)
    vps, xs, saved = [], [], []
    xc = x2
    w13g, w2v = [w13_first], [w2_first.reshape(1, 2, Fh, D)]
    for l in range(2):
        for j in range(3):
            k = 3 * l + j
            vp = _rows8(npre[l, j], mod[l, j, 0], mod[l, j, 1], mod[l, j, 2], npost[l, j])
            vps.append(vp)
            xs.append(xc)
            gi = 2 * l + j // 2
            if k == 0:
                xc, *keep = _ffn_fwd(xc, vp, w13g[0], w2v[0], res_ws[j], "ffn_fwd0", carry=gather_carry)
                keep, landed = keep[:4], keep[4:]
                full = _forward_sibling(landed, "forward_sibling")
                w13g += list(full[0:3])
                w2v += [w.reshape(1, 2, Fh, D) for w in full[3:6]]
                abin_g, about_g, cin_g, cout_g = full[6:]
                ab_ops = (abin_g[0], cwa, wg, bg, cwb, v512, pavg, about_g.reshape(D, D))
                c_ops = (cin_g[0], v2d, v1d, ws, bsb, cout_g.reshape(D, D))
            elif j != 1:
                xc, *keep = _ffn_fwd(xc, vp, w13g[gi], w2v[gi], res_ws[j], f"ffn_fwd{k}")
            elif l == 0:
                xc, *keep = _mix_ab_fwd(xc, vp, *ab_ops, res_ws[j], "mix_ab_fwd")
            else:
                xc, *keep = _mix_c_fwd(xc, vp, *c_ops, res_ws[j], "mix_c_fwd")
            saved.append(keep)

    dout, lrow = _loss_head(xc, tgt, "loss_head")

    joined = {"w13": lax.empty((4, D, Fh), F32), "w2": lax.empty((4, Fq, D), F32), "abin": lax.empty((1, D, 4 * W // 4), F32),
              "about": lax.empty((1, D // 4, D), F32), "cin": lax.empty((1, D, 2 * D // 4), F32),
              "cout": lax.empty((1, D // 4, D), F32)}

    stage = {"send": None, "scatter": None}

    def add_halves(group, got, dtype=F32):
        grads, keys, g, tag = group
        parts = [_add_half(gr, la, f"add_half_{tag}{i}", dtype) for i, (gr, la) in enumerate(zip(grads, got))]
        return parts, keys, g

    def sum_landed(group, landed_slots):
        parts, keys, g = group
        for part, key, slots in zip(parts, keys, landed_slots):
            joined[key] = _sum_chips(part, slots, joined[key], g, f"sum_chips_{key}{g}")

    def ride(call, n_own):
        scat, send = stage["scatter"], stage["send"]
        carries = ([_scatter_carry(scat[0])] if scat is not None else []) + (
            [_send_half_carry(send[0])] if send is not None else [])
        res = call(_merge_carries(carries))
        own, extra = res[:n_own], list(res[n_own:])
        stage["scatter"] = stage["send"] = None
        if scat is not None:
            sum_landed(scat, extra[:len(scat[0])])
            extra = extra[len(scat[0]):]
        if send is not None:
            stage["scatter"] = add_halves(send, extra)
        return own

    d_npre = [[None] * 3 for _ in range(2)]
    d_npost = [[None] * 3 for _ in range(2)]
    d_mod = [[None] * 3 for _ in range(2)]
    for l in (1, 0):
        for j in (2, 1, 0):
            k = 3 * l + j
            if j != 1:
                f, h, gpre, upre = saved[k]
                gi = 2 * l + j // 2
                dw13 = lax.empty((1, 4, D, Fh), F32)
                dw2v = lax.empty((1, 2, Fh, D), F32)
                df, dh0, s_a, dw13, dw2v = ride(lambda cy: _ffn_bwd_half(
                    0, xs[k], f, dout, vps[k], h, gpre, upre, w13g[gi], w2v[gi], dw13, dw2v, res_ws[j],
                    f"ffn_bwd{k}a", carry=cy), 5)
                dout, s_b, dw13, dw2v = ride(lambda cy: _ffn_bwd_half(
                    1, xs[k], f, dout, vps[k], h, gpre, upre, w13g[gi], w2v[gi], dw13, dw2v, res_ws[j],
                    f"ffn_bwd{k}b", df=df, dh0=dh0, carry=cy), 4)
                sums = s_a + s_b
                made = ([dw13, dw2v.reshape(1, 4, Fq, D)], ["w13", "w2"], gi, f"ffn{gi}")
            elif l == 0:
                dout, sums, d_abin, d_about, d_wg, d_cwa, d_cwb, d_bg, d_v512 = ride(lambda cy: _mix_ab_bwd(
                    xs[k], saved[k][0], dout, vps[k], *saved[k][1:], *ab_ops, res_ws[j], "mix_ab_bwd", carry=cy), 9)
                made = ([d_abin[None], d_about.reshape(1, 4, D // 4, D)], ["abin", "about"], 0, "ab")
            else:
                dout, sums, d_cin, d_cout, d_ws, d_bsb, d_v2, d_v1 = ride(lambda cy: _mix_c_bwd(
                    xs[k], saved[k][0], dout, vps[k], saved[k][1], *c_ops, res_ws[j], "mix_c_bwd", carry=cy), 8)
                made = ([d_cin[None], d_cout.reshape(1, 4, D // 4, D)], ["cin", "cout"], 0, "c")
            stage["send"] = made
            d_npre[l][j], d_npost[l][j] = sums[4], sums[1]
            d_mod[l][j] = jnp.stack([sums[2], sums[3], sums[0]])
    last = add_halves(stage["send"], _send_other_half(stage["send"][0], "send_half_last"), MXU_DT)
    sum_landed(last, _scatter_chips(last[0], "scatter_last"))
    grad_x = dout[None]

    dmod = jnp.stack([jnp.stack(d_mod[l]) for l in range(2)]).reshape(2, 9 * D)
    d_gate_w = jnp.concatenate([jnp.einsum("hdhe->hde", d_wg[:, :W].reshape(8, dh_a, 8, dh_a)),
                                jnp.einsum("hdhe->hde", d_wg[:, W:].reshape(8, dh_a, 8, dh_a))], axis=-1)
    d_gate_b = jnp.concatenate([d_bg[0, :W].reshape(8, dh_a), d_bg[0, W:].reshape(8, dh_a)], axis=-1)
    small_grads = [
        dmod, jnp.stack([jnp.stack(r) for r in d_npre]), jnp.stack([jnp.stack(r) for r in d_npost]),
        d_cwa[:4][None], d_v512[0][None], d_gate_w[None], d_gate_b[None], d_v512[1][None], d_cwb[:31][None],
        d_v512[2][None], d_v512[3][None], d_v512[4][None], d_v2[0][None], d_v1[0][None], d_v1[1][None],
        jnp.where(tril, d_ws, 0.0)[None], jnp.transpose(d_bsb.reshape(CHUNK, N_HEAD, D // N_HEAD).sum(-1))[None]]
    small_w = [ada_b, norm_pre, norm_post, a_conv_w, a_conv_b, a_gate_w, a_gate_b, a_lam, b_conv_w, b_conv_b,
               b_norm_g, b_norm_b, c_b_in, c_norm_g, c_norm_b, c_w_s, c_b_s]
    small_m = [m_ada_b, m_norm_pre, m_norm_post, m_a_conv_w, m_a_conv_b, m_a_gate_w, m_a_gate_b, m_a_lam, m_b_conv_w,
               m_b_conv_b, m_b_norm_g, m_b_norm_b, m_c_b_in, m_c_norm_g, m_c_norm_b, m_c_w_s, m_c_b_s]
    small_v = [v_ada_b, v_norm_pre, v_norm_post, v_a_conv_w, v_a_conv_b, v_a_gate_w, v_a_gate_b, v_a_lam, v_b_conv_w,
               v_b_conv_b, v_b_norm_g, v_b_norm_b, v_c_b_in, v_c_norm_g, v_c_norm_b, v_c_w_s, v_c_b_s]
    full_shapes = [g.shape for g in small_grads]
    loss_part = jnp.sum(lrow[0]).reshape(1, 1)
    sg_all = _all_gather8(_pack(small_grads + [loss_part]), "gather_small_grads").reshape(N_DEV, -1, 128)
    sg_sum = _sum_slots(sg_all[None], "sum_small_grads")[0]
    *g_full, loss_sum = _unpack(sg_sum, full_shapes + [(1, 1)])
    loss = loss_sum[0, 0]
    g_small = [g if g.shape == w.shape else _shard_last(g, shard, w.shape[-1]) for g, w in zip(g_full, small_w)]
    small_shapes = [w.shape for w in small_w]
    d_s, m_s, v_s = _adamw(_pack(small_w), _pack(g_small), _pack(small_m), _pack(small_v), "adamw_small")
    delta_small, newm_small, newv_small = (_unpack(a, small_shapes) for a in (d_s, m_s, v_s))

    dmod_all = jnp.stack([_unpack(sg_all[d], full_shapes[:1])[0] for d in range(N_DEV)], axis=1)
    n_ada = ada_w.shape[-1]
    g_ada_w = _ada_bwd(c_all, _shard_last(dmod_all, shard, n_ada), "ada_bwd")

    g_big = _join_halves([joined[key] for key in ("w13", "w2", "abin", "about", "cin", "cout")])

    big_w = [ffn_w13, ffn_w2, ab_w_in, ab_w_out, c_w_in, c_w_out, ada_w]
    big_m = [m_ffn_w13, m_ffn_w2, m_ab_w_in, m_ab_w_out, m_c_w_in, m_c_w_out, m_ada_w]
    big_v = [v_ffn_w13, v_ffn_w2, v_ab_w_in, v_ab_w_out, v_c_w_in, v_c_w_out, v_ada_w]
    big_g = [g.reshape(w.shape) for g, w in zip(list(g_big) + [g_ada_w], big_w)]
    big_out = []
    for k, (w, g, m, v) in enumerate(zip(big_w, big_g, big_m, big_v)):
        two_d = lambda a: a.reshape(-1, a.shape[-1])
        res = _adamw(two_d(w), two_d(g), two_d(m), two_d(v), f"adamw_big{k}")
        big_out.append([r.reshape(w.shape) for r in res])

    names = ["ada_w", "ada_b", "norm_pre", "norm_post", "ffn_w13", "ffn_w2", "ab_w_in", "a_conv_w", "a_conv_b",
             "a_gate_w", "a_gate_b", "a_lam", "b_conv_w", "b_conv_b", "b_norm_g", "b_norm_b", "ab_w_out", "c_w_in",
             "c_b_in", "c_norm_g", "c_norm_b", "c_w_s", "c_b_s", "c_w_out"]
    big_names = ["ffn_w13", "ffn_w2", "ab_w_in", "ab_w_out", "c_w_in", "c_w_out", "ada_w"]
    small_names = ["ada_b", "norm_pre", "norm_post", "a_conv_w", "a_conv_b", "a_gate_w", "a_gate_b", "a_lam",
                   "b_conv_w", "b_conv_b", "b_norm_g", "b_norm_b", "c_b_in", "c_norm_g", "c_norm_b", "c_w_s", "c_b_s"]
    table = {}
    for k, n in enumerate(big_names):
        table[n] = (big_g[k], *big_out[k])
    for k, n in enumerate(small_names):
        table[n] = (g_small[k], delta_small[k], newm_small[k], newv_small[k])
    outs = [loss, grad_x]
    for field in range(4):
        outs += [table[n][field] for n in names]
    return tuple(outs)
```

```python
import functools
import math

import jax
import jax.numpy as jnp
from jax import lax
from jax.experimental import pallas as pl
from jax.experimental.pallas import tpu as pltpu

F32 = jnp.float32
MXU_DT = jnp.bfloat16
EPS = 1e-6
LRU_C = 8.0
N_SHARD = 4
N_DEV = 8
CHUNK = 128
N_HEAD = 8
ADAM_LR, ADAM_B1, ADAM_B2, ADAM_EPS, ADAM_WD, ADAM_STEP = 0.001, 0.9, 0.999, 1e-08, 0.01, 10
GELU_K0 = math.sqrt(2.0 / math.pi)
GELU_K1 = 0.044715
VMEM_LIMIT = 58 * 1024 * 1024
MESH = pl.DeviceIdType.MESH
ANY = pl.BlockSpec(memory_space=pl.ANY)


def _cp(sem=None, **kw):
    if sem is not None:
        kw["dimension_semantics"] = sem
    return pltpu.CompilerParams(vmem_limit_bytes=VMEM_LIMIT, **kw)


def _resident(a):
    return pl.BlockSpec(a.shape, lambda *_: (0,) * a.ndim, pipeline_mode=pl.Buffered(1))


def _whole(shape):
    return pl.BlockSpec(shape, lambda *_: (0,) * len(shape))


def _dot(a, b):
    return jnp.dot(a.astype(MXU_DT), b.astype(MXU_DT), preferred_element_type=F32)


def _dot_nt(a, b):
    return lax.dot_general(a.astype(MXU_DT), b.astype(MXU_DT), (((1,), (1,)), ((), ())), preferred_element_type=F32)


def _dot_tn(a, b):
    return lax.dot_general(a.astype(MXU_DT), b.astype(MXU_DT), (((0,), (0,)), ((), ())), preferred_element_type=F32)


def _dot_hi(a, b):
    return jnp.dot(a, b, precision=lax.Precision.HIGHEST, preferred_element_type=F32)


def _sig(x):
    return 1.0 / (1.0 + jnp.exp(-x))


def _logsig(x):
    return jnp.minimum(x, 0.0) - jnp.log(1.0 + jnp.exp(-jnp.abs(x)))


def _gelu(x):
    x2 = x * x
    t = jnp.tanh(GELU_K0 * (x + GELU_K1 * x * x2))
    val = 0.5 * x * (1.0 + t)
    der = 0.5 * (1.0 + t) + 0.5 * x * (1.0 - t * t) * (GELU_K0 * (1.0 + 3.0 * GELU_K1 * x2))
    return val, der


def _neg_expm1(x):
    small = -(x * (1.0 + x * (0.5 + x * (1.0 / 6.0 + x * (1.0 / 24.0)))))
    return jnp.where(x > -0.05, small, 1.0 - jnp.exp(x))


def _colsum(v):
    return jnp.sum(v, axis=0, keepdims=True)


def _rowmean(v):
    return jnp.mean(v, axis=-1, keepdims=True)


def _copy_out(pairs, sem):
    for src, dst in pairs:
        cp = pltpu.make_async_copy(src, dst, sem)
        cp.start()
        cp.wait()


class _Carry:
    def __init__(self, srcs, out_shapes, aliases, n, plan):
        self.srcs, self.out_shapes, self.aliases, self.n, self.plan = list(srcs), list(out_shapes), aliases, n, plan


def _carry_args(carry, n_in, n_out):
    if carry is None:
        return [], [], [], [], {}
    sems = [pltpu.SemaphoreType.DMA((carry.n,)), pltpu.SemaphoreType.DMA((carry.n,))]
    aliases = {n_in + i: n_out + o for i, o in carry.aliases.items()}
    return [ANY] * len(carry.srcs), [ANY] * len(carry.out_shapes), carry.out_shapes, sems, aliases


def _split_refs(refs, n_in, n_out, n_scratch, carry):
    nci, nco = (len(carry.srcs), len(carry.out_shapes)) if carry is not None else (0, 0)
    cuts = [n_in, nci, n_out, nco, n_scratch]
    parts, i = [], 0
    for n in cuts:
        parts.append(refs[i:i + n])
        i += n
    ins, cins, outs, couts, scr = parts
    return ins, outs, scr, (cins, couts, refs[i:])


def _carry_run(carry, carry_refs, first, last):
    if carry is None:
        return
    cins, couts, (send_sems, recv_sems) = carry_refs

    def remote(k, src, dst, to):
        return pltpu.make_async_remote_copy(src_ref=src, dst_ref=dst, send_sem=send_sems.at[k],
                                            recv_sem=recv_sems.at[k], device_id=to, device_id_type=MESH)

    @pl.when(first)
    def _():
        for cp in carry.plan(cins, couts, remote):
            cp.start()

    @pl.when(last)
    def _():
        for cp in carry.plan(cins, couts, remote):
            cp.wait()


def _shift_down(v, k):
    return v if k == 0 else pltpu.roll(v, k, 0)


def _shift_up(v, k):
    return v if k == 0 else pltpu.roll(v, v.shape[0] - k, 0)


def _shell_pre(xv, vp_ref):
    r = lax.rsqrt(_rowmean(xv * xv) + EPS)
    return xv * r * (vp_ref[0:1, :] * (1.0 + vp_ref[2:3, :])) + vp_ref[1:2, :]


def _shell_post(xv, fv, vp_ref, res_w):
    r = lax.rsqrt(_rowmean(fv * fv) + EPS)
    return xv + fv * r * (res_w * (1.0 + vp_ref[3:4, :]) * vp_ref[4:5, :])


def _shell_post_bwd(fv, dov, vp_ref, res_w, s_ref):
    r = lax.rsqrt(_rowmean(fv * fv) + EPS)
    fn = fv * r
    pg = vp_ref[4:5, :]
    dy = (res_w * (1.0 + vp_ref[3:4, :])) * dov
    if s_ref is not None:
        s_ref[0:1, :] += _colsum(res_w * fn * pg * dov)
        s_ref[1:2, :] += _colsum(fn * dy)
    q = dy * pg
    return r * (q - fn * _rowmean(fn * q))


def _shell_pre_bwd(xv, dh, dov, vp_ref, s_ref):
    r = lax.rsqrt(_rowmean(xv * xv) + EPS)
    xn = xv * r
    pg = vp_ref[0:1, :]
    sc1 = 1.0 + vp_ref[2:3, :]
    s_ref[2:3, :] += _colsum(dh)
    s_ref[3:4, :] += _colsum(xn * pg * dh)
    s_ref[4:5, :] += _colsum(xn * dh * sc1)
    q = dh * (sc1 * pg)
    return dov + r * (q - xn * _rowmean(xn * q))


def _loss_head(y, tgt, name, tm=512):
    S, D = y.shape

    def body(y_ref, t_ref, dy_ref, l_ref):
        @pl.when(pl.program_id(0) == 0)
        def _():
            l_ref[...] = jnp.zeros_like(l_ref)

        e = y_ref[...] - t_ref[...]
        dy_ref[...] = e * (1.0 / D)
        l_ref[0:1, :] += _colsum(e * e) * (0.5 / D)

    return pl.pallas_call(
        body, name=name, grid=(S // tm,),
        in_specs=[pl.BlockSpec((tm, D), lambda t: (t, 0)), pl.BlockSpec((tm, D), lambda t: (t, 0))],
        out_specs=[pl.BlockSpec((tm, D), lambda t: (t, 0)), pl.BlockSpec((8, D), lambda t: (0, 0))],
        out_shape=[jax.ShapeDtypeStruct((S, D), F32), jax.ShapeDtypeStruct((8, D), F32)],
        compiler_params=_cp(("arbitrary",)),
    )(y, tgt)


MXU_COLS = 256


def _col_chunks(n, width=2 * MXU_COLS):
    return [(c0, min(c0 + width, n)) for c0 in range(0, n, width)]


def _ffn_fwd(x, vp, w13g, w2v, res_w, name, carry=None, tm=512):
    S, D = x.shape
    Fh = w13g.shape[-1]
    T = S // tm

    def body(*refs):
        (x_ref, vp_ref, w13_ref, w2_ref), (xo_ref, f_ref, h_ref, g_ref, u_ref), _, carry_refs = _split_refs(
            refs, 4, 5, 0, carry)
        _carry_run(carry, carry_refs, pl.program_id(0) == 0, pl.program_id(0) == T - 1)
        xv = x_ref[...]
        hb = _shell_pre(xv, vp_ref).astype(MXU_DT)
        h_ref[...] = hb
        acc = None
        for j in range(2):
            gg = _dot(hb, w13_ref[j])
            uu = _dot(hb, w13_ref[2 + j])
            g_ref[:, j * Fh:(j + 1) * Fh] = gg.astype(g_ref.dtype)
            u_ref[:, j * Fh:(j + 1) * Fh] = uu.astype(u_ref.dtype)
            part = _dot(gg * _sig(gg) * uu, w2_ref[j])
            acc = part if acc is None else acc + part
        f_ref[...] = acc
        xo_ref[...] = _shell_post(xv, acc, vp_ref, res_w)

    tile = lambda w: pl.BlockSpec((tm, w), lambda t: (t, 0))
    c_in, c_out, c_shapes, c_sems, c_alias = _carry_args(carry, 4, 5)
    return pl.pallas_call(
        body, name=name, grid=(T,),
        in_specs=[tile(D), _whole((8, D)),
                  pl.BlockSpec((None, 4, D, Fh), lambda t: (0, 0, 0, 0), pipeline_mode=pl.Buffered(1)),
                  pl.BlockSpec((None, 2, Fh, D), lambda t: (0, 0, 0, 0), pipeline_mode=pl.Buffered(1))] + c_in,
        out_specs=[tile(D), tile(D), tile(D), tile(2 * Fh), tile(2 * Fh)] + c_out,
        out_shape=[jax.ShapeDtypeStruct((S, D), F32), jax.ShapeDtypeStruct((S, D), F32),
                   jax.ShapeDtypeStruct((S, D), MXU_DT), jax.ShapeDtypeStruct((S, 2 * Fh), MXU_DT),
                   jax.ShapeDtypeStruct((S, 2 * Fh), MXU_DT)] + c_shapes,
        scratch_shapes=c_sems, input_output_aliases=c_alias,
        compiler_params=_cp(("arbitrary",), has_side_effects=carry is not None),
    )(x, vp, w13g, w2v, *(carry.srcs if carry is not None else []))


def _ffn_bwd_half(j, x, f, dout, vp, h, gpre, upre, w13g, w2v, dw13, dw2v, res_w, name, df=None, dh0=None, carry=None,
                  tm=256):
    S, D = h.shape
    Fh = w13g.shape[-1]
    T = S // tm
    first = j == 0
    n_in, n_out = (11, 5) if first else (13, 4)

    def body(*refs):
        ins, outs, (a1, a3, a2, sem), carry_refs = _split_refs(refs, n_in, n_out, 4, carry)
        if first:
            f_ref, do_ref, vp_ref, h_ref, g_ref, u_ref, w1_ref, w3_ref, w2_ref, _, _ = ins
            df_ref, dh_ref, s_ref, dw13_ref, dw2_ref = outs
        else:
            x_ref, do_ref, vp_ref, h_ref, g_ref, u_ref, w1_ref, w3_ref, w2_ref, dfi_ref, dh0_ref, _, _ = ins
            dx_ref, s_ref, dw13_ref, dw2_ref = outs
        t = pl.program_id(0)
        _carry_run(carry, carry_refs, t == 0, t == T - 1)

        @pl.when(t == 0)
        def _():
            for ref in (a1, a3, a2, s_ref):
                ref[...] = jnp.zeros_like(ref)

        hv = h_ref[...]
        if first:
            dfv = _shell_post_bwd(f_ref[...], do_ref[...], vp_ref, res_w, s_ref).astype(MXU_DT)
            df_ref[...] = dfv
        else:
            dfv = dfi_ref[...]
        dh = None
        for c0, c1 in _col_chunks(Fh):
            gg = g_ref[:, c0:c1].astype(F32)
            uu = u_ref[:, c0:c1].astype(F32)
            sg = _sig(gg)
            si = gg * sg
            da = _dot_nt(dfv, w2_ref[c0:c1, :])
            a2[c0:c1, :] += _dot_tn(si * uu, dfv)
            dg = da * uu * (sg * (1.0 + gg * (1.0 - sg)))
            du = da * si
            a1[:, c0:c1] += _dot_tn(hv, dg)
            a3[:, c0:c1] += _dot_tn(hv, du)
            part = _dot_nt(dg, w1_ref[:, c0:c1]) + _dot_nt(du, w3_ref[:, c0:c1])
            dh = part if dh is None else dh + part
        if first:
            dh_ref[...] = dh
        else:
            dx_ref[...] = _shell_pre_bwd(x_ref[...], dh0_ref[...] + dh, do_ref[...], vp_ref, s_ref)

        @pl.when(t == T - 1)
        def _():
            _copy_out(((a1, dw13_ref.at[0, j]), (a3, dw13_ref.at[0, 2 + j]), (a2, dw2_ref.at[0, j])), sem)

    tile = lambda w: pl.BlockSpec((tm, w), lambda t: (t, 0))
    half = pl.BlockSpec((tm, Fh), lambda t: (t, j))
    weights = [pl.BlockSpec((None, None, D, Fh), lambda t: (0, j, 0, 0), pipeline_mode=pl.Buffered(1)),
               pl.BlockSpec((None, None, D, Fh), lambda t: (0, 2 + j, 0, 0), pipeline_mode=pl.Buffered(1)),
               pl.BlockSpec((None, None, Fh, D), lambda t: (0, j, 0, 0), pipeline_mode=pl.Buffered(1))]
    sd = jax.ShapeDtypeStruct
    grads = [sd(dw13.shape, F32), sd(dw2v.shape, F32)]
    scratch = [pltpu.VMEM((D, Fh), F32), pltpu.VMEM((D, Fh), F32), pltpu.VMEM((Fh, D), F32), pltpu.SemaphoreType.DMA]
    c_in, c_out, c_shapes, c_sems, c_alias = _carry_args(carry, n_in, n_out)
    params = _cp(("arbitrary",), has_side_effects=carry is not None)
    extra = carry.srcs if carry is not None else []
    if first:
        return pl.pallas_call(
            body, name=name, grid=(T,),
            in_specs=[tile(D), tile(D), _whole((8, D)), tile(D), half, half] + weights + [ANY, ANY] + c_in,
            out_specs=[tile(D), tile(D), _whole((8, D)), ANY, ANY] + c_out,
            out_shape=[sd((S, D), MXU_DT), sd((S, D), F32), sd((8, D), F32)] + grads + c_shapes,
            scratch_shapes=scratch + c_sems, input_output_aliases={9: 3, 10: 4, **c_alias}, compiler_params=params,
        )(f, dout, vp, h, gpre, upre, w13g, w13g, w2v, dw13, dw2v, *extra)
    return pl.pallas_call(
        body, name=name, grid=(T,),
        in_specs=[tile(D), tile(D), _whole((8, D)), tile(D), half, half] + weights + [tile(D), tile(D), ANY, ANY] + c_in,
        out_specs=[tile(D), _whole((8, D)), ANY, ANY] + c_out,
        out_shape=[sd((S, D), F32), sd((8, D), F32)] + grads + c_shapes,
        scratch_shapes=scratch + c_sems, input_output_aliases={11: 2, 12: 3, **c_alias}, compiler_params=params,
    )(x, dout, vp, h, gpre, upre, w13g, w13g, w2v, df, dh0, dw13, dw2v, *extra)


def _scan_fwd(a, u, rows):
    n = a.shape[0]
    d = 1
    while d < n:
        m = rows >= d
        u = u + a * jnp.where(m, _shift_down(u, d), 0.0)
        a = a * jnp.where(m, _shift_down(a, d), 1.0)
        d *= 2
    return a, u


def _scan_bwd(a, u, rows):
    n = a.shape[0]
    d = 1
    while d < n:
        m = rows < n - d
        u = u + a * jnp.where(m, _shift_up(u, d), 0.0)
        a = a * jnp.where(m, _shift_up(a, d), 1.0)
        d *= 2
    return a, u


def _causal_conv(ext, w_ref, K, halo, tm):
    acc = None
    for k in range(K):
        term = w_ref[k:k + 1, :] * _shift_down(ext, K - 1 - k)[halo:, :]
        acc = term if acc is None else acc + term
    return acc


def _anticausal_conv(ext, w_ref, K, tm):
    acc = None
    for k in range(K):
        term = w_ref[k:k + 1, :] * _shift_up(ext, K - 1 - k)[:tm, :]
        acc = term if acc is None else acc + term
    return acc


def _dot_split(a, b):
    hi = a.astype(MXU_DT)
    lo = (a - hi.astype(F32)).astype(MXU_DT)
    return jnp.dot(hi, b, preferred_element_type=F32) + jnp.dot(lo, b, preferred_element_type=F32)


def _group_mean(v, p_ref):
    return _dot_nt_exact(_dot_split(v, p_ref[0]), p_ref[1])


def _dot_nt_exact(a, bt):
    hi = a.astype(MXU_DT)
    lo = (a - hi.astype(F32)).astype(MXU_DT)
    dims = (((1,), (1,)), ((), ()))
    return (lax.dot_general(hi, bt, dims, preferred_element_type=F32)
            + lax.dot_general(lo, bt, dims, preferred_element_type=F32))


def _group_norm(vc, p_ref, g, b):
    mu = _group_mean(vc, p_ref)
    dv = vc - mu
    rstd = lax.rsqrt(_group_mean(dv * dv, p_ref) + EPS)
    vhat = dv * rstd
    return vhat, rstd, vhat * g + b


def _lru_gates(axc, wg_ref, bg_ref, lam, W):
    gp = _dot(axc, wg_ref[...]) + bg_ref[0:1, :]
    r = _sig(gp[:, :W])
    i = _sig(gp[:, W:])
    ls = _logsig(lam)
    L = (LRU_C * ls) * r
    a = jnp.exp(L)
    mult = jnp.sqrt(_neg_expm1(2.0 * L))
    return r, i, ls, a, mult


def _mix_ab_fwd(x, vp, win4, cwa, wg, bg, cwb, v512, pavg, wout, res_w, name, tm=256):
    S, D = x.shape
    W = win4.shape[-1]
    KA, KB, HA, HB = 4, 31, 8, 32

    def body(x_ref, vp_ref, win_ref, cwa_ref, wg_ref, bg_ref, cwb_ref, v_ref, p_ref, wo_ref,
             xo_ref, f_ref, h_ref, sav_ref, ahalo, bhalo, carry):
        @pl.when(pl.program_id(0) == 0)
        def _():
            ahalo[...] = jnp.zeros_like(ahalo)
            bhalo[...] = jnp.zeros_like(bhalo)
            carry[...] = jnp.zeros_like(carry)

        xv = x_ref[...]
        hv = _shell_pre(xv, vp_ref).astype(MXU_DT)
        h_ref[...] = hv
        a_gate = _dot(hv, win_ref[0])
        axp = _dot(hv, win_ref[1])
        b_val = _dot(hv, win_ref[2])
        b_gate = _dot(hv, win_ref[3])
        rows = lax.broadcasted_iota(jnp.int32, (tm, W), 0)
        axc = _causal_conv(jnp.concatenate([ahalo[...], axp], axis=0), cwa_ref, KA, HA, tm) + v_ref[0:1, :]
        ahalo[...] = axp[tm - HA:, :]
        r, i, ls, a, mult = _lru_gates(axc, wg_ref, bg_ref, v_ref[1:2, :], W)
        acum, hloc = _scan_fwd(a, mult * i * axc, rows)
        hs = hloc + acum * carry[7:8, :]
        carry[...] = hs[tm - 8:, :]
        ya = hs * _gelu(a_gate)[0]
        bv = b_val * _sig(b_gate)
        vc = _causal_conv(jnp.concatenate([bhalo[...], bv], axis=0), cwb_ref, KB, HB, tm) + v_ref[2:3, :]
        bhalo[...] = bv[tm - HB:, :]
        _, _, vn = _group_norm(vc, p_ref, v_ref[3:4, :], v_ref[4:5, :])
        yb = vn * _sig(vn)
        fv = _dot(ya, wo_ref[0:W, :]) + _dot(yb, wo_ref[W:, :])
        f_ref[...] = fv
        xo_ref[...] = _shell_post(xv, fv, vp_ref, res_w)
        for n, val in enumerate((hs, axp, axc, bv, vc)):
            sav_ref[:, n * W:(n + 1) * W] = val

    tile = lambda w: pl.BlockSpec((tm, w), lambda t: (t, 0))
    sd = jax.ShapeDtypeStruct
    return pl.pallas_call(
        body, name=name, grid=(S // tm,),
        in_specs=[tile(D), _whole((8, D))] + [_resident(a) for a in (win4, cwa, wg, bg, cwb, v512, pavg, wout)],
        out_specs=[tile(D), tile(D), tile(D), tile(5 * W)],
        out_shape=[sd((S, D), F32), sd((S, D), F32), sd((S, D), MXU_DT), sd((S, 5 * W), F32)],
        scratch_shapes=[pltpu.VMEM((HA, W), F32), pltpu.VMEM((HB, W), F32), pltpu.VMEM((8, W), F32)],
        compiler_params=_cp(("arbitrary",)),
    )(x, vp, win4, cwa, wg, bg, cwb, v512, pavg, wout)


def _mix_ab_bwd(x, f, dout, vp, h, sav, win4, cwa, wg, bg, cwb, v512, pavg, wout, res_w, name, carry=None, tm=256):
    S, D = x.shape
    W = win4.shape[-1]
    T = S // tm
    KA, KB, HA, HB = 4, 31, 8, 32

    def body(*refs):
        ins, outs, scr, carry_refs = _split_refs(refs, 15, 9, 8, carry)
        (x_ref, f_ref, do_ref, vp_ref, h_ref, sav_ref, prev_ref,
         win_ref, cwa_ref, wg_ref, bg_ref, cwb_ref, v_ref, p_ref, wo_ref) = ins
        hs_ref, axp_ref, axc_ref, bv_ref, vc_ref = (sav_ref.at[:, n * W:(n + 1) * W] for n in range(5))
        hsp_ref = prev_ref.at[HB - HA:, 0:W]
        axpp_ref = prev_ref.at[HB - HA:, W:2 * W]
        bvp_ref = prev_ref.at[:, 3 * W:4 * W]
        dx_ref, s_ref, dwin_out, dwo_out, dwg_out, dcwa_ref, dcwb_ref, dbg_ref, dv_ref = outs
        danext, dvnext, gfirst, afirst, dwin_ref, dwo_ref, dwg_ref, sem = scr
        t = pl.program_id(0)
        _carry_run(carry, carry_refs, t == 0, t == T - 1)

        @pl.when(t == 0)
        def _():
            for ref in (s_ref, dwin_ref, dwo_ref, dwg_ref, dcwa_ref, dcwb_ref, dbg_ref, dv_ref, danext, dvnext,
                        gfirst, afirst):
                ref[...] = jnp.zeros_like(ref)

        notfirst = jnp.where(t < T - 1, 1.0, 0.0).astype(F32)
        hv = h_ref[...]
        dov = do_ref[...]
        dfv = _shell_post_bwd(f_ref[...], dov, vp_ref, res_w, s_ref).astype(MXU_DT)
        a_gate = _dot(hv, win_ref[0])
        b_val = _dot(hv, win_ref[2])
        b_gate = _dot(hv, win_ref[3])
        rows = lax.broadcasted_iota(jnp.int32, (tm, W), 0)
        ge, dge = _gelu(a_gate)
        hsv = hs_ref[...]
        ya = hsv * ge
        vhat, rstd, vn = _group_norm(vc_ref[...], p_ref, v_ref[3:4, :], v_ref[4:5, :])
        sgn = _sig(vn)
        yb = vn * sgn
        dma = _dot_nt(dfv, wo_ref[0:W, :])
        dmb = _dot_nt(dfv, wo_ref[W:, :])
        dwo_ref[0:W, :] += _dot_tn(ya, dfv)
        dwo_ref[W:, :] += _dot_tn(yb, dfv)
        dhs = dma * ge
        d_a_gate = dma * hsv * dge
        axcv = axc_ref[...]
        lam = v_ref[1:2, :]
        r, i, ls, a, mult = _lru_gates(axcv, wg_ref, bg_ref, lam, W)
        ash = jnp.where(rows == tm - 1, afirst[0:1, :], _shift_up(a, 1))
        asuf, gloc = _scan_bwd(ash, dhs, rows)
        gsc = gloc + asuf * gfirst[0:1, :]
        afirst[...] = a[0:8, :]
        gfirst[...] = gsc[0:8, :]
        hprev = jnp.where(rows == 0, hsp_ref[HA - 1:HA, :] * notfirst, _shift_down(hsv, 1))
        da = gsc * hprev
        dL = da * a - gsc * (i * axcv) * (a * a) / mult
        dix = gsc * mult
        daxc = dix * i
        dr = dL * (LRU_C * ls)
        dv_ref[1:2, :] += _colsum(dL * r) * (LRU_C * _sig(-lam))
        dgate = jnp.concatenate([dr * r * (1.0 - r), (dix * axcv) * i * (1.0 - i)], axis=1)
        dbg_ref[0:1, :] += _colsum(dgate)
        dwg_ref[...] += _dot_tn(axcv, dgate)
        daxc = daxc + _dot_nt(dgate, wg_ref[...])
        daxp = _anticausal_conv(jnp.concatenate([daxc, danext[...]], axis=0), cwa_ref, KA, tm)
        ext = jnp.concatenate([axpp_ref[...] * notfirst, axp_ref[...]], axis=0)
        for k in range(KA):
            dcwa_ref[k:k + 1, :] += _colsum(daxc * _shift_down(ext, KA - 1 - k)[HA:, :])
        dv_ref[0:1, :] += _colsum(daxc)
        danext[...] = daxc[0:HA, :]
        dvn = dmb * (sgn * (1.0 + vn * (1.0 - sgn)))
        dv_ref[4:5, :] += _colsum(dvn)
        dv_ref[3:4, :] += _colsum(dvn * vhat)
        dvh = dvn * v_ref[3:4, :]
        dvc = rstd * (dvh - _group_mean(dvh, p_ref) - vhat * _group_mean(dvh * vhat, p_ref))
        dbv = _anticausal_conv(jnp.concatenate([dvc, dvnext[...]], axis=0), cwb_ref, KB, tm)
        ext = jnp.concatenate([bvp_ref[...] * notfirst, bv_ref[...]], axis=0)
        for k in range(KB):
            dcwb_ref[k:k + 1, :] += _colsum(dvc * _shift_down(ext, KB - 1 - k)[HB:, :])
        dv_ref[2:3, :] += _colsum(dvc)
        dvnext[...] = dvc[0:HB, :]
        sb = _sig(b_gate)
        dzs = (d_a_gate, daxp, dbv * sb, dbv * b_val * sb * (1.0 - sb))
        dh = None
        for s in range(4):
            part = _dot_nt(dzs[s], win_ref[s])
            dh = part if dh is None else dh + part
            dwin_ref[s] += _dot_tn(hv, dzs[s])
        dx_ref[...] = _shell_pre_bwd(x_ref[...], dh, dov, vp_ref, s_ref)

        @pl.when(t == T - 1)
        def _():
            _copy_out(((dwin_ref, dwin_out), (dwo_ref, dwo_out), (dwg_ref, dwg_out)), sem)

    tile = lambda w: pl.BlockSpec((tm, w), lambda t: (T - 1 - t, 0))
    prev = pl.BlockSpec((HB, 5 * W), lambda t: (jnp.maximum((T - 1 - t) * (tm // HB) - 1, 0), 0))
    out_shapes = [(S, D), (8, D), win4.shape, wout.shape, wg.shape, (8, W), (32, W), (8, 2 * W), (8, W)]
    c_in, c_out, c_shapes, c_sems, c_alias = _carry_args(carry, 15, 9)
    return pl.pallas_call(
        body, name=name, grid=(T,),
        in_specs=[tile(D), tile(D), tile(D), _whole((8, D)), tile(D), tile(5 * W), prev]
        + [_resident(a) for a in (win4, cwa, wg, bg, cwb, v512, pavg, wout)] + c_in,
        out_specs=[tile(D), _whole((8, D)), ANY, ANY, ANY] + [_whole(s) for s in out_shapes[5:]] + c_out,
        out_shape=[jax.ShapeDtypeStruct(s, F32) for s in out_shapes] + c_shapes,
        scratch_shapes=[pltpu.VMEM((HA, W), F32), pltpu.VMEM((HB, W), F32), pltpu.VMEM((8, W), F32),
                        pltpu.VMEM((8, W), F32), pltpu.VMEM(win4.shape, F32), pltpu.VMEM(wout.shape, F32),
                        pltpu.VMEM(wg.shape, F32), pltpu.SemaphoreType.DMA] + c_sems,
        input_output_aliases=c_alias, compiler_params=_cp(("arbitrary",), has_side_effects=carry is not None),
    )(x, f, dout, vp, h, sav, sav, win4, cwa, wg, bg, cwb, v512, pavg, wout,
      *(carry.srcs if carry is not None else []))


def _mix_c_core(hv, win_ref, v2_ref, v1_ref, ws_ref, bsb_ref, tm, D):
    zp = jnp.concatenate([_dot(hv, win_ref[s]) for s in range(4)], axis=1) + v2_ref[0:1, :]
    z, dz = _gelu(zp)
    u, v = z[:, :D], z[:, D:]
    mu = _rowmean(v)
    dv = v - mu
    rstd = lax.rsqrt(_rowmean(dv * dv) + EPS)
    vhat = dv * rstd
    vn = vhat * v1_ref[0:1, :] + v1_ref[1:2, :]
    rows_out = []
    for cidx in range(tm // CHUNK):
        blk = vn[cidx * CHUNK:(cidx + 1) * CHUNK, :]
        heads = [_dot(ws_ref[hd], blk[:, hd * CHUNK:(hd + 1) * CHUNK]) for hd in range(N_HEAD)]
        rows_out.append(jnp.concatenate(heads, axis=1) + bsb_ref[...])
    mixed = jnp.concatenate(rows_out, axis=0)
    return dz, u, rstd, vhat, vn, mixed


def _mix_c_fwd(x, vp, win4, v2d, v1d, ws, bsb, wout, res_w, name, tm=512):
    S, D = x.shape

    def body(x_ref, vp_ref, win_ref, v2_ref, v1_ref, ws_ref, bsb_ref, wo_ref, xo_ref, f_ref, h_ref):
        xv = x_ref[...]
        hv = _shell_pre(xv, vp_ref).astype(MXU_DT)
        h_ref[...] = hv
        _, u, _, _, _, mixed = _mix_c_core(hv, win_ref, v2_ref, v1_ref, ws_ref, bsb_ref, tm, D)
        fv = _dot(u * mixed, wo_ref[...])
        f_ref[...] = fv
        xo_ref[...] = _shell_post(xv, fv, vp_ref, res_w)

    tile = pl.BlockSpec((tm, D), lambda t: (t, 0))
    sd = jax.ShapeDtypeStruct
    return pl.pallas_call(
        body, name=name, grid=(S // tm,),
        in_specs=[tile, _whole((8, D))] + [_resident(a) for a in (win4, v2d, v1d, ws, bsb, wout)],
        out_specs=[tile, tile, tile],
        out_shape=[sd((S, D), F32), sd((S, D), F32), sd((S, D), MXU_DT)], compiler_params=_cp(("arbitrary",)),
    )(x, vp, win4, v2d, v1d, ws, bsb, wout)


def _mix_c_bwd(x, f, dout, vp, h, win4, v2d, v1d, ws, bsb, wout, res_w, name, carry=None, tm=256):
    S, D = x.shape

    def body(*refs):
        ins, outs, (dwin_ref, dwo_ref, sem), carry_refs = _split_refs(refs, 11, 8, 3, carry)
        x_ref, f_ref, do_ref, vp_ref, h_ref, win_ref, v2_ref, v1_ref, ws_ref, bsb_ref, wo_ref = ins
        dx_ref, s_ref, dwin_out, dwo_out, dws_ref, dbsb_ref, dv2_ref, dv1_ref = outs
        _carry_run(carry, carry_refs, pl.program_id(0) == 0, pl.program_id(0) == S // tm - 1)

        @pl.when(pl.program_id(0) == 0)
        def _():
            for ref in (s_ref, dwin_ref, dwo_ref, dws_ref, dbsb_ref, dv2_ref, dv1_ref):
                ref[...] = jnp.zeros_like(ref)

        hv = h_ref[...]
        dov = do_ref[...]
        dfv = _shell_post_bwd(f_ref[...], dov, vp_ref, res_w, s_ref).astype(MXU_DT)
        dz, u, rstd, vhat, vn, mixed = _mix_c_core(hv, win_ref, v2_ref, v1_ref, ws_ref, bsb_ref, tm, D)
        dp = _dot_nt(dfv, wo_ref[...])
        dwo_ref[...] += _dot_tn(u * mixed, dfv)
        du = dp * mixed
        dmx = dp * u
        rows_out = []
        for cidx in range(tm // CHUNK):
            dblk = dmx[cidx * CHUNK:(cidx + 1) * CHUNK, :]
            vblk = vn[cidx * CHUNK:(cidx + 1) * CHUNK, :]
            dbsb_ref[...] += dblk
            heads = []
            for hd in range(N_HEAD):
                dsl = dblk[:, hd * CHUNK:(hd + 1) * CHUNK]
                heads.append(_dot_tn(ws_ref[hd], dsl))
                dws_ref[hd] += _dot_nt(dsl, vblk[:, hd * CHUNK:(hd + 1) * CHUNK])
            rows_out.append(jnp.concatenate(heads, axis=1))
        dvn = jnp.concatenate(rows_out, axis=0)
        dv1_ref[1:2, :] += _colsum(dvn)
        dv1_ref[0:1, :] += _colsum(dvn * vhat)
        dvh = dvn * v1_ref[0:1, :]
        dv = rstd * (dvh - _rowmean(dvh) - vhat * _rowmean(dvh * vhat))
        dzp = jnp.concatenate([du, dv], axis=1) * dz
        dv2_ref[0:1, :] += _colsum(dzp)
        W = win_ref.shape[-1]
        dh = None
        for s in range(4):
            dzs = dzp[:, s * W:(s + 1) * W]
            part = _dot_nt(dzs, win_ref[s])
            dh = part if dh is None else dh + part
            dwin_ref[s] += _dot_tn(hv, dzs)
        dx_ref[...] = _shell_pre_bwd(x_ref[...], dh, dov, vp_ref, s_ref)

        @pl.when(pl.program_id(0) == S // tm - 1)
        def _():
            _copy_out(((dwin_ref, dwin_out), (dwo_ref, dwo_out)), sem)

    tile = pl.BlockSpec((tm, D), lambda t: (t, 0))
    out_shapes = [(S, D), (8, D), win4.shape, wout.shape, ws.shape, bsb.shape, (8, 2 * D), (8, D)]
    c_in, c_out, c_shapes, c_sems, c_alias = _carry_args(carry, 11, 8)
    return pl.pallas_call(
        body, name=name, grid=(S // tm,),
        in_specs=[tile, tile, tile, _whole((8, D)), tile] + [_resident(a) for a in (win4, v2d, v1d, ws, bsb, wout)]
        + c_in,
        out_specs=[tile, _whole((8, D)), ANY, ANY] + [_whole(s) for s in out_shapes[4:]] + c_out,
        out_shape=[jax.ShapeDtypeStruct(s, F32) for s in out_shapes] + c_shapes,
        scratch_shapes=[pltpu.VMEM(win4.shape, F32), pltpu.VMEM(wout.shape, F32), pltpu.SemaphoreType.DMA] + c_sems,
        input_output_aliases=c_alias, compiler_params=_cp(("arbitrary",), has_side_effects=carry is not None),
    )(x, f, dout, vp, h, win4, v2d, v1d, ws, bsb, wout, *(carry.srcs if carry is not None else []))


def _ada_fwd(c_all, ada_w, ada_b_my, name):
    L, D, N = ada_w.shape
    tn = N // 3

    def body(c_ref, w_ref, b_ref, o_ref):
        cv = c_ref[...]
        o_ref[...] = _dot_hi(cv * _sig(cv), w_ref[...]) + b_ref[...]

    return pl.pallas_call(
        body, name=name, grid=(L, 3),
        in_specs=[pl.BlockSpec((8, D), lambda l, n: (0, 0)), pl.BlockSpec((None, D, tn), lambda l, n: (l, 0, n)),
                  pl.BlockSpec((None, 1, tn), lambda l, n: (l, 0, n))],
        out_specs=pl.BlockSpec((None, 8, tn), lambda l, n: (l, 0, n)),
        out_shape=jax.ShapeDtypeStruct((L, 8, N), F32), compiler_params=_cp(("arbitrary", "arbitrary")),
    )(c_all, ada_w, ada_b_my)


def _ada_bwd(c_all, dmod_my, name):
    L, _, N = dmod_my.shape
    D = c_all.shape[1]
    tn = N // 3

    def body(c_ref, d_ref, o_ref):
        cv = c_ref[...]
        o_ref[...] = lax.dot_general(cv * _sig(cv), d_ref[...], (((0,), (0,)), ((), ())),
                                     precision=lax.Precision.HIGHEST, preferred_element_type=F32)

    return pl.pallas_call(
        body, name=name, grid=(L, 3),
        in_specs=[pl.BlockSpec((8, D), lambda l, n: (0, 0)), pl.BlockSpec((None, 8, tn), lambda l, n: (l, 0, n))],
        out_specs=pl.BlockSpec((None, D, tn), lambda l, n: (l, 0, n)),
        out_shape=jax.ShapeDtypeStruct((L, D, N), F32), compiler_params=_cp(("arbitrary", "arbitrary")),
    )(c_all, dmod_my)


def _row_tile(rows, cols, budget=1 << 20):
    best = 8
    for rt in range(8, rows + 1, 8):
        if rows % rt == 0 and rt * cols * 4 <= budget:
            best = rt
    return best


def _my_chip():
    return 2 * lax.axis_index("x") + lax.axis_index("y")


def _cast_place(w, g, name):
    _, R, C = w.shape
    rt = _row_tile(R, C)

    def body(w_ref, o_ref):
        o_ref[...] = w_ref[...].astype(o_ref.dtype)

    return pl.pallas_call(
        body, name=name, grid=(R // rt,),
        in_specs=[pl.BlockSpec((None, rt, C), lambda r: (g, r, 0))],
        out_specs=pl.BlockSpec((None, None, rt, C), lambda r: (0, _my_chip(), r, 0)),
        out_shape=jax.ShapeDtypeStruct((1, N_SHARD, R, C), MXU_DT), compiler_params=_cp(("arbitrary",)),
    )(w)


def _add_half(gk, la, name, out_dtype=F32):
    Gk, _, R, C = gk.shape
    Rh = R // 2
    n = Gk * N_SHARD
    gv = gk.reshape(n, 2, Rh, C)
    lv = la.reshape(n, Rh, C)
    rt = _row_tile(Rh, C)

    def body(g_ref, l_ref, o_ref):
        o_ref[...] = (g_ref[...] + l_ref[...]).astype(o_ref.dtype)

    out = pl.pallas_call(
        body, name=name, grid=(n, Rh // rt),
        in_specs=[pl.BlockSpec((None, None, rt, C), lambda i, r: (i, lax.axis_index("c"), r, 0)),
                  pl.BlockSpec((None, rt, C), lambda i, r: (i, r, 0))],
        out_specs=pl.BlockSpec((None, rt, C), lambda i, r: (i, r, 0)),
        out_shape=jax.ShapeDtypeStruct((n, Rh, C), out_dtype), compiler_params=_cp(("arbitrary", "arbitrary")),
    )(gv, lv)
    return out.reshape(Gk, N_SHARD, Rh, C)


def _sum_chips(part, landed, joined, g, name):
    _, _, Rh, C = part.shape
    rt = _row_tile(Rh, C)
    nb = Rh // rt

    def body(p_ref, l_ref, j_ref, o_ref):
        up = lambda v: v.astype(F32)
        o_ref[...] = ((up(p_ref[...]) + up(l_ref[0])) + up(l_ref[1])) + up(l_ref[2])

    return pl.pallas_call(
        body, name=name, grid=(nb,),
        in_specs=[pl.BlockSpec((None, None, rt, C), lambda r: (0, _my_chip(), r, 0)),
                  pl.BlockSpec((None, 3, rt, C), lambda r: (0, 0, r, 0)), ANY],
        out_specs=pl.BlockSpec((None, rt, C), lambda r: (g, lax.axis_index("c") * nb + r, 0)),
        out_shape=jax.ShapeDtypeStruct(joined.shape, F32), input_output_aliases={2: 0},
        compiler_params=_cp(("arbitrary",)),
    )(part, landed, joined)


def _sum_slots(lb, name):
    Gk, n, Rh, C = lb.shape
    rt = _row_tile(Rh, C)

    def body(l_ref, o_ref):
        acc = l_ref[0]
        for s in range(1, n):
            acc = acc + l_ref[s]
        o_ref[...] = acc

    return pl.pallas_call(
        body, name=name, grid=(Gk, Rh // rt),
        in_specs=[pl.BlockSpec((None, n, rt, C), lambda g, r: (g, 0, r, 0))],
        out_specs=pl.BlockSpec((None, rt, C), lambda g, r: (g, r, 0)),
        out_shape=jax.ShapeDtypeStruct((Gk, Rh, C), F32), compiler_params=_cp(("arbitrary", "arbitrary")),
    )(lb)


def _adamw(w, g, m, v, name):
    rows, cols = w.shape
    rt = _row_tile(rows, cols) if rows % 8 == 0 else rows
    c1 = 1.0 - ADAM_B1 ** ADAM_STEP
    c2 = 1.0 - ADAM_B2 ** ADAM_STEP

    def body(w_ref, g_ref, m_ref, v_ref, d_ref, mo_ref, vo_ref):
        gv = g_ref[...]
        mn = ADAM_B1 * m_ref[...] + (1.0 - ADAM_B1) * gv
        vn = ADAM_B2 * v_ref[...] + (1.0 - ADAM_B2) * (gv * gv)
        d_ref[...] = -ADAM_LR * ((mn / c1) / (jnp.sqrt(vn / c2) + ADAM_EPS) + ADAM_WD * w_ref[...])
        mo_ref[...] = mn
        vo_ref[...] = vn

    spec = pl.BlockSpec((rt, cols), lambda r: (r, 0))
    sds = jax.ShapeDtypeStruct((rows, cols), F32)
    return pl.pallas_call(
        body, name=name, grid=(rows // rt,), in_specs=[spec] * 4, out_specs=[spec] * 3,
        out_shape=[sds] * 3, compiler_params=_cp(("arbitrary",)),
    )(w, g, m, v)


def _coords():
    return lax.axis_index("x"), lax.axis_index("y"), lax.axis_index("c")


def _all_gather8(blk, name):
    m_per, n = blk.shape

    def body(x_ref, out_ref, send_sems, recv_sems, local_sem):
        x, y, c = _coords()
        me, sibling = (x, y, c), (x, y, 1 - c)
        chips = [(1 - x, y), (x, 1 - y), (1 - x, 1 - y)]

        def rows(px, py, pc):
            return out_ref.at[pl.ds((4 * px + 2 * py + pc) * m_per, m_per), :]

        def copy(k, block, to, src=None):
            return pltpu.make_async_remote_copy(
                src_ref=rows(*block) if src is None else src, dst_ref=rows(*block),
                send_sem=send_sems.at[k], recv_sem=recv_sems.at[k], device_id=to, device_id_type=MESH)

        mine = pltpu.make_async_copy(x_ref, rows(*me), local_sem)
        mine.start()
        first = [copy(0, me, sibling, src=x_ref)]
        first += [copy(1 + j, me, (*chip, c), src=x_ref) for j, chip in enumerate(chips)]
        for cp in first:
            cp.start()
        passed = [copy(4 + j, (*chip, c), sibling) for j, chip in enumerate(chips)]
        for j, chip in enumerate(chips):
            copy(1 + j, (*chip, c), me).wait_recv()
            passed[j].start()
        copy(0, sibling, me).wait_recv()
        for j, chip in enumerate(chips):
            copy(4 + j, (*chip, 1 - c), me).wait_recv()
        for cp in first + passed:
            cp.wait_send()
        mine.wait()

    return pl.pallas_call(
        body, name=name, out_shape=jax.ShapeDtypeStruct((N_DEV * m_per, n), blk.dtype),
        in_specs=[pl.BlockSpec(memory_space=pltpu.VMEM)], out_specs=pl.BlockSpec(memory_space=pltpu.VMEM),
        scratch_shapes=[pltpu.SemaphoreType.DMA((7,)), pltpu.SemaphoreType.DMA((7,)), pltpu.SemaphoreType.DMA],
        compiler_params=_cp(),
    )(blk)


def _comm_call(name, inputs, out_shapes, plan, n_remote, aliases=None):
    n_in, n_out = len(inputs), len(out_shapes)

    def body(*refs):
        in_refs, out_refs = refs[:n_in], refs[n_in:n_in + n_out]
        send_sems, recv_sems = refs[n_in + n_out:]

        def remote(k, src, dst, to):
            return pltpu.make_async_remote_copy(src_ref=src, dst_ref=dst, send_sem=send_sems.at[k],
                                                recv_sem=recv_sems.at[k], device_id=to, device_id_type=MESH)

        plan(in_refs, out_refs, remote)

    return pl.pallas_call(
        body, name=name, out_shape=out_shapes, in_specs=[ANY] * n_in, out_specs=[ANY] * n_out,
        scratch_shapes=[pltpu.SemaphoreType.DMA((n_remote,)), pltpu.SemaphoreType.DMA((n_remote,))],
        input_output_aliases=aliases or {}, compiler_params=_cp(has_side_effects=True),
    )(*inputs)


def _gather_ici_carry(placed):
    K = len(placed)

    def plan(ins, outs, remote):
        x, y, c = _coords()
        s_me = 2 * x + y
        cps = []
        for j, (px, py) in enumerate([(1 - x, y), (x, 1 - y), (1 - x, 1 - y)]):
            for k in range(K):
                rh = placed[k].shape[2] // 2
                own = outs[k].at[:, s_me, pl.ds(c * rh, rh), :]
                cps.append(remote(j * K + k, own, own, (px, py, c)))
        return cps

    shapes = [jax.ShapeDtypeStruct(p.shape, p.dtype) for p in placed]
    return _Carry(placed, shapes, {k: k for k in range(K)}, 3 * K, plan)


def _forward_sibling(placed, name):
    K = len(placed)

    def plan(ins, outs, remote):
        x, y, c = _coords()
        cps = []
        for j, (px, py) in enumerate([(1 - x, y), (x, 1 - y), (1 - x, 1 - y)]):
            for k in range(K):
                rh = placed[k].shape[2] // 2
                landed = outs[k].at[:, 2 * px + py, pl.ds(c * rh, rh), :]
                cps.append(remote(j * K + k, landed, landed, (x, y, 1 - c)))
        for cp in cps:
            cp.start()
        for j, (px, py) in enumerate([(1 - x, y), (x, 1 - y), (1 - x, 1 - y)]):
            for k in range(K):
                rh = placed[k].shape[2] // 2
                other = outs[k].at[:, 2 * px + py, pl.ds((1 - c) * rh, rh), :]
                remote(j * K + k, other, other, (x, y, 1 - c)).wait_recv()
        for cp in cps:
            cp.wait_send()

    out_shapes = [jax.ShapeDtypeStruct(p.shape, p.dtype) for p in placed]
    return _comm_call(name, placed, out_shapes, plan, 3 * K, aliases={k: k for k in range(K)})


def _gather_weights(placed, name):
    K = len(placed)

    def plan(ins, outs, remote):
        x, y, c = _coords()
        s_me = 2 * x + y
        sibling = (x, y, 1 - c)
        chips = [(1 - x, y), (x, 1 - y), (1 - x, 1 - y)]
        half = lambda k, s, cc: outs[k].at[:, s, pl.ds(cc * (placed[k].shape[2] // 2), placed[k].shape[2] // 2), :]
        sent = []
        for j, (px, py) in enumerate(chips):
            for k in range(K):
                own = half(k, s_me, c)
                cp = remote(j * K + k, own, own, (px, py, c))
                cp.start()
                sent.append(cp)
        for j, (px, py) in enumerate(chips):
            s_from = 2 * px + py
            for k in range(K):
                landed = half(k, s_from, c)
                remote(j * K + k, landed, landed, (px, py, c)).wait_recv()
                cp = remote((3 + j) * K + k, landed, landed, sibling)
                cp.start()
                sent.append(cp)
        for j, (px, py) in enumerate(chips):
            s_from = 2 * px + py
            for k in range(K):
                other = half(k, s_from, 1 - c)
                remote((3 + j) * K + k, other, other, sibling).wait_recv()
        for cp in sent:
            cp.wait_send()

    out_shapes = [jax.ShapeDtypeStruct(p.shape, p.dtype) for p in placed]
    return _comm_call(name, placed, out_shapes, plan, 6 * K, aliases={k: k for k in range(K)})


def _send_other_half(grads, name):
    K = len(grads)

    def plan(ins, outs, remote):
        x, y, c = _coords()
        cps = []
        for k in range(K):
            rh = grads[k].shape[2] // 2
            cps.append(remote(k, ins[k].at[:, :, pl.ds((1 - c) * rh, rh), :], outs[k], (x, y, 1 - c)))
        for cp in cps:
            cp.start()
        for cp in cps:
            cp.wait()

    out_shapes = [jax.ShapeDtypeStruct(g.shape[:2] + (g.shape[2] // 2, g.shape[3]), g.dtype) for g in grads]
    return _comm_call(name, grads, out_shapes, plan, K)


def _send_half_carry(grads):
    K = len(grads)

    def plan(ins, outs, remote):
        x, y, c = _coords()
        cps = []
        for k in range(K):
            rh = grads[k].shape[2] // 2
            cps.append(remote(k, ins[k].at[:, :, pl.ds((1 - c) * rh, rh), :], outs[k], (x, y, 1 - c)))
        return cps

    out_shapes = [jax.ShapeDtypeStruct(g.shape[:2] + (g.shape[2] // 2, g.shape[3]), g.dtype) for g in grads]
    return _Carry(grads, out_shapes, {}, K, plan)


def _merge_carries(carries):
    if not carries:
        return None
    if len(carries) == 1:
        return carries[0]
    srcs, shapes, aliases = [], [], {}
    for cy in carries:
        aliases.update({len(srcs) + i: len(shapes) + o for i, o in cy.aliases.items()})
        srcs += cy.srcs
        shapes += cy.out_shapes

    def plan(ins, outs, remote):
        cps, i0, o0, k0 = [], 0, 0, 0
        for cy in carries:
            shifted = functools.partial(lambda k, src, dst, to, base: remote(base + k, src, dst, to), base=k0)
            cps += cy.plan(ins[i0:i0 + len(cy.srcs)], outs[o0:o0 + len(cy.out_shapes)], shifted)
            i0, o0, k0 = i0 + len(cy.srcs), o0 + len(cy.out_shapes), k0 + cy.n
        return cps

    return _Carry(srcs, shapes, aliases, sum(cy.n for cy in carries), plan)


def _scatter_carry(parts):
    K = len(parts)

    def plan(ins, outs, remote):
        x, y, c = _coords()
        chips = [(1 - x, y), (x, 1 - y), (1 - x, 1 - y)]
        return [remote(j * K + k, ins[k].at[:, 2 * px + py], outs[k].at[:, j], (px, py, c))
                for j, (px, py) in enumerate(chips) for k in range(K)]

    out_shapes = [jax.ShapeDtypeStruct((p.shape[0], 3) + p.shape[2:], p.dtype) for p in parts]
    return _Carry(parts, out_shapes, {}, 3 * K, plan)


def _scatter_chips(parts, name):
    carry = _scatter_carry(parts)

    def plan(ins, outs, remote):
        cps = carry.plan(ins, outs, remote)
        for cp in cps:
            cp.start()
        for cp in cps:
            cp.wait()

    return _comm_call(name, parts, carry.out_shapes, plan, carry.n)


def _join_halves(joined):
    K = len(joined)

    def plan(ins, outs, remote):
        x, y, c = _coords()
        cps = []
        for k in range(K):
            rh = joined[k].shape[1] // 2
            mine = outs[k].at[:, pl.ds(c * rh, rh), :]
            cps.append(remote(k, mine, mine, (x, y, 1 - c)))
        for cp in cps:
            cp.start()
        for k in range(K):
            rh = joined[k].shape[1] // 2
            other = outs[k].at[:, pl.ds((1 - c) * rh, rh), :]
            remote(k, other, other, (x, y, 1 - c)).wait_recv()
        for cp in cps:
            cp.wait_send()

    out_shapes = [jax.ShapeDtypeStruct(h.shape, h.dtype) for h in joined]
    return _comm_call("join_halves", joined, out_shapes, plan, K, aliases={k: k for k in range(K)})


def _pack(parts):
    flat = []
    for p in parts:
        v = p.reshape(-1).astype(F32)
        pad = (-v.shape[0]) % 1024
        flat.append(jnp.pad(v, (0, pad)) if pad else v)
    return jnp.concatenate(flat).reshape(-1, 128)


def _unpack(packed, shapes):
    flat = packed.reshape(-1)
    out, off = [], 0
    for shp in shapes:
        n = math.prod(shp)
        out.append(flat[off:off + n].reshape(shp))
        off += n + (-n) % 1024
    return out


def _shard_last(a, s, n):
    return lax.dynamic_slice_in_dim(a, s * n, n, axis=a.ndim - 1)


def _rows8(*vecs):
    n = vecs[0].shape[-1]
    rows = [v.reshape(1, n).astype(F32) for v in vecs]
    return jnp.concatenate(rows + [jnp.zeros((8 - len(rows), n), F32)], axis=0)


def kernel(x, c, ada_w, ada_b, norm_pre, norm_post, ffn_w13, ffn_w2, ab_w_in, a_conv_w, a_conv_b, a_gate_w, a_gate_b, a_lam, b_conv_w, b_conv_b, b_norm_g, b_norm_b, ab_w_out, c_w_in, c_b_in, c_norm_g, c_norm_b, c_w_s, c_b_s, c_w_out, loss_target, m_ada_w, m_ada_b, m_norm_pre, m_norm_post, m_ffn_w13, m_ffn_w2, m_ab_w_in, m_a_conv_w, m_a_conv_b, m_a_gate_w, m_a_gate_b, m_a_lam, m_b_conv_w, m_b_conv_b, m_b_norm_g, m_b_norm_b, m_ab_w_out, m_c_w_in, m_c_b_in, m_c_norm_g, m_c_norm_b, m_c_w_s, m_c_b_s, m_c_w_out, v_ada_w, v_ada_b, v_norm_pre, v_norm_post, v_ffn_w13, v_ffn_w2, v_ab_w_in, v_a_conv_w, v_a_conv_b, v_a_gate_w, v_a_gate_b, v_a_lam, v_b_conv_w, v_b_conv_b, v_b_norm_g, v_b_norm_b, v_ab_w_out, v_c_w_in, v_c_b_in, v_c_norm_g, v_c_norm_b, v_c_w_s, v_c_b_s, v_c_w_out):
    S, D = x.shape[1], x.shape[2]
    W = a_lam.shape[-1]
    Fh = ffn_w13.shape[-1]
    Fq = ffn_w2.shape[2]
    xi, yi, ci = _coords()
    shard = 2 * xi + yi
    me = 4 * xi + 2 * yi + ci
    x2, tgt = x[0], loss_target[0]

    sharded_small = [norm_pre, norm_post, a_conv_w, b_conv_w, c_b_in, c_norm_g, c_norm_b]
    gathered = _all_gather8(_pack([c] + sharded_small), "gather_small")
    blocks = gathered.reshape(N_DEV, -1, 128)
    per_dev = [_unpack(blocks[d], [c.shape] + [p.shape for p in sharded_small]) for d in range(0, N_DEV, 2)]
    c_all = jnp.concatenate([_unpack(blocks[d], [c.shape])[0] for d in range(N_DEV)], axis=0)
    npre, npost, acw, bcw, cbin, cng, cnb = [jnp.concatenate([per_dev[s][1 + i] for s in range(N_SHARD)], axis=-1)
                                             for i in range(len(sharded_small))]

    ada_b_my = _shard_last(ada_b, shard, ada_w.shape[-1])[:, None, :]
    modp = _ada_fwd(c_all, ada_w, ada_b_my, "ada_fwd")
    modg = _all_gather8(modp.reshape(16, -1), "gather_mod").reshape(N_DEV, 2, 8, -1)
    mod_me = lax.dynamic_index_in_dim(modg[0::2], me, axis=2, keepdims=False)
    mod = jnp.transpose(mod_me, (1, 0, 2)).reshape(2, 3, 3, D)

    w13s, w2s = ffn_w13.reshape(4, D, Fh), ffn_w2.reshape(4, Fq, D)
    placed13 = [_cast_place(w13s, g, f"cast_w13_{g}") for g in range(4)]
    placed2 = [_cast_place(w2s, g, f"cast_w2_{g}") for g in range(4)]
    placed_mix = [_cast_place(w, 0, f"cast_mix{k}") for k, w in enumerate((ab_w_in, ab_w_out, c_w_in, c_w_out))]
    w13_first, w2_first = _gather_weights([placed13[0], placed2[0]], "gather_first")
    gather_carry = _gather_ici_carry(placed13[1:] + placed2[1:] + placed_mix)

    eye = jnp.eye(8, dtype=F32)
    dh_a = W // 8
    blockdiag = lambda w: jnp.einsum("hde,hg->hdge", w, eye).reshape(W, W)
    gw = a_gate_w[0]
    wg = jnp.concatenate([blockdiag(gw[:, :, :dh_a]), blockdiag(gw[:, :, dh_a:])], axis=1).astype(MXU_DT)
    bgv = jnp.concatenate([a_gate_b[0][:, :dh_a].reshape(-1), a_gate_b[0][:, dh_a:].reshape(-1)])
    bg = _rows8(bgv)
    cwa = jnp.concatenate([acw[0], jnp.zeros((4, W), F32)], axis=0)
    cwb = jnp.concatenate([bcw[0], jnp.zeros((1, W), F32)], axis=0)
    v512 = _rows8(a_conv_b[0], a_lam[0], b_conv_b[0], b_norm_g[0], b_norm_b[0])
    dg_b = W // 8
    gid = jnp.arange(W) // dg_b
    member = (gid[:, None] == jnp.arange(128)[None, :]).astype(F32)
    pavg = jnp.stack([member / dg_b, member]).astype(MXU_DT)
    v2d = _rows8(cbin[0])
    v1d = _rows8(cng[0], cnb[0])
    tril = jnp.tril(jnp.ones((CHUNK, CHUNK), dtype=bool))
    ws = jnp.where(tril, c_w_s[0], 0.0).astype(MXU_DT)
    bsb = jnp.repeat(jnp.transpose(c_b_s[0]), D // N_HEAD, axis=1)

    res_ws = (0.5, 1.0, 0.5)
    vps, xs, saved = [], [], []
    xc = x2
    w13g, w2v = [w13_first], [w2_first.reshape(1, 2, Fh, D)]
    for l in range(2):
        for j in range(3):
            k = 3 * l + j
            vp = _rows8(npre[l, j], mod[l, j, 0], mod[l, j, 1], mod[l, j, 2], npost[l, j])
            vps.append(vp)
            xs.append(xc)
            gi = 2 * l + j // 2
            if k == 0:
                xc, *keep = _ffn_fwd(xc, vp, w13g[0], w2v[0], res_ws[j], "ffn_fwd0", carry=gather_carry)
                keep, landed = keep[:4], keep[4:]
                full = _forward_sibling(landed, "forward_sibling")
                w13g += list(full[0:3])
                w2v += [w.reshape(1, 2, Fh, D) for w in full[3:6]]
                abin_g, about_g, cin_g, cout_g = full[6:]
                ab_ops = (abin_g[0], cwa, wg, bg, cwb, v512, pavg, about_g.reshape(D, D))
                c_ops = (cin_g[0], v2d, v1d, ws, bsb, cout_g.reshape(D, D))
            elif j != 1:
                xc, *keep = _ffn_fwd(xc, vp, w13g[gi], w2v[gi], res_ws[j], f"ffn_fwd{k}")
            elif l == 0:
                xc, *keep = _mix_ab_fwd(xc, vp, *ab_ops, res_ws[j], "mix_ab_fwd")
            else:
                xc, *keep = _mix_c_fwd(xc, vp, *c_ops, res_ws[j], "mix_c_fwd")
            saved.append(keep)

    dout, lrow = _loss_head(xc, tgt, "loss_head")

    joined = {"w13": lax.empty((4, D, Fh), F32), "w2": lax.empty((4, Fq, D), F32), "abin": lax.empty((1, D, 4 * W // 4), F32),
              "about": lax.empty((1, D // 4, D), F32), "cin": lax.empty((1, D, 2 * D // 4), F32),
              "cout": lax.empty((1, D // 4, D), F32)}

    stage = {"send": None, "scatter": None}

    def add_halves(group, got, dtype=F32):
        grads, keys, g, tag = group
        parts = [_add_half(gr, la, f"add_half_{tag}{i}", dtype) for i, (gr, la) in enumerate(zip(grads, got))]
        return parts, keys, g

    def sum_landed(group, landed_slots):
        parts, keys, g = group
        for part, key, slots in zip(parts, keys, landed_slots):
            joined[key] = _sum_chips(part, slots, joined[key], g, f"sum_chips_{key}{g}")

    def ride(call, n_own, carries_ok=True):
        if not carries_ok:
            return call(None)[:n_own]
        scat, send = stage["scatter"], stage["send"]
        carries = ([_scatter_carry(scat[0])] if scat is not None else []) + (
            [_send_half_carry(send[0])] if send is not None else [])
        res = call(_merge_carries(carries))
        own, extra = res[:n_own], list(res[n_own:])
        stage["scatter"] = stage["send"] = None
        if scat is not None:
            sum_landed(scat, extra[:len(scat[0])])
            extra = extra[len(scat[0]):]
        if send is not None:
            stage["scatter"] = add_halves(send, extra)
        return own

    d_npre = [[None] * 3 for _ in range(2)]
    d_npost = [[None] * 3 for _ in range(2)]
    d_mod = [[None] * 3 for _ in range(2)]
    for l in (1, 0):
        for j in (2, 1, 0):
            k = 3 * l + j
            if j != 1:
                f, h, gpre, upre = saved[k]
                gi = 2 * l + j // 2
                dw13 = lax.empty((1, 4, D, Fh), F32)
                dw2v = lax.empty((1, 2, Fh, D), F32)
                df, dh0, s_a, dw13, dw2v = ride(lambda cy: _ffn_bwd_half(
                    0, xs[k], f, dout, vps[k], h, gpre, upre, w13g[gi], w2v[gi], dw13, dw2v, res_ws[j],
                    f"ffn_bwd{k}a", carry=cy), 5)
                dout, s_b, dw13, dw2v = ride(lambda cy: _ffn_bwd_half(
                    1, xs[k], f, dout, vps[k], h, gpre, upre, w13g[gi], w2v[gi], dw13, dw2v, res_ws[j],
                    f"ffn_bwd{k}b", df=df, dh0=dh0, carry=cy), 4, carries_ok=k > 0)
                sums = s_a + s_b
                made = ([dw13, dw2v.reshape(1, 4, Fq, D)], ["w13", "w2"], gi, f"ffn{gi}")
            elif l == 0:
                dout, sums, d_abin, d_about, d_wg, d_cwa, d_cwb, d_bg, d_v512 = ride(lambda cy: _mix_ab_bwd(
                    xs[k], saved[k][0], dout, vps[k], *saved[k][1:], *ab_ops, res_ws[j], "mix_ab_bwd", carry=cy), 9)
                made = ([d_abin[None], d_about.reshape(1, 4, D // 4, D)], ["abin", "about"], 0, "ab")
            else:
                dout, sums, d_cin, d_cout, d_ws, d_bsb, d_v2, d_v1 = ride(lambda cy: _mix_c_bwd(
                    xs[k], saved[k][0], dout, vps[k], saved[k][1], *c_ops, res_ws[j], "mix_c_bwd", carry=cy), 8)
                made = ([d_cin[None], d_cout.reshape(1, 4, D // 4, D)], ["cin", "cout"], 0, "c")
            stage["send"] = made
            d_npre[l][j], d_npost[l][j] = sums[4], sums[1]
            d_mod[l][j] = jnp.stack([sums[2], sums[3], sums[0]])
    last = add_halves(stage["send"], _send_other_half(stage["send"][0], "send_half_last"), MXU_DT)
    left = ([stage["scatter"]] if stage["scatter"] is not None else []) + [last]
    slots = _scatter_chips([part for group in left for part in group[0]], "scatter_last")
    for group in left:
        sum_landed(group, slots[:len(group[0])])
        slots = slots[len(group[0]):]
    grad_x = dout[None]

    dmod = jnp.stack([jnp.stack(d_mod[l]) for l in range(2)]).reshape(2, 9 * D)
    d_gate_w = jnp.concatenate([jnp.einsum("hdhe->hde", d_wg[:, :W].reshape(8, dh_a, 8, dh_a)),
                                jnp.einsum("hdhe->hde", d_wg[:, W:].reshape(8, dh_a, 8, dh_a))], axis=-1)
    d_gate_b = jnp.concatenate([d_bg[0, :W].reshape(8, dh_a), d_bg[0, W:].reshape(8, dh_a)], axis=-1)
    small_grads = [
        dmod, jnp.stack([jnp.stack(r) for r in d_npre]), jnp.stack([jnp.stack(r) for r in d_npost]),
        d_cwa[:4][None], d_v512[0][None], d_gate_w[None], d_gate_b[None], d_v512[1][None], d_cwb[:31][None],
        d_v512[2][None], d_v512[3][None], d_v512[4][None], d_v2[0][None], d_v1[0][None], d_v1[1][None],
        jnp.where(tril, d_ws, 0.0)[None], jnp.transpose(d_bsb.reshape(CHUNK, N_HEAD, D // N_HEAD).sum(-1))[None]]
    small_w = [ada_b, norm_pre, norm_post, a_conv_w, a_conv_b, a_gate_w, a_gate_b, a_lam, b_conv_w, b_conv_b,
               b_norm_g, b_norm_b, c_b_in, c_norm_g, c_norm_b, c_w_s, c_b_s]
    small_m = [m_ada_b, m_norm_pre, m_norm_post, m_a_conv_w, m_a_conv_b, m_a_gate_w, m_a_gate_b, m_a_lam, m_b_conv_w,
               m_b_conv_b, m_b_norm_g, m_b_norm_b, m_c_b_in, m_c_norm_g, m_c_norm_b, m_c_w_s, m_c_b_s]
    small_v = [v_ada_b, v_norm_pre, v_norm_post, v_a_conv_w, v_a_conv_b, v_a_gate_w, v_a_gate_b, v_a_lam, v_b_conv_w,
               v_b_conv_b, v_b_norm_g, v_b_norm_b, v_c_b_in, v_c_norm_g, v_c_norm_b, v_c_w_s, v_c_b_s]
    full_shapes = [g.shape for g in small_grads]
    loss_part = jnp.sum(lrow[0]).reshape(1, 1)
    sg_all = _all_gather8(_pack(small_grads + [loss_part]), "gather_small_grads").reshape(N_DEV, -1, 128)
    sg_sum = _sum_slots(sg_all[None], "sum_small_grads")[0]
    *g_full, loss_sum = _unpack(sg_sum, full_shapes + [(1, 1)])
    loss = loss_sum[0, 0]
    g_small = [g if g.shape == w.shape else _shard_last(g, shard, w.shape[-1]) for g, w in zip(g_full, small_w)]
    small_shapes = [w.shape for w in small_w]
    d_s, m_s, v_s = _adamw(_pack(small_w), _pack(g_small), _pack(small_m), _pack(small_v), "adamw_small")
    delta_small, newm_small, newv_small = (_unpack(a, small_shapes) for a in (d_s, m_s, v_s))

    dmod_all = jnp.stack([_unpack(sg_all[d], full_shapes[:1])[0] for d in range(N_DEV)], axis=1)
    n_ada = ada_w.shape[-1]
    g_ada_w = _ada_bwd(c_all, _shard_last(dmod_all, shard, n_ada), "ada_bwd")

    g_big = _join_halves([joined[key] for key in ("w13", "w2", "abin", "about", "cin", "cout")])

    big_w = [ffn_w13, ffn_w2, ab_w_in, ab_w_out, c_w_in, c_w_out, ada_w]
    big_m = [m_ffn_w13, m_ffn_w2, m_ab_w_in, m_ab_w_out, m_c_w_in, m_c_w_out, m_ada_w]
    big_v = [v_ffn_w13, v_ffn_w2, v_ab_w_in, v_ab_w_out, v_c_w_in, v_c_w_out, v_ada_w]
    big_g = [g.reshape(w.shape) for g, w in zip(list(g_big) + [g_ada_w], big_w)]
    big_out = []
    for k, (w, g, m, v) in enumerate(zip(big_w, big_g, big_m, big_v)):
        two_d = lambda a: a.reshape(-1, a.shape[-1])
        res = _adamw(two_d(w), two_d(g), two_d(m), two_d(v), f"adamw_big{k}")
        big_out.append([r.reshape(w.shape) for r in res])

    names = ["ada_w", "ada_b", "norm_pre", "norm_post", "ffn_w13", "ffn_w2", "ab_w_in", "a_conv_w", "a_conv_b",
             "a_gate_w", "a_gate_b", "a_lam", "b_conv_w", "b_conv_b", "b_norm_g", "b_norm_b", "ab_w_out", "c_w_in",
             "c_b_in", "c_norm_g", "c_norm_b", "c_w_s", "c_b_s", "c_w_out"]
    big_names = ["ffn_w13", "ffn_w2", "ab_w_in", "ab_w_out", "c_w_in", "c_w_out", "ada_w"]
    small_names = ["ada_b", "norm_pre", "norm_post", "a_conv_w", "a_conv_b", "a_gate_w", "a_gate_b", "a_lam",
                   "b_conv_w", "b_conv_b", "b_norm_g", "b_norm_b", "c_b_in", "c_norm_g", "c_norm_b", "c_w_s", "c_b_s"]
    table = {}
    for k, n in enumerate(big_names):
        table[n] = (big_g[k], *big_out[k])
    for k, n in enumerate(small_names):
        table[n] = (g_small[k], delta_small[k], newm_small[k], newv_small[k])
    outs = [loss, grad_x]
    for field in range(4):
        outs += [table[n][field] for n in names]
    return tuple(outs)
```

```python
import functools
import math

import jax
import jax.numpy as jnp
from jax import lax
from jax.experimental import pallas as pl
from jax.experimental.pallas import tpu as pltpu

F32 = jnp.float32
MXU_DT = jnp.bfloat16
EPS = 1e-6
LRU_C = 8.0
N_SHARD = 4
N_DEV = 8
CHUNK = 128
N_HEAD = 8
ADAM_LR, ADAM_B1, ADAM_B2, ADAM_EPS, ADAM_WD, ADAM_STEP = 0.001, 0.9, 0.999, 1e-08, 0.01, 10
GELU_K0 = math.sqrt(2.0 / math.pi)
GELU_K1 = 0.044715
VMEM_LIMIT = 58 * 1024 * 1024
MESH = pl.DeviceIdType.MESH
ANY = pl.BlockSpec(memory_space=pl.ANY)


def _cp(sem=None, **kw):
    if sem is not None:
        kw["dimension_semantics"] = sem
    return pltpu.CompilerParams(vmem_limit_bytes=VMEM_LIMIT, **kw)


def _resident(a):
    return pl.BlockSpec(a.shape, lambda *_: (0,) * a.ndim, pipeline_mode=pl.Buffered(1))


def _whole(shape):
    return pl.BlockSpec(shape, lambda *_: (0,) * len(shape))


def _dot(a, b):
    return jnp.dot(a.astype(MXU_DT), b.astype(MXU_DT), preferred_element_type=F32)


def _dot_nt(a, b):
    return lax.dot_general(a.astype(MXU_DT), b.astype(MXU_DT), (((1,), (1,)), ((), ())), preferred_element_type=F32)


def _dot_tn(a, b):
    return lax.dot_general(a.astype(MXU_DT), b.astype(MXU_DT), (((0,), (0,)), ((), ())), preferred_element_type=F32)


def _dot_hi(a, b):
    return jnp.dot(a, b, precision=lax.Precision.HIGHEST, preferred_element_type=F32)


def _sig(x):
    return 1.0 / (1.0 + jnp.exp(-x))


def _logsig(x):
    return jnp.minimum(x, 0.0) - jnp.log(1.0 + jnp.exp(-jnp.abs(x)))


def _gelu(x):
    x2 = x * x
    t = jnp.tanh(GELU_K0 * (x + GELU_K1 * x * x2))
    val = 0.5 * x * (1.0 + t)
    der = 0.5 * (1.0 + t) + 0.5 * x * (1.0 - t * t) * (GELU_K0 * (1.0 + 3.0 * GELU_K1 * x2))
    return val, der


def _neg_expm1(x):
    small = -(x * (1.0 + x * (0.5 + x * (1.0 / 6.0 + x * (1.0 / 24.0)))))
    return jnp.where(x > -0.05, small, 1.0 - jnp.exp(x))


def _colsum(v):
    return jnp.sum(v, axis=0, keepdims=True)


def _rowmean(v):
    return jnp.mean(v, axis=-1, keepdims=True)


def _copy_out(pairs, sem):
    for src, dst in pairs:
        cp = pltpu.make_async_copy(src, dst, sem)
        cp.start()
        cp.wait()


class _Carry:
    def __init__(self, srcs, out_shapes, aliases, n, plan):
        self.srcs, self.out_shapes, self.aliases, self.n, self.plan = list(srcs), list(out_shapes), aliases, n, plan


def _carry_args(carry, n_in, n_out):
    if carry is None:
        return [], [], [], [], {}
    sems = [pltpu.SemaphoreType.DMA((carry.n,)), pltpu.SemaphoreType.DMA((carry.n,))]
    aliases = {n_in + i: n_out + o for i, o in carry.aliases.items()}
    return [ANY] * len(carry.srcs), [ANY] * len(carry.out_shapes), carry.out_shapes, sems, aliases


def _split_refs(refs, n_in, n_out, n_scratch, carry):
    nci, nco = (len(carry.srcs), len(carry.out_shapes)) if carry is not None else (0, 0)
    cuts = [n_in, nci, n_out, nco, n_scratch]
    parts, i = [], 0
    for n in cuts:
        parts.append(refs[i:i + n])
        i += n
    ins, cins, outs, couts, scr = parts
    return ins, outs, scr, (cins, couts, refs[i:])


def _carry_copies(carry, carry_refs):
    if carry is None:
        return []
    cins, couts, (send_sems, recv_sems) = carry_refs

    def remote(k, src, dst, to):
        return pltpu.make_async_remote_copy(src_ref=src, dst_ref=dst, send_sem=send_sems.at[k],
                                            recv_sem=recv_sems.at[k], device_id=to, device_id_type=MESH)

    return carry.plan(cins, couts, remote)


def _carry_run(carry, carry_refs, first, last):
    if carry is None:
        return

    @pl.when(first)
    def _():
        for cp in _carry_copies(carry, carry_refs):
            cp.start()

    @pl.when(last)
    def _():
        for cp in _carry_copies(carry, carry_refs):
            cp.wait()


def _shift_down(v, k):
    return v if k == 0 else pltpu.roll(v, k, 0)


def _shift_up(v, k):
    return v if k == 0 else pltpu.roll(v, v.shape[0] - k, 0)


def _shell_pre(xv, vp_ref):
    r = lax.rsqrt(_rowmean(xv * xv) + EPS)
    return xv * r * (vp_ref[0:1, :] * (1.0 + vp_ref[2:3, :])) + vp_ref[1:2, :]


def _shell_post(xv, fv, vp_ref, res_w):
    r = lax.rsqrt(_rowmean(fv * fv) + EPS)
    return xv + fv * r * (res_w * (1.0 + vp_ref[3:4, :]) * vp_ref[4:5, :])


def _shell_post_bwd(fv, dov, vp_ref, res_w, s_ref):
    r = lax.rsqrt(_rowmean(fv * fv) + EPS)
    fn = fv * r
    pg = vp_ref[4:5, :]
    dy = (res_w * (1.0 + vp_ref[3:4, :])) * dov
    if s_ref is not None:
        s_ref[0:1, :] += _colsum(res_w * fn * pg * dov)
        s_ref[1:2, :] += _colsum(fn * dy)
    q = dy * pg
    return r * (q - fn * _rowmean(fn * q))


def _shell_pre_bwd(xv, dh, dov, vp_ref, s_ref):
    r = lax.rsqrt(_rowmean(xv * xv) + EPS)
    xn = xv * r
    pg = vp_ref[0:1, :]
    sc1 = 1.0 + vp_ref[2:3, :]
    s_ref[2:3, :] += _colsum(dh)
    s_ref[3:4, :] += _colsum(xn * pg * dh)
    s_ref[4:5, :] += _colsum(xn * dh * sc1)
    q = dh * (sc1 * pg)
    return dov + r * (q - xn * _rowmean(xn * q))


MXU_COLS = 256


def _col_chunks(n, width=2 * MXU_COLS):
    return [(c0, min(c0 + width, n)) for c0 in range(0, n, width)]


def _ffn_fwd(x, vp, w13g, w2v, res_w, name, carry=None, tgt=None, tm=512):
    S, D = x.shape
    Fh = w13g.shape[-1]
    T = S // tm
    head = tgt is not None
    n_in, n_out = (5, 6) if head else (4, 5)

    def body(*refs):
        ins, outs, _, carry_refs = _split_refs(refs, n_in, n_out, 0, carry)
        x_ref, vp_ref, w13_ref, w2_ref = ins[:4]
        xo_ref, f_ref, h_ref, g_ref, u_ref = outs[:5]
        _carry_run(carry, carry_refs, pl.program_id(0) == 0, pl.program_id(0) == T - 1)
        xv = x_ref[...]
        hb = _shell_pre(xv, vp_ref).astype(MXU_DT)
        h_ref[...] = hb
        acc = None
        for j in range(2):
            gg = _dot(hb, w13_ref[j])
            uu = _dot(hb, w13_ref[2 + j])
            g_ref[:, j * Fh:(j + 1) * Fh] = gg.astype(g_ref.dtype)
            u_ref[:, j * Fh:(j + 1) * Fh] = uu.astype(u_ref.dtype)
            part = _dot(gg * _sig(gg) * uu, w2_ref[j])
            acc = part if acc is None else acc + part
        f_ref[...] = acc
        xo = _shell_post(xv, acc, vp_ref, res_w)
        if head:
            l_ref = outs[5]

            @pl.when(pl.program_id(0) == 0)
            def _():
                l_ref[...] = jnp.zeros_like(l_ref)

            e = xo - ins[4][...]
            xo_ref[...] = e * (1.0 / D)
            l_ref[0:1, :] += _colsum(e * e) * (0.5 / D)
        else:
            xo_ref[...] = xo

    tile = lambda w: pl.BlockSpec((tm, w), lambda t: (t, 0))
    sd = jax.ShapeDtypeStruct
    c_in, c_out, c_shapes, c_sems, c_alias = _carry_args(carry, n_in, n_out)
    return pl.pallas_call(
        body, name=name, grid=(T,),
        in_specs=[tile(D), _whole((8, D)),
                  pl.BlockSpec((None, 4, D, Fh), lambda t: (0, 0, 0, 0), pipeline_mode=pl.Buffered(1)),
                  pl.BlockSpec((None, 2, Fh, D), lambda t: (0, 0, 0, 0), pipeline_mode=pl.Buffered(1))]
        + ([tile(D)] if head else []) + c_in,
        out_specs=[tile(D), tile(D), tile(D), tile(2 * Fh), tile(2 * Fh)] + ([_whole((8, D))] if head else []) + c_out,
        out_shape=[sd((S, D), F32), sd((S, D), F32), sd((S, D), MXU_DT), sd((S, 2 * Fh), MXU_DT),
                   sd((S, 2 * Fh), MXU_DT)] + ([sd((8, D), F32)] if head else []) + c_shapes,
        scratch_shapes=c_sems, input_output_aliases=c_alias,
        compiler_params=_cp(("arbitrary",), has_side_effects=carry is not None),
    )(x, vp, w13g, w2v, *([tgt] if head else []), *(carry.srcs if carry is not None else []))


def _ffn_bwd_half(j, x, f, dout, vp, h, gpre, upre, w13g, w2v, dw13, dw2v, res_w, name, df=None, dh0=None, carry=None,
                  tm=256):
    S, D = h.shape
    Fh = w13g.shape[-1]
    T = S // tm
    first = j == 0
    n_in, n_out = (11, 5) if first else (13, 4)

    def body(*refs):
        ins, outs, (a1, a3, a2, sem), carry_refs = _split_refs(refs, n_in, n_out, 4, carry)
        if first:
            f_ref, do_ref, vp_ref, h_ref, g_ref, u_ref, w1_ref, w3_ref, w2_ref, _, _ = ins
            df_ref, dh_ref, s_ref, dw13_ref, dw2_ref = outs
        else:
            x_ref, do_ref, vp_ref, h_ref, g_ref, u_ref, w1_ref, w3_ref, w2_ref, dfi_ref, dh0_ref, _, _ = ins
            dx_ref, s_ref, dw13_ref, dw2_ref = outs
        t = pl.program_id(0)
        _carry_run(carry, carry_refs, t == 0, t == T - 1)

        @pl.when(t == 0)
        def _():
            for ref in (a1, a3, a2, s_ref):
                ref[...] = jnp.zeros_like(ref)

        hv = h_ref[...]
        if first:
            dfv = _shell_post_bwd(f_ref[...], do_ref[...], vp_ref, res_w, s_ref).astype(MXU_DT)
            df_ref[...] = dfv
        else:
            dfv = dfi_ref[...]
        dh = None
        for c0, c1 in _col_chunks(Fh):
            gg = g_ref[:, c0:c1].astype(F32)
            uu = u_ref[:, c0:c1].astype(F32)
            sg = _sig(gg)
            si = gg * sg
            da = _dot_nt(dfv, w2_ref[c0:c1, :])
            a2[c0:c1, :] += _dot_tn(si * uu, dfv)
            dg = da * uu * (sg * (1.0 + gg * (1.0 - sg)))
            du = da * si
            a1[:, c0:c1] += _dot_tn(hv, dg)
            a3[:, c0:c1] += _dot_tn(hv, du)
            part = _dot_nt(dg, w1_ref[:, c0:c1]) + _dot_nt(du, w3_ref[:, c0:c1])
            dh = part if dh is None else dh + part
        if first:
            dh_ref[...] = dh
        else:
            dx_ref[...] = _shell_pre_bwd(x_ref[...], dh0_ref[...] + dh, do_ref[...], vp_ref, s_ref)

        @pl.when(t == T - 1)
        def _():
            _copy_out(((a1, dw13_ref.at[0, j]), (a3, dw13_ref.at[0, 2 + j]), (a2, dw2_ref.at[0, j])), sem)

    tile = lambda w: pl.BlockSpec((tm, w), lambda t: (t, 0))
    half = pl.BlockSpec((tm, Fh), lambda t: (t, j))
    weights = [pl.BlockSpec((None, None, D, Fh), lambda t: (0, j, 0, 0), pipeline_mode=pl.Buffered(1)),
               pl.BlockSpec((None, None, D, Fh), lambda t: (0, 2 + j, 0, 0), pipeline_mode=pl.Buffered(1)),
               pl.BlockSpec((None, None, Fh, D), lambda t: (0, j, 0, 0), pipeline_mode=pl.Buffered(1))]
    sd = jax.ShapeDtypeStruct
    grads = [sd(dw13.shape, F32), sd(dw2v.shape, F32)]
    scratch = [pltpu.VMEM((D, Fh), F32), pltpu.VMEM((D, Fh), F32), pltpu.VMEM((Fh, D), F32), pltpu.SemaphoreType.DMA]
    c_in, c_out, c_shapes, c_sems, c_alias = _carry_args(carry, n_in, n_out)
    params = _cp(("arbitrary",), has_side_effects=carry is not None)
    extra = carry.srcs if carry is not None else []
    if first:
        return pl.pallas_call(
            body, name=name, grid=(T,),
            in_specs=[tile(D), tile(D), _whole((8, D)), tile(D), half, half] + weights + [ANY, ANY] + c_in,
            out_specs=[tile(D), tile(D), _whole((8, D)), ANY, ANY] + c_out,
            out_shape=[sd((S, D), MXU_DT), sd((S, D), F32), sd((8, D), F32)] + grads + c_shapes,
            scratch_shapes=scratch + c_sems, input_output_aliases={9: 3, 10: 4, **c_alias}, compiler_params=params,
        )(f, dout, vp, h, gpre, upre, w13g, w13g, w2v, dw13, dw2v, *extra)
    return pl.pallas_call(
        body, name=name, grid=(T,),
        in_specs=[tile(D), tile(D), _whole((8, D)), tile(D), half, half] + weights + [tile(D), tile(D), ANY, ANY] + c_in,
        out_specs=[tile(D), _whole((8, D)), ANY, ANY] + c_out,
        out_shape=[sd((S, D), F32), sd((8, D), F32)] + grads + c_shapes,
        scratch_shapes=scratch + c_sems, input_output_aliases={11: 2, 12: 3, **c_alias}, compiler_params=params,
    )(x, dout, vp, h, gpre, upre, w13g, w13g, w2v, df, dh0, dw13, dw2v, *extra)


def _scan_fwd(a, u, rows):
    n = a.shape[0]
    d = 1
    while d < n:
        m = rows >= d
        u = u + a * jnp.where(m, _shift_down(u, d), 0.0)
        a = a * jnp.where(m, _shift_down(a, d), 1.0)
        d *= 2
    return a, u


def _scan_bwd(a, u, rows):
    n = a.shape[0]
    d = 1
    while d < n:
        m = rows < n - d
        u = u + a * jnp.where(m, _shift_up(u, d), 0.0)
        a = a * jnp.where(m, _shift_up(a, d), 1.0)
        d *= 2
    return a, u


def _causal_conv(ext, w_ref, K, halo, tm):
    acc = None
    for k in range(K):
        term = w_ref[k:k + 1, :] * _shift_down(ext, K - 1 - k)[halo:, :]
        acc = term if acc is None else acc + term
    return acc


def _anticausal_conv(ext, w_ref, K, tm):
    acc = None
    for k in range(K):
        term = w_ref[k:k + 1, :] * _shift_up(ext, K - 1 - k)[:tm, :]
        acc = term if acc is None else acc + term
    return acc


def _dot_split(a, b):
    hi = a.astype(MXU_DT)
    lo = (a - hi.astype(F32)).astype(MXU_DT)
    return jnp.dot(hi, b, preferred_element_type=F32) + jnp.dot(lo, b, preferred_element_type=F32)


def _group_mean(v, p_ref):
    return _dot_nt_exact(_dot_split(v, p_ref[0]), p_ref[1])


def _dot_nt_exact(a, bt):
    hi = a.astype(MXU_DT)
    lo = (a - hi.astype(F32)).astype(MXU_DT)
    dims = (((1,), (1,)), ((), ()))
    return (lax.dot_general(hi, bt, dims, preferred_element_type=F32)
            + lax.dot_general(lo, bt, dims, preferred_element_type=F32))


def _group_norm(vc, p_ref, g, b):
    mu = _group_mean(vc, p_ref)
    dv = vc - mu
    rstd = lax.rsqrt(_group_mean(dv * dv, p_ref) + EPS)
    vhat = dv * rstd
    return vhat, rstd, vhat * g + b


def _lru_gates(axc, wg_ref, bg_ref, lam, W):
    gp = _dot(axc, wg_ref[...]) + bg_ref[0:1, :]
    r = _sig(gp[:, :W])
    i = _sig(gp[:, W:])
    ls = _logsig(lam)
    L = (LRU_C * ls) * r
    a = jnp.exp(L)
    mult = jnp.sqrt(_neg_expm1(2.0 * L))
    return r, i, ls, a, mult


def _mix_ab_fwd(x, vp, win4, cwa, wg, bg, cwb, v512, pavg, wout, res_w, name, tm=256):
    S, D = x.shape
    W = win4.shape[-1]
    KA, KB, HA, HB = 4, 31, 8, 32

    def body(x_ref, vp_ref, win_ref, cwa_ref, wg_ref, bg_ref, cwb_ref, v_ref, p_ref, wo_ref,
             xo_ref, f_ref, h_ref, sav_ref, ahalo, bhalo, carry):
        @pl.when(pl.program_id(0) == 0)
        def _():
            ahalo[...] = jnp.zeros_like(ahalo)
            bhalo[...] = jnp.zeros_like(bhalo)
            carry[...] = jnp.zeros_like(carry)

        xv = x_ref[...]
        hv = _shell_pre(xv, vp_ref).astype(MXU_DT)
        h_ref[...] = hv
        a_gate = _dot(hv, win_ref[0])
        axp = _dot(hv, win_ref[1])
        b_val = _dot(hv, win_ref[2])
        b_gate = _dot(hv, win_ref[3])
        rows = lax.broadcasted_iota(jnp.int32, (tm, W), 0)
        axc = _causal_conv(jnp.concatenate([ahalo[...], axp], axis=0), cwa_ref, KA, HA, tm) + v_ref[0:1, :]
        ahalo[...] = axp[tm - HA:, :]
        r, i, ls, a, mult = _lru_gates(axc, wg_ref, bg_ref, v_ref[1:2, :], W)
        acum, hloc = _scan_fwd(a, mult * i * axc, rows)
        hs = hloc + acum * carry[7:8, :]
        carry[...] = hs[tm - 8:, :]
        ya = hs * _gelu(a_gate)[0]
        bv = b_val * _sig(b_gate)
        vc = _causal_conv(jnp.concatenate([bhalo[...], bv], axis=0), cwb_ref, KB, HB, tm) + v_ref[2:3, :]
        bhalo[...] = bv[tm - HB:, :]
        _, _, vn = _group_norm(vc, p_ref, v_ref[3:4, :], v_ref[4:5, :])
        yb = vn * _sig(vn)
        fv = _dot(ya, wo_ref[0:W, :]) + _dot(yb, wo_ref[W:, :])
        f_ref[...] = fv
        xo_ref[...] = _shell_post(xv, fv, vp_ref, res_w)
        for n, val in enumerate((hs, axp, axc, bv, vc)):
            sav_ref[:, n * W:(n + 1) * W] = val

    tile = lambda w: pl.BlockSpec((tm, w), lambda t: (t, 0))
    sd = jax.ShapeDtypeStruct
    return pl.pallas_call(
        body, name=name, grid=(S // tm,),
        in_specs=[tile(D), _whole((8, D))] + [_resident(a) for a in (win4, cwa, wg, bg, cwb, v512, pavg, wout)],
        out_specs=[tile(D), tile(D), tile(D), tile(5 * W)],
        out_shape=[sd((S, D), F32), sd((S, D), F32), sd((S, D), MXU_DT), sd((S, 5 * W), F32)],
        scratch_shapes=[pltpu.VMEM((HA, W), F32), pltpu.VMEM((HB, W), F32), pltpu.VMEM((8, W), F32)],
        compiler_params=_cp(("arbitrary",)),
    )(x, vp, win4, cwa, wg, bg, cwb, v512, pavg, wout)


def _mix_ab_bwd(x, f, dout, vp, h, sav, win4, cwa, wg, bg, cwb, v512, pavg, wout, res_w, name, carry=None, tm=256):
    S, D = x.shape
    W = win4.shape[-1]
    T = S // tm
    KA, KB, HA, HB = 4, 31, 8, 32

    def body(*refs):
        ins, outs, scr, carry_refs = _split_refs(refs, 15, 9, 8, carry)
        (x_ref, f_ref, do_ref, vp_ref, h_ref, sav_ref, prev_ref,
         win_ref, cwa_ref, wg_ref, bg_ref, cwb_ref, v_ref, p_ref, wo_ref) = ins
        hs_ref, axp_ref, axc_ref, bv_ref, vc_ref = (sav_ref.at[:, n * W:(n + 1) * W] for n in range(5))
        hsp_ref = prev_ref.at[HB - HA:, 0:W]
        axpp_ref = prev_ref.at[HB - HA:, W:2 * W]
        bvp_ref = prev_ref.at[:, 3 * W:4 * W]
        dx_ref, s_ref, dwin_out, dwo_out, dwg_out, dcwa_ref, dcwb_ref, dbg_ref, dv_ref = outs
        danext, dvnext, gfirst, afirst, dwin_ref, dwo_ref, dwg_ref, sem = scr
        t = pl.program_id(0)
        _carry_run(carry, carry_refs, t == 0, t == T - 1)

        @pl.when(t == 0)
        def _():
            for ref in (s_ref, dwin_ref, dwo_ref, dwg_ref, dcwa_ref, dcwb_ref, dbg_ref, dv_ref, danext, dvnext,
                        gfirst, afirst):
                ref[...] = jnp.zeros_like(ref)

        notfirst = jnp.where(t < T - 1, 1.0, 0.0).astype(F32)
        hv = h_ref[...]
        dov = do_ref[...]
        dfv = _shell_post_bwd(f_ref[...], dov, vp_ref, res_w, s_ref).astype(MXU_DT)
        a_gate = _dot(hv, win_ref[0])
        b_val = _dot(hv, win_ref[2])
        b_gate = _dot(hv, win_ref[3])
        rows = lax.broadcasted_iota(jnp.int32, (tm, W), 0)
        ge, dge = _gelu(a_gate)
        hsv = hs_ref[...]
        ya = hsv * ge
        vhat, rstd, vn = _group_norm(vc_ref[...], p_ref, v_ref[3:4, :], v_ref[4:5, :])
        sgn = _sig(vn)
        yb = vn * sgn
        dma = _dot_nt(dfv, wo_ref[0:W, :])
        dmb = _dot_nt(dfv, wo_ref[W:, :])
        dwo_ref[0:W, :] += _dot_tn(ya, dfv)
        dwo_ref[W:, :] += _dot_tn(yb, dfv)
        dhs = dma * ge
        d_a_gate = dma * hsv * dge
        axcv = axc_ref[...]
        lam = v_ref[1:2, :]
        r, i, ls, a, mult = _lru_gates(axcv, wg_ref, bg_ref, lam, W)
        ash = jnp.where(rows == tm - 1, afirst[0:1, :], _shift_up(a, 1))
        asuf, gloc = _scan_bwd(ash, dhs, rows)
        gsc = gloc + asuf * gfirst[0:1, :]
        afirst[...] = a[0:8, :]
        gfirst[...] = gsc[0:8, :]
        hprev = jnp.where(rows == 0, hsp_ref[HA - 1:HA, :] * notfirst, _shift_down(hsv, 1))
        da = gsc * hprev
        dL = da * a - gsc * (i * axcv) * (a * a) / mult
        dix = gsc * mult
        daxc = dix * i
        dr = dL * (LRU_C * ls)
        dv_ref[1:2, :] += _colsum(dL * r) * (LRU_C * _sig(-lam))
        dgate = jnp.concatenate([dr * r * (1.0 - r), (dix * axcv) * i * (1.0 - i)], axis=1)
        dbg_ref[0:1, :] += _colsum(dgate)
        dwg_ref[...] += _dot_tn(axcv, dgate)
        daxc = daxc + _dot_nt(dgate, wg_ref[...])
        daxp = _anticausal_conv(jnp.concatenate([daxc, danext[...]], axis=0), cwa_ref, KA, tm)
        ext = jnp.concatenate([axpp_ref[...] * notfirst, axp_ref[...]], axis=0)
        for k in range(KA):
            dcwa_ref[k:k + 1, :] += _colsum(daxc * _shift_down(ext, KA - 1 - k)[HA:, :])
        dv_ref[0:1, :] += _colsum(daxc)
        danext[...] = daxc[0:HA, :]
        dvn = dmb * (sgn * (1.0 + vn * (1.0 - sgn)))
        dv_ref[4:5, :] += _colsum(dvn)
        dv_ref[3:4, :] += _colsum(dvn * vhat)
        dvh = dvn * v_ref[3:4, :]
        dvc = rstd * (dvh - _group_mean(dvh, p_ref) - vhat * _group_mean(dvh * vhat, p_ref))
        dbv = _anticausal_conv(jnp.concatenate([dvc, dvnext[...]], axis=0), cwb_ref, KB, tm)
        ext = jnp.concatenate([bvp_ref[...] * notfirst, bv_ref[...]], axis=0)
        for k in range(KB):
            dcwb_ref[k:k + 1, :] += _colsum(dvc * _shift_down(ext, KB - 1 - k)[HB:, :])
        dv_ref[2:3, :] += _colsum(dvc)
        dvnext[...] = dvc[0:HB, :]
        sb = _sig(b_gate)
        dzs = (d_a_gate, daxp, dbv * sb, dbv * b_val * sb * (1.0 - sb))
        dh = None
        for s in range(4):
            part = _dot_nt(dzs[s], win_ref[s])
            dh = part if dh is None else dh + part
            dwin_ref[s] += _dot_tn(hv, dzs[s])
        dx_ref[...] = _shell_pre_bwd(x_ref[...], dh, dov, vp_ref, s_ref)

        @pl.when(t == T - 1)
        def _():
            _copy_out(((dwin_ref, dwin_out), (dwo_ref, dwo_out), (dwg_ref, dwg_out)), sem)

    tile = lambda w: pl.BlockSpec((tm, w), lambda t: (T - 1 - t, 0))
    prev = pl.BlockSpec((HB, 5 * W), lambda t: (jnp.maximum((T - 1 - t) * (tm // HB) - 1, 0), 0))
    out_shapes = [(S, D), (8, D), win4.shape, wout.shape, wg.shape, (8, W), (32, W), (8, 2 * W), (8, W)]
    c_in, c_out, c_shapes, c_sems, c_alias = _carry_args(carry, 15, 9)
    return pl.pallas_call(
        body, name=name, grid=(T,),
        in_specs=[tile(D), tile(D), tile(D), _whole((8, D)), tile(D), tile(5 * W), prev]
        + [_resident(a) for a in (win4, cwa, wg, bg, cwb, v512, pavg, wout)] + c_in,
        out_specs=[tile(D), _whole((8, D)), ANY, ANY, ANY] + [_whole(s) for s in out_shapes[5:]] + c_out,
        out_shape=[jax.ShapeDtypeStruct(s, F32) for s in out_shapes] + c_shapes,
        scratch_shapes=[pltpu.VMEM((HA, W), F32), pltpu.VMEM((HB, W), F32), pltpu.VMEM((8, W), F32),
                        pltpu.VMEM((8, W), F32), pltpu.VMEM(win4.shape, F32), pltpu.VMEM(wout.shape, F32),
                        pltpu.VMEM(wg.shape, F32), pltpu.SemaphoreType.DMA] + c_sems,
        input_output_aliases=c_alias, compiler_params=_cp(("arbitrary",), has_side_effects=carry is not None),
    )(x, f, dout, vp, h, sav, sav, win4, cwa, wg, bg, cwb, v512, pavg, wout,
      *(carry.srcs if carry is not None else []))


def _mix_c_core(hv, win_ref, v2_ref, v1_ref, ws_ref, bsb_ref, tm, D):
    zp = jnp.concatenate([_dot(hv, win_ref[s]) for s in range(4)], axis=1) + v2_ref[0:1, :]
    z, dz = _gelu(zp)
    u, v = z[:, :D], z[:, D:]
    mu = _rowmean(v)
    dv = v - mu
    rstd = lax.rsqrt(_rowmean(dv * dv) + EPS)
    vhat = dv * rstd
    vn = vhat * v1_ref[0:1, :] + v1_ref[1:2, :]
    rows_out = []
    for cidx in range(tm // CHUNK):
        blk = vn[cidx * CHUNK:(cidx + 1) * CHUNK, :]
        heads = [_dot(ws_ref[hd], blk[:, hd * CHUNK:(hd + 1) * CHUNK]) for hd in range(N_HEAD)]
        rows_out.append(jnp.concatenate(heads, axis=1) + bsb_ref[...])
    mixed = jnp.concatenate(rows_out, axis=0)
    return dz, u, rstd, vhat, vn, mixed


def _mix_c_fwd(x, vp, win4, v2d, v1d, ws, bsb, wout, res_w, name, tm=512):
    S, D = x.shape

    def body(x_ref, vp_ref, win_ref, v2_ref, v1_ref, ws_ref, bsb_ref, wo_ref, xo_ref, f_ref, h_ref):
        xv = x_ref[...]
        hv = _shell_pre(xv, vp_ref).astype(MXU_DT)
        h_ref[...] = hv
        _, u, _, _, _, mixed = _mix_c_core(hv, win_ref, v2_ref, v1_ref, ws_ref, bsb_ref, tm, D)
        fv = _dot(u * mixed, wo_ref[...])
        f_ref[...] = fv
        xo_ref[...] = _shell_post(xv, fv, vp_ref, res_w)

    tile = pl.BlockSpec((tm, D), lambda t: (t, 0))
    sd = jax.ShapeDtypeStruct
    return pl.pallas_call(
        body, name=name, grid=(S // tm,),
        in_specs=[tile, _whole((8, D))] + [_resident(a) for a in (win4, v2d, v1d, ws, bsb, wout)],
        out_specs=[tile, tile, tile],
        out_shape=[sd((S, D), F32), sd((S, D), F32), sd((S, D), MXU_DT)], compiler_params=_cp(("arbitrary",)),
    )(x, vp, win4, v2d, v1d, ws, bsb, wout)


def _mix_c_bwd(x, f, dout, vp, h, win4, v2d, v1d, ws, bsb, wout, res_w, name, carry=None, tm=256):
    S, D = x.shape

    def body(*refs):
        ins, outs, (dwin_ref, dwo_ref, sem), carry_refs = _split_refs(refs, 11, 8, 3, carry)
        x_ref, f_ref, do_ref, vp_ref, h_ref, win_ref, v2_ref, v1_ref, ws_ref, bsb_ref, wo_ref = ins
        dx_ref, s_ref, dwin_out, dwo_out, dws_ref, dbsb_ref, dv2_ref, dv1_ref = outs
        _carry_run(carry, carry_refs, pl.program_id(0) == 0, pl.program_id(0) == S // tm - 1)

        @pl.when(pl.program_id(0) == 0)
        def _():
            for ref in (s_ref, dwin_ref, dwo_ref, dws_ref, dbsb_ref, dv2_ref, dv1_ref):
                ref[...] = jnp.zeros_like(ref)

        hv = h_ref[...]
        dov = do_ref[...]
        dfv = _shell_post_bwd(f_ref[...], dov, vp_ref, res_w, s_ref).astype(MXU_DT)
        dz, u, rstd, vhat, vn, mixed = _mix_c_core(hv, win_ref, v2_ref, v1_ref, ws_ref, bsb_ref, tm, D)
        dp = _dot_nt(dfv, wo_ref[...])
        dwo_ref[...] += _dot_tn(u * mixed, dfv)
        du = dp * mixed
        dmx = dp * u
        rows_out = []
        for cidx in range(tm // CHUNK):
            dblk = dmx[cidx * CHUNK:(cidx + 1) * CHUNK, :]
            vblk = vn[cidx * CHUNK:(cidx + 1) * CHUNK, :]
            dbsb_ref[...] += dblk
            heads = []
            for hd in range(N_HEAD):
                dsl = dblk[:, hd * CHUNK:(hd + 1) * CHUNK]
                heads.append(_dot_tn(ws_ref[hd], dsl))
                dws_ref[hd] += _dot_nt(dsl, vblk[:, hd * CHUNK:(hd + 1) * CHUNK])
            rows_out.append(jnp.concatenate(heads, axis=1))
        dvn = jnp.concatenate(rows_out, axis=0)
        dv1_ref[1:2, :] += _colsum(dvn)
        dv1_ref[0:1, :] += _colsum(dvn * vhat)
        dvh = dvn * v1_ref[0:1, :]
        dv = rstd * (dvh - _rowmean(dvh) - vhat * _rowmean(dvh * vhat))
        dzp = jnp.concatenate([du, dv], axis=1) * dz
        dv2_ref[0:1, :] += _colsum(dzp)
        W = win_ref.shape[-1]
        dh = None
        for s in range(4):
            dzs = dzp[:, s * W:(s + 1) * W]
            part = _dot_nt(dzs, win_ref[s])
            dh = part if dh is None else dh + part
            dwin_ref[s] += _dot_tn(hv, dzs)
        dx_ref[...] = _shell_pre_bwd(x_ref[...], dh, dov, vp_ref, s_ref)

        @pl.when(pl.program_id(0) == S // tm - 1)
        def _():
            _copy_out(((dwin_ref, dwin_out), (dwo_ref, dwo_out)), sem)

    tile = pl.BlockSpec((tm, D), lambda t: (t, 0))
    out_shapes = [(S, D), (8, D), win4.shape, wout.shape, ws.shape, bsb.shape, (8, 2 * D), (8, D)]
    c_in, c_out, c_shapes, c_sems, c_alias = _carry_args(carry, 11, 8)
    return pl.pallas_call(
        body, name=name, grid=(S // tm,),
        in_specs=[tile, tile, tile, _whole((8, D)), tile] + [_resident(a) for a in (win4, v2d, v1d, ws, bsb, wout)]
        + c_in,
        out_specs=[tile, _whole((8, D)), ANY, ANY] + [_whole(s) for s in out_shapes[4:]] + c_out,
        out_shape=[jax.ShapeDtypeStruct(s, F32) for s in out_shapes] + c_shapes,
        scratch_shapes=[pltpu.VMEM(win4.shape, F32), pltpu.VMEM(wout.shape, F32), pltpu.SemaphoreType.DMA] + c_sems,
        input_output_aliases=c_alias, compiler_params=_cp(("arbitrary",), has_side_effects=carry is not None),
    )(x, f, dout, vp, h, win4, v2d, v1d, ws, bsb, wout, *(carry.srcs if carry is not None else []))


def _ada_fwd(c_all, ada_w, ada_b_my, name):
    L, D, N = ada_w.shape
    tn = N // 3

    def body(c_ref, w_ref, b_ref, o_ref):
        cv = c_ref[...]
        o_ref[...] = _dot_hi(cv * _sig(cv), w_ref[...]) + b_ref[...]

    return pl.pallas_call(
        body, name=name, grid=(L, 3),
        in_specs=[pl.BlockSpec((8, D), lambda l, n: (0, 0)), pl.BlockSpec((None, D, tn), lambda l, n: (l, 0, n)),
                  pl.BlockSpec((None, 1, tn), lambda l, n: (l, 0, n))],
        out_specs=pl.BlockSpec((None, 8, tn), lambda l, n: (l, 0, n)),
        out_shape=jax.ShapeDtypeStruct((L, 8, N), F32), compiler_params=_cp(("arbitrary", "arbitrary")),
    )(c_all, ada_w, ada_b_my)


def _ada_bwd(c_all, dmod_my, name):
    L, _, N = dmod_my.shape
    D = c_all.shape[1]
    tn = N // 3

    def body(c_ref, d_ref, o_ref):
        cv = c_ref[...]
        o_ref[...] = lax.dot_general(cv * _sig(cv), d_ref[...], (((0,), (0,)), ((), ())),
                                     precision=lax.Precision.HIGHEST, preferred_element_type=F32)

    return pl.pallas_call(
        body, name=name, grid=(L, 3),
        in_specs=[pl.BlockSpec((8, D), lambda l, n: (0, 0)), pl.BlockSpec((None, 8, tn), lambda l, n: (l, 0, n))],
        out_specs=pl.BlockSpec((None, D, tn), lambda l, n: (l, 0, n)),
        out_shape=jax.ShapeDtypeStruct((L, D, N), F32), compiler_params=_cp(("arbitrary", "arbitrary")),
    )(c_all, dmod_my)


def _row_tile(rows, cols, budget=1 << 20):
    best = 8
    for rt in range(8, rows + 1, 8):
        if rows % rt == 0 and rt * cols * 4 <= budget:
            best = rt
    return best


def _my_chip():
    return 2 * lax.axis_index("x") + lax.axis_index("y")


def _cast_place(w, g, name):
    _, R, C = w.shape
    rt = _row_tile(R, C)

    def body(w_ref, o_ref):
        o_ref[...] = w_ref[...].astype(o_ref.dtype)

    return pl.pallas_call(
        body, name=name, grid=(R // rt,),
        in_specs=[pl.BlockSpec((None, rt, C), lambda r: (g, r, 0))],
        out_specs=pl.BlockSpec((None, None, rt, C), lambda r: (0, _my_chip(), r, 0)),
        out_shape=jax.ShapeDtypeStruct((1, N_SHARD, R, C), MXU_DT), compiler_params=_cp(("arbitrary",)),
    )(w)


def _add_half(gk, la, name, out_dtype=F32):
    Gk, _, R, C = gk.shape
    Rh = R // 2
    n = Gk * N_SHARD
    gv = gk.reshape(n, 2, Rh, C)
    lv = la.reshape(n, Rh, C)
    rt = _row_tile(Rh, C)

    def body(g_ref, l_ref, o_ref):
        o_ref[...] = (g_ref[...] + l_ref[...]).astype(o_ref.dtype)

    out = pl.pallas_call(
        body, name=name, grid=(n, Rh // rt),
        in_specs=[pl.BlockSpec((None, None, rt, C), lambda i, r: (i, lax.axis_index("c"), r, 0)),
                  pl.BlockSpec((None, rt, C), lambda i, r: (i, r, 0))],
        out_specs=pl.BlockSpec((None, rt, C), lambda i, r: (i, r, 0)),
        out_shape=jax.ShapeDtypeStruct((n, Rh, C), out_dtype), compiler_params=_cp(("arbitrary", "arbitrary")),
    )(gv, lv)
    return out.reshape(Gk, N_SHARD, Rh, C)


def _sum_chips(part, landed, joined, g, name):
    _, _, Rh, C = part.shape
    rt = _row_tile(Rh, C)
    nb = Rh // rt

    def body(p_ref, l_ref, j_ref, o_ref):
        up = lambda v: v.astype(F32)
        o_ref[...] = ((up(p_ref[...]) + up(l_ref[0])) + up(l_ref[1])) + up(l_ref[2])

    return pl.pallas_call(
        body, name=name, grid=(nb,),
        in_specs=[pl.BlockSpec((None, None, rt, C), lambda r: (0, _my_chip(), r, 0)),
                  pl.BlockSpec((None, 3, rt, C), lambda r: (0, 0, r, 0)), ANY],
        out_specs=pl.BlockSpec((None, rt, C), lambda r: (g, lax.axis_index("c") * nb + r, 0)),
        out_shape=jax.ShapeDtypeStruct(joined.shape, F32), input_output_aliases={2: 0},
        compiler_params=_cp(("arbitrary",)),
    )(part, landed, joined)


def _sum_slots(lb, name):
    Gk, n, Rh, C = lb.shape
    rt = _row_tile(Rh, C)

    def body(l_ref, o_ref):
        acc = l_ref[0]
        for s in range(1, n):
            acc = acc + l_ref[s]
        o_ref[...] = acc

    return pl.pallas_call(
        body, name=name, grid=(Gk, Rh // rt),
        in_specs=[pl.BlockSpec((None, n, rt, C), lambda g, r: (g, 0, r, 0))],
        out_specs=pl.BlockSpec((None, rt, C), lambda g, r: (g, r, 0)),
        out_shape=jax.ShapeDtypeStruct((Gk, Rh, C), F32), compiler_params=_cp(("arbitrary", "arbitrary")),
    )(lb)


def _adamw(w, g, m, v, name):
    rows, cols = w.shape
    rt = _row_tile(rows, cols) if rows % 8 == 0 else rows
    c1 = 1.0 - ADAM_B1 ** ADAM_STEP
    c2 = 1.0 - ADAM_B2 ** ADAM_STEP

    def body(w_ref, g_ref, m_ref, v_ref, d_ref, mo_ref, vo_ref):
        gv = g_ref[...]
        mn = ADAM_B1 * m_ref[...] + (1.0 - ADAM_B1) * gv
        vn = ADAM_B2 * v_ref[...] + (1.0 - ADAM_B2) * (gv * gv)
        d_ref[...] = -ADAM_LR * ((mn / c1) / (jnp.sqrt(vn / c2) + ADAM_EPS) + ADAM_WD * w_ref[...])
        mo_ref[...] = mn
        vo_ref[...] = vn

    spec = pl.BlockSpec((rt, cols), lambda r: (r, 0))
    sds = jax.ShapeDtypeStruct((rows, cols), F32)
    return pl.pallas_call(
        body, name=name, grid=(rows // rt,), in_specs=[spec] * 4, out_specs=[spec] * 3,
        out_shape=[sds] * 3, compiler_params=_cp(("arbitrary",)),
    )(w, g, m, v)


def _coords():
    return lax.axis_index("x"), lax.axis_index("y"), lax.axis_index("c")


def _all_gather8(blk, name, carry=None):
    m_per, n = blk.shape

    def body(*refs):
        (x_ref,), (out_ref,), (send_sems, recv_sems, local_sem), carry_refs = _split_refs(refs, 1, 1, 3, carry)
        riding = _carry_copies(carry, carry_refs)
        for cp in riding:
            cp.start()
        x, y, c = _coords()
        me, sibling = (x, y, c), (x, y, 1 - c)
        chips = [(1 - x, y), (x, 1 - y), (1 - x, 1 - y)]

        def rows(px, py, pc):
            return out_ref.at[pl.ds((4 * px + 2 * py + pc) * m_per, m_per), :]

        def copy(k, block, to, src=None):
            return pltpu.make_async_remote_copy(
                src_ref=rows(*block) if src is None else src, dst_ref=rows(*block),
                send_sem=send_sems.at[k], recv_sem=recv_sems.at[k], device_id=to, device_id_type=MESH)

        mine = pltpu.make_async_copy(x_ref, rows(*me), local_sem)
        mine.start()
        first = [copy(0, me, sibling, src=x_ref)]
        first += [copy(1 + j, me, (*chip, c), src=x_ref) for j, chip in enumerate(chips)]
        for cp in first:
            cp.start()
        passed = [copy(4 + j, (*chip, c), sibling) for j, chip in enumerate(chips)]
        for j, chip in enumerate(chips):
            copy(1 + j, (*chip, c), me).wait_recv()
            passed[j].start()
        copy(0, sibling, me).wait_recv()
        for j, chip in enumerate(chips):
            copy(4 + j, (*chip, 1 - c), me).wait_recv()
        for cp in first + passed:
            cp.wait_send()
        mine.wait()
        for cp in riding:
            cp.wait()

    c_in, c_out, c_shapes, c_sems, c_alias = _carry_args(carry, 1, 1)
    res = pl.pallas_call(
        body, name=name, out_shape=[jax.ShapeDtypeStruct((N_DEV * m_per, n), blk.dtype)] + c_shapes,
        in_specs=[pl.BlockSpec(memory_space=pltpu.VMEM)] + c_in,
        out_specs=[pl.BlockSpec(memory_space=pltpu.VMEM)] + c_out,
        scratch_shapes=[pltpu.SemaphoreType.DMA((7,)), pltpu.SemaphoreType.DMA((7,)), pltpu.SemaphoreType.DMA] + c_sems,
        input_output_aliases=c_alias, compiler_params=_cp(),
    )(blk, *(carry.srcs if carry is not None else []))
    return res[0] if carry is None else res


def _comm_call(name, inputs, out_shapes, plan, n_remote, aliases=None):
    n_in, n_out = len(inputs), len(out_shapes)

    def body(*refs):
        in_refs, out_refs = refs[:n_in], refs[n_in:n_in + n_out]
        send_sems, recv_sems = refs[n_in + n_out:]

        def remote(k, src, dst, to):
            return pltpu.make_async_remote_copy(src_ref=src, dst_ref=dst, send_sem=send_sems.at[k],
                                                recv_sem=recv_sems.at[k], device_id=to, device_id_type=MESH)

        plan(in_refs, out_refs, remote)

    return pl.pallas_call(
        body, name=name, out_shape=out_shapes, in_specs=[ANY] * n_in, out_specs=[ANY] * n_out,
        scratch_shapes=[pltpu.SemaphoreType.DMA((n_remote,)), pltpu.SemaphoreType.DMA((n_remote,))],
        input_output_aliases=aliases or {}, compiler_params=_cp(has_side_effects=True),
    )(*inputs)


def _gather_ici_carry(placed):
    K = len(placed)

    def plan(ins, outs, remote):
        x, y, c = _coords()
        s_me = 2 * x + y
        cps = []
        for j, (px, py) in enumerate([(1 - x, y), (x, 1 - y), (1 - x, 1 - y)]):
            for k in range(K):
                rh = placed[k].shape[2] // 2
                own = outs[k].at[:, s_me, pl.ds(c * rh, rh), :]
                cps.append(remote(j * K + k, own, own, (px, py, c)))
        return cps

    shapes = [jax.ShapeDtypeStruct(p.shape, p.dtype) for p in placed]
    return _Carry(placed, shapes, {k: k for k in range(K)}, 3 * K, plan)


def _forward_sibling(placed, name):
    K = len(placed)

    def plan(ins, outs, remote):
        x, y, c = _coords()
        cps = []
        for j, (px, py) in enumerate([(1 - x, y), (x, 1 - y), (1 - x, 1 - y)]):
            for k in range(K):
                rh = placed[k].shape[2] // 2
                landed = outs[k].at[:, 2 * px + py, pl.ds(c * rh, rh), :]
                cps.append(remote(j * K + k, landed, landed, (x, y, 1 - c)))
        for cp in cps:
            cp.start()
        for j, (px, py) in enumerate([(1 - x, y), (x, 1 - y), (1 - x, 1 - y)]):
            for k in range(K):
                rh = placed[k].shape[2] // 2
                other = outs[k].at[:, 2 * px + py, pl.ds((1 - c) * rh, rh), :]
                remote(j * K + k, other, other, (x, y, 1 - c)).wait_recv()
        for cp in cps:
            cp.wait_send()

    out_shapes = [jax.ShapeDtypeStruct(p.shape, p.dtype) for p in placed]
    return _comm_call(name, placed, out_shapes, plan, 3 * K, aliases={k: k for k in range(K)})


def _gather_weights(placed, name):
    K = len(placed)

    def plan(ins, outs, remote):
        x, y, c = _coords()
        s_me = 2 * x + y
        sibling = (x, y, 1 - c)
        chips = [(1 - x, y), (x, 1 - y), (1 - x, 1 - y)]
        half = lambda k, s, cc: outs[k].at[:, s, pl.ds(cc * (placed[k].shape[2] // 2), placed[k].shape[2] // 2), :]
        sent = []
        for j, (px, py) in enumerate(chips):
            for k in range(K):
                own = half(k, s_me, c)
                cp = remote(j * K + k, own, own, (px, py, c))
                cp.start()
                sent.append(cp)
        for j, (px, py) in enumerate(chips):
            s_from = 2 * px + py
            for k in range(K):
                landed = half(k, s_from, c)
                remote(j * K + k, landed, landed, (px, py, c)).wait_recv()
                cp = remote((3 + j) * K + k, landed, landed, sibling)
                cp.start()
                sent.append(cp)
        for j, (px, py) in enumerate(chips):
            s_from = 2 * px + py
            for k in range(K):
                other = half(k, s_from, 1 - c)
                remote((3 + j) * K + k, other, other, sibling).wait_recv()
        for cp in sent:
            cp.wait_send()

    out_shapes = [jax.ShapeDtypeStruct(p.shape, p.dtype) for p in placed]
    return _comm_call(name, placed, out_shapes, plan, 6 * K, aliases={k: k for k in range(K)})


def _send_other_half(grads, name):
    K = len(grads)

    def plan(ins, outs, remote):
        x, y, c = _coords()
        cps = []
        for k in range(K):
            rh = grads[k].shape[2] // 2
            cps.append(remote(k, ins[k].at[:, :, pl.ds((1 - c) * rh, rh), :], outs[k], (x, y, 1 - c)))
        for cp in cps:
            cp.start()
        for cp in cps:
            cp.wait()

    out_shapes = [jax.ShapeDtypeStruct(g.shape[:2] + (g.shape[2] // 2, g.shape[3]), g.dtype) for g in grads]
    return _comm_call(name, grads, out_shapes, plan, K)


def _send_half_carry(grads):
    K = len(grads)

    def plan(ins, outs, remote):
        x, y, c = _coords()
        cps = []
        for k in range(K):
            rh = grads[k].shape[2] // 2
            cps.append(remote(k, ins[k].at[:, :, pl.ds((1 - c) * rh, rh), :], outs[k], (x, y, 1 - c)))
        return cps

    out_shapes = [jax.ShapeDtypeStruct(g.shape[:2] + (g.shape[2] // 2, g.shape[3]), g.dtype) for g in grads]
    return _Carry(grads, out_shapes, {}, K, plan)


def _merge_carries(carries):
    if not carries:
        return None
    if len(carries) == 1:
        return carries[0]
    srcs, shapes, aliases = [], [], {}
    for cy in carries:
        aliases.update({len(srcs) + i: len(shapes) + o for i, o in cy.aliases.items()})
        srcs += cy.srcs
        shapes += cy.out_shapes

    def plan(ins, outs, remote):
        cps, i0, o0, k0 = [], 0, 0, 0
        for cy in carries:
            shifted = functools.partial(lambda k, src, dst, to, base: remote(base + k, src, dst, to), base=k0)
            cps += cy.plan(ins[i0:i0 + len(cy.srcs)], outs[o0:o0 + len(cy.out_shapes)], shifted)
            i0, o0, k0 = i0 + len(cy.srcs), o0 + len(cy.out_shapes), k0 + cy.n
        return cps

    return _Carry(srcs, shapes, aliases, sum(cy.n for cy in carries), plan)


def _scatter_carry(parts):
    K = len(parts)

    def plan(ins, outs, remote):
        x, y, c = _coords()
        chips = [(1 - x, y), (x, 1 - y), (1 - x, 1 - y)]
        return [remote(j * K + k, ins[k].at[:, 2 * px + py], outs[k].at[:, j], (px, py, c))
                for j, (px, py) in enumerate(chips) for k in range(K)]

    out_shapes = [jax.ShapeDtypeStruct((p.shape[0], 3) + p.shape[2:], p.dtype) for p in parts]
    return _Carry(parts, out_shapes, {}, 3 * K, plan)


def _join_halves(joined):
    K = len(joined)

    def plan(ins, outs, remote):
        x, y, c = _coords()
        cps = []
        for k in range(K):
            rh = joined[k].shape[1] // 2
            mine = outs[k].at[:, pl.ds(c * rh, rh), :]
            cps.append(remote(k, mine, mine, (x, y, 1 - c)))
        for cp in cps:
            cp.start()
        for k in range(K):
            rh = joined[k].shape[1] // 2
            other = outs[k].at[:, pl.ds((1 - c) * rh, rh), :]
            remote(k, other, other, (x, y, 1 - c)).wait_recv()
        for cp in cps:
            cp.wait_send()

    out_shapes = [jax.ShapeDtypeStruct(h.shape, h.dtype) for h in joined]
    return _comm_call("join_halves", joined, out_shapes, plan, K, aliases={k: k for k in range(K)})


def _pack(parts):
    flat = []
    for p in parts:
        v = p.reshape(-1).astype(F32)
        pad = (-v.shape[0]) % 1024
        flat.append(jnp.pad(v, (0, pad)) if pad else v)
    return jnp.concatenate(flat).reshape(-1, 128)


def _unpack(packed, shapes):
    flat = packed.reshape(-1)
    out, off = [], 0
    for shp in shapes:
        n = math.prod(shp)
        out.append(flat[off:off + n].reshape(shp))
        off += n + (-n) % 1024
    return out


def _shard_last(a, s, n):
    return lax.dynamic_slice_in_dim(a, s * n, n, axis=a.ndim - 1)


def _rows8(*vecs):
    n = vecs[0].shape[-1]
    rows = [v.reshape(1, n).astype(F32) for v in vecs]
    return jnp.concatenate(rows + [jnp.zeros((8 - len(rows), n), F32)], axis=0)


def kernel(x, c, ada_w, ada_b, norm_pre, norm_post, ffn_w13, ffn_w2, ab_w_in, a_conv_w, a_conv_b, a_gate_w, a_gate_b, a_lam, b_conv_w, b_conv_b, b_norm_g, b_norm_b, ab_w_out, c_w_in, c_b_in, c_norm_g, c_norm_b, c_w_s, c_b_s, c_w_out, loss_target, m_ada_w, m_ada_b, m_norm_pre, m_norm_post, m_ffn_w13, m_ffn_w2, m_ab_w_in, m_a_conv_w, m_a_conv_b, m_a_gate_w, m_a_gate_b, m_a_lam, m_b_conv_w, m_b_conv_b, m_b_norm_g, m_b_norm_b, m_ab_w_out, m_c_w_in, m_c_b_in, m_c_norm_g, m_c_norm_b, m_c_w_s, m_c_b_s, m_c_w_out, v_ada_w, v_ada_b, v_norm_pre, v_norm_post, v_ffn_w13, v_ffn_w2, v_ab_w_in, v_a_conv_w, v_a_conv_b, v_a_gate_w, v_a_gate_b, v_a_lam, v_b_conv_w, v_b_conv_b, v_b_norm_g, v_b_norm_b, v_ab_w_out, v_c_w_in, v_c_b_in, v_c_norm_g, v_c_norm_b, v_c_w_s, v_c_b_s, v_c_w_out):
    S, D = x.shape[1], x.shape[2]
    W = a_lam.shape[-1]
    Fh = ffn_w13.shape[-1]
    Fq = ffn_w2.shape[2]
    xi, yi, ci = _coords()
    shard = 2 * xi + yi
    me = 4 * xi + 2 * yi + ci
    x2, tgt = x[0], loss_target[0]

    sharded_small = [norm_pre, norm_post, a_conv_w, b_conv_w, c_b_in, c_norm_g, c_norm_b]
    gathered = _all_gather8(_pack([c] + sharded_small), "gather_small")
    blocks = gathered.reshape(N_DEV, -1, 128)
    per_dev = [_unpack(blocks[d], [c.shape] + [p.shape for p in sharded_small]) for d in range(0, N_DEV, 2)]
    c_all = jnp.concatenate([_unpack(blocks[d], [c.shape])[0] for d in range(N_DEV)], axis=0)
    npre, npost, acw, bcw, cbin, cng, cnb = [jnp.concatenate([per_dev[s][1 + i] for s in range(N_SHARD)], axis=-1)
                                             for i in range(len(sharded_small))]

    ada_b_my = _shard_last(ada_b, shard, ada_w.shape[-1])[:, None, :]
    modp = _ada_fwd(c_all, ada_w, ada_b_my, "ada_fwd")
    modg = _all_gather8(modp.reshape(16, -1), "gather_mod").reshape(N_DEV, 2, 8, -1)
    mod_me = lax.dynamic_index_in_dim(modg[0::2], me, axis=2, keepdims=False)
    mod = jnp.transpose(mod_me, (1, 0, 2)).reshape(2, 3, 3, D)

    w13s, w2s = ffn_w13.reshape(4, D, Fh), ffn_w2.reshape(4, Fq, D)
    placed13 = [_cast_place(w13s, g, f"cast_w13_{g}") for g in range(4)]
    placed2 = [_cast_place(w2s, g, f"cast_w2_{g}") for g in range(4)]
    placed_mix = [_cast_place(w, 0, f"cast_mix{k}") for k, w in enumerate((ab_w_in, ab_w_out, c_w_in, c_w_out))]
    w13_first, w2_first = _gather_weights([placed13[0], placed2[0]], "gather_first")
    gather_carry = _gather_ici_carry(placed13[1:] + placed2[1:] + placed_mix)

    eye = jnp.eye(8, dtype=F32)
    dh_a = W // 8
    blockdiag = lambda w: jnp.einsum("hde,hg->hdge", w, eye).reshape(W, W)
    gw = a_gate_w[0]
    wg = jnp.concatenate([blockdiag(gw[:, :, :dh_a]), blockdiag(gw[:, :, dh_a:])], axis=1).astype(MXU_DT)
    bgv = jnp.concatenate([a_gate_b[0][:, :dh_a].reshape(-1), a_gate_b[0][:, dh_a:].reshape(-1)])
    bg = _rows8(bgv)
    cwa = jnp.concatenate([acw[0], jnp.zeros((4, W), F32)], axis=0)
    cwb = jnp.concatenate([bcw[0], jnp.zeros((1, W), F32)], axis=0)
    v512 = _rows8(a_conv_b[0], a_lam[0], b_conv_b[0], b_norm_g[0], b_norm_b[0])
    dg_b = W // 8
    gid = jnp.arange(W) // dg_b
    member = (gid[:, None] == jnp.arange(128)[None, :]).astype(F32)
    pavg = jnp.stack([member / dg_b, member]).astype(MXU_DT)
    v2d = _rows8(cbin[0])
    v1d = _rows8(cng[0], cnb[0])
    tril = jnp.tril(jnp.ones((CHUNK, CHUNK), dtype=bool))
    ws = jnp.where(tril, c_w_s[0], 0.0).astype(MXU_DT)
    bsb = jnp.repeat(jnp.transpose(c_b_s[0]), D // N_HEAD, axis=1)

    res_ws = (0.5, 1.0, 0.5)
    vps, xs, saved = [], [], []
    xc = x2
    w13g, w2v = [w13_first], [w2_first.reshape(1, 2, Fh, D)]
    for l in range(2):
        for j in range(3):
            k = 3 * l + j
            vp = _rows8(npre[l, j], mod[l, j, 0], mod[l, j, 1], mod[l, j, 2], npost[l, j])
            vps.append(vp)
            xs.append(xc)
            gi = 2 * l + j // 2
            if k == 0:
                xc, *keep = _ffn_fwd(xc, vp, w13g[0], w2v[0], res_ws[j], "ffn_fwd0", carry=gather_carry)
                keep, landed = keep[:4], keep[4:]
                full = _forward_sibling(landed, "forward_sibling")
                w13g += list(full[0:3])
                w2v += [w.reshape(1, 2, Fh, D) for w in full[3:6]]
                abin_g, about_g, cin_g, cout_g = full[6:]
                ab_ops = (abin_g[0], cwa, wg, bg, cwb, v512, pavg, about_g.reshape(D, D))
                c_ops = (cin_g[0], v2d, v1d, ws, bsb, cout_g.reshape(D, D))
            elif k == 5:
                dout, *keep, lrow = _ffn_fwd(xc, vp, w13g[gi], w2v[gi], res_ws[j], f"ffn_fwd{k}", tgt=tgt)
            elif j != 1:
                xc, *keep = _ffn_fwd(xc, vp, w13g[gi], w2v[gi], res_ws[j], f"ffn_fwd{k}")
            elif l == 0:
                xc, *keep = _mix_ab_fwd(xc, vp, *ab_ops, res_ws[j], "mix_ab_fwd")
            else:
                xc, *keep = _mix_c_fwd(xc, vp, *c_ops, res_ws[j], "mix_c_fwd")
            saved.append(keep)

    joined = {"w13": lax.empty((4, D, Fh), F32), "w2": lax.empty((4, Fq, D), F32), "abin": lax.empty((1, D, 4 * W // 4), F32),
              "about": lax.empty((1, D // 4, D), F32), "cin": lax.empty((1, D, 2 * D // 4), F32),
              "cout": lax.empty((1, D // 4, D), F32)}

    stage = {"send": None, "scatter": []}

    def add_halves(group, got, dtype=F32):
        grads, keys, g, tag = group
        parts = [_add_half(gr, la, f"add_half_{tag}{i}", dtype) for i, (gr, la) in enumerate(zip(grads, got))]
        return parts, keys, g

    def sum_landed(groups, landed_slots):
        landed_slots = list(landed_slots)
        for parts, keys, g in groups:
            for part, key in zip(parts, keys):
                joined[key] = _sum_chips(part, landed_slots.pop(0), joined[key], g, f"sum_chips_{key}{g}")
        return landed_slots

    def all_parts(groups):
        return [part for group in groups for part in group[0]]

    def ride(call, n_own, carries_ok=True):
        if not carries_ok:
            return call(None)[:n_own]
        scat, send = stage["scatter"], stage["send"]
        carries = ([_scatter_carry(all_parts(scat))] if scat else []) + (
            [_send_half_carry(send[0])] if send is not None else [])
        res = call(_merge_carries(carries))
        own, extra = res[:n_own], res[n_own:]
        stage["scatter"], stage["send"] = [], None
        extra = sum_landed(scat, extra)
        if send is not None:
            stage["scatter"].append(add_halves(send, extra))
        return own

    d_npre = [[None] * 3 for _ in range(2)]
    d_npost = [[None] * 3 for _ in range(2)]
    d_mod = [[None] * 3 for _ in range(2)]
    for l in (1, 0):
        for j in (2, 1, 0):
            k = 3 * l + j
            if j != 1:
                f, h, gpre, upre = saved[k]
                gi = 2 * l + j // 2
                dw13 = lax.empty((1, 4, D, Fh), F32)
                dw2v = lax.empty((1, 2, Fh, D), F32)
                df, dh0, s_a, dw13, dw2v = ride(lambda cy: _ffn_bwd_half(
                    0, xs[k], f, dout, vps[k], h, gpre, upre, w13g[gi], w2v[gi], dw13, dw2v, res_ws[j],
                    f"ffn_bwd{k}a", carry=cy), 5)
                dout, s_b, dw13, dw2v = ride(lambda cy: _ffn_bwd_half(
                    1, xs[k], f, dout, vps[k], h, gpre, upre, w13g[gi], w2v[gi], dw13, dw2v, res_ws[j],
                    f"ffn_bwd{k}b", df=df, dh0=dh0, carry=cy), 4, carries_ok=k > 0)
                sums = s_a + s_b
                made = ([dw13, dw2v.reshape(1, 4, Fq, D)], ["w13", "w2"], gi, f"ffn{gi}")
            elif l == 0:
                dout, sums, d_abin, d_about, d_wg, d_cwa, d_cwb, d_bg, d_v512 = ride(lambda cy: _mix_ab_bwd(
                    xs[k], saved[k][0], dout, vps[k], *saved[k][1:], *ab_ops, res_ws[j], "mix_ab_bwd", carry=cy), 9)
                made = ([d_abin[None], d_about.reshape(1, 4, D // 4, D)], ["abin", "about"], 0, "ab")
                stage["scatter"].append(add_halves(made, _send_other_half(made[0], "send_half_ab")))
                made = None
            else:
                dout, sums, d_cin, d_cout, d_ws, d_bsb, d_v2, d_v1 = ride(lambda cy: _mix_c_bwd(
                    xs[k], saved[k][0], dout, vps[k], saved[k][1], *c_ops, res_ws[j], "mix_c_bwd", carry=cy), 8)
                made = ([d_cin[None], d_cout.reshape(1, 4, D // 4, D)], ["cin", "cout"], 0, "c")
            stage["send"] = made
            d_npre[l][j], d_npost[l][j] = sums[4], sums[1]
            d_mod[l][j] = jnp.stack([sums[2], sums[3], sums[0]])
    last = stage["scatter"] + [add_halves(stage["send"], _send_other_half(stage["send"][0], "send_half_last"), MXU_DT)]
    grad_x = dout[None]

    dmod = jnp.stack([jnp.stack(d_mod[l]) for l in range(2)]).reshape(2, 9 * D)
    d_gate_w = jnp.concatenate([jnp.einsum("hdhe->hde", d_wg[:, :W].reshape(8, dh_a, 8, dh_a)),
                                jnp.einsum("hdhe->hde", d_wg[:, W:].reshape(8, dh_a, 8, dh_a))], axis=-1)
    d_gate_b = jnp.concatenate([d_bg[0, :W].reshape(8, dh_a), d_bg[0, W:].reshape(8, dh_a)], axis=-1)
    small_grads = [
        dmod, jnp.stack([jnp.stack(r) for r in d_npre]), jnp.stack([jnp.stack(r) for r in d_npost]),
        d_cwa[:4][None], d_v512[0][None], d_gate_w[None], d_gate_b[None], d_v512[1][None], d_cwb[:31][None],
        d_v512[2][None], d_v512[3][None], d_v512[4][None], d_v2[0][None], d_v1[0][None], d_v1[1][None],
        jnp.where(tril, d_ws, 0.0)[None], jnp.transpose(d_bsb.reshape(CHUNK, N_HEAD, D // N_HEAD).sum(-1))[None]]
    small_w = [ada_b, norm_pre, norm_post, a_conv_w, a_conv_b, a_gate_w, a_gate_b, a_lam, b_conv_w, b_conv_b,
               b_norm_g, b_norm_b, c_b_in, c_norm_g, c_norm_b, c_w_s, c_b_s]
    small_m = [m_ada_b, m_norm_pre, m_norm_post, m_a_conv_w, m_a_conv_b, m_a_gate_w, m_a_gate_b, m_a_lam, m_b_conv_w,
               m_b_conv_b, m_b_norm_g, m_b_norm_b, m_c_b_in, m_c_norm_g, m_c_norm_b, m_c_w_s, m_c_b_s]
    small_v = [v_ada_b, v_norm_pre, v_norm_post, v_a_conv_w, v_a_conv_b, v_a_gate_w, v_a_gate_b, v_a_lam, v_b_conv_w,
               v_b_conv_b, v_b_norm_g, v_b_norm_b, v_c_b_in, v_c_norm_g, v_c_norm_b, v_c_w_s, v_c_b_s]
    full_shapes = [g.shape for g in small_grads]
    loss_part = jnp.sum(lrow[0]).reshape(1, 1)
    sg_all, *landed_last = _all_gather8(_pack(small_grads + [loss_part]), "gather_small_grads",
                                        carry=_scatter_carry(all_parts(last)))
    sum_landed(last, landed_last)
    sg_all = sg_all.reshape(N_DEV, -1, 128)
    sg_sum = _sum_slots(sg_all[None], "sum_small_grads")[0]
    *g_full, loss_sum = _unpack(sg_sum, full_shapes + [(1, 1)])
    loss = loss_sum[0, 0]
    g_small = [g if g.shape == w.shape else _shard_last(g, shard, w.shape[-1]) for g, w in zip(g_full, small_w)]
    small_shapes = [w.shape for w in small_w]
    d_s, m_s, v_s = _adamw(_pack(small_w), _pack(g_small), _pack(small_m), _pack(small_v), "adamw_small")
    delta_small, newm_small, newv_small = (_unpack(a, small_shapes) for a in (d_s, m_s, v_s))

    dmod_all = jnp.stack([_unpack(sg_all[d], full_shapes[:1])[0] for d in range(N_DEV)], axis=1)
    n_ada = ada_w.shape[-1]
    g_ada_w = _ada_bwd(c_all, _shard_last(dmod_all, shard, n_ada), "ada_bwd")

    g_big = _join_halves([joined[key] for key in ("w13", "w2", "abin", "about", "cin", "cout")])

    big_w = [ffn_w13, ffn_w2, ab_w_in, ab_w_out, c_w_in, c_w_out, ada_w]
    big_m = [m_ffn_w13, m_ffn_w2, m_ab_w_in, m_ab_w_out, m_c_w_in, m_c_w_out, m_ada_w]
    big_v = [v_ffn_w13, v_ffn_w2, v_ab_w_in, v_ab_w_out, v_c_w_in, v_c_w_out, v_ada_w]
    big_g = [g.reshape(w.shape) for g, w in zip(list(g_big) + [g_ada_w], big_w)]
    big_out = []
    for k, (w, g, m, v) in enumerate(zip(big_w, big_g, big_m, big_v)):
        two_d = lambda a: a.reshape(-1, a.shape[-1])
        res = _adamw(two_d(w), two_d(g), two_d(m), two_d(v), f"adamw_big{k}")
        big_out.append([r.reshape(w.shape) for r in res])

    names = ["ada_w", "ada_b", "norm_pre", "norm_post", "ffn_w13", "ffn_w2", "ab_w_in", "a_conv_w", "a_conv_b",
             "a_gate_w", "a_gate_b", "a_lam", "b_conv_w", "b_conv_b", "b_norm_g", "b_norm_b", "ab_w_out", "c_w_in",
             "c_b_in", "c_norm_g", "c_norm_b", "c_w_s", "c_b_s", "c_w_out"]
    big_names = ["ffn_w13", "ffn_w2", "ab_w_in", "ab_w_out", "c_w_in", "c_w_out", "ada_w"]
    small_names = ["ada_b", "norm_pre", "norm_post", "a_conv_w", "a_conv_b", "a_gate_w", "a_gate_b", "a_lam",
                   "b_conv_w", "b_conv_b", "b_norm_g", "b_norm_b", "c_b_in", "c_norm_g", "c_norm_b", "c_w_s", "c_b_s"]
    table = {}
    for k, n in enumerate(big_names):
        table[n] = (big_g[k], *big_out[k])
    for k, n in enumerate(small_names):
        table[n] = (g_small[k], delta_small[k], newm_small[k], newv_small[k])
    outs = [loss, grad_x]
    for field in range(4):
        outs += [table[n][field] for n in names]
    return tuple(outs)
```

```python
import functools
import math

import jax
import jax.numpy as jnp
from jax import lax
from jax.experimental import pallas as pl
from jax.experimental.pallas import tpu as pltpu

F32 = jnp.float32
MXU_DT = jnp.bfloat16
EPS = 1e-6
LRU_C = 8.0
N_SHARD = 4
N_DEV = 8
CHUNK = 128
N_HEAD = 8
ADAM_LR, ADAM_B1, ADAM_B2, ADAM_EPS, ADAM_WD, ADAM_STEP = 0.001, 0.9, 0.999, 1e-08, 0.01, 10
GELU_K0 = math.sqrt(2.0 / math.pi)
GELU_K1 = 0.044715
VMEM_LIMIT = 58 * 1024 * 1024
MESH = pl.DeviceIdType.MESH
ANY = pl.BlockSpec(memory_space=pl.ANY)


def _cp(sem=None, **kw):
    if sem is not None:
        kw["dimension_semantics"] = sem
    return pltpu.CompilerParams(vmem_limit_bytes=VMEM_LIMIT, **kw)


def _resident(a):
    return pl.BlockSpec(a.shape, lambda *_: (0,) * a.ndim, pipeline_mode=pl.Buffered(1))


def _whole(shape):
    return pl.BlockSpec(shape, lambda *_: (0,) * len(shape))


def _dot(a, b):
    return jnp.dot(a.astype(MXU_DT), b.astype(MXU_DT), preferred_element_type=F32)


def _dot_nt(a, b):
    return lax.dot_general(a.astype(MXU_DT), b.astype(MXU_DT), (((1,), (1,)), ((), ())), preferred_element_type=F32)


def _dot_tn(a, b):
    return lax.dot_general(a.astype(MXU_DT), b.astype(MXU_DT), (((0,), (0,)), ((), ())), preferred_element_type=F32)


def _dot_hi(a, b):
    return jnp.dot(a, b, precision=lax.Precision.HIGHEST, preferred_element_type=F32)


def _sig(x):
    return 1.0 / (1.0 + jnp.exp(-x))


def _logsig(x):
    return jnp.minimum(x, 0.0) - jnp.log(1.0 + jnp.exp(-jnp.abs(x)))


def _gelu(x):
    x2 = x * x
    t = jnp.tanh(GELU_K0 * (x + GELU_K1 * x * x2))
    val = 0.5 * x * (1.0 + t)
    der = 0.5 * (1.0 + t) + 0.5 * x * (1.0 - t * t) * (GELU_K0 * (1.0 + 3.0 * GELU_K1 * x2))
    return val, der


def _neg_expm1(x):
    small = -(x * (1.0 + x * (0.5 + x * (1.0 / 6.0 + x * (1.0 / 24.0)))))
    return jnp.where(x > -0.05, small, 1.0 - jnp.exp(x))


def _colsum(v):
    return jnp.sum(v, axis=0, keepdims=True)


def _rowmean(v):
    return jnp.mean(v, axis=-1, keepdims=True)


def _copy_out(pairs, sem):
    for src, dst in pairs:
        cp = pltpu.make_async_copy(src, dst, sem)
        cp.start()
        cp.wait()


class _Carry:
    def __init__(self, srcs, out_shapes, aliases, n, plan):
        self.srcs, self.out_shapes, self.aliases, self.n, self.plan = list(srcs), list(out_shapes), aliases, n, plan


def _carry_args(carry, n_in, n_out):
    if carry is None:
        return [], [], [], [], {}
    sems = [pltpu.SemaphoreType.DMA((carry.n,)), pltpu.SemaphoreType.DMA((carry.n,))]
    aliases = {n_in + i: n_out + o for i, o in carry.aliases.items()}
    return [ANY] * len(carry.srcs), [ANY] * len(carry.out_shapes), carry.out_shapes, sems, aliases


def _split_refs(refs, n_in, n_out, n_scratch, carry):
    nci, nco = (len(carry.srcs), len(carry.out_shapes)) if carry is not None else (0, 0)
    cuts = [n_in, nci, n_out, nco, n_scratch]
    parts, i = [], 0
    for n in cuts:
        parts.append(refs[i:i + n])
        i += n
    ins, cins, outs, couts, scr = parts
    return ins, outs, scr, (cins, couts, refs[i:])


def _carry_copies(carry, carry_refs):
    if carry is None:
        return []
    cins, couts, (send_sems, recv_sems) = carry_refs

    def remote(k, src, dst, to):
        return pltpu.make_async_remote_copy(src_ref=src, dst_ref=dst, send_sem=send_sems.at[k],
                                            recv_sem=recv_sems.at[k], device_id=to, device_id_type=MESH)

    return carry.plan(cins, couts, remote)


def _carry_run(carry, carry_refs, first, last):
    if carry is None:
        return

    @pl.when(first)
    def _():
        for cp in _carry_copies(carry, carry_refs):
            cp.start()

    @pl.when(last)
    def _():
        for cp in _carry_copies(carry, carry_refs):
            cp.wait()


def _shift_down(v, k):
    return v if k == 0 else pltpu.roll(v, k, 0)


def _shift_up(v, k):
    return v if k == 0 else pltpu.roll(v, v.shape[0] - k, 0)


def _shell_pre(xv, vp_ref):
    r = lax.rsqrt(_rowmean(xv * xv) + EPS)
    return xv * r * (vp_ref[0:1, :] * (1.0 + vp_ref[2:3, :])) + vp_ref[1:2, :]


def _shell_post(xv, fv, vp_ref, res_w):
    r = lax.rsqrt(_rowmean(fv * fv) + EPS)
    return xv + fv * r * (res_w * (1.0 + vp_ref[3:4, :]) * vp_ref[4:5, :])


def _shell_post_bwd(fv, dov, vp_ref, res_w, s_ref):
    r = lax.rsqrt(_rowmean(fv * fv) + EPS)
    fn = fv * r
    pg = vp_ref[4:5, :]
    dy = (res_w * (1.0 + vp_ref[3:4, :])) * dov
    if s_ref is not None:
        s_ref[0:1, :] += _colsum(res_w * fn * pg * dov)
        s_ref[1:2, :] += _colsum(fn * dy)
    q = dy * pg
    return r * (q - fn * _rowmean(fn * q))


def _shell_pre_bwd(xv, dh, dov, vp_ref, s_ref):
    r = lax.rsqrt(_rowmean(xv * xv) + EPS)
    xn = xv * r
    pg = vp_ref[0:1, :]
    sc1 = 1.0 + vp_ref[2:3, :]
    s_ref[2:3, :] += _colsum(dh)
    s_ref[3:4, :] += _colsum(xn * pg * dh)
    s_ref[4:5, :] += _colsum(xn * dh * sc1)
    q = dh * (sc1 * pg)
    return dov + r * (q - xn * _rowmean(xn * q))


MXU_COLS = 256


def _col_chunks(n, width=2 * MXU_COLS):
    return [(c0, min(c0 + width, n)) for c0 in range(0, n, width)]


def _ffn_fwd(x, vp, w13g, w2v, res_w, name, carry=None, tgt=None, tm=512):
    S, D = x.shape
    Fh = w13g.shape[-1]
    T = S // tm
    head = tgt is not None
    n_in, n_out = (5, 6) if head else (4, 5)

    def body(*refs):
        ins, outs, _, carry_refs = _split_refs(refs, n_in, n_out, 0, carry)
        x_ref, vp_ref, w13_ref, w2_ref = ins[:4]
        xo_ref, f_ref, h_ref, g_ref, u_ref = outs[:5]
        _carry_run(carry, carry_refs, pl.program_id(0) == 0, pl.program_id(0) == T - 1)
        xv = x_ref[...]
        hb = _shell_pre(xv, vp_ref).astype(MXU_DT)
        h_ref[...] = hb
        acc = None
        for j in range(2):
            gg = _dot(hb, w13_ref[j])
            uu = _dot(hb, w13_ref[2 + j])
            g_ref[:, j * Fh:(j + 1) * Fh] = gg.astype(g_ref.dtype)
            u_ref[:, j * Fh:(j + 1) * Fh] = uu.astype(u_ref.dtype)
            part = _dot(gg * _sig(gg) * uu, w2_ref[j])
            acc = part if acc is None else acc + part
        f_ref[...] = acc
        xo = _shell_post(xv, acc, vp_ref, res_w)
        if head:
            l_ref = outs[5]

            @pl.when(pl.program_id(0) == 0)
            def _():
                l_ref[...] = jnp.zeros_like(l_ref)

            e = xo - ins[4][...]
            xo_ref[...] = e * (1.0 / D)
            l_ref[0:1, :] += _colsum(e * e) * (0.5 / D)
        else:
            xo_ref[...] = xo

    tile = lambda w: pl.BlockSpec((tm, w), lambda t: (t, 0))
    sd = jax.ShapeDtypeStruct
    c_in, c_out, c_shapes, c_sems, c_alias = _carry_args(carry, n_in, n_out)
    return pl.pallas_call(
        body, name=name, grid=(T,),
        in_specs=[tile(D), _whole((8, D)),
                  pl.BlockSpec((None, 4, D, Fh), lambda t: (0, 0, 0, 0), pipeline_mode=pl.Buffered(1)),
                  pl.BlockSpec((None, 2, Fh, D), lambda t: (0, 0, 0, 0), pipeline_mode=pl.Buffered(1))]
        + ([tile(D)] if head else []) + c_in,
        out_specs=[tile(D), tile(D), tile(D), tile(2 * Fh), tile(2 * Fh)] + ([_whole((8, D))] if head else []) + c_out,
        out_shape=[sd((S, D), F32), sd((S, D), F32), sd((S, D), MXU_DT), sd((S, 2 * Fh), MXU_DT),
                   sd((S, 2 * Fh), MXU_DT)] + ([sd((8, D), F32)] if head else []) + c_shapes,
        scratch_shapes=c_sems, input_output_aliases=c_alias,
        compiler_params=_cp(("arbitrary",), has_side_effects=carry is not None),
    )(x, vp, w13g, w2v, *([tgt] if head else []), *(carry.srcs if carry is not None else []))


def _ffn_bwd_half(j, x, f, dout, vp, h, gpre, upre, w13g, w2v, dw13, dw2v, res_w, name, df=None, dh0=None, carry=None,
                  tm=256):
    S, D = h.shape
    Fh = w13g.shape[-1]
    T = S // tm
    first = j == 0
    n_in, n_out = (11, 5) if first else (13, 4)

    def body(*refs):
        ins, outs, (a1, a3, a2, sem), carry_refs = _split_refs(refs, n_in, n_out, 4, carry)
        if first:
            f_ref, do_ref, vp_ref, h_ref, g_ref, u_ref, w1_ref, w3_ref, w2_ref, _, _ = ins
            df_ref, dh_ref, s_ref, dw13_ref, dw2_ref = outs
        else:
            x_ref, do_ref, vp_ref, h_ref, g_ref, u_ref, w1_ref, w3_ref, w2_ref, dfi_ref, dh0_ref, _, _ = ins
            dx_ref, s_ref, dw13_ref, dw2_ref = outs
        t = pl.program_id(0)
        _carry_run(carry, carry_refs, t == 0, t == T - 1)

        @pl.when(t == 0)
        def _():
            for ref in (a1, a3, a2, s_ref):
                ref[...] = jnp.zeros_like(ref)

        hv = h_ref[...]
        if first:
            dfv = _shell_post_bwd(f_ref[...], do_ref[...], vp_ref, res_w, s_ref).astype(MXU_DT)
            df_ref[...] = dfv
        else:
            dfv = dfi_ref[...]
        dh = None
        for c0, c1 in _col_chunks(Fh):
            gg = g_ref[:, c0:c1].astype(F32)
            uu = u_ref[:, c0:c1].astype(F32)
            sg = _sig(gg)
            si = gg * sg
            da = _dot_nt(dfv, w2_ref[c0:c1, :])
            a2[c0:c1, :] += _dot_tn(si * uu, dfv)
            dg = da * uu * (sg * (1.0 + gg * (1.0 - sg)))
            du = da * si
            a1[:, c0:c1] += _dot_tn(hv, dg)
            a3[:, c0:c1] += _dot_tn(hv, du)
            part = _dot_nt(dg, w1_ref[:, c0:c1]) + _dot_nt(du, w3_ref[:, c0:c1])
            dh = part if dh is None else dh + part
        if first:
            dh_ref[...] = dh
        else:
            dx_ref[...] = _shell_pre_bwd(x_ref[...], dh0_ref[...] + dh, do_ref[...], vp_ref, s_ref)

        @pl.when(t == T - 1)
        def _():
            _copy_out(((a1, dw13_ref.at[0, j]), (a3, dw13_ref.at[0, 2 + j]), (a2, dw2_ref.at[0, j])), sem)

    tile = lambda w: pl.BlockSpec((tm, w), lambda t: (t, 0))
    half = pl.BlockSpec((tm, Fh), lambda t: (t, j))
    weights = [pl.BlockSpec((None, None, D, Fh), lambda t: (0, j, 0, 0), pipeline_mode=pl.Buffered(1)),
               pl.BlockSpec((None, None, D, Fh), lambda t: (0, 2 + j, 0, 0), pipeline_mode=pl.Buffered(1)),
               pl.BlockSpec((None, None, Fh, D), lambda t: (0, j, 0, 0), pipeline_mode=pl.Buffered(1))]
    sd = jax.ShapeDtypeStruct
    grads = [sd(dw13.shape, F32), sd(dw2v.shape, F32)]
    scratch = [pltpu.VMEM((D, Fh), F32), pltpu.VMEM((D, Fh), F32), pltpu.VMEM((Fh, D), F32), pltpu.SemaphoreType.DMA]
    c_in, c_out, c_shapes, c_sems, c_alias = _carry_args(carry, n_in, n_out)
    params = _cp(("arbitrary",), has_side_effects=carry is not None)
    extra = carry.srcs if carry is not None else []
    if first:
        return pl.pallas_call(
            body, name=name, grid=(T,),
            in_specs=[tile(D), tile(D), _whole((8, D)), tile(D), half, half] + weights + [ANY, ANY] + c_in,
            out_specs=[tile(D), tile(D), _whole((8, D)), ANY, ANY] + c_out,
            out_shape=[sd((S, D), MXU_DT), sd((S, D), F32), sd((8, D), F32)] + grads + c_shapes,
            scratch_shapes=scratch + c_sems, input_output_aliases={9: 3, 10: 4, **c_alias}, compiler_params=params,
        )(f, dout, vp, h, gpre, upre, w13g, w13g, w2v, dw13, dw2v, *extra)
    return pl.pallas_call(
        body, name=name, grid=(T,),
        in_specs=[tile(D), tile(D), _whole((8, D)), tile(D), half, half] + weights + [tile(D), tile(D), ANY, ANY] + c_in,
        out_specs=[tile(D), _whole((8, D)), ANY, ANY] + c_out,
        out_shape=[sd((S, D), F32), sd((8, D), F32)] + grads + c_shapes,
        scratch_shapes=scratch + c_sems, input_output_aliases={11: 2, 12: 3, **c_alias}, compiler_params=params,
    )(x, dout, vp, h, gpre, upre, w13g, w13g, w2v, df, dh0, dw13, dw2v, *extra)


def _scan_fwd(a, u, rows):
    n = a.shape[0]
    d = 1
    while d < n:
        m = rows >= d
        u = u + a * jnp.where(m, _shift_down(u, d), 0.0)
        a = a * jnp.where(m, _shift_down(a, d), 1.0)
        d *= 2
    return a, u


def _scan_bwd(a, u, rows):
    n = a.shape[0]
    d = 1
    while d < n:
        m = rows < n - d
        u = u + a * jnp.where(m, _shift_up(u, d), 0.0)
        a = a * jnp.where(m, _shift_up(a, d), 1.0)
        d *= 2
    return a, u


def _causal_conv(ext, w_ref, K, halo, tm):
    acc = None
    for k in range(K):
        term = w_ref[k:k + 1, :] * _shift_down(ext, K - 1 - k)[halo:, :]
        acc = term if acc is None else acc + term
    return acc


def _anticausal_conv(ext, w_ref, K, tm):
    acc = None
    for k in range(K):
        term = w_ref[k:k + 1, :] * _shift_up(ext, K - 1 - k)[:tm, :]
        acc = term if acc is None else acc + term
    return acc


def _dot_split(a, b):
    hi = a.astype(MXU_DT)
    lo = (a - hi.astype(F32)).astype(MXU_DT)
    return jnp.dot(hi, b, preferred_element_type=F32) + jnp.dot(lo, b, preferred_element_type=F32)


def _group_mean(v, p_ref):
    return _dot_nt_exact(_dot_split(v, p_ref[0]), p_ref[1])


def _dot_nt_exact(a, bt):
    hi = a.astype(MXU_DT)
    lo = (a - hi.astype(F32)).astype(MXU_DT)
    dims = (((1,), (1,)), ((), ()))
    return (lax.dot_general(hi, bt, dims, preferred_element_type=F32)
            + lax.dot_general(lo, bt, dims, preferred_element_type=F32))


def _group_norm(vc, p_ref, g, b):
    mu = _group_mean(vc, p_ref)
    dv = vc - mu
    rstd = lax.rsqrt(_group_mean(dv * dv, p_ref) + EPS)
    vhat = dv * rstd
    return vhat, rstd, vhat * g + b


def _lru_gates(axc, wg_ref, bg_ref, lam, W):
    gp = _dot(axc, wg_ref[...]) + bg_ref[0:1, :]
    r = _sig(gp[:, :W])
    i = _sig(gp[:, W:])
    ls = _logsig(lam)
    L = (LRU_C * ls) * r
    a = jnp.exp(L)
    mult = jnp.sqrt(_neg_expm1(2.0 * L))
    return r, i, ls, a, mult


def _mix_ab_fwd(x, vp, win4, cwa, wg, bg, cwb, v512, pavg, wout, res_w, name, carry=None, tm=256):
    S, D = x.shape
    W = win4.shape[-1]
    KA, KB, HA, HB = 4, 31, 8, 32
    rider = carry

    def body(*refs):
        ins, (xo_ref, f_ref, h_ref, sav_ref), (ahalo, bhalo, carry), carry_refs = _split_refs(refs, 10, 4, 3, rider)
        x_ref, vp_ref, win_ref, cwa_ref, wg_ref, bg_ref, cwb_ref, v_ref, p_ref, wo_ref = ins
        _carry_run(rider, carry_refs, pl.program_id(0) == 0, pl.program_id(0) == S // tm - 1)

        @pl.when(pl.program_id(0) == 0)
        def _():
            ahalo[...] = jnp.zeros_like(ahalo)
            bhalo[...] = jnp.zeros_like(bhalo)
            carry[...] = jnp.zeros_like(carry)

        xv = x_ref[...]
        hv = _shell_pre(xv, vp_ref).astype(MXU_DT)
        h_ref[...] = hv
        a_gate = _dot(hv, win_ref[0])
        axp = _dot(hv, win_ref[1])
        b_val = _dot(hv, win_ref[2])
        b_gate = _dot(hv, win_ref[3])
        rows = lax.broadcasted_iota(jnp.int32, (tm, W), 0)
        axc = _causal_conv(jnp.concatenate([ahalo[...], axp], axis=0), cwa_ref, KA, HA, tm) + v_ref[0:1, :]
        ahalo[...] = axp[tm - HA:, :]
        r, i, ls, a, mult = _lru_gates(axc, wg_ref, bg_ref, v_ref[1:2, :], W)
        acum, hloc = _scan_fwd(a, mult * i * axc, rows)
        hs = hloc + acum * carry[7:8, :]
        carry[...] = hs[tm - 8:, :]
        ya = hs * _gelu(a_gate)[0]
        bv = b_val * _sig(b_gate)
        vc = _causal_conv(jnp.concatenate([bhalo[...], bv], axis=0), cwb_ref, KB, HB, tm) + v_ref[2:3, :]
        bhalo[...] = bv[tm - HB:, :]
        _, _, vn = _group_norm(vc, p_ref, v_ref[3:4, :], v_ref[4:5, :])
        yb = vn * _sig(vn)
        fv = _dot(ya, wo_ref[0:W, :]) + _dot(yb, wo_ref[W:, :])
        f_ref[...] = fv
        xo_ref[...] = _shell_post(xv, fv, vp_ref, res_w)
        for n, val in enumerate((hs, axp, axc, bv, vc)):
            sav_ref[:, n * W:(n + 1) * W] = val

    tile = lambda w: pl.BlockSpec((tm, w), lambda t: (t, 0))
    sd = jax.ShapeDtypeStruct
    c_in, c_out, c_shapes, c_sems, c_alias = _carry_args(rider, 10, 4)
    return pl.pallas_call(
        body, name=name, grid=(S // tm,),
        in_specs=[tile(D), _whole((8, D))] + [_resident(a) for a in (win4, cwa, wg, bg, cwb, v512, pavg, wout)] + c_in,
        out_specs=[tile(D), tile(D), tile(D), tile(5 * W)] + c_out,
        out_shape=[sd((S, D), F32), sd((S, D), F32), sd((S, D), MXU_DT), sd((S, 5 * W), F32)] + c_shapes,
        scratch_shapes=[pltpu.VMEM((HA, W), F32), pltpu.VMEM((HB, W), F32), pltpu.VMEM((8, W), F32)] + c_sems,
        input_output_aliases=c_alias, compiler_params=_cp(("arbitrary",), has_side_effects=rider is not None),
    )(x, vp, win4, cwa, wg, bg, cwb, v512, pavg, wout, *(rider.srcs if rider is not None else []))


def _mix_ab_bwd(x, f, dout, vp, h, sav, win4, cwa, wg, bg, cwb, v512, pavg, wout, res_w, name, carry=None, tm=256):
    S, D = x.shape
    W = win4.shape[-1]
    T = S // tm
    KA, KB, HA, HB = 4, 31, 8, 32

    def body(*refs):
        ins, outs, scr, carry_refs = _split_refs(refs, 15, 9, 8, carry)
        (x_ref, f_ref, do_ref, vp_ref, h_ref, sav_ref, prev_ref,
         win_ref, cwa_ref, wg_ref, bg_ref, cwb_ref, v_ref, p_ref, wo_ref) = ins
        hs_ref, axp_ref, axc_ref, bv_ref, vc_ref = (sav_ref.at[:, n * W:(n + 1) * W] for n in range(5))
        hsp_ref = prev_ref.at[HB - HA:, 0:W]
        axpp_ref = prev_ref.at[HB - HA:, W:2 * W]
        bvp_ref = prev_ref.at[:, 3 * W:4 * W]
        dx_ref, s_ref, dwin_out, dwo_out, dwg_out, dcwa_ref, dcwb_ref, dbg_ref, dv_ref = outs
        danext, dvnext, gfirst, afirst, dwin_ref, dwo_ref, dwg_ref, sem = scr
        t = pl.program_id(0)
        _carry_run(carry, carry_refs, t == 0, t == T - 1)

        @pl.when(t == 0)
        def _():
            for ref in (s_ref, dwin_ref, dwo_ref, dwg_ref, dcwa_ref, dcwb_ref, dbg_ref, dv_ref, danext, dvnext,
                        gfirst, afirst):
                ref[...] = jnp.zeros_like(ref)

        notfirst = jnp.where(t < T - 1, 1.0, 0.0).astype(F32)
        hv = h_ref[...]
        dov = do_ref[...]
        dfv = _shell_post_bwd(f_ref[...], dov, vp_ref, res_w, s_ref).astype(MXU_DT)
        a_gate = _dot(hv, win_ref[0])
        b_val = _dot(hv, win_ref[2])
        b_gate = _dot(hv, win_ref[3])
        rows = lax.broadcasted_iota(jnp.int32, (tm, W), 0)
        ge, dge = _gelu(a_gate)
        hsv = hs_ref[...]
        ya = hsv * ge
        vhat, rstd, vn = _group_norm(vc_ref[...], p_ref, v_ref[3:4, :], v_ref[4:5, :])
        sgn = _sig(vn)
        yb = vn * sgn
        dma = _dot_nt(dfv, wo_ref[0:W, :])
        dmb = _dot_nt(dfv, wo_ref[W:, :])
        dwo_ref[0:W, :] += _dot_tn(ya, dfv)
        dwo_ref[W:, :] += _dot_tn(yb, dfv)
        dhs = dma * ge
        d_a_gate = dma * hsv * dge
        axcv = axc_ref[...]
        lam = v_ref[1:2, :]
        r, i, ls, a, mult = _lru_gates(axcv, wg_ref, bg_ref, lam, W)
        ash = jnp.where(rows == tm - 1, afirst[0:1, :], _shift_up(a, 1))
        asuf, gloc = _scan_bwd(ash, dhs, rows)
        gsc = gloc + asuf * gfirst[0:1, :]
        afirst[...] = a[0:8, :]
        gfirst[...] = gsc[0:8, :]
        hprev = jnp.where(rows == 0, hsp_ref[HA - 1:HA, :] * notfirst, _shift_down(hsv, 1))
        da = gsc * hprev
        dL = da * a - gsc * (i * axcv) * (a * a) / mult
        dix = gsc * mult
        daxc = dix * i
        dr = dL * (LRU_C * ls)
        dv_ref[1:2, :] += _colsum(dL * r) * (LRU_C * _sig(-lam))
        dgate = jnp.concatenate([dr * r * (1.0 - r), (dix * axcv) * i * (1.0 - i)], axis=1)
        dbg_ref[0:1, :] += _colsum(dgate)
        dwg_ref[...] += _dot_tn(axcv, dgate)
        daxc = daxc + _dot_nt(dgate, wg_ref[...])
        daxp = _anticausal_conv(jnp.concatenate([daxc, danext[...]], axis=0), cwa_ref, KA, tm)
        ext = jnp.concatenate([axpp_ref[...] * notfirst, axp_ref[...]], axis=0)
        for k in range(KA):
            dcwa_ref[k:k + 1, :] += _colsum(daxc * _shift_down(ext, KA - 1 - k)[HA:, :])
        dv_ref[0:1, :] += _colsum(daxc)
        danext[...] = daxc[0:HA, :]
        dvn = dmb * (sgn * (1.0 + vn * (1.0 - sgn)))
        dv_ref[4:5, :] += _colsum(dvn)
        dv_ref[3:4, :] += _colsum(dvn * vhat)
        dvh = dvn * v_ref[3:4, :]
        dvc = rstd * (dvh - _group_mean(dvh, p_ref) - vhat * _group_mean(dvh * vhat, p_ref))
        dbv = _anticausal_conv(jnp.concatenate([dvc, dvnext[...]], axis=0), cwb_ref, KB, tm)
        ext = jnp.concatenate([bvp_ref[...] * notfirst, bv_ref[...]], axis=0)
        for k in range(KB):
            dcwb_ref[k:k + 1, :] += _colsum(dvc * _shift_down(ext, KB - 1 - k)[HB:, :])
        dv_ref[2:3, :] += _colsum(dvc)
        dvnext[...] = dvc[0:HB, :]
        sb = _sig(b_gate)
        dzs = (d_a_gate, daxp, dbv * sb, dbv * b_val * sb * (1.0 - sb))
        dh = None
        for s in range(4):
            part = _dot_nt(dzs[s], win_ref[s])
            dh = part if dh is None else dh + part
            dwin_ref[s] += _dot_tn(hv, dzs[s])
        dx_ref[...] = _shell_pre_bwd(x_ref[...], dh, dov, vp_ref, s_ref)

        @pl.when(t == T - 1)
        def _():
            _copy_out(((dwin_ref, dwin_out), (dwo_ref, dwo_out), (dwg_ref, dwg_out)), sem)

    tile = lambda w: pl.BlockSpec((tm, w), lambda t: (T - 1 - t, 0))
    prev = pl.BlockSpec((HB, 5 * W), lambda t: (jnp.maximum((T - 1 - t) * (tm // HB) - 1, 0), 0))
    out_shapes = [(S, D), (8, D), win4.shape, wout.shape, wg.shape, (8, W), (32, W), (8, 2 * W), (8, W)]
    c_in, c_out, c_shapes, c_sems, c_alias = _carry_args(carry, 15, 9)
    return pl.pallas_call(
        body, name=name, grid=(T,),
        in_specs=[tile(D), tile(D), tile(D), _whole((8, D)), tile(D), tile(5 * W), prev]
        + [_resident(a) for a in (win4, cwa, wg, bg, cwb, v512, pavg, wout)] + c_in,
        out_specs=[tile(D), _whole((8, D)), ANY, ANY, ANY] + [_whole(s) for s in out_shapes[5:]] + c_out,
        out_shape=[jax.ShapeDtypeStruct(s, F32) for s in out_shapes] + c_shapes,
        scratch_shapes=[pltpu.VMEM((HA, W), F32), pltpu.VMEM((HB, W), F32), pltpu.VMEM((8, W), F32),
                        pltpu.VMEM((8, W), F32), pltpu.VMEM(win4.shape, F32), pltpu.VMEM(wout.shape, F32),
                        pltpu.VMEM(wg.shape, F32), pltpu.SemaphoreType.DMA] + c_sems,
        input_output_aliases=c_alias, compiler_params=_cp(("arbitrary",), has_side_effects=carry is not None),
    )(x, f, dout, vp, h, sav, sav, win4, cwa, wg, bg, cwb, v512, pavg, wout,
      *(carry.srcs if carry is not None else []))


def _mix_c_core(hv, win_ref, v2_ref, v1_ref, ws_ref, bsb_ref, tm, D):
    zp = jnp.concatenate([_dot(hv, win_ref[s]) for s in range(4)], axis=1) + v2_ref[0:1, :]
    z, dz = _gelu(zp)
    u, v = z[:, :D], z[:, D:]
    mu = _rowmean(v)
    dv = v - mu
    rstd = lax.rsqrt(_rowmean(dv * dv) + EPS)
    vhat = dv * rstd
    vn = vhat * v1_ref[0:1, :] + v1_ref[1:2, :]
    rows_out = []
    for cidx in range(tm // CHUNK):
        blk = vn[cidx * CHUNK:(cidx + 1) * CHUNK, :]
        heads = [_dot(ws_ref[hd], blk[:, hd * CHUNK:(hd + 1) * CHUNK]) for hd in range(N_HEAD)]
        rows_out.append(jnp.concatenate(heads, axis=1) + bsb_ref[...])
    mixed = jnp.concatenate(rows_out, axis=0)
    return dz, u, rstd, vhat, vn, mixed


def _mix_c_fwd(x, vp, win4, v2d, v1d, ws, bsb, wout, res_w, name, tm=512):
    S, D = x.shape

    def body(x_ref, vp_ref, win_ref, v2_ref, v1_ref, ws_ref, bsb_ref, wo_ref, xo_ref, f_ref, h_ref):
        xv = x_ref[...]
        hv = _shell_pre(xv, vp_ref).astype(MXU_DT)
        h_ref[...] = hv
        _, u, _, _, _, mixed = _mix_c_core(hv, win_ref, v2_ref, v1_ref, ws_ref, bsb_ref, tm, D)
        fv = _dot(u * mixed, wo_ref[...])
        f_ref[...] = fv
        xo_ref[...] = _shell_post(xv, fv, vp_ref, res_w)

    tile = pl.BlockSpec((tm, D), lambda t: (t, 0))
    sd = jax.ShapeDtypeStruct
    return pl.pallas_call(
        body, name=name, grid=(S // tm,),
        in_specs=[tile, _whole((8, D))] + [_resident(a) for a in (win4, v2d, v1d, ws, bsb, wout)],
        out_specs=[tile, tile, tile],
        out_shape=[sd((S, D), F32), sd((S, D), F32), sd((S, D), MXU_DT)], compiler_params=_cp(("arbitrary",)),
    )(x, vp, win4, v2d, v1d, ws, bsb, wout)


def _mix_c_bwd(x, f, dout, vp, h, win4, v2d, v1d, ws, bsb, wout, res_w, name, carry=None, tm=256):
    S, D = x.shape

    def body(*refs):
        ins, outs, (dwin_ref, dwo_ref, sem), carry_refs = _split_refs(refs, 11, 8, 3, carry)
        x_ref, f_ref, do_ref, vp_ref, h_ref, win_ref, v2_ref, v1_ref, ws_ref, bsb_ref, wo_ref = ins
        dx_ref, s_ref, dwin_out, dwo_out, dws_ref, dbsb_ref, dv2_ref, dv1_ref = outs
        _carry_run(carry, carry_refs, pl.program_id(0) == 0, pl.program_id(0) == S // tm - 1)

        @pl.when(pl.program_id(0) == 0)
        def _():
            for ref in (s_ref, dwin_ref, dwo_ref, dws_ref, dbsb_ref, dv2_ref, dv1_ref):
                ref[...] = jnp.zeros_like(ref)

        hv = h_ref[...]
        dov = do_ref[...]
        dfv = _shell_post_bwd(f_ref[...], dov, vp_ref, res_w, s_ref).astype(MXU_DT)
        dz, u, rstd, vhat, vn, mixed = _mix_c_core(hv, win_ref, v2_ref, v1_ref, ws_ref, bsb_ref, tm, D)
        dp = _dot_nt(dfv, wo_ref[...])
        dwo_ref[...] += _dot_tn(u * mixed, dfv)
        du = dp * mixed
        dmx = dp * u
        rows_out = []
        for cidx in range(tm // CHUNK):
            dblk = dmx[cidx * CHUNK:(cidx + 1) * CHUNK, :]
            vblk = vn[cidx * CHUNK:(cidx + 1) * CHUNK, :]
            dbsb_ref[...] += dblk
            heads = []
            for hd in range(N_HEAD):
                dsl = dblk[:, hd * CHUNK:(hd + 1) * CHUNK]
                heads.append(_dot_tn(ws_ref[hd], dsl))
                dws_ref[hd] += _dot_nt(dsl, vblk[:, hd * CHUNK:(hd + 1) * CHUNK])
            rows_out.append(jnp.concatenate(heads, axis=1))
        dvn = jnp.concatenate(rows_out, axis=0)
        dv1_ref[1:2, :] += _colsum(dvn)
        dv1_ref[0:1, :] += _colsum(dvn * vhat)
        dvh = dvn * v1_ref[0:1, :]
        dv = rstd * (dvh - _rowmean(dvh) - vhat * _rowmean(dvh * vhat))
        dzp = jnp.concatenate([du, dv], axis=1) * dz
        dv2_ref[0:1, :] += _colsum(dzp)
        W = win_ref.shape[-1]
        dh = None
        for s in range(4):
            dzs = dzp[:, s * W:(s + 1) * W]
            part = _dot_nt(dzs, win_ref[s])
            dh = part if dh is None else dh + part
            dwin_ref[s] += _dot_tn(hv, dzs)
        dx_ref[...] = _shell_pre_bwd(x_ref[...], dh, dov, vp_ref, s_ref)

        @pl.when(pl.program_id(0) == S // tm - 1)
        def _():
            _copy_out(((dwin_ref, dwin_out), (dwo_ref, dwo_out)), sem)

    tile = pl.BlockSpec((tm, D), lambda t: (t, 0))
    out_shapes = [(S, D), (8, D), win4.shape, wout.shape, ws.shape, bsb.shape, (8, 2 * D), (8, D)]
    c_in, c_out, c_shapes, c_sems, c_alias = _carry_args(carry, 11, 8)
    return pl.pallas_call(
        body, name=name, grid=(S // tm,),
        in_specs=[tile, tile, tile, _whole((8, D)), tile] + [_resident(a) for a in (win4, v2d, v1d, ws, bsb, wout)]
        + c_in,
        out_specs=[tile, _whole((8, D)), ANY, ANY] + [_whole(s) for s in out_shapes[4:]] + c_out,
        out_shape=[jax.ShapeDtypeStruct(s, F32) for s in out_shapes] + c_shapes,
        scratch_shapes=[pltpu.VMEM(win4.shape, F32), pltpu.VMEM(wout.shape, F32), pltpu.SemaphoreType.DMA] + c_sems,
        input_output_aliases=c_alias, compiler_params=_cp(("arbitrary",), has_side_effects=carry is not None),
    )(x, f, dout, vp, h, win4, v2d, v1d, ws, bsb, wout, *(carry.srcs if carry is not None else []))


def _ada_fwd(c_all, ada_w, ada_b_my, name):
    L, D, N = ada_w.shape
    tn = N // 3

    def body(c_ref, w_ref, b_ref, o_ref):
        cv = c_ref[...]
        o_ref[...] = _dot_hi(cv * _sig(cv), w_ref[...]) + b_ref[...]

    return pl.pallas_call(
        body, name=name, grid=(L, 3),
        in_specs=[pl.BlockSpec((8, D), lambda l, n: (0, 0)), pl.BlockSpec((None, D, tn), lambda l, n: (l, 0, n)),
                  pl.BlockSpec((None, 1, tn), lambda l, n: (l, 0, n))],
        out_specs=pl.BlockSpec((None, 8, tn), lambda l, n: (l, 0, n)),
        out_shape=jax.ShapeDtypeStruct((L, 8, N), F32), compiler_params=_cp(("arbitrary", "arbitrary")),
    )(c_all, ada_w, ada_b_my)


def _ada_bwd(c_all, dmod_my, name):
    L, _, N = dmod_my.shape
    D = c_all.shape[1]
    tn = N // 3

    def body(c_ref, d_ref, o_ref):
        cv = c_ref[...]
        o_ref[...] = lax.dot_general(cv * _sig(cv), d_ref[...], (((0,), (0,)), ((), ())),
                                     precision=lax.Precision.HIGHEST, preferred_element_type=F32)

    return pl.pallas_call(
        body, name=name, grid=(L, 3),
        in_specs=[pl.BlockSpec((8, D), lambda l, n: (0, 0)), pl.BlockSpec((None, 8, tn), lambda l, n: (l, 0, n))],
        out_specs=pl.BlockSpec((None, D, tn), lambda l, n: (l, 0, n)),
        out_shape=jax.ShapeDtypeStruct((L, D, N), F32), compiler_params=_cp(("arbitrary", "arbitrary")),
    )(c_all, dmod_my)


def _row_tile(rows, cols, budget=1 << 20):
    best = 8
    for rt in range(8, rows + 1, 8):
        if rows % rt == 0 and rt * cols * 4 <= budget:
            best = rt
    return best


def _my_chip():
    return 2 * lax.axis_index("x") + lax.axis_index("y")


def _cast_place(w, g, name):
    _, R, C = w.shape
    rt = _row_tile(R, C)

    def body(w_ref, o_ref):
        o_ref[...] = w_ref[...].astype(o_ref.dtype)

    return pl.pallas_call(
        body, name=name, grid=(R // rt,),
        in_specs=[pl.BlockSpec((None, rt, C), lambda r: (g, r, 0))],
        out_specs=pl.BlockSpec((None, None, rt, C), lambda r: (0, _my_chip(), r, 0)),
        out_shape=jax.ShapeDtypeStruct((1, N_SHARD, R, C), MXU_DT), compiler_params=_cp(("arbitrary",)),
    )(w)


def _add_half(gk, la, name, out_dtype=F32):
    Gk, _, R, C = gk.shape
    Rh = R // 2
    n = Gk * N_SHARD
    gv = gk.reshape(n, 2, Rh, C)
    lv = la.reshape(n, Rh, C)
    rt = _row_tile(Rh, C)

    def body(g_ref, l_ref, o_ref):
        o_ref[...] = (g_ref[...] + l_ref[...]).astype(o_ref.dtype)

    out = pl.pallas_call(
        body, name=name, grid=(n, Rh // rt),
        in_specs=[pl.BlockSpec((None, None, rt, C), lambda i, r: (i, lax.axis_index("c"), r, 0)),
                  pl.BlockSpec((None, rt, C), lambda i, r: (i, r, 0))],
        out_specs=pl.BlockSpec((None, rt, C), lambda i, r: (i, r, 0)),
        out_shape=jax.ShapeDtypeStruct((n, Rh, C), out_dtype), compiler_params=_cp(("arbitrary", "arbitrary")),
    )(gv, lv)
    return out.reshape(Gk, N_SHARD, Rh, C)


def _sum_chips(part, landed, joined, g, name):
    _, _, Rh, C = part.shape
    rt = _row_tile(Rh, C)
    nb = Rh // rt

    def body(p_ref, l_ref, j_ref, o_ref):
        up = lambda v: v.astype(F32)
        o_ref[...] = ((up(p_ref[...]) + up(l_ref[0])) + up(l_ref[1])) + up(l_ref[2])

    return pl.pallas_call(
        body, name=name, grid=(nb,),
        in_specs=[pl.BlockSpec((None, None, rt, C), lambda r: (0, _my_chip(), r, 0)),
                  pl.BlockSpec((None, 3, rt, C), lambda r: (0, 0, r, 0)), ANY],
        out_specs=pl.BlockSpec((None, rt, C), lambda r: (g, lax.axis_index("c") * nb + r, 0)),
        out_shape=jax.ShapeDtypeStruct(joined.shape, F32), input_output_aliases={2: 0},
        compiler_params=_cp(("arbitrary",)),
    )(part, landed, joined)


def _sum_slots(lb, name):
    Gk, n, Rh, C = lb.shape
    rt = _row_tile(Rh, C)

    def body(l_ref, o_ref):
        acc = l_ref[0]
        for s in range(1, n):
            acc = acc + l_ref[s]
        o_ref[...] = acc

    return pl.pallas_call(
        body, name=name, grid=(Gk, Rh // rt),
        in_specs=[pl.BlockSpec((None, n, rt, C), lambda g, r: (g, 0, r, 0))],
        out_specs=pl.BlockSpec((None, rt, C), lambda g, r: (g, r, 0)),
        out_shape=jax.ShapeDtypeStruct((Gk, Rh, C), F32), compiler_params=_cp(("arbitrary", "arbitrary")),
    )(lb)


def _adamw(w, g, m, v, name):
    rows, cols = w.shape
    rt = _row_tile(rows, cols) if rows % 8 == 0 else rows
    c1 = 1.0 - ADAM_B1 ** ADAM_STEP
    c2 = 1.0 - ADAM_B2 ** ADAM_STEP

    def body(w_ref, g_ref, m_ref, v_ref, d_ref, mo_ref, vo_ref):
        gv = g_ref[...]
        mn = ADAM_B1 * m_ref[...] + (1.0 - ADAM_B1) * gv
        vn = ADAM_B2 * v_ref[...] + (1.0 - ADAM_B2) * (gv * gv)
        d_ref[...] = -ADAM_LR * ((mn / c1) / (jnp.sqrt(vn / c2) + ADAM_EPS) + ADAM_WD * w_ref[...])
        mo_ref[...] = mn
        vo_ref[...] = vn

    spec = pl.BlockSpec((rt, cols), lambda r: (r, 0))
    sds = jax.ShapeDtypeStruct((rows, cols), F32)
    return pl.pallas_call(
        body, name=name, grid=(rows // rt,), in_specs=[spec] * 4, out_specs=[spec] * 3,
        out_shape=[sds] * 3, compiler_params=_cp(("arbitrary",)),
    )(w, g, m, v)


def _coords():
    return lax.axis_index("x"), lax.axis_index("y"), lax.axis_index("c")


def _all_gather8(blk, name, carry=None):
    m_per, n = blk.shape

    def body(*refs):
        (x_ref,), (out_ref,), (send_sems, recv_sems, local_sem), carry_refs = _split_refs(refs, 1, 1, 3, carry)
        x, y, c = _coords()
        me, sibling = (x, y, c), (x, y, 1 - c)
        chips = [(1 - x, y), (x, 1 - y), (1 - x, 1 - y)]

        def rows(px, py, pc):
            return out_ref.at[pl.ds((4 * px + 2 * py + pc) * m_per, m_per), :]

        def copy(k, block, to, src=None):
            return pltpu.make_async_remote_copy(
                src_ref=rows(*block) if src is None else src, dst_ref=rows(*block),
                send_sem=send_sems.at[k], recv_sem=recv_sems.at[k], device_id=to, device_id_type=MESH)

        mine = pltpu.make_async_copy(x_ref, rows(*me), local_sem)
        mine.start()
        first = [copy(0, me, sibling, src=x_ref)]
        first += [copy(1 + j, me, (*chip, c), src=x_ref) for j, chip in enumerate(chips)]
        for cp in first:
            cp.start()
        riding = _carry_copies(carry, carry_refs)
        for cp in riding:
            cp.start()
        passed = [copy(4 + j, (*chip, c), sibling) for j, chip in enumerate(chips)]
        for j, chip in enumerate(chips):
            copy(1 + j, (*chip, c), me).wait_recv()
            passed[j].start()
        copy(0, sibling, me).wait_recv()
        for j, chip in enumerate(chips):
            copy(4 + j, (*chip, 1 - c), me).wait_recv()
        for cp in first + passed:
            cp.wait_send()
        mine.wait()
        for cp in riding:
            cp.wait()

    c_in, c_out, c_shapes, c_sems, c_alias = _carry_args(carry, 1, 1)
    res = pl.pallas_call(
        body, name=name, out_shape=[jax.ShapeDtypeStruct((N_DEV * m_per, n), blk.dtype)] + c_shapes,
        in_specs=[pl.BlockSpec(memory_space=pltpu.VMEM)] + c_in,
        out_specs=[pl.BlockSpec(memory_space=pltpu.VMEM)] + c_out,
        scratch_shapes=[pltpu.SemaphoreType.DMA((7,)), pltpu.SemaphoreType.DMA((7,)), pltpu.SemaphoreType.DMA] + c_sems,
        input_output_aliases=c_alias, compiler_params=_cp(),
    )(blk, *(carry.srcs if carry is not None else []))
    return res[0] if carry is None else res


def _comm_call(name, inputs, out_shapes, plan, n_remote, aliases=None):
    n_in, n_out = len(inputs), len(out_shapes)

    def body(*refs):
        in_refs, out_refs = refs[:n_in], refs[n_in:n_in + n_out]
        send_sems, recv_sems = refs[n_in + n_out:]

        def remote(k, src, dst, to):
            return pltpu.make_async_remote_copy(src_ref=src, dst_ref=dst, send_sem=send_sems.at[k],
                                                recv_sem=recv_sems.at[k], device_id=to, device_id_type=MESH)

        plan(in_refs, out_refs, remote)

    return pl.pallas_call(
        body, name=name, out_shape=out_shapes, in_specs=[ANY] * n_in, out_specs=[ANY] * n_out,
        scratch_shapes=[pltpu.SemaphoreType.DMA((n_remote,)), pltpu.SemaphoreType.DMA((n_remote,))],
        input_output_aliases=aliases or {}, compiler_params=_cp(has_side_effects=True),
    )(*inputs)


def _gather_ici_carry(placed):
    K = len(placed)

    def plan(ins, outs, remote):
        x, y, c = _coords()
        s_me = 2 * x + y
        cps = []
        for j, (px, py) in enumerate([(1 - x, y), (x, 1 - y), (1 - x, 1 - y)]):
            for k in range(K):
                rh = placed[k].shape[2] // 2
                own = outs[k].at[:, s_me, pl.ds(c * rh, rh), :]
                cps.append(remote(j * K + k, own, own, (px, py, c)))
        return cps

    shapes = [jax.ShapeDtypeStruct(p.shape, p.dtype) for p in placed]
    return _Carry(placed, shapes, {k: k for k in range(K)}, 3 * K, plan)


class _Exchange:
    def __init__(self, send, recv):
        self.send, self.recv = send, recv

    def start(self):
        self.send.start()

    def wait(self):
        self.send.wait_send()
        self.recv.wait_recv()


def _forward_carry(placed):
    K = len(placed)

    def plan(ins, outs, remote):
        x, y, c = _coords()
        cps = []
        for j, (px, py) in enumerate([(1 - x, y), (x, 1 - y), (1 - x, 1 - y)]):
            for k in range(K):
                rh = placed[k].shape[2] // 2
                landed = outs[k].at[:, 2 * px + py, pl.ds(c * rh, rh), :]
                other = outs[k].at[:, 2 * px + py, pl.ds((1 - c) * rh, rh), :]
                cps.append(_Exchange(remote(j * K + k, landed, landed, (x, y, 1 - c)),
                                     remote(j * K + k, other, other, (x, y, 1 - c))))
        return cps

    shapes = [jax.ShapeDtypeStruct(p.shape, p.dtype) for p in placed]
    return _Carry(placed, shapes, {k: k for k in range(K)}, 3 * K, plan)


def _gather_weights(placed, name):
    K = len(placed)

    def plan(ins, outs, remote):
        x, y, c = _coords()
        s_me = 2 * x + y
        sibling = (x, y, 1 - c)
        chips = [(1 - x, y), (x, 1 - y), (1 - x, 1 - y)]
        half = lambda k, s, cc: outs[k].at[:, s, pl.ds(cc * (placed[k].shape[2] // 2), placed[k].shape[2] // 2), :]
        sent = []
        for j, (px, py) in enumerate(chips):
            for k in range(K):
                own = half(k, s_me, c)
                cp = remote(j * K + k, own, own, (px, py, c))
                cp.start()
                sent.append(cp)
        for j, (px, py) in enumerate(chips):
            s_from = 2 * px + py
            for k in range(K):
                landed = half(k, s_from, c)
                remote(j * K + k, landed, landed, (px, py, c)).wait_recv()
                cp = remote((3 + j) * K + k, landed, landed, sibling)
                cp.start()
                sent.append(cp)
        for j, (px, py) in enumerate(chips):
            s_from = 2 * px + py
            for k in range(K):
                other = half(k, s_from, 1 - c)
                remote((3 + j) * K + k, other, other, sibling).wait_recv()
        for cp in sent:
            cp.wait_send()

    out_shapes = [jax.ShapeDtypeStruct(p.shape, p.dtype) for p in placed]
    return _comm_call(name, placed, out_shapes, plan, 6 * K, aliases={k: k for k in range(K)})


def _send_other_half(grads, name):
    K = len(grads)

    def plan(ins, outs, remote):
        x, y, c = _coords()
        cps = []
        for k in range(K):
            rh = grads[k].shape[2] // 2
            cps.append(remote(k, ins[k].at[:, :, pl.ds((1 - c) * rh, rh), :], outs[k], (x, y, 1 - c)))
        for cp in cps:
            cp.start()
        for cp in cps:
            cp.wait()

    out_shapes = [jax.ShapeDtypeStruct(g.shape[:2] + (g.shape[2] // 2, g.shape[3]), g.dtype) for g in grads]
    return _comm_call(name, grads, out_shapes, plan, K)


def _send_half_carry(grads):
    K = len(grads)

    def plan(ins, outs, remote):
        x, y, c = _coords()
        cps = []
        for k in range(K):
            rh = grads[k].shape[2] // 2
            cps.append(remote(k, ins[k].at[:, :, pl.ds((1 - c) * rh, rh), :], outs[k], (x, y, 1 - c)))
        return cps

    out_shapes = [jax.ShapeDtypeStruct(g.shape[:2] + (g.shape[2] // 2, g.shape[3]), g.dtype) for g in grads]
    return _Carry(grads, out_shapes, {}, K, plan)


def _merge_carries(carries):
    if not carries:
        return None
    if len(carries) == 1:
        return carries[0]
    srcs, shapes, aliases = [], [], {}
    for cy in carries:
        aliases.update({len(srcs) + i: len(shapes) + o for i, o in cy.aliases.items()})
        srcs += cy.srcs
        shapes += cy.out_shapes

    def plan(ins, outs, remote):
        cps, i0, o0, k0 = [], 0, 0, 0
        for cy in carries:
            shifted = functools.partial(lambda k, src, dst, to, base: remote(base + k, src, dst, to), base=k0)
            cps += cy.plan(ins[i0:i0 + len(cy.srcs)], outs[o0:o0 + len(cy.out_shapes)], shifted)
            i0, o0, k0 = i0 + len(cy.srcs), o0 + len(cy.out_shapes), k0 + cy.n
        return cps

    return _Carry(srcs, shapes, aliases, sum(cy.n for cy in carries), plan)


def _scatter_carry(parts):
    K = len(parts)

    def plan(ins, outs, remote):
        x, y, c = _coords()
        chips = [(1 - x, y), (x, 1 - y), (1 - x, 1 - y)]
        return [remote(j * K + k, ins[k].at[:, 2 * px + py], outs[k].at[:, j], (px, py, c))
                for j, (px, py) in enumerate(chips) for k in range(K)]

    out_shapes = [jax.ShapeDtypeStruct((p.shape[0], 3) + p.shape[2:], p.dtype) for p in parts]
    return _Carry(parts, out_shapes, {}, 3 * K, plan)


def _join_halves(joined):
    K = len(joined)

    def plan(ins, outs, remote):
        x, y, c = _coords()
        cps = []
        for k in range(K):
            rh = joined[k].shape[1] // 2
            mine = outs[k].at[:, pl.ds(c * rh, rh), :]
            cps.append(remote(k, mine, mine, (x, y, 1 - c)))
        for cp in cps:
            cp.start()
        for k in range(K):
            rh = joined[k].shape[1] // 2
            other = outs[k].at[:, pl.ds((1 - c) * rh, rh), :]
            remote(k, other, other, (x, y, 1 - c)).wait_recv()
        for cp in cps:
            cp.wait_send()

    out_shapes = [jax.ShapeDtypeStruct(h.shape, h.dtype) for h in joined]
    return _comm_call("join_halves", joined, out_shapes, plan, K, aliases={k: k for k in range(K)})


def _pack(parts):
    flat = []
    for p in parts:
        v = p.reshape(-1).astype(F32)
        pad = (-v.shape[0]) % 1024
        flat.append(jnp.pad(v, (0, pad)) if pad else v)
    return jnp.concatenate(flat).reshape(-1, 128)


def _unpack(packed, shapes):
    flat = packed.reshape(-1)
    out, off = [], 0
    for shp in shapes:
        n = math.prod(shp)
        out.append(flat[off:off + n].reshape(shp))
        off += n + (-n) % 1024
    return out


def _shard_last(a, s, n):
    return lax.dynamic_slice_in_dim(a, s * n, n, axis=a.ndim - 1)


def _rows8(*vecs):
    n = vecs[0].shape[-1]
    rows = [v.reshape(1, n).astype(F32) for v in vecs]
    return jnp.concatenate(rows + [jnp.zeros((8 - len(rows), n), F32)], axis=0)


def kernel(x, c, ada_w, ada_b, norm_pre, norm_post, ffn_w13, ffn_w2, ab_w_in, a_conv_w, a_conv_b, a_gate_w, a_gate_b, a_lam, b_conv_w, b_conv_b, b_norm_g, b_norm_b, ab_w_out, c_w_in, c_b_in, c_norm_g, c_norm_b, c_w_s, c_b_s, c_w_out, loss_target, m_ada_w, m_ada_b, m_norm_pre, m_norm_post, m_ffn_w13, m_ffn_w2, m_ab_w_in, m_a_conv_w, m_a_conv_b, m_a_gate_w, m_a_gate_b, m_a_lam, m_b_conv_w, m_b_conv_b, m_b_norm_g, m_b_norm_b, m_ab_w_out, m_c_w_in, m_c_b_in, m_c_norm_g, m_c_norm_b, m_c_w_s, m_c_b_s, m_c_w_out, v_ada_w, v_ada_b, v_norm_pre, v_norm_post, v_ffn_w13, v_ffn_w2, v_ab_w_in, v_a_conv_w, v_a_conv_b, v_a_gate_w, v_a_gate_b, v_a_lam, v_b_conv_w, v_b_conv_b, v_b_norm_g, v_b_norm_b, v_ab_w_out, v_c_w_in, v_c_b_in, v_c_norm_g, v_c_norm_b, v_c_w_s, v_c_b_s, v_c_w_out):
    S, D = x.shape[1], x.shape[2]
    W = a_lam.shape[-1]
    Fh = ffn_w13.shape[-1]
    Fq = ffn_w2.shape[2]
    xi, yi, ci = _coords()
    shard = 2 * xi + yi
    me = 4 * xi + 2 * yi + ci
    x2, tgt = x[0], loss_target[0]

    sharded_small = [norm_pre, norm_post, a_conv_w, b_conv_w, c_b_in, c_norm_g, c_norm_b]
    gathered = _all_gather8(_pack([c] + sharded_small), "gather_small")
    blocks = gathered.reshape(N_DEV, -1, 128)
    per_dev = [_unpack(blocks[d], [c.shape] + [p.shape for p in sharded_small]) for d in range(0, N_DEV, 2)]
    c_all = jnp.concatenate([_unpack(blocks[d], [c.shape])[0] for d in range(N_DEV)], axis=0)
    npre, npost, acw, bcw, cbin, cng, cnb = [jnp.concatenate([per_dev[s][1 + i] for s in range(N_SHARD)], axis=-1)
                                             for i in range(len(sharded_small))]

    ada_b_my = _shard_last(ada_b, shard, ada_w.shape[-1])[:, None, :]
    modp = _ada_fwd(c_all, ada_w, ada_b_my, "ada_fwd")
    modg = _all_gather8(modp.reshape(16, -1), "gather_mod").reshape(N_DEV, 2, 8, -1)
    mod_me = lax.dynamic_index_in_dim(modg[0::2], me, axis=2, keepdims=False)
    mod = jnp.transpose(mod_me, (1, 0, 2)).reshape(2, 3, 3, D)

    w13s, w2s = ffn_w13.reshape(4, D, Fh), ffn_w2.reshape(4, Fq, D)
    placed13 = [_cast_place(w13s, g, f"cast_w13_{g}") for g in range(4)]
    placed2 = [_cast_place(w2s, g, f"cast_w2_{g}") for g in range(4)]
    placed_mix = [_cast_place(w, 0, f"cast_mix{k}") for k, w in enumerate((ab_w_in, ab_w_out, c_w_in, c_w_out))]
    w13_first, w2_first, abin_g, about_g = _gather_weights([placed13[0], placed2[0]] + placed_mix[:2], "gather_first")
    gather_carry = _gather_ici_carry(placed13[1:] + placed2[1:] + placed_mix[2:])

    eye = jnp.eye(8, dtype=F32)
    dh_a = W // 8
    blockdiag = lambda w: jnp.einsum("hde,hg->hdge", w, eye).reshape(W, W)
    gw = a_gate_w[0]
    wg = jnp.concatenate([blockdiag(gw[:, :, :dh_a]), blockdiag(gw[:, :, dh_a:])], axis=1).astype(MXU_DT)
    bgv = jnp.concatenate([a_gate_b[0][:, :dh_a].reshape(-1), a_gate_b[0][:, dh_a:].reshape(-1)])
    bg = _rows8(bgv)
    cwa = jnp.concatenate([acw[0], jnp.zeros((4, W), F32)], axis=0)
    cwb = jnp.concatenate([bcw[0], jnp.zeros((1, W), F32)], axis=0)
    v512 = _rows8(a_conv_b[0], a_lam[0], b_conv_b[0], b_norm_g[0], b_norm_b[0])
    dg_b = W // 8
    gid = jnp.arange(W) // dg_b
    member = (gid[:, None] == jnp.arange(128)[None, :]).astype(F32)
    pavg = jnp.stack([member / dg_b, member]).astype(MXU_DT)
    v2d = _rows8(cbin[0])
    v1d = _rows8(cng[0], cnb[0])
    tril = jnp.tril(jnp.ones((CHUNK, CHUNK), dtype=bool))
    ws = jnp.where(tril, c_w_s[0], 0.0).astype(MXU_DT)
    bsb = jnp.repeat(jnp.transpose(c_b_s[0]), D // N_HEAD, axis=1)

    res_ws = (0.5, 1.0, 0.5)
    vps, xs, saved = [], [], []
    xc = x2
    w13g, w2v = [w13_first], [w2_first.reshape(1, 2, Fh, D)]
    for l in range(2):
        for j in range(3):
            k = 3 * l + j
            vp = _rows8(npre[l, j], mod[l, j, 0], mod[l, j, 1], mod[l, j, 2], npost[l, j])
            vps.append(vp)
            xs.append(xc)
            gi = 2 * l + j // 2
            if k == 0:
                xc, *keep = _ffn_fwd(xc, vp, w13g[0], w2v[0], res_ws[j], "ffn_fwd0", carry=gather_carry)
                keep, landed = keep[:4], keep[4:]
            elif k == 1:
                ab_ops = (abin_g[0], cwa, wg, bg, cwb, v512, pavg, about_g.reshape(D, D))
                xc, *keep = _mix_ab_fwd(xc, vp, *ab_ops, res_ws[j], "mix_ab_fwd", carry=_forward_carry(landed))
                keep, full = keep[:3], keep[3:]
                w13g += list(full[0:3])
                w2v += [w.reshape(1, 2, Fh, D) for w in full[3:6]]
                cin_g, cout_g = full[6:]
                c_ops = (cin_g[0], v2d, v1d, ws, bsb, cout_g.reshape(D, D))
            elif k == 5:
                dout, *keep, lrow = _ffn_fwd(xc, vp, w13g[gi], w2v[gi], res_ws[j], f"ffn_fwd{k}", tgt=tgt)
            elif j != 1:
                xc, *keep = _ffn_fwd(xc, vp, w13g[gi], w2v[gi], res_ws[j], f"ffn_fwd{k}")
            else:
                xc, *keep = _mix_c_fwd(xc, vp, *c_ops, res_ws[j], "mix_c_fwd")
            saved.append(keep)

    joined = {"w13": lax.empty((4, D, Fh), F32), "w2": lax.empty((4, Fq, D), F32), "abin": lax.empty((1, D, 4 * W // 4), F32),
              "about": lax.empty((1, D // 4, D), F32), "cin": lax.empty((1, D, 2 * D // 4), F32),
              "cout": lax.empty((1, D // 4, D), F32)}

    stage = {"send": None, "scatter": []}

    def add_halves(group, got, dtype=F32):
        grads, keys, g, tag = group
        parts = [_add_half(gr, la, f"add_half_{tag}{i}", dtype) for i, (gr, la) in enumerate(zip(grads, got))]
        return parts, keys, g

    def sum_landed(groups, landed_slots):
        landed_slots = list(landed_slots)
        for parts, keys, g in groups:
            for part, key in zip(parts, keys):
                joined[key] = _sum_chips(part, landed_slots.pop(0), joined[key], g, f"sum_chips_{key}{g}")
        return landed_slots

    def all_parts(groups):
        return [part for group in groups for part in group[0]]

    def ride(call, n_own, carries_ok=True):
        if not carries_ok:
            return call(None)[:n_own]
        scat, send = stage["scatter"], stage["send"]
        carries = ([_scatter_carry(all_parts(scat))] if scat else []) + (
            [_send_half_carry(send[0])] if send is not None else [])
        res = call(_merge_carries(carries))
        own, extra = res[:n_own], res[n_own:]
        stage["scatter"], stage["send"] = [], None
        extra = sum_landed(scat, extra)
        if send is not None:
            stage["scatter"].append(add_halves(send, extra))
        return own

    d_npre = [[None] * 3 for _ in range(2)]
    d_npost = [[None] * 3 for _ in range(2)]
    d_mod = [[None] * 3 for _ in range(2)]
    for l in (1, 0):
        for j in (2, 1, 0):
            k = 3 * l + j
            if j != 1:
                f, h, gpre, upre = saved[k]
                gi = 2 * l + j // 2
                dw13 = lax.empty((1, 4, D, Fh), F32)
                dw2v = lax.empty((1, 2, Fh, D), F32)
                df, dh0, s_a, dw13, dw2v = ride(lambda cy: _ffn_bwd_half(
                    0, xs[k], f, dout, vps[k], h, gpre, upre, w13g[gi], w2v[gi], dw13, dw2v, res_ws[j],
                    f"ffn_bwd{k}a", carry=cy), 5)
                dout, s_b, dw13, dw2v = ride(lambda cy: _ffn_bwd_half(
                    1, xs[k], f, dout, vps[k], h, gpre, upre, w13g[gi], w2v[gi], dw13, dw2v, res_ws[j],
                    f"ffn_bwd{k}b", df=df, dh0=dh0, carry=cy), 4, carries_ok=k > 0)
                sums = s_a + s_b
                made = ([dw13, dw2v.reshape(1, 4, Fq, D)], ["w13", "w2"], gi, f"ffn{gi}")
            elif l == 0:
                dout, sums, d_abin, d_about, d_wg, d_cwa, d_cwb, d_bg, d_v512 = ride(lambda cy: _mix_ab_bwd(
                    xs[k], saved[k][0], dout, vps[k], *saved[k][1:], *ab_ops, res_ws[j], "mix_ab_bwd", carry=cy), 9)
                made = ([d_abin[None], d_about.reshape(1, 4, D // 4, D)], ["abin", "about"], 0, "ab")
                stage["scatter"].append(add_halves(made, _send_other_half(made[0], "send_half_ab")))
                made = None
            else:
                dout, sums, d_cin, d_cout, d_ws, d_bsb, d_v2, d_v1 = ride(lambda cy: _mix_c_bwd(
                    xs[k], saved[k][0], dout, vps[k], saved[k][1], *c_ops, res_ws[j], "mix_c_bwd", carry=cy), 8)
                made = ([d_cin[None], d_cout.reshape(1, 4, D // 4, D)], ["cin", "cout"], 0, "c")
            stage["send"] = made
            d_npre[l][j], d_npost[l][j] = sums[4], sums[1]
            d_mod[l][j] = jnp.stack([sums[2], sums[3], sums[0]])
    last = stage["scatter"] + [add_halves(stage["send"], _send_other_half(stage["send"][0], "send_half_last"), MXU_DT)]
    grad_x = dout[None]

    dmod = jnp.stack([jnp.stack(d_mod[l]) for l in range(2)]).reshape(2, 9 * D)
    d_gate_w = jnp.concatenate([jnp.einsum("hdhe->hde", d_wg[:, :W].reshape(8, dh_a, 8, dh_a)),
                                jnp.einsum("hdhe->hde", d_wg[:, W:].reshape(8, dh_a, 8, dh_a))], axis=-1)
    d_gate_b = jnp.concatenate([d_bg[0, :W].reshape(8, dh_a), d_bg[0, W:].reshape(8, dh_a)], axis=-1)
    small_grads = [
        dmod, jnp.stack([jnp.stack(r) for r in d_npre]), jnp.stack([jnp.stack(r) for r in d_npost]),
        d_cwa[:4][None], d_v512[0][None], d_gate_w[None], d_gate_b[None], d_v512[1][None], d_cwb[:31][None],
        d_v512[2][None], d_v512[3][None], d_v512[4][None], d_v2[0][None], d_v1[0][None], d_v1[1][None],
        jnp.where(tril, d_ws, 0.0)[None], jnp.transpose(d_bsb.reshape(CHUNK, N_HEAD, D // N_HEAD).sum(-1))[None]]
    small_w = [ada_b, norm_pre, norm_post, a_conv_w, a_conv_b, a_gate_w, a_gate_b, a_lam, b_conv_w, b_conv_b,
               b_norm_g, b_norm_b, c_b_in, c_norm_g, c_norm_b, c_w_s, c_b_s]
    small_m = [m_ada_b, m_norm_pre, m_norm_post, m_a_conv_w, m_a_conv_b, m_a_gate_w, m_a_gate_b, m_a_lam, m_b_conv_w,
               m_b_conv_b, m_b_norm_g, m_b_norm_b, m_c_b_in, m_c_norm_g, m_c_norm_b, m_c_w_s, m_c_b_s]
    small_v = [v_ada_b, v_norm_pre, v_norm_post, v_a_conv_w, v_a_conv_b, v_a_gate_w, v_a_gate_b, v_a_lam, v_b_conv_w,
               v_b_conv_b, v_b_norm_g, v_b_norm_b, v_c_b_in, v_c_norm_g, v_c_norm_b, v_c_w_s, v_c_b_s]
    full_shapes = [g.shape for g in small_grads]
    loss_part = jnp.sum(lrow[0]).reshape(1, 1)
    sg_all, *landed_last = _all_gather8(_pack(small_grads + [loss_part]), "gather_small_grads",
                                        carry=_scatter_carry(all_parts(last)))
    sum_landed(last, landed_last)
    sg_all = sg_all.reshape(N_DEV, -1, 128)
    sg_sum = _sum_slots(sg_all[None], "sum_small_grads")[0]
    *g_full, loss_sum = _unpack(sg_sum, full_shapes + [(1, 1)])
    loss = loss_sum[0, 0]
    g_small = [g if g.shape == w.shape else _shard_last(g, shard, w.shape[-1]) for g, w in zip(g_full, small_w)]
    small_shapes = [w.shape for w in small_w]
    d_s, m_s, v_s = _adamw(_pack(small_w), _pack(g_small), _pack(small_m), _pack(small_v), "adamw_small")
    delta_small, newm_small, newv_small = (_unpack(a, small_shapes) for a in (d_s, m_s, v_s))

    dmod_all = jnp.stack([_unpack(sg_all[d], full_shapes[:1])[0] for d in range(N_DEV)], axis=1)
    n_ada = ada_w.shape[-1]
    g_ada_w = _ada_bwd(c_all, _shard_last(dmod_all, shard, n_ada), "ada_bwd")

    g_big = _join_halves([joined[key] for key in ("w13", "w2", "abin", "about", "cin", "cout")])

    big_w = [ffn_w13, ffn_w2, ab_w_in, ab_w_out, c_w_in, c_w_out, ada_w]
    big_m = [m_ffn_w13, m_ffn_w2, m_ab_w_in, m_ab_w_out, m_c_w_in, m_c_w_out, m_ada_w]
    big_v = [v_ffn_w13, v_ffn_w2, v_ab_w_in, v_ab_w_out, v_c_w_in, v_c_w_out, v_ada_w]
    big_g = [g.reshape(w.shape) for g, w in zip(list(g_big) + [g_ada_w], big_w)]
    big_out = []
    for k, (w, g, m, v) in enumerate(zip(big_w, big_g, big_m, big_v)):
        two_d = lambda a: a.reshape(-1, a.shape[-1])
        res = _adamw(two_d(w), two_d(g), two_d(m), two_d(v), f"adamw_big{k}")
        big_out.append([r.reshape(w.shape) for r in res])

    names = ["ada_w", "ada_b", "norm_pre", "norm_post", "ffn_w13", "ffn_w2", "ab_w_in", "a_conv_w", "a_conv_b",
             "a_gate_w", "a_gate_b", "a_lam", "b_conv_w", "b_conv_b", "b_norm_g", "b_norm_b", "ab_w_out", "c_w_in",
             "c_b_in", "c_norm_g", "c_norm_b", "c_w_s", "c_b_s", "c_w_out"]
    big_names = ["ffn_w13", "ffn_w2", "ab_w_in", "ab_w_out", "c_w_in", "c_w_out", "ada_w"]
    small_names = ["ada_b", "norm_pre", "norm_post", "a_conv_w", "a_conv_b", "a_gate_w", "a_gate_b", "a_lam",
                   "b_conv_w", "b_conv_b", "b_norm_g", "b_norm_b", "c_b_in", "c_norm_g", "c_norm_b", "c_w_s", "c_b_s"]
    table = {}
    for k, n in enumerate(big_names):
        table[n] = (big_g[k], *big_out[k])
    for k, n in enumerate(small_names):
        table[n] = (g_small[k], delta_small[k], newm_small[k], newv_small[k])
    outs = [loss, grad_x]
    for field in range(4):
        outs += [table[n][field] for n in names]
    return tuple(outs)
```

```python
import functools
import math

import jax
import jax.numpy as jnp
from jax import lax
from jax.experimental import pallas as pl
from jax.experimental.pallas import tpu as pltpu

F32 = jnp.float32
MXU_DT = jnp.bfloat16
EPS = 1e-6
LRU_C = 8.0
N_SHARD = 4
N_DEV = 8
CHUNK = 128
N_HEAD = 8
ADAM_LR, ADAM_B1, ADAM_B2, ADAM_EPS, ADAM_WD, ADAM_STEP = 0.001, 0.9, 0.999, 1e-08, 0.01, 10
GELU_K0 = math.sqrt(2.0 / math.pi)
GELU_K1 = 0.044715
VMEM_LIMIT = 58 * 1024 * 1024
MESH = pl.DeviceIdType.MESH
ANY = pl.BlockSpec(memory_space=pl.ANY)


def _cp(sem=None, **kw):
    if sem is not None:
        kw["dimension_semantics"] = sem
    return pltpu.CompilerParams(vmem_limit_bytes=VMEM_LIMIT, **kw)


def _resident(a):
    return pl.BlockSpec(a.shape, lambda *_: (0,) * a.ndim, pipeline_mode=pl.Buffered(1))


def _whole(shape):
    return pl.BlockSpec(shape, lambda *_: (0,) * len(shape))


def _dot(a, b):
    return jnp.dot(a.astype(MXU_DT), b.astype(MXU_DT), preferred_element_type=F32)


def _dot_nt(a, b):
    return lax.dot_general(a.astype(MXU_DT), b.astype(MXU_DT), (((1,), (1,)), ((), ())), preferred_element_type=F32)


def _dot_tn(a, b):
    return lax.dot_general(a.astype(MXU_DT), b.astype(MXU_DT), (((0,), (0,)), ((), ())), preferred_element_type=F32)


def _dot_hi(a, b):
    return jnp.dot(a, b, precision=lax.Precision.HIGHEST, preferred_element_type=F32)


def _sig(x):
    return 1.0 / (1.0 + jnp.exp(-x))


def _logsig(x):
    return jnp.minimum(x, 0.0) - jnp.log(1.0 + jnp.exp(-jnp.abs(x)))


def _gelu(x):
    x2 = x * x
    t = jnp.tanh(GELU_K0 * (x + GELU_K1 * x * x2))
    val = 0.5 * x * (1.0 + t)
    der = 0.5 * (1.0 + t) + 0.5 * x * (1.0 - t * t) * (GELU_K0 * (1.0 + 3.0 * GELU_K1 * x2))
    return val, der


def _neg_expm1(x):
    small = -(x * (1.0 + x * (0.5 + x * (1.0 / 6.0 + x * (1.0 / 24.0)))))
    return jnp.where(x > -0.05, small, 1.0 - jnp.exp(x))


def _colsum(v):
    return jnp.sum(v, axis=0, keepdims=True)


def _rowmean(v):
    return jnp.mean(v, axis=-1, keepdims=True)


def _copy_out(pairs, sem):
    for src, dst in pairs:
        cp = pltpu.make_async_copy(src, dst, sem)
        cp.start()
        cp.wait()


class _Carry:
    def __init__(self, srcs, out_shapes, aliases, n, plan):
        self.srcs, self.out_shapes, self.aliases, self.n, self.plan = list(srcs), list(out_shapes), aliases, n, plan


def _carry_args(carry, n_in, n_out):
    if carry is None:
        return [], [], [], [], {}
    sems = [pltpu.SemaphoreType.DMA((carry.n,)), pltpu.SemaphoreType.DMA((carry.n,))]
    aliases = {n_in + i: n_out + o for i, o in carry.aliases.items()}
    return [ANY] * len(carry.srcs), [ANY] * len(carry.out_shapes), carry.out_shapes, sems, aliases


def _split_refs(refs, n_in, n_out, n_scratch, carry):
    nci, nco = (len(carry.srcs), len(carry.out_shapes)) if carry is not None else (0, 0)
    cuts = [n_in, nci, n_out, nco, n_scratch]
    parts, i = [], 0
    for n in cuts:
        parts.append(refs[i:i + n])
        i += n
    ins, cins, outs, couts, scr = parts
    return ins, outs, scr, (cins, couts, refs[i:])


def _carry_copies(carry, carry_refs):
    if carry is None:
        return []
    cins, couts, (send_sems, recv_sems) = carry_refs

    def remote(k, src, dst, to):
        return pltpu.make_async_remote_copy(src_ref=src, dst_ref=dst, send_sem=send_sems.at[k],
                                            recv_sem=recv_sems.at[k], device_id=to, device_id_type=MESH)

    return carry.plan(cins, couts, remote)


def _carry_run(carry, carry_refs, first, last):
    if carry is None:
        return

    @pl.when(first)
    def _():
        for cp in _carry_copies(carry, carry_refs):
            cp.start()

    @pl.when(last)
    def _():
        for cp in _carry_copies(carry, carry_refs):
            cp.wait()


def _shift_down(v, k):
    return v if k == 0 else pltpu.roll(v, k, 0)


def _shift_up(v, k):
    return v if k == 0 else pltpu.roll(v, v.shape[0] - k, 0)


def _shell_pre(xv, vp_ref):
    r = lax.rsqrt(_rowmean(xv * xv) + EPS)
    return xv * r * (vp_ref[0:1, :] * (1.0 + vp_ref[2:3, :])) + vp_ref[1:2, :]


def _shell_post(xv, fv, vp_ref, res_w):
    r = lax.rsqrt(_rowmean(fv * fv) + EPS)
    return xv + fv * r * (res_w * (1.0 + vp_ref[3:4, :]) * vp_ref[4:5, :])


def _shell_post_bwd(fv, dov, vp_ref, res_w, s_ref, keep=1.0):
    r = lax.rsqrt(_rowmean(fv * fv) + EPS)
    fn = fv * r
    pg = vp_ref[4:5, :]
    dy = (res_w * (1.0 + vp_ref[3:4, :])) * dov
    if s_ref is not None:
        s_ref[0:1, :] += _colsum(res_w * fn * pg * dov) * keep
        s_ref[1:2, :] += _colsum(fn * dy) * keep
    q = dy * pg
    return r * (q - fn * _rowmean(fn * q))


def _shell_pre_bwd(xv, dh, dov, vp_ref, s_ref):
    r = lax.rsqrt(_rowmean(xv * xv) + EPS)
    xn = xv * r
    pg = vp_ref[0:1, :]
    sc1 = 1.0 + vp_ref[2:3, :]
    s_ref[2:3, :] += _colsum(dh)
    s_ref[3:4, :] += _colsum(xn * pg * dh)
    s_ref[4:5, :] += _colsum(xn * dh * sc1)
    q = dh * (sc1 * pg)
    return dov + r * (q - xn * _rowmean(xn * q))


MXU_COLS = 256


def _col_chunks(n, width=2 * MXU_COLS):
    return [(c0, min(c0 + width, n)) for c0 in range(0, n, width)]


def _ffn_fwd(x, vp, w13g, w2v, res_w, name, carry=None, tgt=None, tm=512):
    S, D = x.shape
    Fh = w13g.shape[-1]
    T = S // tm
    head = tgt is not None
    n_in, n_out = (5, 6) if head else (4, 5)

    def body(*refs):
        ins, outs, _, carry_refs = _split_refs(refs, n_in, n_out, 0, carry)
        x_ref, vp_ref, w13_ref, w2_ref = ins[:4]
        xo_ref, f_ref, h_ref, g_ref, u_ref = outs[:5]
        _carry_run(carry, carry_refs, pl.program_id(0) == 0, pl.program_id(0) == T - 1)
        xv = x_ref[...]
        hb = _shell_pre(xv, vp_ref).astype(MXU_DT)
        h_ref[...] = hb
        acc = None
        for j in range(2):
            gg = _dot(hb, w13_ref[j])
            uu = _dot(hb, w13_ref[2 + j])
            g_ref[:, j * Fh:(j + 1) * Fh] = gg.astype(g_ref.dtype)
            u_ref[:, j * Fh:(j + 1) * Fh] = uu.astype(u_ref.dtype)
            part = _dot(gg * _sig(gg) * uu, w2_ref[j])
            acc = part if acc is None else acc + part
        f_ref[...] = acc
        xo = _shell_post(xv, acc, vp_ref, res_w)
        if head:
            l_ref = outs[5]

            @pl.when(pl.program_id(0) == 0)
            def _():
                l_ref[...] = jnp.zeros_like(l_ref)

            e = xo - ins[4][...]
            xo_ref[...] = e * (1.0 / D)
            l_ref[0:1, :] += _colsum(e * e) * (0.5 / D)
        else:
            xo_ref[...] = xo

    tile = lambda w: pl.BlockSpec((tm, w), lambda t: (t, 0))
    sd = jax.ShapeDtypeStruct
    c_in, c_out, c_shapes, c_sems, c_alias = _carry_args(carry, n_in, n_out)
    return pl.pallas_call(
        body, name=name, grid=(T,),
        in_specs=[tile(D), _whole((8, D)),
                  pl.BlockSpec((None, 4, D, Fh), lambda t: (0, 0, 0, 0), pipeline_mode=pl.Buffered(1)),
                  pl.BlockSpec((None, 2, Fh, D), lambda t: (0, 0, 0, 0), pipeline_mode=pl.Buffered(1))]
        + ([tile(D)] if head else []) + c_in,
        out_specs=[tile(D), tile(D), tile(D), tile(2 * Fh), tile(2 * Fh)] + ([_whole((8, D))] if head else []) + c_out,
        out_shape=[sd((S, D), F32), sd((S, D), F32), sd((S, D), MXU_DT), sd((S, 2 * Fh), MXU_DT),
                   sd((S, 2 * Fh), MXU_DT)] + ([sd((8, D), F32)] if head else []) + c_shapes,
        scratch_shapes=c_sems, input_output_aliases=c_alias,
        compiler_params=_cp(("arbitrary",), has_side_effects=carry is not None),
    )(x, vp, w13g, w2v, *([tgt] if head else []), *(carry.srcs if carry is not None else []))


def _ffn_bwd_half(j, x, f, dout, vp, h, gpre, upre, w13g, w2v, dw13, dw2v, res_w, name, df=None, dh0=None, carry=None,
                  tm=256):
    S, D = h.shape
    Fh = w13g.shape[-1]
    T = S // tm
    first = j == 0
    n_in, n_out = (11, 5) if first else (13, 4)

    steps = T + 1 if first else T

    def body(*refs):
        ins, outs, scr, carry_refs = _split_refs(refs, n_in, n_out, 5 if first else 4, carry)
        a1, a3, a2, sem = scr[:4]
        if first:
            f_ref, do_ref, vp_ref, h_ref, g_ref, u_ref, w1_ref, w3_ref, w2_ref, _, _ = ins
            df_ref, dh_ref, s_ref, dw13_ref, dw2_ref = outs
            dfs = scr[4]
        else:
            x_ref, do_ref, vp_ref, h_ref, g_ref, u_ref, w1_ref, w3_ref, w2_ref, dfi_ref, dh0_ref, _, _ = ins
            dx_ref, s_ref, dw13_ref, dw2_ref = outs
        t = pl.program_id(0)
        _carry_run(carry, carry_refs, t == 0, t == steps - 1)

        @pl.when(t == 0)
        def _():
            for ref in (a1, a3, a2, s_ref) + ((dfs,) if first else ()):
                ref[...] = jnp.zeros_like(ref)

        hv = h_ref[...]
        if first:
            dfv = dfs[(t + 1) % 2]
        else:
            dfv = dfi_ref[...]
        dh = None
        for n_chunk, (c0, c1) in enumerate(_col_chunks(Fh)):
            gg = g_ref[:, c0:c1].astype(F32)
            uu = u_ref[:, c0:c1].astype(F32)
            sg = _sig(gg)
            si = gg * sg
            da = _dot_nt(dfv, w2_ref[c0:c1, :])
            a2[c0:c1, :] += _dot_tn(si * uu, dfv)
            if first and n_chunk == 0:
                keep = jnp.where(t < T, 1.0, 0.0).astype(F32)
                nxt = _shell_post_bwd(f_ref[...], do_ref[...], vp_ref, res_w, s_ref, keep).astype(MXU_DT)
                dfs[t % 2] = nxt
                df_ref[...] = nxt
            dg = da * uu * (sg * (1.0 + gg * (1.0 - sg)))
            du = da * si
            a1[:, c0:c1] += _dot_tn(hv, dg)
            a3[:, c0:c1] += _dot_tn(hv, du)
            part = _dot_nt(dg, w1_ref[:, c0:c1]) + _dot_nt(du, w3_ref[:, c0:c1])
            dh = part if dh is None else dh + part
        if first:
            dh_ref[...] = dh
        else:
            dx_ref[...] = _shell_pre_bwd(x_ref[...], dh0_ref[...] + dh, do_ref[...], vp_ref, s_ref)

        @pl.when(t == steps - 1)
        def _():
            _copy_out(((a1, dw13_ref.at[0, j]), (a3, dw13_ref.at[0, 2 + j]), (a2, dw2_ref.at[0, j])), sem)

    if first:
        made_at = lambda t: jnp.minimum(t, T - 1)
        used_at = lambda t: jnp.maximum(t - 1, 0)
    else:
        made_at = used_at = lambda t: t
    tile = lambda w, at=used_at: pl.BlockSpec((tm, w), lambda t: (at(t), 0))
    half = pl.BlockSpec((tm, Fh), lambda t: (used_at(t), j))
    weights = [pl.BlockSpec((None, None, D, Fh), lambda t: (0, j, 0, 0), pipeline_mode=pl.Buffered(1)),
               pl.BlockSpec((None, None, D, Fh), lambda t: (0, 2 + j, 0, 0), pipeline_mode=pl.Buffered(1)),
               pl.BlockSpec((None, None, Fh, D), lambda t: (0, j, 0, 0), pipeline_mode=pl.Buffered(1))]
    sd = jax.ShapeDtypeStruct
    grads = [sd(dw13.shape, F32), sd(dw2v.shape, F32)]
    scratch = [pltpu.VMEM((D, Fh), F32), pltpu.VMEM((D, Fh), F32), pltpu.VMEM((Fh, D), F32), pltpu.SemaphoreType.DMA]
    c_in, c_out, c_shapes, c_sems, c_alias = _carry_args(carry, n_in, n_out)
    params = _cp(("arbitrary",), has_side_effects=carry is not None)
    extra = carry.srcs if carry is not None else []
    if first:
        return pl.pallas_call(
            body, name=name, grid=(steps,),
            in_specs=[tile(D, made_at), tile(D, made_at), _whole((8, D)), tile(D), half, half] + weights + [ANY, ANY]
            + c_in,
            out_specs=[tile(D, made_at), tile(D), _whole((8, D)), ANY, ANY] + c_out,
            out_shape=[sd((S, D), MXU_DT), sd((S, D), F32), sd((8, D), F32)] + grads + c_shapes,
            scratch_shapes=scratch + [pltpu.VMEM((2, tm, D), MXU_DT)] + c_sems,
            input_output_aliases={9: 3, 10: 4, **c_alias}, compiler_params=params,
        )(f, dout, vp, h, gpre, upre, w13g, w13g, w2v, dw13, dw2v, *extra)
    return pl.pallas_call(
        body, name=name, grid=(T,),
        in_specs=[tile(D), tile(D), _whole((8, D)), tile(D), half, half] + weights + [tile(D), tile(D), ANY, ANY] + c_in,
        out_specs=[tile(D), _whole((8, D)), ANY, ANY] + c_out,
        out_shape=[sd((S, D), F32), sd((8, D), F32)] + grads + c_shapes,
        scratch_shapes=scratch + c_sems, input_output_aliases={11: 2, 12: 3, **c_alias}, compiler_params=params,
    )(x, dout, vp, h, gpre, upre, w13g, w13g, w2v, df, dh0, dw13, dw2v, *extra)


def _scan_fwd(a, u, rows):
    n = a.shape[0]
    d = 1
    while d < n:
        m = rows >= d
        u = u + a * jnp.where(m, _shift_down(u, d), 0.0)
        a = a * jnp.where(m, _shift_down(a, d), 1.0)
        d *= 2
    return a, u


def _scan_bwd(a, u, rows):
    n = a.shape[0]
    d = 1
    while d < n:
        m = rows < n - d
        u = u + a * jnp.where(m, _shift_up(u, d), 0.0)
        a = a * jnp.where(m, _shift_up(a, d), 1.0)
        d *= 2
    return a, u


def _causal_conv(ext, w_ref, K, halo, tm):
    acc = None
    for k in range(K):
        term = w_ref[k:k + 1, :] * _shift_down(ext, K - 1 - k)[halo:, :]
        acc = term if acc is None else acc + term
    return acc


def _anticausal_conv(ext, w_ref, K, tm):
    acc = None
    for k in range(K):
        term = w_ref[k:k + 1, :] * _shift_up(ext, K - 1 - k)[:tm, :]
        acc = term if acc is None else acc + term
    return acc


def _dot_split(a, b):
    hi = a.astype(MXU_DT)
    lo = (a - hi.astype(F32)).astype(MXU_DT)
    return jnp.dot(hi, b, preferred_element_type=F32) + jnp.dot(lo, b, preferred_element_type=F32)


def _group_mean(v, p_ref):
    return _dot_nt_exact(_dot_split(v, p_ref[0]), p_ref[1])


def _dot_nt_exact(a, bt):
    hi = a.astype(MXU_DT)
    lo = (a - hi.astype(F32)).astype(MXU_DT)
    dims = (((1,), (1,)), ((), ()))
    return (lax.dot_general(hi, bt, dims, preferred_element_type=F32)
            + lax.dot_general(lo, bt, dims, preferred_element_type=F32))


def _group_norm(vc, p_ref, g, b):
    mu = _group_mean(vc, p_ref)
    dv = vc - mu
    rstd = lax.rsqrt(_group_mean(dv * dv, p_ref) + EPS)
    vhat = dv * rstd
    return vhat, rstd, vhat * g + b


def _lru_gates(axc, wg_ref, bg_ref, lam, W):
    gp = _dot(axc, wg_ref[...]) + bg_ref[0:1, :]
    r = _sig(gp[:, :W])
    i = _sig(gp[:, W:])
    ls = _logsig(lam)
    L = (LRU_C * ls) * r
    a = jnp.exp(L)
    mult = jnp.sqrt(_neg_expm1(2.0 * L))
    return r, i, ls, a, mult


def _mix_ab_fwd(x, vp, win4, cwa, wg, bg, cwb, v512, pavg, wout, res_w, name, carry=None, tm=256):
    S, D = x.shape
    W = win4.shape[-1]
    KA, KB, HA, HB = 4, 31, 8, 32
    rider = carry

    def body(*refs):
        ins, (xo_ref, f_ref, h_ref, sav_ref), (ahalo, bhalo, carry), carry_refs = _split_refs(refs, 10, 4, 3, rider)
        x_ref, vp_ref, win_ref, cwa_ref, wg_ref, bg_ref, cwb_ref, v_ref, p_ref, wo_ref = ins
        _carry_run(rider, carry_refs, pl.program_id(0) == 0, pl.program_id(0) == S // tm - 1)

        @pl.when(pl.program_id(0) == 0)
        def _():
            ahalo[...] = jnp.zeros_like(ahalo)
            bhalo[...] = jnp.zeros_like(bhalo)
            carry[...] = jnp.zeros_like(carry)

        xv = x_ref[...]
        hv = _shell_pre(xv, vp_ref).astype(MXU_DT)
        h_ref[...] = hv
        a_gate = _dot(hv, win_ref[0])
        axp = _dot(hv, win_ref[1])
        b_val = _dot(hv, win_ref[2])
        b_gate = _dot(hv, win_ref[3])
        rows = lax.broadcasted_iota(jnp.int32, (tm, W), 0)
        axc = _causal_conv(jnp.concatenate([ahalo[...], axp], axis=0), cwa_ref, KA, HA, tm) + v_ref[0:1, :]
        ahalo[...] = axp[tm - HA:, :]
        r, i, ls, a, mult = _lru_gates(axc, wg_ref, bg_ref, v_ref[1:2, :], W)
        acum, hloc = _scan_fwd(a, mult * i * axc, rows)
        hs = hloc + acum * carry[7:8, :]
        carry[...] = hs[tm - 8:, :]
        ya = hs * _gelu(a_gate)[0]
        bv = b_val * _sig(b_gate)
        vc = _causal_conv(jnp.concatenate([bhalo[...], bv], axis=0), cwb_ref, KB, HB, tm) + v_ref[2:3, :]
        bhalo[...] = bv[tm - HB:, :]
        _, _, vn = _group_norm(vc, p_ref, v_ref[3:4, :], v_ref[4:5, :])
        yb = vn * _sig(vn)
        fv = _dot(ya, wo_ref[0:W, :]) + _dot(yb, wo_ref[W:, :])
        f_ref[...] = fv
        xo_ref[...] = _shell_post(xv, fv, vp_ref, res_w)
        for n, val in enumerate((hs, axp, axc, bv, vc)):
            sav_ref[:, n * W:(n + 1) * W] = val

    tile = lambda w: pl.BlockSpec((tm, w), lambda t: (t, 0))
    sd = jax.ShapeDtypeStruct
    c_in, c_out, c_shapes, c_sems, c_alias = _carry_args(rider, 10, 4)
    return pl.pallas_call(
        body, name=name, grid=(S // tm,),
        in_specs=[tile(D), _whole((8, D))] + [_resident(a) for a in (win4, cwa, wg, bg, cwb, v512, pavg, wout)] + c_in,
        out_specs=[tile(D), tile(D), tile(D), tile(5 * W)] + c_out,
        out_shape=[sd((S, D), F32), sd((S, D), F32), sd((S, D), MXU_DT), sd((S, 5 * W), F32)] + c_shapes,
        scratch_shapes=[pltpu.VMEM((HA, W), F32), pltpu.VMEM((HB, W), F32), pltpu.VMEM((8, W), F32)] + c_sems,
        input_output_aliases=c_alias, compiler_params=_cp(("arbitrary",), has_side_effects=rider is not None),
    )(x, vp, win4, cwa, wg, bg, cwb, v512, pavg, wout, *(rider.srcs if rider is not None else []))


def _mix_ab_bwd(x, f, dout, vp, h, sav, win4, cwa, wg, bg, cwb, v512, pavg, wout, res_w, name, carry=None, tm=256):
    S, D = x.shape
    W = win4.shape[-1]
    T = S // tm
    KA, KB, HA, HB = 4, 31, 8, 32

    def body(*refs):
        ins, outs, scr, carry_refs = _split_refs(refs, 15, 9, 8, carry)
        (x_ref, f_ref, do_ref, vp_ref, h_ref, sav_ref, prev_ref,
         win_ref, cwa_ref, wg_ref, bg_ref, cwb_ref, v_ref, p_ref, wo_ref) = ins
        hs_ref, axp_ref, axc_ref, bv_ref, vc_ref = (sav_ref.at[:, n * W:(n + 1) * W] for n in range(5))
        hsp_ref = prev_ref.at[HB - HA:, 0:W]
        axpp_ref = prev_ref.at[HB - HA:, W:2 * W]
        bvp_ref = prev_ref.at[:, 3 * W:4 * W]
        dx_ref, s_ref, dwin_out, dwo_out, dwg_out, dcwa_ref, dcwb_ref, dbg_ref, dv_ref = outs
        danext, dvnext, gfirst, afirst, dwin_ref, dwo_ref, dwg_ref, sem = scr
        t = pl.program_id(0)
        _carry_run(carry, carry_refs, t == 0, t == T - 1)

        @pl.when(t == 0)
        def _():
            for ref in (s_ref, dwin_ref, dwo_ref, dwg_ref, dcwa_ref, dcwb_ref, dbg_ref, dv_ref, danext, dvnext,
                        gfirst, afirst):
                ref[...] = jnp.zeros_like(ref)

        notfirst = jnp.where(t < T - 1, 1.0, 0.0).astype(F32)
        hv = h_ref[...]
        dov = do_ref[...]
        dfv = _shell_post_bwd(f_ref[...], dov, vp_ref, res_w, s_ref).astype(MXU_DT)
        a_gate = _dot(hv, win_ref[0])
        b_val = _dot(hv, win_ref[2])
        b_gate = _dot(hv, win_ref[3])
        rows = lax.broadcasted_iota(jnp.int32, (tm, W), 0)
        ge, dge = _gelu(a_gate)
        hsv = hs_ref[...]
        ya = hsv * ge
        vhat, rstd, vn = _group_norm(vc_ref[...], p_ref, v_ref[3:4, :], v_ref[4:5, :])
        sgn = _sig(vn)
        yb = vn * sgn
        dma = _dot_nt(dfv, wo_ref[0:W, :])
        dmb = _dot_nt(dfv, wo_ref[W:, :])
        dwo_ref[0:W, :] += _dot_tn(ya, dfv)
        dwo_ref[W:, :] += _dot_tn(yb, dfv)
        dhs = dma * ge
        d_a_gate = dma * hsv * dge
        axcv = axc_ref[...]
        lam = v_ref[1:2, :]
        r, i, ls, a, mult = _lru_gates(axcv, wg_ref, bg_ref, lam, W)
        ash = jnp.where(rows == tm - 1, afirst[0:1, :], _shift_up(a, 1))
        asuf, gloc = _scan_bwd(ash, dhs, rows)
        gsc = gloc + asuf * gfirst[0:1, :]
        afirst[...] = a[0:8, :]
        gfirst[...] = gsc[0:8, :]
        hprev = jnp.where(rows == 0, hsp_ref[HA - 1:HA, :] * notfirst, _shift_down(hsv, 1))
        da = gsc * hprev
        dL = da * a - gsc * (i * axcv) * (a * a) / mult
        dix = gsc * mult
        daxc = dix * i
        dr = dL * (LRU_C * ls)
        dv_ref[1:2, :] += _colsum(dL * r) * (LRU_C * _sig(-lam))
        dgate = jnp.concatenate([dr * r * (1.0 - r), (dix * axcv) * i * (1.0 - i)], axis=1)
        dbg_ref[0:1, :] += _colsum(dgate)
        dwg_ref[...] += _dot_tn(axcv, dgate)
        daxc = daxc + _dot_nt(dgate, wg_ref[...])
        daxp = _anticausal_conv(jnp.concatenate([daxc, danext[...]], axis=0), cwa_ref, KA, tm)
        ext = jnp.concatenate([axpp_ref[...] * notfirst, axp_ref[...]], axis=0)
        for k in range(KA):
            dcwa_ref[k:k + 1, :] += _colsum(daxc * _shift_down(ext, KA - 1 - k)[HA:, :])
        dv_ref[0:1, :] += _colsum(daxc)
        danext[...] = daxc[0:HA, :]
        dvn = dmb * (sgn * (1.0 + vn * (1.0 - sgn)))
        dv_ref[4:5, :] += _colsum(dvn)
        dv_ref[3:4, :] += _colsum(dvn * vhat)
        dvh = dvn * v_ref[3:4, :]
        dvc = rstd * (dvh - _group_mean(dvh, p_ref) - vhat * _group_mean(dvh * vhat, p_ref))
        dbv = _anticausal_conv(jnp.concatenate([dvc, dvnext[...]], axis=0), cwb_ref, KB, tm)
        ext = jnp.concatenate([bvp_ref[...] * notfirst, bv_ref[...]], axis=0)
        for k in range(KB):
            dcwb_ref[k:k + 1, :] += _colsum(dvc * _shift_down(ext, KB - 1 - k)[HB:, :])
        dv_ref[2:3, :] += _colsum(dvc)
        dvnext[...] = dvc[0:HB, :]
        sb = _sig(b_gate)
        dzs = (d_a_gate, daxp, dbv * sb, dbv * b_val * sb * (1.0 - sb))
        dh = None
        for s in range(4):
            part = _dot_nt(dzs[s], win_ref[s])
            dh = part if dh is None else dh + part
            dwin_ref[s] += _dot_tn(hv, dzs[s])
        dx_ref[...] = _shell_pre_bwd(x_ref[...], dh, dov, vp_ref, s_ref)

        @pl.when(t == T - 1)
        def _():
            _copy_out(((dwin_ref, dwin_out), (dwo_ref, dwo_out), (dwg_ref, dwg_out)), sem)

    tile = lambda w: pl.BlockSpec((tm, w), lambda t: (T - 1 - t, 0))
    prev = pl.BlockSpec((HB, 5 * W), lambda t: (jnp.maximum((T - 1 - t) * (tm // HB) - 1, 0), 0))
    out_shapes = [(S, D), (8, D), win4.shape, wout.shape, wg.shape, (8, W), (32, W), (8, 2 * W), (8, W)]
    c_in, c_out, c_shapes, c_sems, c_alias = _carry_args(carry, 15, 9)
    return pl.pallas_call(
        body, name=name, grid=(T,),
        in_specs=[tile(D), tile(D), tile(D), _whole((8, D)), tile(D), tile(5 * W), prev]
        + [_resident(a) for a in (win4, cwa, wg, bg, cwb, v512, pavg, wout)] + c_in,
        out_specs=[tile(D), _whole((8, D)), ANY, ANY, ANY] + [_whole(s) for s in out_shapes[5:]] + c_out,
        out_shape=[jax.ShapeDtypeStruct(s, F32) for s in out_shapes] + c_shapes,
        scratch_shapes=[pltpu.VMEM((HA, W), F32), pltpu.VMEM((HB, W), F32), pltpu.VMEM((8, W), F32),
                        pltpu.VMEM((8, W), F32), pltpu.VMEM(win4.shape, F32), pltpu.VMEM(wout.shape, F32),
                        pltpu.VMEM(wg.shape, F32), pltpu.SemaphoreType.DMA] + c_sems,
        input_output_aliases=c_alias, compiler_params=_cp(("arbitrary",), has_side_effects=carry is not None),
    )(x, f, dout, vp, h, sav, sav, win4, cwa, wg, bg, cwb, v512, pavg, wout,
      *(carry.srcs if carry is not None else []))


def _mix_c_core(hv, win_ref, v2_ref, v1_ref, ws_ref, bsb_ref, tm, D):
    zp = jnp.concatenate([_dot(hv, win_ref[s]) for s in range(4)], axis=1) + v2_ref[0:1, :]
    z, dz = _gelu(zp)
    u, v = z[:, :D], z[:, D:]
    mu = _rowmean(v)
    dv = v - mu
    rstd = lax.rsqrt(_rowmean(dv * dv) + EPS)
    vhat = dv * rstd
    vn = vhat * v1_ref[0:1, :] + v1_ref[1:2, :]
    rows_out = []
    for cidx in range(tm // CHUNK):
        blk = vn[cidx * CHUNK:(cidx + 1) * CHUNK, :]
        heads = [_dot(ws_ref[hd], blk[:, hd * CHUNK:(hd + 1) * CHUNK]) for hd in range(N_HEAD)]
        rows_out.append(jnp.concatenate(heads, axis=1) + bsb_ref[...])
    mixed = jnp.concatenate(rows_out, axis=0)
    return dz, u, rstd, vhat, vn, mixed


def _mix_c_fwd(x, vp, win4, v2d, v1d, ws, bsb, wout, res_w, name, tm=512):
    S, D = x.shape

    def body(x_ref, vp_ref, win_ref, v2_ref, v1_ref, ws_ref, bsb_ref, wo_ref, xo_ref, f_ref, h_ref):
        xv = x_ref[...]
        hv = _shell_pre(xv, vp_ref).astype(MXU_DT)
        h_ref[...] = hv
        _, u, _, _, _, mixed = _mix_c_core(hv, win_ref, v2_ref, v1_ref, ws_ref, bsb_ref, tm, D)
        fv = _dot(u * mixed, wo_ref[...])
        f_ref[...] = fv
        xo_ref[...] = _shell_post(xv, fv, vp_ref, res_w)

    tile = pl.BlockSpec((tm, D), lambda t: (t, 0))
    sd = jax.ShapeDtypeStruct
    return pl.pallas_call(
        body, name=name, grid=(S // tm,),
        in_specs=[tile, _whole((8, D))] + [_resident(a) for a in (win4, v2d, v1d, ws, bsb, wout)],
        out_specs=[tile, tile, tile],
        out_shape=[sd((S, D), F32), sd((S, D), F32), sd((S, D), MXU_DT)], compiler_params=_cp(("arbitrary",)),
    )(x, vp, win4, v2d, v1d, ws, bsb, wout)


def _mix_c_bwd(x, f, dout, vp, h, win4, v2d, v1d, ws, bsb, wout, res_w, name, carry=None, tm=256):
    S, D = x.shape

    def body(*refs):
        ins, outs, (dwin_ref, dwo_ref, sem), carry_refs = _split_refs(refs, 11, 8, 3, carry)
        x_ref, f_ref, do_ref, vp_ref, h_ref, win_ref, v2_ref, v1_ref, ws_ref, bsb_ref, wo_ref = ins
        dx_ref, s_ref, dwin_out, dwo_out, dws_ref, dbsb_ref, dv2_ref, dv1_ref = outs
        _carry_run(carry, carry_refs, pl.program_id(0) == 0, pl.program_id(0) == S // tm - 1)

        @pl.when(pl.program_id(0) == 0)
        def _():
            for ref in (s_ref, dwin_ref, dwo_ref, dws_ref, dbsb_ref, dv2_ref, dv1_ref):
                ref[...] = jnp.zeros_like(ref)

        hv = h_ref[...]
        dov = do_ref[...]
        dfv = _shell_post_bwd(f_ref[...], dov, vp_ref, res_w, s_ref).astype(MXU_DT)
        dz, u, rstd, vhat, vn, mixed = _mix_c_core(hv, win_ref, v2_ref, v1_ref, ws_ref, bsb_ref, tm, D)
        dp = _dot_nt(dfv, wo_ref[...])
        dwo_ref[...] += _dot_tn(u * mixed, dfv)
        du = dp * mixed
        dmx = dp * u
        rows_out = []
        for cidx in range(tm // CHUNK):
            dblk = dmx[cidx * CHUNK:(cidx + 1) * CHUNK, :]
            vblk = vn[cidx * CHUNK:(cidx + 1) * CHUNK, :]
            dbsb_ref[...] += dblk
            heads = []
            for hd in range(N_HEAD):
                dsl = dblk[:, hd * CHUNK:(hd + 1) * CHUNK]
                heads.append(_dot_tn(ws_ref[hd], dsl))
                dws_ref[hd] += _dot_nt(dsl, vblk[:, hd * CHUNK:(hd + 1) * CHUNK])
            rows_out.append(jnp.concatenate(heads, axis=1))
        dvn = jnp.concatenate(rows_out, axis=0)
        dv1_ref[1:2, :] += _colsum(dvn)
        dv1_ref[0:1, :] += _colsum(dvn * vhat)
        dvh = dvn * v1_ref[0:1, :]
        dv = rstd * (dvh - _rowmean(dvh) - vhat * _rowmean(dvh * vhat))
        dzp = jnp.concatenate([du, dv], axis=1) * dz
        dv2_ref[0:1, :] += _colsum(dzp)
        W = win_ref.shape[-1]
        dh = None
        for s in range(4):
            dzs = dzp[:, s * W:(s + 1) * W]
            part = _dot_nt(dzs, win_ref[s])
            dh = part if dh is None else dh + part
            dwin_ref[s] += _dot_tn(hv, dzs)
        dx_ref[...] = _shell_pre_bwd(x_ref[...], dh, dov, vp_ref, s_ref)

        @pl.when(pl.program_id(0) == S // tm - 1)
        def _():
            _copy_out(((dwin_ref, dwin_out), (dwo_ref, dwo_out)), sem)

    tile = pl.BlockSpec((tm, D), lambda t: (t, 0))
    out_shapes = [(S, D), (8, D), win4.shape, wout.shape, ws.shape, bsb.shape, (8, 2 * D), (8, D)]
    c_in, c_out, c_shapes, c_sems, c_alias = _carry_args(carry, 11, 8)
    return pl.pallas_call(
        body, name=name, grid=(S // tm,),
        in_specs=[tile, tile, tile, _whole((8, D)), tile] + [_resident(a) for a in (win4, v2d, v1d, ws, bsb, wout)]
        + c_in,
        out_specs=[tile, _whole((8, D)), ANY, ANY] + [_whole(s) for s in out_shapes[4:]] + c_out,
        out_shape=[jax.ShapeDtypeStruct(s, F32) for s in out_shapes] + c_shapes,
        scratch_shapes=[pltpu.VMEM(win4.shape, F32), pltpu.VMEM(wout.shape, F32), pltpu.SemaphoreType.DMA] + c_sems,
        input_output_aliases=c_alias, compiler_params=_cp(("arbitrary",), has_side_effects=carry is not None),
    )(x, f, dout, vp, h, win4, v2d, v1d, ws, bsb, wout, *(carry.srcs if carry is not None else []))


def _ada_fwd(c_all, ada_w, ada_b_my, name):
    L, D, N = ada_w.shape
    tn = N // 3

    def body(c_ref, w_ref, b_ref, o_ref):
        cv = c_ref[...]
        o_ref[...] = _dot_hi(cv * _sig(cv), w_ref[...]) + b_ref[...]

    return pl.pallas_call(
        body, name=name, grid=(L, 3),
        in_specs=[pl.BlockSpec((8, D), lambda l, n: (0, 0)), pl.BlockSpec((None, D, tn), lambda l, n: (l, 0, n)),
                  pl.BlockSpec((None, 1, tn), lambda l, n: (l, 0, n))],
        out_specs=pl.BlockSpec((None, 8, tn), lambda l, n: (l, 0, n)),
        out_shape=jax.ShapeDtypeStruct((L, 8, N), F32), compiler_params=_cp(("arbitrary", "arbitrary")),
    )(c_all, ada_w, ada_b_my)


def _ada_bwd(c_all, dmod_my, name):
    L, _, N = dmod_my.shape
    D = c_all.shape[1]
    tn = N // 3

    def body(c_ref, d_ref, o_ref):
        cv = c_ref[...]
        o_ref[...] = lax.dot_general(cv * _sig(cv), d_ref[...], (((0,), (0,)), ((), ())),
                                     precision=lax.Precision.HIGHEST, preferred_element_type=F32)

    return pl.pallas_call(
        body, name=name, grid=(L, 3),
        in_specs=[pl.BlockSpec((8, D), lambda l, n: (0, 0)), pl.BlockSpec((None, 8, tn), lambda l, n: (l, 0, n))],
        out_specs=pl.BlockSpec((None, D, tn), lambda l, n: (l, 0, n)),
        out_shape=jax.ShapeDtypeStruct((L, D, N), F32), compiler_params=_cp(("arbitrary", "arbitrary")),
    )(c_all, dmod_my)


def _row_tile(rows, cols, budget=1 << 20):
    best = 8
    for rt in range(8, rows + 1, 8):
        if rows % rt == 0 and rt * cols * 4 <= budget:
            best = rt
    return best


def _my_chip():
    return 2 * lax.axis_index("x") + lax.axis_index("y")


def _cast_place(w, g, name):
    _, R, C = w.shape
    rt = _row_tile(R, C)

    def body(w_ref, o_ref):
        o_ref[...] = w_ref[...].astype(o_ref.dtype)

    return pl.pallas_call(
        body, name=name, grid=(R // rt,),
        in_specs=[pl.BlockSpec((None, rt, C), lambda r: (g, r, 0))],
        out_specs=pl.BlockSpec((None, None, rt, C), lambda r: (0, _my_chip(), r, 0)),
        out_shape=jax.ShapeDtypeStruct((1, N_SHARD, R, C), MXU_DT), compiler_params=_cp(("arbitrary",)),
    )(w)


def _add_half(gk, la, name, out_dtype=F32):
    Gk, _, R, C = gk.shape
    Rh = R // 2
    n = Gk * N_SHARD
    gv = gk.reshape(n, 2, Rh, C)
    lv = la.reshape(n, Rh, C)
    rt = _row_tile(Rh, C)

    def body(g_ref, l_ref, o_ref):
        o_ref[...] = (g_ref[...] + l_ref[...]).astype(o_ref.dtype)

    out = pl.pallas_call(
        body, name=name, grid=(n, Rh // rt),
        in_specs=[pl.BlockSpec((None, None, rt, C), lambda i, r: (i, lax.axis_index("c"), r, 0)),
                  pl.BlockSpec((None, rt, C), lambda i, r: (i, r, 0))],
        out_specs=pl.BlockSpec((None, rt, C), lambda i, r: (i, r, 0)),
        out_shape=jax.ShapeDtypeStruct((n, Rh, C), out_dtype), compiler_params=_cp(("arbitrary", "arbitrary")),
    )(gv, lv)
    return out.reshape(Gk, N_SHARD, Rh, C)


def _sum_chips(part, landed, joined, g, name):
    _, _, Rh, C = part.shape
    rt = _row_tile(Rh, C)
    nb = Rh // rt

    def body(p_ref, l_ref, j_ref, o_ref):
        up = lambda v: v.astype(F32)
        o_ref[...] = ((up(p_ref[...]) + up(l_ref[0])) + up(l_ref[1])) + up(l_ref[2])

    return pl.pallas_call(
        body, name=name, grid=(nb,),
        in_specs=[pl.BlockSpec((None, None, rt, C), lambda r: (0, _my_chip(), r, 0)),
                  pl.BlockSpec((None, 3, rt, C), lambda r: (0, 0, r, 0)), ANY],
        out_specs=pl.BlockSpec((None, rt, C), lambda r: (g, lax.axis_index("c") * nb + r, 0)),
        out_shape=jax.ShapeDtypeStruct(joined.shape, F32), input_output_aliases={2: 0},
        compiler_params=_cp(("arbitrary",)),
    )(part, landed, joined)


def _sum_slots(lb, name):
    Gk, n, Rh, C = lb.shape
    rt = _row_tile(Rh, C)

    def body(l_ref, o_ref):
        acc = l_ref[0]
        for s in range(1, n):
            acc = acc + l_ref[s]
        o_ref[...] = acc

    return pl.pallas_call(
        body, name=name, grid=(Gk, Rh // rt),
        in_specs=[pl.BlockSpec((None, n, rt, C), lambda g, r: (g, 0, r, 0))],
        out_specs=pl.BlockSpec((None, rt, C), lambda g, r: (g, r, 0)),
        out_shape=jax.ShapeDtypeStruct((Gk, Rh, C), F32), compiler_params=_cp(("arbitrary", "arbitrary")),
    )(lb)


def _adamw(w, g, m, v, name):
    rows, cols = w.shape
    rt = _row_tile(rows, cols) if rows % 8 == 0 else rows
    c1 = 1.0 - ADAM_B1 ** ADAM_STEP
    c2 = 1.0 - ADAM_B2 ** ADAM_STEP

    def body(w_ref, g_ref, m_ref, v_ref, d_ref, mo_ref, vo_ref):
        gv = g_ref[...]
        mn = ADAM_B1 * m_ref[...] + (1.0 - ADAM_B1) * gv
        vn = ADAM_B2 * v_ref[...] + (1.0 - ADAM_B2) * (gv * gv)
        d_ref[...] = -ADAM_LR * ((mn / c1) / (jnp.sqrt(vn / c2) + ADAM_EPS) + ADAM_WD * w_ref[...])
        mo_ref[...] = mn
        vo_ref[...] = vn

    spec = pl.BlockSpec((rt, cols), lambda r: (r, 0))
    sds = jax.ShapeDtypeStruct((rows, cols), F32)
    return pl.pallas_call(
        body, name=name, grid=(rows // rt,), in_specs=[spec] * 4, out_specs=[spec] * 3,
        out_shape=[sds] * 3, compiler_params=_cp(("arbitrary",)),
    )(w, g, m, v)


def _coords():
    return lax.axis_index("x"), lax.axis_index("y"), lax.axis_index("c")


def _all_gather8(blk, name, carry=None):
    m_per, n = blk.shape

    def body(*refs):
        (x_ref,), (out_ref,), (send_sems, recv_sems, local_sem), carry_refs = _split_refs(refs, 1, 1, 3, carry)
        x, y, c = _coords()
        me, sibling = (x, y, c), (x, y, 1 - c)
        chips = [(1 - x, y), (x, 1 - y), (1 - x, 1 - y)]

        def rows(px, py, pc):
            return out_ref.at[pl.ds((4 * px + 2 * py + pc) * m_per, m_per), :]

        def copy(k, block, to, src=None):
            return pltpu.make_async_remote_copy(
                src_ref=rows(*block) if src is None else src, dst_ref=rows(*block),
                send_sem=send_sems.at[k], recv_sem=recv_sems.at[k], device_id=to, device_id_type=MESH)

        mine = pltpu.make_async_copy(x_ref, rows(*me), local_sem)
        mine.start()
        first = [copy(0, me, sibling, src=x_ref)]
        first += [copy(1 + j, me, (*chip, c), src=x_ref) for j, chip in enumerate(chips)]
        for cp in first:
            cp.start()
        riding = _carry_copies(carry, carry_refs)
        for cp in riding:
            cp.start()
        passed = [copy(4 + j, (*chip, c), sibling) for j, chip in enumerate(chips)]
        for j, chip in enumerate(chips):
            copy(1 + j, (*chip, c), me).wait_recv()
            passed[j].start()
        copy(0, sibling, me).wait_recv()
        for j, chip in enumerate(chips):
            copy(4 + j, (*chip, 1 - c), me).wait_recv()
        for cp in first + passed:
            cp.wait_send()
        mine.wait()
        for cp in riding:
            cp.wait()

    c_in, c_out, c_shapes, c_sems, c_alias = _carry_args(carry, 1, 1)
    res = pl.pallas_call(
        body, name=name, out_shape=[jax.ShapeDtypeStruct((N_DEV * m_per, n), blk.dtype)] + c_shapes,
        in_specs=[pl.BlockSpec(memory_space=pltpu.VMEM)] + c_in,
        out_specs=[pl.BlockSpec(memory_space=pltpu.VMEM)] + c_out,
        scratch_shapes=[pltpu.SemaphoreType.DMA((7,)), pltpu.SemaphoreType.DMA((7,)), pltpu.SemaphoreType.DMA] + c_sems,
        input_output_aliases=c_alias, compiler_params=_cp(),
    )(blk, *(carry.srcs if carry is not None else []))
    return res[0] if carry is None else res


def _comm_call(name, inputs, out_shapes, plan, n_remote, aliases=None):
    n_in, n_out = len(inputs), len(out_shapes)

    def body(*refs):
        in_refs, out_refs = refs[:n_in], refs[n_in:n_in + n_out]
        send_sems, recv_sems = refs[n_in + n_out:]

        def remote(k, src, dst, to):
            return pltpu.make_async_remote_copy(src_ref=src, dst_ref=dst, send_sem=send_sems.at[k],
                                                recv_sem=recv_sems.at[k], device_id=to, device_id_type=MESH)

        plan(in_refs, out_refs, remote)

    return pl.pallas_call(
        body, name=name, out_shape=out_shapes, in_specs=[ANY] * n_in, out_specs=[ANY] * n_out,
        scratch_shapes=[pltpu.SemaphoreType.DMA((n_remote,)), pltpu.SemaphoreType.DMA((n_remote,))],
        input_output_aliases=aliases or {}, compiler_params=_cp(has_side_effects=True),
    )(*inputs)


def _gather_ici_carry(placed):
    K = len(placed)

    def plan(ins, outs, remote):
        x, y, c = _coords()
        s_me = 2 * x + y
        cps = []
        for j, (px, py) in enumerate([(1 - x, y), (x, 1 - y), (1 - x, 1 - y)]):
            for k in range(K):
                rh = placed[k].shape[2] // 2
                own = outs[k].at[:, s_me, pl.ds(c * rh, rh), :]
                cps.append(remote(j * K + k, own, own, (px, py, c)))
        return cps

    shapes = [jax.ShapeDtypeStruct(p.shape, p.dtype) for p in placed]
    return _Carry(placed, shapes, {k: k for k in range(K)}, 3 * K, plan)


class _Exchange:
    def __init__(self, send, recv):
        self.send, self.recv = send, recv

    def start(self):
        self.send.start()

    def wait(self):
        self.send.wait_send()
        self.recv.wait_recv()


def _forward_carry(placed):
    K = len(placed)

    def plan(ins, outs, remote):
        x, y, c = _coords()
        cps = []
        for j, (px, py) in enumerate([(1 - x, y), (x, 1 - y), (1 - x, 1 - y)]):
            for k in range(K):
                rh = placed[k].shape[2] // 2
                landed = outs[k].at[:, 2 * px + py, pl.ds(c * rh, rh), :]
                other = outs[k].at[:, 2 * px + py, pl.ds((1 - c) * rh, rh), :]
                cps.append(_Exchange(remote(j * K + k, landed, landed, (x, y, 1 - c)),
                                     remote(j * K + k, other, other, (x, y, 1 - c))))
        return cps

    shapes = [jax.ShapeDtypeStruct(p.shape, p.dtype) for p in placed]
    return _Carry(placed, shapes, {k: k for k in range(K)}, 3 * K, plan)


def _gather_weights(placed, name):
    K = len(placed)

    def plan(ins, outs, remote):
        x, y, c = _coords()
        s_me = 2 * x + y
        sibling = (x, y, 1 - c)
        chips = [(1 - x, y), (x, 1 - y), (1 - x, 1 - y)]
        half = lambda k, s, cc: outs[k].at[:, s, pl.ds(cc * (placed[k].shape[2] // 2), placed[k].shape[2] // 2), :]
        sent = []
        for j, (px, py) in enumerate(chips):
            for k in range(K):
                own = half(k, s_me, c)
                cp = remote(j * K + k, own, own, (px, py, c))
                cp.start()
                sent.append(cp)
        for j, (px, py) in enumerate(chips):
            s_from = 2 * px + py
            for k in range(K):
                landed = half(k, s_from, c)
                remote(j * K + k, landed, landed, (px, py, c)).wait_recv()
                cp = remote((3 + j) * K + k, landed, landed, sibling)
                cp.start()
                sent.append(cp)
        for j, (px, py) in enumerate(chips):
            s_from = 2 * px + py
            for k in range(K):
                other = half(k, s_from, 1 - c)
                remote((3 + j) * K + k, other, other, sibling).wait_recv()
        for cp in sent:
            cp.wait_send()

    out_shapes = [jax.ShapeDtypeStruct(p.shape, p.dtype) for p in placed]
    return _comm_call(name, placed, out_shapes, plan, 6 * K, aliases={k: k for k in range(K)})


def _send_other_half(grads, name):
    K = len(grads)

    def plan(ins, outs, remote):
        x, y, c = _coords()
        cps = []
        for k in range(K):
            rh = grads[k].shape[2] // 2
            cps.append(remote(k, ins[k].at[:, :, pl.ds((1 - c) * rh, rh), :], outs[k], (x, y, 1 - c)))
        for cp in cps:
            cp.start()
        for cp in cps:
            cp.wait()

    out_shapes = [jax.ShapeDtypeStruct(g.shape[:2] + (g.shape[2] // 2, g.shape[3]), g.dtype) for g in grads]
    return _comm_call(name, grads, out_shapes, plan, K)


def _send_half_carry(grads):
    K = len(grads)

    def plan(ins, outs, remote):
        x, y, c = _coords()
        cps = []
        for k in range(K):
            rh = grads[k].shape[2] // 2
            cps.append(remote(k, ins[k].at[:, :, pl.ds((1 - c) * rh, rh), :], outs[k], (x, y, 1 - c)))
        return cps

    out_shapes = [jax.ShapeDtypeStruct(g.shape[:2] + (g.shape[2] // 2, g.shape[3]), g.dtype) for g in grads]
    return _Carry(grads, out_shapes, {}, K, plan)


def _merge_carries(carries):
    if not carries:
        return None
    if len(carries) == 1:
        return carries[0]
    srcs, shapes, aliases = [], [], {}
    for cy in carries:
        aliases.update({len(srcs) + i: len(shapes) + o for i, o in cy.aliases.items()})
        srcs += cy.srcs
        shapes += cy.out_shapes

    def plan(ins, outs, remote):
        cps, i0, o0, k0 = [], 0, 0, 0
        for cy in carries:
            shifted = functools.partial(lambda k, src, dst, to, base: remote(base + k, src, dst, to), base=k0)
            cps += cy.plan(ins[i0:i0 + len(cy.srcs)], outs[o0:o0 + len(cy.out_shapes)], shifted)
            i0, o0, k0 = i0 + len(cy.srcs), o0 + len(cy.out_shapes), k0 + cy.n
        return cps

    return _Carry(srcs, shapes, aliases, sum(cy.n for cy in carries), plan)


def _scatter_carry(parts):
    K = len(parts)

    def plan(ins, outs, remote):
        x, y, c = _coords()
        chips = [(1 - x, y), (x, 1 - y), (1 - x, 1 - y)]
        return [remote(j * K + k, ins[k].at[:, 2 * px + py], outs[k].at[:, j], (px, py, c))
                for j, (px, py) in enumerate(chips) for k in range(K)]

    out_shapes = [jax.ShapeDtypeStruct((p.shape[0], 3) + p.shape[2:], p.dtype) for p in parts]
    return _Carry(parts, out_shapes, {}, 3 * K, plan)


def _join_halves(joined):
    K = len(joined)

    def plan(ins, outs, remote):
        x, y, c = _coords()
        cps = []
        for k in range(K):
            rh = joined[k].shape[1] // 2
            mine = outs[k].at[:, pl.ds(c * rh, rh), :]
            cps.append(remote(k, mine, mine, (x, y, 1 - c)))
        for cp in cps:
            cp.start()
        for k in range(K):
            rh = joined[k].shape[1] // 2
            other = outs[k].at[:, pl.ds((1 - c) * rh, rh), :]
            remote(k, other, other, (x, y, 1 - c)).wait_recv()
        for cp in cps:
            cp.wait_send()

    out_shapes = [jax.ShapeDtypeStruct(h.shape, h.dtype) for h in joined]
    return _comm_call("join_halves", joined, out_shapes, plan, K, aliases={k: k for k in range(K)})


def _pack(parts):
    flat = []
    for p in parts:
        v = p.reshape(-1).astype(F32)
        pad = (-v.shape[0]) % 1024
        flat.append(jnp.pad(v, (0, pad)) if pad else v)
    return jnp.concatenate(flat).reshape(-1, 128)


def _unpack(packed, shapes):
    flat = packed.reshape(-1)
    out, off = [], 0
    for shp in shapes:
        n = math.prod(shp)
        out.append(flat[off:off + n].reshape(shp))
        off += n + (-n) % 1024
    return out


def _shard_last(a, s, n):
    return lax.dynamic_slice_in_dim(a, s * n, n, axis=a.ndim - 1)


def _rows8(*vecs):
    n = vecs[0].shape[-1]
    rows = [v.reshape(1, n).astype(F32) for v in vecs]
    return jnp.concatenate(rows + [jnp.zeros((8 - len(rows), n), F32)], axis=0)


def kernel(x, c, ada_w, ada_b, norm_pre, norm_post, ffn_w13, ffn_w2, ab_w_in, a_conv_w, a_conv_b, a_gate_w, a_gate_b, a_lam, b_conv_w, b_conv_b, b_norm_g, b_norm_b, ab_w_out, c_w_in, c_b_in, c_norm_g, c_norm_b, c_w_s, c_b_s, c_w_out, loss_target, m_ada_w, m_ada_b, m_norm_pre, m_norm_post, m_ffn_w13, m_ffn_w2, m_ab_w_in, m_a_conv_w, m_a_conv_b, m_a_gate_w, m_a_gate_b, m_a_lam, m_b_conv_w, m_b_conv_b, m_b_norm_g, m_b_norm_b, m_ab_w_out, m_c_w_in, m_c_b_in, m_c_norm_g, m_c_norm_b, m_c_w_s, m_c_b_s, m_c_w_out, v_ada_w, v_ada_b, v_norm_pre, v_norm_post, v_ffn_w13, v_ffn_w2, v_ab_w_in, v_a_conv_w, v_a_conv_b, v_a_gate_w, v_a_gate_b, v_a_lam, v_b_conv_w, v_b_conv_b, v_b_norm_g, v_b_norm_b, v_ab_w_out, v_c_w_in, v_c_b_in, v_c_norm_g, v_c_norm_b, v_c_w_s, v_c_b_s, v_c_w_out):
    S, D = x.shape[1], x.shape[2]
    W = a_lam.shape[-1]
    Fh = ffn_w13.shape[-1]
    Fq = ffn_w2.shape[2]
    xi, yi, ci = _coords()
    shard = 2 * xi + yi
    me = 4 * xi + 2 * yi + ci
    x2, tgt = x[0], loss_target[0]

    sharded_small = [norm_pre, norm_post, a_conv_w, b_conv_w, c_b_in, c_norm_g, c_norm_b]
    gathered = _all_gather8(_pack([c] + sharded_small), "gather_small")
    blocks = gathered.reshape(N_DEV, -1, 128)
    per_dev = [_unpack(blocks[d], [c.shape] + [p.shape for p in sharded_small]) for d in range(0, N_DEV, 2)]
    c_all = jnp.concatenate([_unpack(blocks[d], [c.shape])[0] for d in range(N_DEV)], axis=0)
    npre, npost, acw, bcw, cbin, cng, cnb = [jnp.concatenate([per_dev[s][1 + i] for s in range(N_SHARD)], axis=-1)
                                             for i in range(len(sharded_small))]

    ada_b_my = _shard_last(ada_b, shard, ada_w.shape[-1])[:, None, :]
    modp = _ada_fwd(c_all, ada_w, ada_b_my, "ada_fwd")
    modg = _all_gather8(modp.reshape(16, -1), "gather_mod").reshape(N_DEV, 2, 8, -1)
    mod_me = lax.dynamic_index_in_dim(modg[0::2], me, axis=2, keepdims=False)
    mod = jnp.transpose(mod_me, (1, 0, 2)).reshape(2, 3, 3, D)

    w13s, w2s = ffn_w13.reshape(4, D, Fh), ffn_w2.reshape(4, Fq, D)
    placed13 = [_cast_place(w13s, g, f"cast_w13_{g}") for g in range(4)]
    placed2 = [_cast_place(w2s, g, f"cast_w2_{g}") for g in range(4)]
    placed_mix = [_cast_place(w, 0, f"cast_mix{k}") for k, w in enumerate((ab_w_in, ab_w_out, c_w_in, c_w_out))]
    w13_first, w2_first, abin_g, about_g = _gather_weights([placed13[0], placed2[0]] + placed_mix[:2], "gather_first")
    gather_carry = _gather_ici_carry(placed13[1:] + placed2[1:] + placed_mix[2:])

    eye = jnp.eye(8, dtype=F32)
    dh_a = W // 8
    blockdiag = lambda w: jnp.einsum("hde,hg->hdge", w, eye).reshape(W, W)
    gw = a_gate_w[0]
    wg = jnp.concatenate([blockdiag(gw[:, :, :dh_a]), blockdiag(gw[:, :, dh_a:])], axis=1).astype(MXU_DT)
    bgv = jnp.concatenate([a_gate_b[0][:, :dh_a].reshape(-1), a_gate_b[0][:, dh_a:].reshape(-1)])
    bg = _rows8(bgv)
    cwa = jnp.concatenate([acw[0], jnp.zeros((4, W), F32)], axis=0)
    cwb = jnp.concatenate([bcw[0], jnp.zeros((1, W), F32)], axis=0)
    v512 = _rows8(a_conv_b[0], a_lam[0], b_conv_b[0], b_norm_g[0], b_norm_b[0])
    dg_b = W // 8
    gid = jnp.arange(W) // dg_b
    member = (gid[:, None] == jnp.arange(128)[None, :]).astype(F32)
    pavg = jnp.stack([member / dg_b, member]).astype(MXU_DT)
    v2d = _rows8(cbin[0])
    v1d = _rows8(cng[0], cnb[0])
    tril = jnp.tril(jnp.ones((CHUNK, CHUNK), dtype=bool))
    ws = jnp.where(tril, c_w_s[0], 0.0).astype(MXU_DT)
    bsb = jnp.repeat(jnp.transpose(c_b_s[0]), D // N_HEAD, axis=1)

    res_ws = (0.5, 1.0, 0.5)
    vps, xs, saved = [], [], []
    xc = x2
    w13g, w2v = [w13_first], [w2_first.reshape(1, 2, Fh, D)]
    for l in range(2):
        for j in range(3):
            k = 3 * l + j
            vp = _rows8(npre[l, j], mod[l, j, 0], mod[l, j, 1], mod[l, j, 2], npost[l, j])
            vps.append(vp)
            xs.append(xc)
            gi = 2 * l + j // 2
            if k == 0:
                xc, *keep = _ffn_fwd(xc, vp, w13g[0], w2v[0], res_ws[j], "ffn_fwd0", carry=gather_carry)
                keep, landed = keep[:4], keep[4:]
            elif k == 1:
                ab_ops = (abin_g[0], cwa, wg, bg, cwb, v512, pavg, about_g.reshape(D, D))
                xc, *keep = _mix_ab_fwd(xc, vp, *ab_ops, res_ws[j], "mix_ab_fwd", carry=_forward_carry(landed))
                keep, full = keep[:3], keep[3:]
                w13g += list(full[0:3])
                w2v += [w.reshape(1, 2, Fh, D) for w in full[3:6]]
                cin_g, cout_g = full[6:]
                c_ops = (cin_g[0], v2d, v1d, ws, bsb, cout_g.reshape(D, D))
            elif k == 5:
                dout, *keep, lrow = _ffn_fwd(xc, vp, w13g[gi], w2v[gi], res_ws[j], f"ffn_fwd{k}", tgt=tgt)
            elif j != 1:
                xc, *keep = _ffn_fwd(xc, vp, w13g[gi], w2v[gi], res_ws[j], f"ffn_fwd{k}")
            else:
                xc, *keep = _mix_c_fwd(xc, vp, *c_ops, res_ws[j], "mix_c_fwd")
            saved.append(keep)

    joined = {"w13": lax.empty((4, D, Fh), F32), "w2": lax.empty((4, Fq, D), F32), "abin": lax.empty((1, D, 4 * W // 4), F32),
              "about": lax.empty((1, D // 4, D), F32), "cin": lax.empty((1, D, 2 * D // 4), F32),
              "cout": lax.empty((1, D // 4, D), F32)}

    stage = {"send": None, "scatter": []}

    def add_halves(group, got, dtype=F32):
        grads, keys, g, tag = group
        parts = [_add_half(gr, la, f"add_half_{tag}{i}", dtype) for i, (gr, la) in enumerate(zip(grads, got))]
        return parts, keys, g

    def sum_landed(groups, landed_slots):
        landed_slots = list(landed_slots)
        for parts, keys, g in groups:
            for part, key in zip(parts, keys):
                joined[key] = _sum_chips(part, landed_slots.pop(0), joined[key], g, f"sum_chips_{key}{g}")
        return landed_slots

    def all_parts(groups):
        return [part for group in groups for part in group[0]]

    def ride(call, n_own, carries_ok=True):
        if not carries_ok:
            return call(None)[:n_own]
        scat, send = stage["scatter"], stage["send"]
        carries = ([_scatter_carry(all_parts(scat))] if scat else []) + (
            [_send_half_carry(send[0])] if send is not None else [])
        res = call(_merge_carries(carries))
        own, extra = res[:n_own], res[n_own:]
        stage["scatter"], stage["send"] = [], None
        extra = sum_landed(scat, extra)
        if send is not None:
            stage["scatter"].append(add_halves(send, extra))
        return own

    d_npre = [[None] * 3 for _ in range(2)]
    d_npost = [[None] * 3 for _ in range(2)]
    d_mod = [[None] * 3 for _ in range(2)]
    for l in (1, 0):
        for j in (2, 1, 0):
            k = 3 * l + j
            if j != 1:
                f, h, gpre, upre = saved[k]
                gi = 2 * l + j // 2
                dw13 = lax.empty((1, 4, D, Fh), F32)
                dw2v = lax.empty((1, 2, Fh, D), F32)
                df, dh0, s_a, dw13, dw2v = ride(lambda cy: _ffn_bwd_half(
                    0, xs[k], f, dout, vps[k], h, gpre, upre, w13g[gi], w2v[gi], dw13, dw2v, res_ws[j],
                    f"ffn_bwd{k}a", carry=cy), 5)
                dout, s_b, dw13, dw2v = ride(lambda cy: _ffn_bwd_half(
                    1, xs[k], f, dout, vps[k], h, gpre, upre, w13g[gi], w2v[gi], dw13, dw2v, res_ws[j],
                    f"ffn_bwd{k}b", df=df, dh0=dh0, carry=cy), 4, carries_ok=k > 0)
                sums = s_a + s_b
                made = ([dw13, dw2v.reshape(1, 4, Fq, D)], ["w13", "w2"], gi, f"ffn{gi}")
            elif l == 0:
                dout, sums, d_abin, d_about, d_wg, d_cwa, d_cwb, d_bg, d_v512 = ride(lambda cy: _mix_ab_bwd(
                    xs[k], saved[k][0], dout, vps[k], *saved[k][1:], *ab_ops, res_ws[j], "mix_ab_bwd", carry=cy), 9)
                made = ([d_abin[None], d_about.reshape(1, 4, D // 4, D)], ["abin", "about"], 0, "ab")
                stage["scatter"].append(add_halves(made, _send_other_half(made[0], "send_half_ab")))
                made = None
            else:
                dout, sums, d_cin, d_cout, d_ws, d_bsb, d_v2, d_v1 = ride(lambda cy: _mix_c_bwd(
                    xs[k], saved[k][0], dout, vps[k], saved[k][1], *c_ops, res_ws[j], "mix_c_bwd", carry=cy), 8)
                made = ([d_cin[None], d_cout.reshape(1, 4, D // 4, D)], ["cin", "cout"], 0, "c")
            stage["send"] = made
            d_npre[l][j], d_npost[l][j] = sums[4], sums[1]
            d_mod[l][j] = jnp.stack([sums[2], sums[3], sums[0]])
    last = stage["scatter"] + [add_halves(stage["send"], _send_other_half(stage["send"][0], "send_half_last"), MXU_DT)]
    grad_x = dout[None]

    dmod = jnp.stack([jnp.stack(d_mod[l]) for l in range(2)]).reshape(2, 9 * D)
    d_gate_w = jnp.concatenate([jnp.einsum("hdhe->hde", d_wg[:, :W].reshape(8, dh_a, 8, dh_a)),
                                jnp.einsum("hdhe->hde", d_wg[:, W:].reshape(8, dh_a, 8, dh_a))], axis=-1)
    d_gate_b = jnp.concatenate([d_bg[0, :W].reshape(8, dh_a), d_bg[0, W:].reshape(8, dh_a)], axis=-1)
    small_grads = [
        dmod, jnp.stack([jnp.stack(r) for r in d_npre]), jnp.stack([jnp.stack(r) for r in d_npost]),
        d_cwa[:4][None], d_v512[0][None], d_gate_w[None], d_gate_b[None], d_v512[1][None], d_cwb[:31][None],
        d_v512[2][None], d_v512[3][None], d_v512[4][None], d_v2[0][None], d_v1[0][None], d_v1[1][None],
        jnp.where(tril, d_ws, 0.0)[None], jnp.transpose(d_bsb.reshape(CHUNK, N_HEAD, D // N_HEAD).sum(-1))[None]]
    small_w = [ada_b, norm_pre, norm_post, a_conv_w, a_conv_b, a_gate_w, a_gate_b, a_lam, b_conv_w, b_conv_b,
               b_norm_g, b_norm_b, c_b_in, c_norm_g, c_norm_b, c_w_s, c_b_s]
    small_m = [m_ada_b, m_norm_pre, m_norm_post, m_a_conv_w, m_a_conv_b, m_a_gate_w, m_a_gate_b, m_a_lam, m_b_conv_w,
               m_b_conv_b, m_b_norm_g, m_b_norm_b, m_c_b_in, m_c_norm_g, m_c_norm_b, m_c_w_s, m_c_b_s]
    small_v = [v_ada_b, v_norm_pre, v_norm_post, v_a_conv_w, v_a_conv_b, v_a_gate_w, v_a_gate_b, v_a_lam, v_b_conv_w,
               v_b_conv_b, v_b_norm_g, v_b_norm_b, v_c_b_in, v_c_norm_g, v_c_norm_b, v_c_w_s, v_c_b_s]
    full_shapes = [g.shape for g in small_grads]
    loss_part = jnp.sum(lrow[0]).reshape(1, 1)
    sg_all, *landed_last = _all_gather8(_pack(small_grads + [loss_part]), "gather_small_grads",
                                        carry=_scatter_carry(all_parts(last)))
    sum_landed(last, landed_last)
    sg_all = sg_all.reshape(N_DEV, -1, 128)
    sg_sum = _sum_slots(sg_all[None], "sum_small_grads")[0]
    *g_full, loss_sum = _unpack(sg_sum, full_shapes + [(1, 1)])
    loss = loss_sum[0, 0]
    g_small = [g if g.shape == w.shape else _shard_last(g, shard, w.shape[-1]) for g, w in zip(g_full, small_w)]
    small_shapes = [w.shape for w in small_w]
    d_s, m_s, v_s = _adamw(_pack(small_w), _pack(g_small), _pack(small_m), _pack(small_v), "adamw_small")
    delta_small, newm_small, newv_small = (_unpack(a, small_shapes) for a in (d_s, m_s, v_s))

    dmod_all = jnp.stack([_unpack(sg_all[d], full_shapes[:1])[0] for d in range(N_DEV)], axis=1)
    n_ada = ada_w.shape[-1]
    g_ada_w = _ada_bwd(c_all, _shard_last(dmod_all, shard, n_ada), "ada_bwd")

    g_big = _join_halves([joined[key] for key in ("w13", "w2", "abin", "about", "cin", "cout")])

    big_w = [ffn_w13, ffn_w2, ab_w_in, ab_w_out, c_w_in, c_w_out, ada_w]
    big_m = [m_ffn_w13, m_ffn_w2, m_ab_w_in, m_ab_w_out, m_c_w_in, m_c_w_out, m_ada_w]
    big_v = [v_ffn_w13, v_ffn_w2, v_ab_w_in, v_ab_w_out, v_c_w_in, v_c_w_out, v_ada_w]
    big_g = [g.reshape(w.shape) for g, w in zip(list(g_big) + [g_ada_w], big_w)]
    big_out = []
    for k, (w, g, m, v) in enumerate(zip(big_w, big_g, big_m, big_v)):
        two_d = lambda a: a.reshape(-1, a.shape[-1])
        res = _adamw(two_d(w), two_d(g), two_d(m), two_d(v), f"adamw_big{k}")
        big_out.append([r.reshape(w.shape) for r in res])

    names = ["ada_w", "ada_b", "norm_pre", "norm_post", "ffn_w13", "ffn_w2", "ab_w_in", "a_conv_w", "a_conv_b",
             "a_gate_w", "a_gate_b", "a_lam", "b_conv_w", "b_conv_b", "b_norm_g", "b_norm_b", "ab_w_out", "c_w_in",
             "c_b_in", "c_norm_g", "c_norm_b", "c_w_s", "c_b_s", "c_w_out"]
    big_names = ["ffn_w13", "ffn_w2", "ab_w_in", "ab_w_out", "c_w_in", "c_w_out", "ada_w"]
    small_names = ["ada_b", "norm_pre", "norm_post", "a_conv_w", "a_conv_b", "a_gate_w", "a_gate_b", "a_lam",
                   "b_conv_w", "b_conv_b", "b_norm_g", "b_norm_b", "c_b_in", "c_norm_g", "c_norm_b", "c_w_s", "c_b_s"]
    table = {}
    for k, n in enumerate(big_names):
        table[n] = (big_g[k], *big_out[k])
    for k, n in enumerate(small_names):
        table[n] = (g_small[k], delta_small[k], newm_small[k], newv_small[k])
    outs = [loss, grad_x]
    for field in range(4):
        outs += [table[n][field] for n in names]
    return tuple(outs)
```

```python
import functools
import math

import jax
import jax.numpy as jnp
from jax import lax
from jax.experimental import pallas as pl
from jax.experimental.pallas import tpu as pltpu

F32 = jnp.float32
MXU_DT = jnp.bfloat16
EPS = 1e-6
LRU_C = 8.0
N_SHARD = 4
N_DEV = 8
CHUNK = 128
N_HEAD = 8
ADAM_LR, ADAM_B1, ADAM_B2, ADAM_EPS, ADAM_WD, ADAM_STEP = 0.001, 0.9, 0.999, 1e-08, 0.01, 10
GELU_K0 = math.sqrt(2.0 / math.pi)
GELU_K1 = 0.044715
VMEM_LIMIT = 58 * 1024 * 1024
MESH = pl.DeviceIdType.MESH
ANY = pl.BlockSpec(memory_space=pl.ANY)


def _cp(sem=None, **kw):
    if sem is not None:
        kw["dimension_semantics"] = sem
    return pltpu.CompilerParams(vmem_limit_bytes=VMEM_LIMIT, **kw)


def _resident(a):
    return pl.BlockSpec(a.shape, lambda *_: (0,) * a.ndim, pipeline_mode=pl.Buffered(1))


def _whole(shape):
    return pl.BlockSpec(shape, lambda *_: (0,) * len(shape))


def _dot(a, b):
    return jnp.dot(a.astype(MXU_DT), b.astype(MXU_DT), preferred_element_type=F32)


def _dot_nt(a, b):
    return lax.dot_general(a.astype(MXU_DT), b.astype(MXU_DT), (((1,), (1,)), ((), ())), preferred_element_type=F32)


def _dot_tn(a, b):
    return lax.dot_general(a.astype(MXU_DT), b.astype(MXU_DT), (((0,), (0,)), ((), ())), preferred_element_type=F32)


def _dot_hi(a, b):
    return jnp.dot(a, b, precision=lax.Precision.HIGHEST, preferred_element_type=F32)


def _sig(x):
    return 1.0 / (1.0 + jnp.exp(-x))


def _logsig(x):
    return jnp.minimum(x, 0.0) - jnp.log(1.0 + jnp.exp(-jnp.abs(x)))


def _gelu(x):
    x2 = x * x
    t = jnp.tanh(GELU_K0 * (x + GELU_K1 * x * x2))
    val = 0.5 * x * (1.0 + t)
    der = 0.5 * (1.0 + t) + 0.5 * x * (1.0 - t * t) * (GELU_K0 * (1.0 + 3.0 * GELU_K1 * x2))
    return val, der


def _neg_expm1(x):
    small = -(x * (1.0 + x * (0.5 + x * (1.0 / 6.0 + x * (1.0 / 24.0)))))
    return jnp.where(x > -0.05, small, 1.0 - jnp.exp(x))


def _colsum(v):
    return jnp.sum(v, axis=0, keepdims=True)


def _rowmean(v):
    return jnp.mean(v, axis=-1, keepdims=True)


def _copy_out(pairs, sem):
    for src, dst in pairs:
        cp = pltpu.make_async_copy(src, dst, sem)
        cp.start()
        cp.wait()


class _Carry:
    def __init__(self, srcs, out_shapes, aliases, n, plan):
        self.srcs, self.out_shapes, self.aliases, self.n, self.plan = list(srcs), list(out_shapes), aliases, n, plan


def _carry_args(carry, n_in, n_out):
    if carry is None:
        return [], [], [], [], {}
    sems = [pltpu.SemaphoreType.DMA((carry.n,)), pltpu.SemaphoreType.DMA((carry.n,))]
    aliases = {n_in + i: n_out + o for i, o in carry.aliases.items()}
    return [ANY] * len(carry.srcs), [ANY] * len(carry.out_shapes), carry.out_shapes, sems, aliases


def _split_refs(refs, n_in, n_out, n_scratch, carry):
    nci, nco = (len(carry.srcs), len(carry.out_shapes)) if carry is not None else (0, 0)
    cuts = [n_in, nci, n_out, nco, n_scratch]
    parts, i = [], 0
    for n in cuts:
        parts.append(refs[i:i + n])
        i += n
    ins, cins, outs, couts, scr = parts
    return ins, outs, scr, (cins, couts, refs[i:])


def _carry_copies(carry, carry_refs):
    if carry is None:
        return []
    cins, couts, (send_sems, recv_sems) = carry_refs

    def remote(k, src, dst, to):
        return pltpu.make_async_remote_copy(src_ref=src, dst_ref=dst, send_sem=send_sems.at[k],
                                            recv_sem=recv_sems.at[k], device_id=to, device_id_type=MESH)

    return carry.plan(cins, couts, remote)


def _carry_run(carry, carry_refs, first, last):
    if carry is None:
        return

    @pl.when(first)
    def _():
        for cp in _carry_copies(carry, carry_refs):
            cp.start()

    @pl.when(last)
    def _():
        for cp in _carry_copies(carry, carry_refs):
            cp.wait()


def _shift_down(v, k):
    return v if k == 0 else pltpu.roll(v, k, 0)


def _shift_up(v, k):
    return v if k == 0 else pltpu.roll(v, v.shape[0] - k, 0)


def _shell_pre(xv, vp_ref):
    r = lax.rsqrt(_rowmean(xv * xv) + EPS)
    return xv * r * (vp_ref[0:1, :] * (1.0 + vp_ref[2:3, :])) + vp_ref[1:2, :]


def _shell_post(xv, fv, vp_ref, res_w):
    r = lax.rsqrt(_rowmean(fv * fv) + EPS)
    return xv + fv * r * (res_w * (1.0 + vp_ref[3:4, :]) * vp_ref[4:5, :])


def _shell_post_bwd(fv, dov, vp_ref, res_w, s_ref, keep=1.0):
    r = lax.rsqrt(_rowmean(fv * fv) + EPS)
    fn = fv * r
    pg = vp_ref[4:5, :]
    dy = (res_w * (1.0 + vp_ref[3:4, :])) * dov
    if s_ref is not None:
        s_ref[0:1, :] += _colsum(res_w * fn * pg * dov) * keep
        s_ref[1:2, :] += _colsum(fn * dy) * keep
    q = dy * pg
    return r * (q - fn * _rowmean(fn * q))


def _shell_pre_bwd(xv, dh, dov, vp_ref, s_ref, keep=1.0):
    r = lax.rsqrt(_rowmean(xv * xv) + EPS)
    xn = xv * r
    pg = vp_ref[0:1, :]
    sc1 = 1.0 + vp_ref[2:3, :]
    s_ref[2:3, :] += _colsum(dh) * keep
    s_ref[3:4, :] += _colsum(xn * pg * dh) * keep
    s_ref[4:5, :] += _colsum(xn * dh * sc1) * keep
    q = dh * (sc1 * pg)
    return dov + r * (q - xn * _rowmean(xn * q))


MXU_COLS = 256


def _col_chunks(n, width=2 * MXU_COLS):
    return [(c0, min(c0 + width, n)) for c0 in range(0, n, width)]


def _ffn_fwd(x, vp, w13g, w2v, res_w, name, carry=None, tgt=None, tm=512):
    S, D = x.shape
    Fh = w13g.shape[-1]
    T = S // tm
    head = tgt is not None
    n_in, n_out = (5, 6) if head else (4, 5)

    def body(*refs):
        ins, outs, _, carry_refs = _split_refs(refs, n_in, n_out, 0, carry)
        x_ref, vp_ref, w13_ref, w2_ref = ins[:4]
        xo_ref, f_ref, h_ref, g_ref, u_ref = outs[:5]
        _carry_run(carry, carry_refs, pl.program_id(0) == 0, pl.program_id(0) == T - 1)
        xv = x_ref[...]
        hb = _shell_pre(xv, vp_ref).astype(MXU_DT)
        h_ref[...] = hb
        acc = None
        for j in range(2):
            gg = _dot(hb, w13_ref[j])
            uu = _dot(hb, w13_ref[2 + j])
            g_ref[:, j * Fh:(j + 1) * Fh] = gg.astype(g_ref.dtype)
            u_ref[:, j * Fh:(j + 1) * Fh] = uu.astype(u_ref.dtype)
            part = _dot(gg * _sig(gg) * uu, w2_ref[j])
            acc = part if acc is None else acc + part
        f_ref[...] = acc
        xo = _shell_post(xv, acc, vp_ref, res_w)
        if head:
            l_ref = outs[5]

            @pl.when(pl.program_id(0) == 0)
            def _():
                l_ref[...] = jnp.zeros_like(l_ref)

            e = xo - ins[4][...]
            xo_ref[...] = e * (1.0 / D)
            l_ref[0:1, :] += _colsum(e * e) * (0.5 / D)
        else:
            xo_ref[...] = xo

    tile = lambda w: pl.BlockSpec((tm, w), lambda t: (t, 0))
    sd = jax.ShapeDtypeStruct
    c_in, c_out, c_shapes, c_sems, c_alias = _carry_args(carry, n_in, n_out)
    return pl.pallas_call(
        body, name=name, grid=(T,),
        in_specs=[tile(D), _whole((8, D)),
                  pl.BlockSpec((None, 4, D, Fh), lambda t: (0, 0, 0, 0), pipeline_mode=pl.Buffered(1)),
                  pl.BlockSpec((None, 2, Fh, D), lambda t: (0, 0, 0, 0), pipeline_mode=pl.Buffered(1))]
        + ([tile(D)] if head else []) + c_in,
        out_specs=[tile(D), tile(D), tile(D), tile(2 * Fh), tile(2 * Fh)] + ([_whole((8, D))] if head else []) + c_out,
        out_shape=[sd((S, D), F32), sd((S, D), F32), sd((S, D), MXU_DT), sd((S, 2 * Fh), MXU_DT),
                   sd((S, 2 * Fh), MXU_DT)] + ([sd((8, D), F32)] if head else []) + c_shapes,
        scratch_shapes=c_sems, input_output_aliases=c_alias,
        compiler_params=_cp(("arbitrary",), has_side_effects=carry is not None),
    )(x, vp, w13g, w2v, *([tgt] if head else []), *(carry.srcs if carry is not None else []))


def _ffn_bwd_half(j, x, f, dout, vp, h, gpre, upre, w13g, w2v, dw13, dw2v, res_w, name, df=None, dh0=None, carry=None,
                  tm=256):
    S, D = h.shape
    Fh = w13g.shape[-1]
    T = S // tm
    first = j == 0
    n_in, n_out = (11, 5) if first else (13, 4)

    steps = T + 1

    def body(*refs):
        ins, outs, scr, carry_refs = _split_refs(refs, n_in, n_out, 5, carry)
        a1, a3, a2, sem = scr[:4]
        if first:
            f_ref, do_ref, vp_ref, h_ref, g_ref, u_ref, w1_ref, w3_ref, w2_ref, _, _ = ins
            df_ref, dh_ref, s_ref, dw13_ref, dw2_ref = outs
            dfs = scr[4]
        else:
            x_ref, do_ref, vp_ref, h_ref, g_ref, u_ref, w1_ref, w3_ref, w2_ref, dfi_ref, dh0_ref, _, _ = ins
            dx_ref, s_ref, dw13_ref, dw2_ref = outs
            dhs = scr[4]
        t = pl.program_id(0)
        _carry_run(carry, carry_refs, t == 0, t == steps - 1)

        @pl.when(t == 0)
        def _():
            for ref in (a1, a3, a2, s_ref, dfs if first else dhs):
                ref[...] = jnp.zeros_like(ref)

        hv = h_ref[...]
        if first:
            dfv = dfs[(t + 1) % 2]
        else:
            dfv = jnp.where(t < T, dfi_ref[...], jnp.zeros((tm, D), MXU_DT))
        dh = None
        for n_chunk, (c0, c1) in enumerate(_col_chunks(Fh)):
            gg = g_ref[:, c0:c1].astype(F32)
            uu = u_ref[:, c0:c1].astype(F32)
            sg = _sig(gg)
            si = gg * sg
            da = _dot_nt(dfv, w2_ref[c0:c1, :])
            a2[c0:c1, :] += _dot_tn(si * uu, dfv)
            if first and n_chunk == 0:
                keep = jnp.where(t < T, 1.0, 0.0).astype(F32)
                nxt = _shell_post_bwd(f_ref[...], do_ref[...], vp_ref, res_w, s_ref, keep).astype(MXU_DT)
                dfs[t % 2] = nxt
                df_ref[...] = nxt
            if not first and n_chunk == 0:
                keep = jnp.where(t > 0, 1.0, 0.0).astype(F32)
                dx_ref[...] = _shell_pre_bwd(x_ref[...], dhs[(t + 1) % 2], do_ref[...], vp_ref, s_ref, keep)
            dg = da * uu * (sg * (1.0 + gg * (1.0 - sg)))
            du = da * si
            a1[:, c0:c1] += _dot_tn(hv, dg)
            a3[:, c0:c1] += _dot_tn(hv, du)
            part = _dot_nt(dg, w1_ref[:, c0:c1]) + _dot_nt(du, w3_ref[:, c0:c1])
            dh = part if dh is None else dh + part
        if first:
            dh_ref[...] = dh
        else:
            dhs[t % 2] = dh0_ref[...] + dh

        @pl.when(t == steps - 1)
        def _():
            _copy_out(((a1, dw13_ref.at[0, j]), (a3, dw13_ref.at[0, 2 + j]), (a2, dw2_ref.at[0, j])), sem)

    if first:
        made_at = lambda t: jnp.minimum(t, T - 1)
        used_at = lambda t: jnp.maximum(t - 1, 0)
    else:
        used_at = lambda t: jnp.minimum(t, T - 1)
        made_at = lambda t: jnp.maximum(t - 1, 0)
    tile = lambda w, at=used_at: pl.BlockSpec((tm, w), lambda t: (at(t), 0))
    half = pl.BlockSpec((tm, Fh), lambda t: (used_at(t), j))
    weights = [pl.BlockSpec((None, None, D, Fh), lambda t: (0, j, 0, 0), pipeline_mode=pl.Buffered(1)),
               pl.BlockSpec((None, None, D, Fh), lambda t: (0, 2 + j, 0, 0), pipeline_mode=pl.Buffered(1)),
               pl.BlockSpec((None, None, Fh, D), lambda t: (0, j, 0, 0), pipeline_mode=pl.Buffered(1))]
    sd = jax.ShapeDtypeStruct
    grads = [sd(dw13.shape, F32), sd(dw2v.shape, F32)]
    scratch = [pltpu.VMEM((D, Fh), F32), pltpu.VMEM((D, Fh), F32), pltpu.VMEM((Fh, D), F32), pltpu.SemaphoreType.DMA]
    c_in, c_out, c_shapes, c_sems, c_alias = _carry_args(carry, n_in, n_out)
    params = _cp(("arbitrary",), has_side_effects=carry is not None)
    extra = carry.srcs if carry is not None else []
    if first:
        return pl.pallas_call(
            body, name=name, grid=(steps,),
            in_specs=[tile(D, made_at), tile(D, made_at), _whole((8, D)), tile(D), half, half] + weights + [ANY, ANY]
            + c_in,
            out_specs=[tile(D, made_at), tile(D), _whole((8, D)), ANY, ANY] + c_out,
            out_shape=[sd((S, D), MXU_DT), sd((S, D), F32), sd((8, D), F32)] + grads + c_shapes,
            scratch_shapes=scratch + [pltpu.VMEM((2, tm, D), MXU_DT)] + c_sems,
            input_output_aliases={9: 3, 10: 4, **c_alias}, compiler_params=params,
        )(f, dout, vp, h, gpre, upre, w13g, w13g, w2v, dw13, dw2v, *extra)
    return pl.pallas_call(
        body, name=name, grid=(steps,),
        in_specs=[tile(D, made_at), tile(D, made_at), _whole((8, D)), tile(D), half, half] + weights
        + [tile(D), tile(D), ANY, ANY] + c_in,
        out_specs=[tile(D, made_at), _whole((8, D)), ANY, ANY] + c_out,
        out_shape=[sd((S, D), F32), sd((8, D), F32)] + grads + c_shapes,
        scratch_shapes=scratch + [pltpu.VMEM((2, tm, D), F32)] + c_sems,
        input_output_aliases={11: 2, 12: 3, **c_alias}, compiler_params=params,
    )(x, dout, vp, h, gpre, upre, w13g, w13g, w2v, df, dh0, dw13, dw2v, *extra)


def _scan_fwd(a, u, rows):
    n = a.shape[0]
    d = 1
    while d < n:
        m = rows >= d
        u = u + a * jnp.where(m, _shift_down(u, d), 0.0)
        a = a * jnp.where(m, _shift_down(a, d), 1.0)
        d *= 2
    return a, u


def _scan_bwd(a, u, rows):
    n = a.shape[0]
    d = 1
    while d < n:
        m = rows < n - d
        u = u + a * jnp.where(m, _shift_up(u, d), 0.0)
        a = a * jnp.where(m, _shift_up(a, d), 1.0)
        d *= 2
    return a, u


def _causal_conv(ext, w_ref, K, halo, tm):
    acc = None
    for k in range(K):
        term = w_ref[k:k + 1, :] * _shift_down(ext, K - 1 - k)[halo:, :]
        acc = term if acc is None else acc + term
    return acc


def _anticausal_conv(ext, w_ref, K, tm):
    acc = None
    for k in range(K):
        term = w_ref[k:k + 1, :] * _shift_up(ext, K - 1 - k)[:tm, :]
        acc = term if acc is None else acc + term
    return acc


def _dot_split(a, b):
    hi = a.astype(MXU_DT)
    lo = (a - hi.astype(F32)).astype(MXU_DT)
    return jnp.dot(hi, b, preferred_element_type=F32) + jnp.dot(lo, b, preferred_element_type=F32)


def _group_mean(v, p_ref):
    return _dot_nt_exact(_dot_split(v, p_ref[0]), p_ref[1])


def _dot_nt_exact(a, bt):
    hi = a.astype(MXU_DT)
    lo = (a - hi.astype(F32)).astype(MXU_DT)
    dims = (((1,), (1,)), ((), ()))
    return (lax.dot_general(hi, bt, dims, preferred_element_type=F32)
            + lax.dot_general(lo, bt, dims, preferred_element_type=F32))


def _group_norm(vc, p_ref, g, b):
    mu = _group_mean(vc, p_ref)
    dv = vc - mu
    rstd = lax.rsqrt(_group_mean(dv * dv, p_ref) + EPS)
    vhat = dv * rstd
    return vhat, rstd, vhat * g + b


def _lru_gates(axc, wg_ref, bg_ref, lam, W):
    gp = _dot(axc, wg_ref[...]) + bg_ref[0:1, :]
    r = _sig(gp[:, :W])
    i = _sig(gp[:, W:])
    ls = _logsig(lam)
    L = (LRU_C * ls) * r
    a = jnp.exp(L)
    mult = jnp.sqrt(_neg_expm1(2.0 * L))
    return r, i, ls, a, mult


def _mix_ab_fwd(x, vp, win4, cwa, wg, bg, cwb, v512, pavg, wout, res_w, name, carry=None, tm=256):
    S, D = x.shape
    W = win4.shape[-1]
    KA, KB, HA, HB = 4, 31, 8, 32
    rider = carry

    def body(*refs):
        ins, (xo_ref, f_ref, h_ref, sav_ref), (ahalo, bhalo, carry), carry_refs = _split_refs(refs, 10, 4, 3, rider)
        x_ref, vp_ref, win_ref, cwa_ref, wg_ref, bg_ref, cwb_ref, v_ref, p_ref, wo_ref = ins
        _carry_run(rider, carry_refs, pl.program_id(0) == 0, pl.program_id(0) == S // tm - 1)

        @pl.when(pl.program_id(0) == 0)
        def _():
            ahalo[...] = jnp.zeros_like(ahalo)
            bhalo[...] = jnp.zeros_like(bhalo)
            carry[...] = jnp.zeros_like(carry)

        xv = x_ref[...]
        hv = _shell_pre(xv, vp_ref).astype(MXU_DT)
        h_ref[...] = hv
        a_gate = _dot(hv, win_ref[0])
        axp = _dot(hv, win_ref[1])
        b_val = _dot(hv, win_ref[2])
        b_gate = _dot(hv, win_ref[3])
        rows = lax.broadcasted_iota(jnp.int32, (tm, W), 0)
        axc = _causal_conv(jnp.concatenate([ahalo[...], axp], axis=0), cwa_ref, KA, HA, tm) + v_ref[0:1, :]
        ahalo[...] = axp[tm - HA:, :]
        r, i, ls, a, mult = _lru_gates(axc, wg_ref, bg_ref, v_ref[1:2, :], W)
        acum, hloc = _scan_fwd(a, mult * i * axc, rows)
        hs = hloc + acum * carry[7:8, :]
        carry[...] = hs[tm - 8:, :]
        ya = hs * _gelu(a_gate)[0]
        bv = b_val * _sig(b_gate)
        vc = _causal_conv(jnp.concatenate([bhalo[...], bv], axis=0), cwb_ref, KB, HB, tm) + v_ref[2:3, :]
        bhalo[...] = bv[tm - HB:, :]
        _, _, vn = _group_norm(vc, p_ref, v_ref[3:4, :], v_ref[4:5, :])
        yb = vn * _sig(vn)
        fv = _dot(ya, wo_ref[0:W, :]) + _dot(yb, wo_ref[W:, :])
        f_ref[...] = fv
        xo_ref[...] = _shell_post(xv, fv, vp_ref, res_w)
        for n, val in enumerate((hs, axp, axc, bv, vc)):
            sav_ref[:, n * W:(n + 1) * W] = val

    tile = lambda w: pl.BlockSpec((tm, w), lambda t: (t, 0))
    sd = jax.ShapeDtypeStruct
    c_in, c_out, c_shapes, c_sems, c_alias = _carry_args(rider, 10, 4)
    return pl.pallas_call(
        body, name=name, grid=(S // tm,),
        in_specs=[tile(D), _whole((8, D))] + [_resident(a) for a in (win4, cwa, wg, bg, cwb, v512, pavg, wout)] + c_in,
        out_specs=[tile(D), tile(D), tile(D), tile(5 * W)] + c_out,
        out_shape=[sd((S, D), F32), sd((S, D), F32), sd((S, D), MXU_DT), sd((S, 5 * W), F32)] + c_shapes,
        scratch_shapes=[pltpu.VMEM((HA, W), F32), pltpu.VMEM((HB, W), F32), pltpu.VMEM((8, W), F32)] + c_sems,
        input_output_aliases=c_alias, compiler_params=_cp(("arbitrary",), has_side_effects=rider is not None),
    )(x, vp, win4, cwa, wg, bg, cwb, v512, pavg, wout, *(rider.srcs if rider is not None else []))


def _mix_ab_bwd(x, f, dout, vp, h, sav, win4, cwa, wg, bg, cwb, v512, pavg, wout, res_w, name, carry=None, tm=256):
    S, D = x.shape
    W = win4.shape[-1]
    T = S // tm
    KA, KB, HA, HB = 4, 31, 8, 32

    def body(*refs):
        ins, outs, scr, carry_refs = _split_refs(refs, 15, 9, 8, carry)
        (x_ref, f_ref, do_ref, vp_ref, h_ref, sav_ref, prev_ref,
         win_ref, cwa_ref, wg_ref, bg_ref, cwb_ref, v_ref, p_ref, wo_ref) = ins
        hs_ref, axp_ref, axc_ref, bv_ref, vc_ref = (sav_ref.at[:, n * W:(n + 1) * W] for n in range(5))
        hsp_ref = prev_ref.at[HB - HA:, 0:W]
        axpp_ref = prev_ref.at[HB - HA:, W:2 * W]
        bvp_ref = prev_ref.at[:, 3 * W:4 * W]
        dx_ref, s_ref, dwin_out, dwo_out, dwg_out, dcwa_ref, dcwb_ref, dbg_ref, dv_ref = outs
        danext, dvnext, gfirst, afirst, dwin_ref, dwo_ref, dwg_ref, sem = scr
        t = pl.program_id(0)
        _carry_run(carry, carry_refs, t == 0, t == T - 1)

        @pl.when(t == 0)
        def _():
            for ref in (s_ref, dwin_ref, dwo_ref, dwg_ref, dcwa_ref, dcwb_ref, dbg_ref, dv_ref, danext, dvnext,
                        gfirst, afirst):
                ref[...] = jnp.zeros_like(ref)

        notfirst = jnp.where(t < T - 1, 1.0, 0.0).astype(F32)
        hv = h_ref[...]
        dov = do_ref[...]
        dfv = _shell_post_bwd(f_ref[...], dov, vp_ref, res_w, s_ref).astype(MXU_DT)
        a_gate = _dot(hv, win_ref[0])
        b_val = _dot(hv, win_ref[2])
        b_gate = _dot(hv, win_ref[3])
        rows = lax.broadcasted_iota(jnp.int32, (tm, W), 0)
        ge, dge = _gelu(a_gate)
        hsv = hs_ref[...]
        ya = hsv * ge
        vhat, rstd, vn = _group_norm(vc_ref[...], p_ref, v_ref[3:4, :], v_ref[4:5, :])
        sgn = _sig(vn)
        yb = vn * sgn
        dma = _dot_nt(dfv, wo_ref[0:W, :])
        dmb = _dot_nt(dfv, wo_ref[W:, :])
        dwo_ref[0:W, :] += _dot_tn(ya, dfv)
        dwo_ref[W:, :] += _dot_tn(yb, dfv)
        dhs = dma * ge
        d_a_gate = dma * hsv * dge
        axcv = axc_ref[...]
        lam = v_ref[1:2, :]
        r, i, ls, a, mult = _lru_gates(axcv, wg_ref, bg_ref, lam, W)
        ash = jnp.where(rows == tm - 1, afirst[0:1, :], _shift_up(a, 1))
        asuf, gloc = _scan_bwd(ash, dhs, rows)
        gsc = gloc + asuf * gfirst[0:1, :]
        afirst[...] = a[0:8, :]
        gfirst[...] = gsc[0:8, :]
        hprev = jnp.where(rows == 0, hsp_ref[HA - 1:HA, :] * notfirst, _shift_down(hsv, 1))
        da = gsc * hprev
        dL = da * a - gsc * (i * axcv) * (a * a) / mult
        dix = gsc * mult
        daxc = dix * i
        dr = dL * (LRU_C * ls)
        dv_ref[1:2, :] += _colsum(dL * r) * (LRU_C * _sig(-lam))
        dgate = jnp.concatenate([dr * r * (1.0 - r), (dix * axcv) * i * (1.0 - i)], axis=1)
        dbg_ref[0:1, :] += _colsum(dgate)
        dwg_ref[...] += _dot_tn(axcv, dgate)
        daxc = daxc + _dot_nt(dgate, wg_ref[...])
        daxp = _anticausal_conv(jnp.concatenate([daxc, danext[...]], axis=0), cwa_ref, KA, tm)
        ext = jnp.concatenate([axpp_ref[...] * notfirst, axp_ref[...]], axis=0)
        for k in range(KA):
            dcwa_ref[k:k + 1, :] += _colsum(daxc * _shift_down(ext, KA - 1 - k)[HA:, :])
        dv_ref[0:1, :] += _colsum(daxc)
        danext[...] = daxc[0:HA, :]
        dvn = dmb * (sgn * (1.0 + vn * (1.0 - sgn)))
        dv_ref[4:5, :] += _colsum(dvn)
        dv_ref[3:4, :] += _colsum(dvn * vhat)
        dvh = dvn * v_ref[3:4, :]
        dvc = rstd * (dvh - _group_mean(dvh, p_ref) - vhat * _group_mean(dvh * vhat, p_ref))
        dbv = _anticausal_conv(jnp.concatenate([dvc, dvnext[...]], axis=0), cwb_ref, KB, tm)
        ext = jnp.concatenate([bvp_ref[...] * notfirst, bv_ref[...]], axis=0)
        for k in range(KB):
            dcwb_ref[k:k + 1, :] += _colsum(dvc * _shift_down(ext, KB - 1 - k)[HB:, :])
        dv_ref[2:3, :] += _colsum(dvc)
        dvnext[...] = dvc[0:HB, :]
        sb = _sig(b_gate)
        dzs = (d_a_gate, daxp, dbv * sb, dbv * b_val * sb * (1.0 - sb))
        dh = None
        for s in range(4):
            part = _dot_nt(dzs[s], win_ref[s])
            dh = part if dh is None else dh + part
            dwin_ref[s] += _dot_tn(hv, dzs[s])
        dx_ref[...] = _shell_pre_bwd(x_ref[...], dh, dov, vp_ref, s_ref)

        @pl.when(t == T - 1)
        def _():
            _copy_out(((dwin_ref, dwin_out), (dwo_ref, dwo_out), (dwg_ref, dwg_out)), sem)

    tile = lambda w: pl.BlockSpec((tm, w), lambda t: (T - 1 - t, 0))
    prev = pl.BlockSpec((HB, 5 * W), lambda t: (jnp.maximum((T - 1 - t) * (tm // HB) - 1, 0), 0))
    out_shapes = [(S, D), (8, D), win4.shape, wout.shape, wg.shape, (8, W), (32, W), (8, 2 * W), (8, W)]
    c_in, c_out, c_shapes, c_sems, c_alias = _carry_args(carry, 15, 9)
    return pl.pallas_call(
        body, name=name, grid=(T,),
        in_specs=[tile(D), tile(D), tile(D), _whole((8, D)), tile(D), tile(5 * W), prev]
        + [_resident(a) for a in (win4, cwa, wg, bg, cwb, v512, pavg, wout)] + c_in,
        out_specs=[tile(D), _whole((8, D)), ANY, ANY, ANY] + [_whole(s) for s in out_shapes[5:]] + c_out,
        out_shape=[jax.ShapeDtypeStruct(s, F32) for s in out_shapes] + c_shapes,
        scratch_shapes=[pltpu.VMEM((HA, W), F32), pltpu.VMEM((HB, W), F32), pltpu.VMEM((8, W), F32),
                        pltpu.VMEM((8, W), F32), pltpu.VMEM(win4.shape, F32), pltpu.VMEM(wout.shape, F32),
                        pltpu.VMEM(wg.shape, F32), pltpu.SemaphoreType.DMA] + c_sems,
        input_output_aliases=c_alias, compiler_params=_cp(("arbitrary",), has_side_effects=carry is not None),
    )(x, f, dout, vp, h, sav, sav, win4, cwa, wg, bg, cwb, v512, pavg, wout,
      *(carry.srcs if carry is not None else []))


def _mix_c_core(hv, win_ref, v2_ref, v1_ref, ws_ref, bsb_ref, tm, D):
    zp = jnp.concatenate([_dot(hv, win_ref[s]) for s in range(4)], axis=1) + v2_ref[0:1, :]
    z, dz = _gelu(zp)
    u, v = z[:, :D], z[:, D:]
    mu = _rowmean(v)
    dv = v - mu
    rstd = lax.rsqrt(_rowmean(dv * dv) + EPS)
    vhat = dv * rstd
    vn = vhat * v1_ref[0:1, :] + v1_ref[1:2, :]
    rows_out = []
    for cidx in range(tm // CHUNK):
        blk = vn[cidx * CHUNK:(cidx + 1) * CHUNK, :]
        heads = [_dot(ws_ref[hd], blk[:, hd * CHUNK:(hd + 1) * CHUNK]) for hd in range(N_HEAD)]
        rows_out.append(jnp.concatenate(heads, axis=1) + bsb_ref[...])
    mixed = jnp.concatenate(rows_out, axis=0)
    return dz, u, rstd, vhat, vn, mixed


def _mix_c_fwd(x, vp, win4, v2d, v1d, ws, bsb, wout, res_w, name, tm=512):
    S, D = x.shape

    def body(x_ref, vp_ref, win_ref, v2_ref, v1_ref, ws_ref, bsb_ref, wo_ref, xo_ref, f_ref, h_ref):
        xv = x_ref[...]
        hv = _shell_pre(xv, vp_ref).astype(MXU_DT)
        h_ref[...] = hv
        _, u, _, _, _, mixed = _mix_c_core(hv, win_ref, v2_ref, v1_ref, ws_ref, bsb_ref, tm, D)
        fv = _dot(u * mixed, wo_ref[...])
        f_ref[...] = fv
        xo_ref[...] = _shell_post(xv, fv, vp_ref, res_w)

    tile = pl.BlockSpec((tm, D), lambda t: (t, 0))
    sd = jax.ShapeDtypeStruct
    return pl.pallas_call(
        body, name=name, grid=(S // tm,),
        in_specs=[tile, _whole((8, D))] + [_resident(a) for a in (win4, v2d, v1d, ws, bsb, wout)],
        out_specs=[tile, tile, tile],
        out_shape=[sd((S, D), F32), sd((S, D), F32), sd((S, D), MXU_DT)], compiler_params=_cp(("arbitrary",)),
    )(x, vp, win4, v2d, v1d, ws, bsb, wout)


def _mix_c_bwd(x, f, dout, vp, h, win4, v2d, v1d, ws, bsb, wout, res_w, name, carry=None, tm=256):
    S, D = x.shape

    def body(*refs):
        ins, outs, (dwin_ref, dwo_ref, sem), carry_refs = _split_refs(refs, 11, 8, 3, carry)
        x_ref, f_ref, do_ref, vp_ref, h_ref, win_ref, v2_ref, v1_ref, ws_ref, bsb_ref, wo_ref = ins
        dx_ref, s_ref, dwin_out, dwo_out, dws_ref, dbsb_ref, dv2_ref, dv1_ref = outs
        _carry_run(carry, carry_refs, pl.program_id(0) == 0, pl.program_id(0) == S // tm - 1)

        @pl.when(pl.program_id(0) == 0)
        def _():
            for ref in (s_ref, dwin_ref, dwo_ref, dws_ref, dbsb_ref, dv2_ref, dv1_ref):
                ref[...] = jnp.zeros_like(ref)

        hv = h_ref[...]
        dov = do_ref[...]
        dfv = _shell_post_bwd(f_ref[...], dov, vp_ref, res_w, s_ref).astype(MXU_DT)
        dz, u, rstd, vhat, vn, mixed = _mix_c_core(hv, win_ref, v2_ref, v1_ref, ws_ref, bsb_ref, tm, D)
        dp = _dot_nt(dfv, wo_ref[...])
        dwo_ref[...] += _dot_tn(u * mixed, dfv)
        du = dp * mixed
        dmx = dp * u
        rows_out = []
        for cidx in range(tm // CHUNK):
            dblk = dmx[cidx * CHUNK:(cidx + 1) * CHUNK, :]
            vblk = vn[cidx * CHUNK:(cidx + 1) * CHUNK, :]
            dbsb_ref[...] += dblk
            heads = []
            for hd in range(N_HEAD):
                dsl = dblk[:, hd * CHUNK:(hd + 1) * CHUNK]
                heads.append(_dot_tn(ws_ref[hd], dsl))
                dws_ref[hd] += _dot_nt(dsl, vblk[:, hd * CHUNK:(hd + 1) * CHUNK])
            rows_out.append(jnp.concatenate(heads, axis=1))
        dvn = jnp.concatenate(rows_out, axis=0)
        dv1_ref[1:2, :] += _colsum(dvn)
        dv1_ref[0:1, :] += _colsum(dvn * vhat)
        dvh = dvn * v1_ref[0:1, :]
        dv = rstd * (dvh - _rowmean(dvh) - vhat * _rowmean(dvh * vhat))
        dzp = jnp.concatenate([du, dv], axis=1) * dz
        dv2_ref[0:1, :] += _colsum(dzp)
        W = win_ref.shape[-1]
        dh = None
        for s in range(4):
            dzs = dzp[:, s * W:(s + 1) * W]
            part = _dot_nt(dzs, win_ref[s])
            dh = part if dh is None else dh + part
            dwin_ref[s] += _dot_tn(hv, dzs)
        dx_ref[...] = _shell_pre_bwd(x_ref[...], dh, dov, vp_ref, s_ref)

        @pl.when(pl.program_id(0) == S // tm - 1)
        def _():
            _copy_out(((dwin_ref, dwin_out), (dwo_ref, dwo_out)), sem)

    tile = pl.BlockSpec((tm, D), lambda t: (t, 0))
    out_shapes = [(S, D), (8, D), win4.shape, wout.shape, ws.shape, bsb.shape, (8, 2 * D), (8, D)]
    c_in, c_out, c_shapes, c_sems, c_alias = _carry_args(carry, 11, 8)
    return pl.pallas_call(
        body, name=name, grid=(S // tm,),
        in_specs=[tile, tile, tile, _whole((8, D)), tile] + [_resident(a) for a in (win4, v2d, v1d, ws, bsb, wout)]
        + c_in,
        out_specs=[tile, _whole((8, D)), ANY, ANY] + [_whole(s) for s in out_shapes[4:]] + c_out,
        out_shape=[jax.ShapeDtypeStruct(s, F32) for s in out_shapes] + c_shapes,
        scratch_shapes=[pltpu.VMEM(win4.shape, F32), pltpu.VMEM(wout.shape, F32), pltpu.SemaphoreType.DMA] + c_sems,
        input_output_aliases=c_alias, compiler_params=_cp(("arbitrary",), has_side_effects=carry is not None),
    )(x, f, dout, vp, h, win4, v2d, v1d, ws, bsb, wout, *(carry.srcs if carry is not None else []))


def _ada_fwd(c_all, ada_w, ada_b_my, name):
    L, D, N = ada_w.shape
    tn = N // 3

    def body(c_ref, w_ref, b_ref, o_ref):
        cv = c_ref[...]
        o_ref[...] = _dot_hi(cv * _sig(cv), w_ref[...]) + b_ref[...]

    return pl.pallas_call(
        body, name=name, grid=(L, 3),
        in_specs=[pl.BlockSpec((8, D), lambda l, n: (0, 0)), pl.BlockSpec((None, D, tn), lambda l, n: (l, 0, n)),
                  pl.BlockSpec((None, 1, tn), lambda l, n: (l, 0, n))],
        out_specs=pl.BlockSpec((None, 8, tn), lambda l, n: (l, 0, n)),
        out_shape=jax.ShapeDtypeStruct((L, 8, N), F32), compiler_params=_cp(("arbitrary", "arbitrary")),
    )(c_all, ada_w, ada_b_my)


def _ada_bwd(c_all, dmod_my, name):
    L, _, N = dmod_my.shape
    D = c_all.shape[1]
    tn = N // 3

    def body(c_ref, d_ref, o_ref):
        cv = c_ref[...]
        o_ref[...] = lax.dot_general(cv * _sig(cv), d_ref[...], (((0,), (0,)), ((), ())),
                                     precision=lax.Precision.HIGHEST, preferred_element_type=F32)

    return pl.pallas_call(
        body, name=name, grid=(L, 3),
        in_specs=[pl.BlockSpec((8, D), lambda l, n: (0, 0)), pl.BlockSpec((None, 8, tn), lambda l, n: (l, 0, n))],
        out_specs=pl.BlockSpec((None, D, tn), lambda l, n: (l, 0, n)),
        out_shape=jax.ShapeDtypeStruct((L, D, N), F32), compiler_params=_cp(("arbitrary", "arbitrary")),
    )(c_all, dmod_my)


def _row_tile(rows, cols, budget=1 << 20):
    best = 8
    for rt in range(8, rows + 1, 8):
        if rows % rt == 0 and rt * cols * 4 <= budget:
            best = rt
    return best


def _my_chip():
    return 2 * lax.axis_index("x") + lax.axis_index("y")


def _cast_place(w, g, name):
    _, R, C = w.shape
    rt = _row_tile(R, C)

    def body(w_ref, o_ref):
        o_ref[...] = w_ref[...].astype(o_ref.dtype)

    return pl.pallas_call(
        body, name=name, grid=(R // rt,),
        in_specs=[pl.BlockSpec((None, rt, C), lambda r: (g, r, 0))],
        out_specs=pl.BlockSpec((None, None, rt, C), lambda r: (0, _my_chip(), r, 0)),
        out_shape=jax.ShapeDtypeStruct((1, N_SHARD, R, C), MXU_DT), compiler_params=_cp(("arbitrary",)),
    )(w)


def _add_half(gk, la, name, out_dtype=F32):
    Gk, _, R, C = gk.shape
    Rh = R // 2
    n = Gk * N_SHARD
    gv = gk.reshape(n, 2, Rh, C)
    lv = la.reshape(n, Rh, C)
    rt = _row_tile(Rh, C)

    def body(g_ref, l_ref, o_ref):
        o_ref[...] = (g_ref[...] + l_ref[...]).astype(o_ref.dtype)

    out = pl.pallas_call(
        body, name=name, grid=(n, Rh // rt),
        in_specs=[pl.BlockSpec((None, None, rt, C), lambda i, r: (i, lax.axis_index("c"), r, 0)),
                  pl.BlockSpec((None, rt, C), lambda i, r: (i, r, 0))],
        out_specs=pl.BlockSpec((None, rt, C), lambda i, r: (i, r, 0)),
        out_shape=jax.ShapeDtypeStruct((n, Rh, C), out_dtype), compiler_params=_cp(("arbitrary", "arbitrary")),
    )(gv, lv)
    return out.reshape(Gk, N_SHARD, Rh, C)


def _sum_chips(part, landed, joined, g, name):
    _, _, Rh, C = part.shape
    rt = _row_tile(Rh, C)
    nb = Rh // rt

    def body(p_ref, l_ref, j_ref, o_ref):
        up = lambda v: v.astype(F32)
        o_ref[...] = ((up(p_ref[...]) + up(l_ref[0])) + up(l_ref[1])) + up(l_ref[2])

    return pl.pallas_call(
        body, name=name, grid=(nb,),
        in_specs=[pl.BlockSpec((None, None, rt, C), lambda r: (0, _my_chip(), r, 0)),
                  pl.BlockSpec((None, 3, rt, C), lambda r: (0, 0, r, 0)), ANY],
        out_specs=pl.BlockSpec((None, rt, C), lambda r: (g, lax.axis_index("c") * nb + r, 0)),
        out_shape=jax.ShapeDtypeStruct(joined.shape, F32), input_output_aliases={2: 0},
        compiler_params=_cp(("arbitrary",)),
    )(part, landed, joined)


def _sum_slots(lb, name):
    Gk, n, Rh, C = lb.shape
    rt = _row_tile(Rh, C)

    def body(l_ref, o_ref):
        acc = l_ref[0]
        for s in range(1, n):
            acc = acc + l_ref[s]
        o_ref[...] = acc

    return pl.pallas_call(
        body, name=name, grid=(Gk, Rh // rt),
        in_specs=[pl.BlockSpec((None, n, rt, C), lambda g, r: (g, 0, r, 0))],
        out_specs=pl.BlockSpec((None, rt, C), lambda g, r: (g, r, 0)),
        out_shape=jax.ShapeDtypeStruct((Gk, Rh, C), F32), compiler_params=_cp(("arbitrary", "arbitrary")),
    )(lb)


def _adamw(w, g, m, v, name):
    rows, cols = w.shape
    rt = _row_tile(rows, cols) if rows % 8 == 0 else rows
    c1 = 1.0 - ADAM_B1 ** ADAM_STEP
    c2 = 1.0 - ADAM_B2 ** ADAM_STEP

    def body(w_ref, g_ref, m_ref, v_ref, d_ref, mo_ref, vo_ref):
        gv = g_ref[...]
        mn = ADAM_B1 * m_ref[...] + (1.0 - ADAM_B1) * gv
        vn = ADAM_B2 * v_ref[...] + (1.0 - ADAM_B2) * (gv * gv)
        d_ref[...] = -ADAM_LR * ((mn / c1) / (jnp.sqrt(vn / c2) + ADAM_EPS) + ADAM_WD * w_ref[...])
        mo_ref[...] = mn
        vo_ref[...] = vn

    spec = pl.BlockSpec((rt, cols), lambda r: (r, 0))
    sds = jax.ShapeDtypeStruct((rows, cols), F32)
    return pl.pallas_call(
        body, name=name, grid=(rows // rt,), in_specs=[spec] * 4, out_specs=[spec] * 3,
        out_shape=[sds] * 3, compiler_params=_cp(("arbitrary",)),
    )(w, g, m, v)


def _coords():
    return lax.axis_index("x"), lax.axis_index("y"), lax.axis_index("c")


def _all_gather8(blk, name, carry=None):
    m_per, n = blk.shape

    def body(*refs):
        (x_ref,), (out_ref,), (send_sems, recv_sems, local_sem), carry_refs = _split_refs(refs, 1, 1, 3, carry)
        x, y, c = _coords()
        me, sibling = (x, y, c), (x, y, 1 - c)
        chips = [(1 - x, y), (x, 1 - y), (1 - x, 1 - y)]

        def rows(px, py, pc):
            return out_ref.at[pl.ds((4 * px + 2 * py + pc) * m_per, m_per), :]

        def copy(k, block, to, src=None):
            return pltpu.make_async_remote_copy(
                src_ref=rows(*block) if src is None else src, dst_ref=rows(*block),
                send_sem=send_sems.at[k], recv_sem=recv_sems.at[k], device_id=to, device_id_type=MESH)

        mine = pltpu.make_async_copy(x_ref, rows(*me), local_sem)
        mine.start()
        first = [copy(0, me, sibling, src=x_ref)]
        first += [copy(1 + j, me, (*chip, c), src=x_ref) for j, chip in enumerate(chips)]
        for cp in first:
            cp.start()
        riding = _carry_copies(carry, carry_refs)
        for cp in riding:
            cp.start()
        passed = [copy(4 + j, (*chip, c), sibling) for j, chip in enumerate(chips)]
        for j, chip in enumerate(chips):
            copy(1 + j, (*chip, c), me).wait_recv()
            passed[j].start()
        copy(0, sibling, me).wait_recv()
        for j, chip in enumerate(chips):
            copy(4 + j, (*chip, 1 - c), me).wait_recv()
        for cp in first + passed:
            cp.wait_send()
        mine.wait()
        for cp in riding:
            cp.wait()

    c_in, c_out, c_shapes, c_sems, c_alias = _carry_args(carry, 1, 1)
    res = pl.pallas_call(
        body, name=name, out_shape=[jax.ShapeDtypeStruct((N_DEV * m_per, n), blk.dtype)] + c_shapes,
        in_specs=[pl.BlockSpec(memory_space=pltpu.VMEM)] + c_in,
        out_specs=[pl.BlockSpec(memory_space=pltpu.VMEM)] + c_out,
        scratch_shapes=[pltpu.SemaphoreType.DMA((7,)), pltpu.SemaphoreType.DMA((7,)), pltpu.SemaphoreType.DMA] + c_sems,
        input_output_aliases=c_alias, compiler_params=_cp(),
    )(blk, *(carry.srcs if carry is not None else []))
    return res[0] if carry is None else res


def _comm_call(name, inputs, out_shapes, plan, n_remote, aliases=None):
    n_in, n_out = len(inputs), len(out_shapes)

    def body(*refs):
        in_refs, out_refs = refs[:n_in], refs[n_in:n_in + n_out]
        send_sems, recv_sems = refs[n_in + n_out:]

        def remote(k, src, dst, to):
            return pltpu.make_async_remote_copy(src_ref=src, dst_ref=dst, send_sem=send_sems.at[k],
                                                recv_sem=recv_sems.at[k], device_id=to, device_id_type=MESH)

        plan(in_refs, out_refs, remote)

    return pl.pallas_call(
        body, name=name, out_shape=out_shapes, in_specs=[ANY] * n_in, out_specs=[ANY] * n_out,
        scratch_shapes=[pltpu.SemaphoreType.DMA((n_remote,)), pltpu.SemaphoreType.DMA((n_remote,))],
        input_output_aliases=aliases or {}, compiler_params=_cp(has_side_effects=True),
    )(*inputs)


def _gather_ici_carry(placed):
    K = len(placed)

    def plan(ins, outs, remote):
        x, y, c = _coords()
        s_me = 2 * x + y
        cps = []
        for j, (px, py) in enumerate([(1 - x, y), (x, 1 - y), (1 - x, 1 - y)]):
            for k in range(K):
                rh = placed[k].shape[2] // 2
                own = outs[k].at[:, s_me, pl.ds(c * rh, rh), :]
                cps.append(remote(j * K + k, own, own, (px, py, c)))
        return cps

    shapes = [jax.ShapeDtypeStruct(p.shape, p.dtype) for p in placed]
    return _Carry(placed, shapes, {k: k for k in range(K)}, 3 * K, plan)


class _Exchange:
    def __init__(self, send, recv):
        self.send, self.recv = send, recv

    def start(self):
        self.send.start()

    def wait(self):
        self.send.wait_send()
        self.recv.wait_recv()


def _forward_carry(placed):
    K = len(placed)

    def plan(ins, outs, remote):
        x, y, c = _coords()
        cps = []
        for j, (px, py) in enumerate([(1 - x, y), (x, 1 - y), (1 - x, 1 - y)]):
            for k in range(K):
                rh = placed[k].shape[2] // 2
                landed = outs[k].at[:, 2 * px + py, pl.ds(c * rh, rh), :]
                other = outs[k].at[:, 2 * px + py, pl.ds((1 - c) * rh, rh), :]
                cps.append(_Exchange(remote(j * K + k, landed, landed, (x, y, 1 - c)),
                                     remote(j * K + k, other, other, (x, y, 1 - c))))
        return cps

    shapes = [jax.ShapeDtypeStruct(p.shape, p.dtype) for p in placed]
    return _Carry(placed, shapes, {k: k for k in range(K)}, 3 * K, plan)


def _gather_weights(placed, name):
    K = len(placed)

    def plan(ins, outs, remote):
        x, y, c = _coords()
        s_me = 2 * x + y
        sibling = (x, y, 1 - c)
        chips = [(1 - x, y), (x, 1 - y), (1 - x, 1 - y)]
        half = lambda k, s, cc: outs[k].at[:, s, pl.ds(cc * (placed[k].shape[2] // 2), placed[k].shape[2] // 2), :]
        sent = []
        for j, (px, py) in enumerate(chips):
            for k in range(K):
                own = half(k, s_me, c)
                cp = remote(j * K + k, own, own, (px, py, c))
                cp.start()
                sent.append(cp)
        for j, (px, py) in enumerate(chips):
            s_from = 2 * px + py
            for k in range(K):
                landed = half(k, s_from, c)
                remote(j * K + k, landed, landed, (px, py, c)).wait_recv()
                cp = remote((3 + j) * K + k, landed, landed, sibling)
                cp.start()
                sent.append(cp)
        for j, (px, py) in enumerate(chips):
            s_from = 2 * px + py
            for k in range(K):
                other = half(k, s_from, 1 - c)
                remote((3 + j) * K + k, other, other, sibling).wait_recv()
        for cp in sent:
            cp.wait_send()

    out_shapes = [jax.ShapeDtypeStruct(p.shape, p.dtype) for p in placed]
    return _comm_call(name, placed, out_shapes, plan, 6 * K, aliases={k: k for k in range(K)})


def _send_other_half(grads, name):
    K = len(grads)

    def plan(ins, outs, remote):
        x, y, c = _coords()
        cps = []
        for k in range(K):
            rh = grads[k].shape[2] // 2
            cps.append(remote(k, ins[k].at[:, :, pl.ds((1 - c) * rh, rh), :], outs[k], (x, y, 1 - c)))
        for cp in cps:
            cp.start()
        for cp in cps:
            cp.wait()

    out_shapes = [jax.ShapeDtypeStruct(g.shape[:2] + (g.shape[2] // 2, g.shape[3]), g.dtype) for g in grads]
    return _comm_call(name, grads, out_shapes, plan, K)


def _send_half_carry(grads):
    K = len(grads)

    def plan(ins, outs, remote):
        x, y, c = _coords()
        cps = []
        for k in range(K):
            rh = grads[k].shape[2] // 2
            cps.append(remote(k, ins[k].at[:, :, pl.ds((1 - c) * rh, rh), :], outs[k], (x, y, 1 - c)))
        return cps

    out_shapes = [jax.ShapeDtypeStruct(g.shape[:2] + (g.shape[2] // 2, g.shape[3]), g.dtype) for g in grads]
    return _Carry(grads, out_shapes, {}, K, plan)


def _merge_carries(carries):
    if not carries:
        return None
    if len(carries) == 1:
        return carries[0]
    srcs, shapes, aliases = [], [], {}
    for cy in carries:
        aliases.update({len(srcs) + i: len(shapes) + o for i, o in cy.aliases.items()})
        srcs += cy.srcs
        shapes += cy.out_shapes

    def plan(ins, outs, remote):
        cps, i0, o0, k0 = [], 0, 0, 0
        for cy in carries:
            shifted = functools.partial(lambda k, src, dst, to, base: remote(base + k, src, dst, to), base=k0)
            cps += cy.plan(ins[i0:i0 + len(cy.srcs)], outs[o0:o0 + len(cy.out_shapes)], shifted)
            i0, o0, k0 = i0 + len(cy.srcs), o0 + len(cy.out_shapes), k0 + cy.n
        return cps

    return _Carry(srcs, shapes, aliases, sum(cy.n for cy in carries), plan)


def _scatter_carry(parts):
    K = len(parts)

    def plan(ins, outs, remote):
        x, y, c = _coords()
        chips = [(1 - x, y), (x, 1 - y), (1 - x, 1 - y)]
        return [remote(j * K + k, ins[k].at[:, 2 * px + py], outs[k].at[:, j], (px, py, c))
                for j, (px, py) in enumerate(chips) for k in range(K)]

    out_shapes = [jax.ShapeDtypeStruct((p.shape[0], 3) + p.shape[2:], p.dtype) for p in parts]
    return _Carry(parts, out_shapes, {}, 3 * K, plan)


def _join_halves(joined):
    K = len(joined)

    def plan(ins, outs, remote):
        x, y, c = _coords()
        cps = []
        for k in range(K):
            rh = joined[k].shape[1] // 2
            mine = outs[k].at[:, pl.ds(c * rh, rh), :]
            cps.append(remote(k, mine, mine, (x, y, 1 - c)))
        for cp in cps:
            cp.start()
        for k in range(K):
            rh = joined[k].shape[1] // 2
            other = outs[k].at[:, pl.ds((1 - c) * rh, rh), :]
            remote(k, other, other, (x, y, 1 - c)).wait_recv()
        for cp in cps:
            cp.wait_send()

    out_shapes = [jax.ShapeDtypeStruct(h.shape, h.dtype) for h in joined]
    return _comm_call("join_halves", joined, out_shapes, plan, K, aliases={k: k for k in range(K)})


def _pack(parts):
    flat = []
    for p in parts:
        v = p.reshape(-1).astype(F32)
        pad = (-v.shape[0]) % 1024
        flat.append(jnp.pad(v, (0, pad)) if pad else v)
    return jnp.concatenate(flat).reshape(-1, 128)


def _unpack(packed, shapes):
    flat = packed.reshape(-1)
    out, off = [], 0
    for shp in shapes:
        n = math.prod(shp)
        out.append(flat[off:off + n].reshape(shp))
        off += n + (-n) % 1024
    return out


def _shard_last(a, s, n):
    return lax.dynamic_slice_in_dim(a, s * n, n, axis=a.ndim - 1)


def _rows8(*vecs):
    n = vecs[0].shape[-1]
    rows = [v.reshape(1, n).astype(F32) for v in vecs]
    return jnp.concatenate(rows + [jnp.zeros((8 - len(rows), n), F32)], axis=0)


def kernel(x, c, ada_w, ada_b, norm_pre, norm_post, ffn_w13, ffn_w2, ab_w_in, a_conv_w, a_conv_b, a_gate_w, a_gate_b, a_lam, b_conv_w, b_conv_b, b_norm_g, b_norm_b, ab_w_out, c_w_in, c_b_in, c_norm_g, c_norm_b, c_w_s, c_b_s, c_w_out, loss_target, m_ada_w, m_ada_b, m_norm_pre, m_norm_post, m_ffn_w13, m_ffn_w2, m_ab_w_in, m_a_conv_w, m_a_conv_b, m_a_gate_w, m_a_gate_b, m_a_lam, m_b_conv_w, m_b_conv_b, m_b_norm_g, m_b_norm_b, m_ab_w_out, m_c_w_in, m_c_b_in, m_c_norm_g, m_c_norm_b, m_c_w_s, m_c_b_s, m_c_w_out, v_ada_w, v_ada_b, v_norm_pre, v_norm_post, v_ffn_w13, v_ffn_w2, v_ab_w_in, v_a_conv_w, v_a_conv_b, v_a_gate_w, v_a_gate_b, v_a_lam, v_b_conv_w, v_b_conv_b, v_b_norm_g, v_b_norm_b, v_ab_w_out, v_c_w_in, v_c_b_in, v_c_norm_g, v_c_norm_b, v_c_w_s, v_c_b_s, v_c_w_out):
    S, D = x.shape[1], x.shape[2]
    W = a_lam.shape[-1]
    Fh = ffn_w13.shape[-1]
    Fq = ffn_w2.shape[2]
    xi, yi, ci = _coords()
    shard = 2 * xi + yi
    me = 4 * xi + 2 * yi + ci
    x2, tgt = x[0], loss_target[0]

    sharded_small = [norm_pre, norm_post, a_conv_w, b_conv_w, c_b_in, c_norm_g, c_norm_b]
    gathered = _all_gather8(_pack([c] + sharded_small), "gather_small")
    blocks = gathered.reshape(N_DEV, -1, 128)
    per_dev = [_unpack(blocks[d], [c.shape] + [p.shape for p in sharded_small]) for d in range(0, N_DEV, 2)]
    c_all = jnp.concatenate([_unpack(blocks[d], [c.shape])[0] for d in range(N_DEV)], axis=0)
    npre, npost, acw, bcw, cbin, cng, cnb = [jnp.concatenate([per_dev[s][1 + i] for s in range(N_SHARD)], axis=-1)
                                             for i in range(len(sharded_small))]

    ada_b_my = _shard_last(ada_b, shard, ada_w.shape[-1])[:, None, :]
    modp = _ada_fwd(c_all, ada_w, ada_b_my, "ada_fwd")
    modg = _all_gather8(modp.reshape(16, -1), "gather_mod").reshape(N_DEV, 2, 8, -1)
    mod_me = lax.dynamic_index_in_dim(modg[0::2], me, axis=2, keepdims=False)
    mod = jnp.transpose(mod_me, (1, 0, 2)).reshape(2, 3, 3, D)

    w13s, w2s = ffn_w13.reshape(4, D, Fh), ffn_w2.reshape(4, Fq, D)
    placed13 = [_cast_place(w13s, g, f"cast_w13_{g}") for g in range(4)]
    placed2 = [_cast_place(w2s, g, f"cast_w2_{g}") for g in range(4)]
    placed_mix = [_cast_place(w, 0, f"cast_mix{k}") for k, w in enumerate((ab_w_in, ab_w_out, c_w_in, c_w_out))]
    w13_first, w2_first, abin_g, about_g = _gather_weights([placed13[0], placed2[0]] + placed_mix[:2], "gather_first")
    gather_carry = _gather_ici_carry(placed13[1:] + placed2[1:] + placed_mix[2:])

    eye = jnp.eye(8, dtype=F32)
    dh_a = W // 8
    blockdiag = lambda w: jnp.einsum("hde,hg->hdge", w, eye).reshape(W, W)
    gw = a_gate_w[0]
    wg = jnp.concatenate([blockdiag(gw[:, :, :dh_a]), blockdiag(gw[:, :, dh_a:])], axis=1).astype(MXU_DT)
    bgv = jnp.concatenate([a_gate_b[0][:, :dh_a].reshape(-1), a_gate_b[0][:, dh_a:].reshape(-1)])
    bg = _rows8(bgv)
    cwa = jnp.concatenate([acw[0], jnp.zeros((4, W), F32)], axis=0)
    cwb = jnp.concatenate([bcw[0], jnp.zeros((1, W), F32)], axis=0)
    v512 = _rows8(a_conv_b[0], a_lam[0], b_conv_b[0], b_norm_g[0], b_norm_b[0])
    dg_b = W // 8
    gid = jnp.arange(W) // dg_b
    member = (gid[:, None] == jnp.arange(128)[None, :]).astype(F32)
    pavg = jnp.stack([member / dg_b, member]).astype(MXU_DT)
    v2d = _rows8(cbin[0])
    v1d = _rows8(cng[0], cnb[0])
    tril = jnp.tril(jnp.ones((CHUNK, CHUNK), dtype=bool))
    ws = jnp.where(tril, c_w_s[0], 0.0).astype(MXU_DT)
    bsb = jnp.repeat(jnp.transpose(c_b_s[0]), D // N_HEAD, axis=1)

    res_ws = (0.5, 1.0, 0.5)
    vps, xs, saved = [], [], []
    xc = x2
    w13g, w2v = [w13_first], [w2_first.reshape(1, 2, Fh, D)]
    for l in range(2):
        for j in range(3):
            k = 3 * l + j
            vp = _rows8(npre[l, j], mod[l, j, 0], mod[l, j, 1], mod[l, j, 2], npost[l, j])
            vps.append(vp)
            xs.append(xc)
            gi = 2 * l + j // 2
            if k == 0:
                xc, *keep = _ffn_fwd(xc, vp, w13g[0], w2v[0], res_ws[j], "ffn_fwd0", carry=gather_carry)
                keep, landed = keep[:4], keep[4:]
            elif k == 1:
                ab_ops = (abin_g[0], cwa, wg, bg, cwb, v512, pavg, about_g.reshape(D, D))
                xc, *keep = _mix_ab_fwd(xc, vp, *ab_ops, res_ws[j], "mix_ab_fwd", carry=_forward_carry(landed))
                keep, full = keep[:3], keep[3:]
                w13g += list(full[0:3])
                w2v += [w.reshape(1, 2, Fh, D) for w in full[3:6]]
                cin_g, cout_g = full[6:]
                c_ops = (cin_g[0], v2d, v1d, ws, bsb, cout_g.reshape(D, D))
            elif k == 5:
                dout, *keep, lrow = _ffn_fwd(xc, vp, w13g[gi], w2v[gi], res_ws[j], f"ffn_fwd{k}", tgt=tgt)
            elif j != 1:
                xc, *keep = _ffn_fwd(xc, vp, w13g[gi], w2v[gi], res_ws[j], f"ffn_fwd{k}")
            else:
                xc, *keep = _mix_c_fwd(xc, vp, *c_ops, res_ws[j], "mix_c_fwd")
            saved.append(keep)

    joined = {"w13": lax.empty((4, D, Fh), F32), "w2": lax.empty((4, Fq, D), F32), "abin": lax.empty((1, D, 4 * W // 4), F32),
              "about": lax.empty((1, D // 4, D), F32), "cin": lax.empty((1, D, 2 * D // 4), F32),
              "cout": lax.empty((1, D // 4, D), F32)}

    stage = {"send": None, "scatter": []}

    def add_halves(group, got, dtype=F32):
        grads, keys, g, tag = group
        parts = [_add_half(gr, la, f"add_half_{tag}{i}", dtype) for i, (gr, la) in enumerate(zip(grads, got))]
        return parts, keys, g

    def sum_landed(groups, landed_slots):
        landed_slots = list(landed_slots)
        for parts, keys, g in groups:
            for part, key in zip(parts, keys):
                joined[key] = _sum_chips(part, landed_slots.pop(0), joined[key], g, f"sum_chips_{key}{g}")
        return landed_slots

    def all_parts(groups):
        return [part for group in groups for part in group[0]]

    def ride(call, n_own, carries_ok=True):
        if not carries_ok:
            return call(None)[:n_own]
        scat, send = stage["scatter"], stage["send"]
        carries = ([_scatter_carry(all_parts(scat))] if scat else []) + (
            [_send_half_carry(send[0])] if send is not None else [])
        res = call(_merge_carries(carries))
        own, extra = res[:n_own], res[n_own:]
        stage["scatter"], stage["send"] = [], None
        extra = sum_landed(scat, extra)
        if send is not None:
            stage["scatter"].append(add_halves(send, extra))
        return own

    d_npre = [[None] * 3 for _ in range(2)]
    d_npost = [[None] * 3 for _ in range(2)]
    d_mod = [[None] * 3 for _ in range(2)]
    for l in (1, 0):
        for j in (2, 1, 0):
            k = 3 * l + j
            if j != 1:
                f, h, gpre, upre = saved[k]
                gi = 2 * l + j // 2
                dw13 = lax.empty((1, 4, D, Fh), F32)
                dw2v = lax.empty((1, 2, Fh, D), F32)
                df, dh0, s_a, dw13, dw2v = ride(lambda cy: _ffn_bwd_half(
                    0, xs[k], f, dout, vps[k], h, gpre, upre, w13g[gi], w2v[gi], dw13, dw2v, res_ws[j],
                    f"ffn_bwd{k}a", carry=cy), 5)
                dout, s_b, dw13, dw2v = ride(lambda cy: _ffn_bwd_half(
                    1, xs[k], f, dout, vps[k], h, gpre, upre, w13g[gi], w2v[gi], dw13, dw2v, res_ws[j],
                    f"ffn_bwd{k}b", df=df, dh0=dh0, carry=cy), 4, carries_ok=k > 0)
                sums = s_a + s_b
                made = ([dw13, dw2v.reshape(1, 4, Fq, D)], ["w13", "w2"], gi, f"ffn{gi}")
            elif l == 0:
                dout, sums, d_abin, d_about, d_wg, d_cwa, d_cwb, d_bg, d_v512 = ride(lambda cy: _mix_ab_bwd(
                    xs[k], saved[k][0], dout, vps[k], *saved[k][1:], *ab_ops, res_ws[j], "mix_ab_bwd", carry=cy), 9)
                made = ([d_abin[None], d_about.reshape(1, 4, D // 4, D)], ["abin", "about"], 0, "ab")
                stage["scatter"].append(add_halves(made, _send_other_half(made[0], "send_half_ab")))
                made = None
            else:
                dout, sums, d_cin, d_cout, d_ws, d_bsb, d_v2, d_v1 = ride(lambda cy: _mix_c_bwd(
                    xs[k], saved[k][0], dout, vps[k], saved[k][1], *c_ops, res_ws[j], "mix_c_bwd", carry=cy), 8)
                made = ([d_cin[None], d_cout.reshape(1, 4, D // 4, D)], ["cin", "cout"], 0, "c")
            stage["send"] = made
            d_npre[l][j], d_npost[l][j] = sums[4], sums[1]
            d_mod[l][j] = jnp.stack([sums[2], sums[3], sums[0]])
    last = stage["scatter"] + [add_halves(stage["send"], _send_other_half(stage["send"][0], "send_half_last"), MXU_DT)]
    grad_x = dout[None]

    dmod = jnp.stack([jnp.stack(d_mod[l]) for l in range(2)]).reshape(2, 9 * D)
    d_gate_w = jnp.concatenate([jnp.einsum("hdhe->hde", d_wg[:, :W].reshape(8, dh_a, 8, dh_a)),
                                jnp.einsum("hdhe->hde", d_wg[:, W:].reshape(8, dh_a, 8, dh_a))], axis=-1)
    d_gate_b = jnp.concatenate([d_bg[0, :W].reshape(8, dh_a), d_bg[0, W:].reshape(8, dh_a)], axis=-1)
    small_grads = [
        dmod, jnp.stack([jnp.stack(r) for r in d_npre]), jnp.stack([jnp.stack(r) for r in d_npost]),
        d_cwa[:4][None], d_v512[0][None], d_gate_w[None], d_gate_b[None], d_v512[1][None], d_cwb[:31][None],
        d_v512[2][None], d_v512[3][None], d_v512[4][None], d_v2[0][None], d_v1[0][None], d_v1[1][None],
        jnp.where(tril, d_ws, 0.0)[None], jnp.transpose(d_bsb.reshape(CHUNK, N_HEAD, D // N_HEAD).sum(-1))[None]]
    small_w = [ada_b, norm_pre, norm_post, a_conv_w, a_conv_b, a_gate_w, a_gate_b, a_lam, b_conv_w, b_conv_b,
               b_norm_g, b_norm_b, c_b_in, c_norm_g, c_norm_b, c_w_s, c_b_s]
    small_m = [m_ada_b, m_norm_pre, m_norm_post, m_a_conv_w, m_a_conv_b, m_a_gate_w, m_a_gate_b, m_a_lam, m_b_conv_w,
               m_b_conv_b, m_b_norm_g, m_b_norm_b, m_c_b_in, m_c_norm_g, m_c_norm_b, m_c_w_s, m_c_b_s]
    small_v = [v_ada_b, v_norm_pre, v_norm_post, v_a_conv_w, v_a_conv_b, v_a_gate_w, v_a_gate_b, v_a_lam, v_b_conv_w,
               v_b_conv_b, v_b_norm_g, v_b_norm_b, v_c_b_in, v_c_norm_g, v_c_norm_b, v_c_w_s, v_c_b_s]
    full_shapes = [g.shape for g in small_grads]
    loss_part = jnp.sum(lrow[0]).reshape(1, 1)
    sg_all, *landed_last = _all_gather8(_pack(small_grads + [loss_part]), "gather_small_grads",
                                        carry=_scatter_carry(all_parts(last)))
    sum_landed(last, landed_last)
    sg_all = sg_all.reshape(N_DEV, -1, 128)
    sg_sum = _sum_slots(sg_all[None], "sum_small_grads")[0]
    *g_full, loss_sum = _unpack(sg_sum, full_shapes + [(1, 1)])
    loss = loss_sum[0, 0]
    g_small = [g if g.shape == w.shape else _shard_last(g, shard, w.shape[-1]) for g, w in zip(g_full, small_w)]
    small_shapes = [w.shape for w in small_w]
    d_s, m_s, v_s = _adamw(_pack(small_w), _pack(g_small), _pack(small_m), _pack(small_v), "adamw_small")
    delta_small, newm_small, newv_small = (_unpack(a, small_shapes) for a in (d_s, m_s, v_s))

    dmod_all = jnp.stack([_unpack(sg_all[d], full_shapes[:1])[0] for d in range(N_DEV)], axis=1)
    n_ada = ada_w.shape[-1]
    g_ada_w = _ada_bwd(c_all, _shard_last(dmod_all, shard, n_ada), "ada_bwd")

    g_big = _join_halves([joined[key] for key in ("w13", "w2", "abin", "about", "cin", "cout")])

    big_w = [ffn_w13, ffn_w2, ab_w_in, ab_w_out, c_w_in, c_w_out, ada_w]
    big_m = [m_ffn_w13, m_ffn_w2, m_ab_w_in, m_ab_w_out, m_c_w_in, m_c_w_out, m_ada_w]
    big_v = [v_ffn_w13, v_ffn_w2, v_ab_w_in, v_ab_w_out, v_c_w_in, v_c_w_out, v_ada_w]
    big_g = [g.reshape(w.shape) for g, w in zip(list(g_big) + [g_ada_w], big_w)]
    big_out = []
    for k, (w, g, m, v) in enumerate(zip(big_w, big_g, big_m, big_v)):
        two_d = lambda a: a.reshape(-1, a.shape[-1])
        res = _adamw(two_d(w), two_d(g), two_d(m), two_d(v), f"adamw_big{k}")
        big_out.append([r.reshape(w.shape) for r in res])

    names = ["ada_w", "ada_b", "norm_pre", "norm_post", "ffn_w13", "ffn_w2", "ab_w_in", "a_conv_w", "a_conv_b",
             "a_gate_w", "a_gate_b", "a_lam", "b_conv_w", "b_conv_b", "b_norm_g", "b_norm_b", "ab_w_out", "c_w_in",
             "c_b_in", "c_norm_g", "c_norm_b", "c_w_s", "c_b_s", "c_w_out"]
    big_names = ["ffn_w13", "ffn_w2", "ab_w_in", "ab_w_out", "c_w_in", "c_w_out", "ada_w"]
    small_names = ["ada_b", "norm_pre", "norm_post", "a_conv_w", "a_conv_b", "a_gate_w", "a_gate_b", "a_lam",
                   "b_conv_w", "b_conv_b", "b_norm_g", "b_norm_b", "c_b_in", "c_norm_g", "c_norm_b", "c_w_s", "c_b_s"]
    table = {}
    for k, n in enumerate(big_names):
        table[n] = (big_g[k], *big_out[k])
    for k, n in enumerate(small_names):
        table[n] = (g_small[k], delta_small[k], newm_small[k], newv_small[k])
    outs = [loss, grad_x]
    for field in range(4):
        outs += [table[n][field] for n in names]
    return tuple(outs)
```
